```python
import jax, jax.numpy as jnp
from jax import lax
import numpy as np

D_MODEL = 1024
BATCH = 2
SEQ = 8192
DEPTH = 1
DEC_BATCH = 32
DEC_SEQ = 8
PAST_LEN = 16384
PAGE_SIZE = 128

ATT_GROUPS = ((128, 1), (512, 4), (2048, 16))
N_GROUPS = 3
ATT_HEADS = 8
ATT_HEAD_DIM = 64
ATT_WIDTH = ATT_HEADS * ATT_HEAD_DIM
WIN_STEPS = 128
ATT_SCALE = ATT_HEAD_DIM ** -0.5
N_BUCKETS = 32
MAX_DISTANCE = 2048
N_BIAS_HEADS = N_GROUPS * ATT_HEADS
M_HEADS = 4
M_WIDTH = D_MODEL
M_V_DIM = M_WIDTH // M_HEADS
M_QK_DIM = M_V_DIM // 2
CONV_WIDTH = 4
M_CHUNK = 64
EPS = 1e-6
PROJ_SIZES = (N_GROUPS * ATT_WIDTH, N_GROUPS * ATT_WIDTH, N_GROUPS * ATT_WIDTH, ATT_WIDTH,
              M_WIDTH, M_WIDTH, M_WIDTH, M_HEADS, M_HEADS, D_MODEL, D_MODEL)
PROJ_WIDTH = sum(PROJ_SIZES)

kernel_name = "dilated_attn_mlstm_gated_hybrid_step"


def _t5_bucket(dist):
    n = np.asarray(dist).astype(np.int64)
    max_exact = N_BUCKETS // 2
    nf = np.maximum(n, 1).astype(np.float32)
    large = max_exact + (np.log(nf / max_exact) / np.log(np.float32(MAX_DISTANCE / max_exact))
                         * (N_BUCKETS - max_exact)).astype(np.int64)
    large = np.minimum(large, N_BUCKETS - 1)
    return np.where(n < max_exact, n, large).astype(np.int32)


def _rmsnorm(x, gain):
    x32 = x.astype(jnp.float32)
    return x32 * lax.rsqrt(jnp.mean(x32 * x32, -1, keepdims=True) + EPS) * gain


def _head_norm(h, weight):
    mu = jnp.mean(h, -1, keepdims=True)
    hc = h - mu
    hn = hc * lax.rsqrt(jnp.mean(hc * hc, -1, keepdims=True) + EPS)
    B, S = h.shape[:2]
    return hn.reshape(B, S, -1) * weight


def _dilated_prompt(q, k, v, tbl, d):
    f32 = jnp.float32
    B, S, H, E = q.shape
    U = S // d
    nb = -(-U // WIN_STEPS)
    Up = nb * WIN_STEPS

    def res(t):
        t = t.astype(f32).reshape(B, U, d, H, E)
        return jnp.pad(t, ((0, 0), (0, Up - U), (0, 0), (0, 0), (0, 0)))

    def band(t):
        tp = jnp.pad(res(t), ((0, 0), (WIN_STEPS, 0), (0, 0), (0, 0), (0, 0)))
        prev = tp[:, :Up].reshape(B, nb, WIN_STEPS, d, H, E)
        cur = tp[:, WIN_STEPS:].reshape(B, nb, WIN_STEPS, d, H, E)
        return jnp.concatenate([prev, cur], axis=2)

    qr = res(q).reshape(B, nb, WIN_STEPS, d, H, E)
    kb, vb = band(k), band(v)
    qi = np.arange(WIN_STEPS)[:, None]
    kj = np.arange(2 * WIN_STEPS)[None, :]
    dist = qi - kj + WIN_STEPS
    in_band = (dist >= 0) & (dist <= WIN_STEPS)
    bucket = _t5_bucket(np.clip(dist, 0, WIN_STEPS) * d)
    bias = jnp.transpose(tbl[bucket], (2, 0, 1)).astype(f32)
    key_u = np.arange(nb)[:, None] * WIN_STEPS - WIN_STEPS + kj
    mask = in_band[None] & (key_u >= 0)[:, None, :]
    logits = jnp.einsum('bnqrhe,bnkrhe->bnrhqk', qr, kb) * ATT_SCALE + bias
    logits = jnp.where(mask[None, :, None, None], logits, -jnp.inf)
    mx = jnp.max(logits, -1, keepdims=True)
    p = jnp.exp(logits - mx)
    s = jnp.sum(p, -1)
    o = jnp.einsum('bnrhqk,bnkrhe->bnqrhe', p, vb) / jnp.transpose(s, (0, 1, 4, 2, 3))[..., None]
    lse = jnp.transpose(mx[..., 0] + jnp.log(s), (0, 1, 4, 2, 3))
    o = o.reshape(B, Up, d, H, E)[:, :U].reshape(B, S, H, E)
    lse = lse.reshape(B, Up, d, H)[:, :U].reshape(B, S, H)
    return o, lse


def _dilated_sample(q, k, v, buf, tbl, d):
    f32 = jnp.float32
    DB, T, H, E = q.shape
    Lb = buf.shape[1]
    kc = jnp.concatenate([buf[:, :, 0].astype(f32), k.astype(f32)], axis=1)
    vc = jnp.concatenate([buf[:, :, 1].astype(f32), v.astype(f32)], axis=1)
    J = WIN_STEPS + 1
    idx = Lb + np.arange(T)[:, None] - d * np.arange(J)[None, :]
    valid = idx >= 0
    idx_c = np.maximum(idx, 0)
    kg = kc[:, idx_c]
    vg = vc[:, idx_c]
    bias = jnp.transpose(tbl[_t5_bucket(d * np.arange(J))], (1, 0)).astype(f32)
    logits = jnp.einsum('bthe,btjhe->bhtj', q.astype(f32), kg) * ATT_SCALE + bias[None, :, None, :]
    logits = jnp.where(valid[None, None], logits, -jnp.inf)
    mx = jnp.max(logits, -1, keepdims=True)
    p = jnp.exp(logits - mx)
    s = jnp.sum(p, -1)
    o = jnp.einsum('bhtj,btjhe->bthe', p, vg) / jnp.transpose(s, (0, 2, 1))[..., None]
    lse = jnp.transpose(mx[..., 0] + jnp.log(s), (0, 2, 1))
    new_buf = jnp.concatenate([buf.astype(f32), jnp.stack([k.astype(f32), v.astype(f32)], axis=2)], axis=1)[:, T:]
    return o, lse, new_buf


def _mlstm(q, k, v, i_pre, f_pre, C0, n0, m0, chunk):
    f32 = jnp.float32
    B, S, NH, DK = q.shape
    DV = v.shape[-1]
    nc = S // chunk
    q = q.astype(f32) * (DK ** -0.5)
    logf = jax.nn.log_sigmoid(f_pre.astype(f32))

    def blocks(t):
        return jnp.swapaxes(t.astype(f32).reshape((B, nc, chunk) + t.shape[2:]), 0, 1).swapaxes(2, 3)

    xs = (blocks(q), blocks(k), blocks(v), blocks(i_pre), blocks(logf))
    causal = np.tril(np.ones((chunk, chunk), dtype=bool))

    def step(carry, inp):
        C, n, m = carry
        qc, kc, vc, ic, lfc = inp
        b = jnp.cumsum(lfc, axis=-1)
        Dm = jnp.where(causal, b[..., :, None] - b[..., None, :] + ic[..., None, :], -jnp.inf)
        inter = b + m[..., None]
        m_t = jnp.maximum(jnp.max(Dm, -1), inter)
        w_intra = jnp.exp(Dm - m_t[..., None])
        w_inter = jnp.exp(inter - m_t)
        sc = jnp.einsum('bhtk,bhsk->bhts', qc, kc) * w_intra
        num = jnp.einsum('bhts,bhsv->bhtv', sc, vc) + w_inter[..., None] * jnp.einsum('bhvk,bhtk->bhtv', C, qc)
        den = jnp.sum(sc, -1) + w_inter * jnp.einsum('bhk,bhtk->bht', n, qc)
        h = num / jnp.maximum(jnp.abs(den), jnp.exp(-m_t))[..., None]
        bL = b[..., -1]
        g = bL[..., None] - b + ic
        m_new = jnp.maximum(bL + m, jnp.max(g, -1))
        wk = jnp.exp(g - m_new[..., None])
        wC = jnp.exp(bL + m - m_new)
        C = wC[..., None, None] * C + jnp.einsum('bhs,bhsv,bhsk->bhvk', wk, vc, kc)
        n = wC[..., None] * n + jnp.einsum('bhs,bhsk->bhk', wk, kc)
        return (C, n, m_new), h

    (C, n, m), hs = lax.scan(step, (C0.astype(f32), n0.astype(f32), m0.astype(f32)), xs)
    h = jnp.transpose(hs, (1, 0, 3, 2, 4)).reshape(B, S, NH, DV)
    return h, C, n, m


def _layer(x, c, lp, rel_table, conv_prev, C0, n0, m0, kv_bufs, chunk):
    (norm_gain, w_ada, b_ada, w_in, b_if, conv_w, conv_b, w_mq, w_mk,
     m_norm, m_skip, w_pa, w_pm, w_out) = lp
    f32 = jnp.float32
    B, S, _ = x.shape
    ada = jax.nn.silu(c.astype(f32)) @ w_ada + b_ada
    shift, scale, gate = jnp.split(ada, 3, axis=-1)
    h = _rmsnorm(x, norm_gain) * (1.0 + scale[:, None]) + shift[:, None]
    proj = h @ w_in
    offs = [int(o) for o in np.cumsum(PROJ_SIZES)[:-1]]
    q_a, k_a, v_a, z_a, x_m, z_m, o_m, i_pre, f_pre, g_a, g_m = jnp.split(proj, offs, axis=-1)
    grp = (B, S, N_GROUPS, ATT_HEADS, ATT_HEAD_DIM)
    q_a, k_a, v_a = q_a.reshape(grp), k_a.reshape(grp), v_a.reshape(grp)

    outs, lses, new_bufs = [], [], []
    for g, (win, dil) in enumerate(ATT_GROUPS):
        tbl = rel_table[:, g * ATT_HEADS:(g + 1) * ATT_HEADS]
        if kv_bufs is None:
            o, lse = _dilated_prompt(q_a[:, :, g], k_a[:, :, g], v_a[:, :, g], tbl, dil)
            keep = min(win, S)
            buf = jnp.stack([k_a[:, S - keep:, g], v_a[:, S - keep:, g]], axis=2)
        else:
            o, lse, buf = _dilated_sample(q_a[:, :, g], k_a[:, :, g], v_a[:, :, g], kv_bufs[g], tbl, dil)
        outs.append(o)
        lses.append(lse)
        new_bufs.append(buf)
    alpha = jax.nn.softmax(jnp.stack(lses, 0), axis=0)
    o_att = jnp.sum(alpha[..., None] * jnp.stack(outs, 0), axis=0).reshape(B, S, ATT_WIDTH)
    a_branch = (o_att * jax.nn.silu(z_a)) @ w_pa

    conv_in = jnp.concatenate([conv_prev.astype(f32), x_m], axis=1)
    conv = conv_b + sum(conv_w[j] * conv_in[:, j:j + S] for j in range(CONV_WIDTH))
    c_act = jax.nn.silu(conv)
    new_conv = conv_in[:, S:]
    ch = c_act.reshape(B, S, M_HEADS, M_V_DIM)
    mq = jnp.einsum('bshe,hed->bshd', ch, w_mq)
    mk = jnp.einsum('bshe,hed->bshd', ch, w_mk)
    mv = x_m.reshape(B, S, M_HEADS, M_V_DIM)
    hcell, C, n, m = _mlstm(mq, mk, mv, i_pre + b_if[:M_HEADS], f_pre + b_if[M_HEADS:], C0, n0, m0, chunk)
    m_out = (jax.nn.sigmoid(o_m) * _head_norm(hcell, m_norm) + m_skip * c_act) * jax.nn.silu(z_m)
    m_branch = m_out @ w_pm

    merged = jax.nn.sigmoid(g_a) * a_branch + jax.nn.sigmoid(g_m) * m_branch
    y = x + gate[:, None] * (merged @ w_out)
    return y, new_bufs, new_conv, C, n, m


def setup_inputs(seed: int = 0) -> dict:
    key = jax.random.key(seed)
    ks = jax.random.split(key, 32)
    f32 = jnp.float32

    def nrm(k, shape, s):
        return jax.random.normal(k, shape, f32) * s

    lb = [min(w, PAST_LEN) for w, _ in ATT_GROUPS]
    kv_shape = lambda L: (DEPTH, DEC_BATCH, L, 2, ATT_HEADS, ATT_HEAD_DIM)
    b_i = nrm(ks[26], (DEPTH, M_HEADS), 0.1)
    b_f = 3.0 + jnp.linspace(0.0, 3.0, M_HEADS, dtype=f32)[None] + nrm(ks[27], (DEPTH, M_HEADS), 0.1)
    return {
        "x_prompt": nrm(ks[0], (BATCH, SEQ, D_MODEL), 1.0),
        "x_sample": nrm(ks[1], (DEC_BATCH, DEC_SEQ, D_MODEL), 1.0),
        "cache_kv_w128": nrm(ks[2], kv_shape(lb[0]), 1.0),
        "cache_kv_w512": nrm(ks[3], kv_shape(lb[1]), 1.0),
        "cache_kv_w2048": nrm(ks[4], kv_shape(lb[2]), 1.0),
        "state_conv": nrm(ks[5], (DEPTH, DEC_BATCH, CONV_WIDTH - 1, M_WIDTH), 1.0),
        "state_C": nrm(ks[6], (DEPTH, DEC_BATCH, M_HEADS, M_V_DIM, M_QK_DIM), 1.0),
        "state_n": nrm(ks[7], (DEPTH, DEC_BATCH, M_HEADS, M_QK_DIM), 1.0),
        "state_m": nrm(ks[8], (DEPTH, DEC_BATCH, M_HEADS), 0.5),
        "c_prompt": nrm(ks[9], (BATCH, D_MODEL), 1.0),
        "c_sample": nrm(ks[10], (DEC_BATCH, D_MODEL), 1.0),
        "rel_table": nrm(ks[11], (N_BUCKETS, N_BIAS_HEADS), 0.5),
        "norm_gain": 1.0 + nrm(ks[12], (DEPTH, D_MODEL), 0.1),
        "w_ada": nrm(ks[13], (DEPTH, D_MODEL, 3 * D_MODEL), 0.3 * D_MODEL ** -0.5),
        "b_ada": nrm(ks[14], (DEPTH, 3 * D_MODEL), 0.02),
        "w_in": nrm(ks[15], (DEPTH, D_MODEL, PROJ_WIDTH), D_MODEL ** -0.5),
        "b_if": jnp.concatenate([b_i, b_f], axis=-1),
        "conv_w": nrm(ks[16], (DEPTH, CONV_WIDTH, M_WIDTH), CONV_WIDTH ** -0.5),
        "conv_b": nrm(ks[17], (DEPTH, M_WIDTH), 0.02),
        "w_mq": nrm(ks[18], (DEPTH, M_HEADS, M_V_DIM, M_QK_DIM), M_V_DIM ** -0.5),
        "w_mk": nrm(ks[19], (DEPTH, M_HEADS, M_V_DIM, M_QK_DIM), M_V_DIM ** -0.5),
        "m_norm": 1.0 + nrm(ks[20], (DEPTH, M_WIDTH), 0.1),
        "m_skip": 1.0 + nrm(ks[21], (DEPTH, M_WIDTH), 0.1),
        "w_pa": nrm(ks[22], (DEPTH, ATT_WIDTH, D_MODEL), ATT_WIDTH ** -0.5),
        "w_pm": nrm(ks[23], (DEPTH, M_WIDTH, D_MODEL), M_WIDTH ** -0.5),
        "w_out": nrm(ks[24], (DEPTH, D_MODEL, D_MODEL), D_MODEL ** -0.5),
        "final_gain": 1.0 + nrm(ks[25], (D_MODEL,), 0.1),
    }


def reference(x_prompt, x_sample, cache_kv_w128, cache_kv_w512, cache_kv_w2048, state_conv, state_C,
              state_n, state_m, c_prompt, c_sample, rel_table, norm_gain, w_ada, b_ada, w_in, b_if,
              conv_w, conv_b, w_mq, w_mk, m_norm, m_skip, w_pa, w_pm, w_out, final_gain):
    f32 = jnp.float32
    B, S = x_prompt.shape[:2]
    T = x_sample.shape[1]
    names = ('kv128', 'kv512', 'kv2048', 'conv', 'C', 'n', 'm')
    new_p = {nm: [] for nm in names}
    new_s = {nm: [] for nm in names}
    xp, xs = x_prompt, x_sample
    for l in range(DEPTH):
        lp = (norm_gain[l], w_ada[l], b_ada[l], w_in[l], b_if[l], conv_w[l], conv_b[l], w_mq[l], w_mk[l],
              m_norm[l], m_skip[l], w_pa[l], w_pm[l], w_out[l])
        xp, bufs, cv, C, n, m = _layer(
            xp, c_prompt, lp, rel_table,
            jnp.zeros((B, CONV_WIDTH - 1, M_WIDTH), f32),
            jnp.zeros((B, M_HEADS, M_V_DIM, M_QK_DIM), f32),
            jnp.zeros((B, M_HEADS, M_QK_DIM), f32),
            jnp.zeros((B, M_HEADS), f32),
            None, min(M_CHUNK, S))
        for nm, val in zip(names, (bufs[0], bufs[1], bufs[2], cv, C, n, m)):
            new_p[nm].append(val)
        xs, bufs, cv, C, n, m = _layer(
            xs, c_sample, lp, rel_table, state_conv[l], state_C[l], state_n[l], state_m[l],
            (cache_kv_w128[l], cache_kv_w512[l], cache_kv_w2048[l]), T)
        for nm, val in zip(names, (bufs[0], bufs[1], bufs[2], cv, C, n, m)):
            new_s[nm].append(val)
    sp = {nm: jnp.stack(v, 0) for nm, v in new_p.items()}
    ss = {nm: jnp.stack(v, 0) for nm, v in new_s.items()}
    y_prompt = _rmsnorm(xp, final_gain).astype(x_prompt.dtype)
    y_sample = _rmsnorm(xs, final_gain).astype(x_sample.dtype)
    return (y_prompt, y_sample, sp['kv128'], ss['kv128'], sp['kv512'], ss['kv512'], sp['kv2048'], ss['kv2048'],
            sp['conv'], ss['conv'], sp['C'], ss['C'], sp['n'], ss['n'], sp['m'], ss['m'])
```

```python
import functools

import numpy as np
import jax
import jax.numpy as jnp
from jax import lax
from jax.experimental import pallas as pl
from jax.experimental.pallas import tpu as pltpu

F32 = jnp.float32
BF16 = jnp.bfloat16

ATT_GROUPS = ((128, 1), (512, 4), (2048, 16))
N_GROUPS = len(ATT_GROUPS)
ATT_HEADS = 8
ATT_HEAD_DIM = 64
ATT_WIDTH = ATT_HEADS * ATT_HEAD_DIM
WIN_STEPS = 128
ATT_SCALE = ATT_HEAD_DIM ** -0.5
N_BUCKETS = 32
MAX_DISTANCE = 2048
M_HEADS = 4
CONV_WIDTH = 4
EPS = 1e-6
NEG = -1e30

LANES = 128
SUBLANES = 8
VMEM_LIMIT = 56 * 1024 * 1024

ATT_TILE = 2048
PROJ_TM = 512
POST_TM = 512
MLSTM_CHUNK = 256


def _params(sem, vmem=VMEM_LIMIT):
    return pltpu.CompilerParams(dimension_semantics=sem, vmem_limit_bytes=vmem)


def _t5_bucket(dist):
    n = np.asarray(dist).astype(np.int64)
    max_exact = N_BUCKETS // 2
    nf = np.maximum(n, 1).astype(np.float32)
    large = max_exact + (np.log(nf / max_exact) / np.log(np.float32(MAX_DISTANCE / max_exact))
                         * (N_BUCKETS - max_exact)).astype(np.int64)
    large = np.minimum(large, N_BUCKETS - 1)
    return np.where(n < max_exact, n, large).astype(np.int32)


def _silu(x):
    return x * jax.nn.sigmoid(x)


def _ada_kernel(c_ref, w_ref, b_ref, o_ref):
    s = _silu(c_ref[...])
    o_ref[...] = jnp.dot(s, w_ref[...], preferred_element_type=F32,
                         precision=lax.Precision.HIGHEST) + b_ref[...]


def _ada_call(c, w, b):
    n, d = c.shape
    width = w.shape[1]
    tn = 512
    return pl.pallas_call(
        _ada_kernel,
        grid=(width // tn,),
        in_specs=[pl.BlockSpec((n, d), lambda j: (0, 0)),
                  pl.BlockSpec((d, tn), lambda j: (0, j)),
                  pl.BlockSpec((1, tn), lambda j: (0, j))],
        out_specs=pl.BlockSpec((n, tn), lambda j: (0, j)),
        out_shape=jax.ShapeDtypeStruct((n, width), F32),
        compiler_params=_params(("arbitrary",)),
        name="ada",
    )(c, w, b.reshape(1, width))


def _proj_kernel(x_ref, gain_ref, scale_ref, shift_ref, w_ref, *out_refs, segs):
    x = x_ref[0]
    ms = jnp.mean(x * x, axis=-1, keepdims=True)
    h = x * lax.rsqrt(ms + EPS) * gain_ref[...] * (1.0 + scale_ref[0]) + shift_ref[0]
    hb = h.astype(BF16)
    for o_ref, (c0, width) in zip(out_refs, segs):
        o_ref[0] = jnp.dot(hb, w_ref[:, c0:c0 + width],
                           preferred_element_type=F32).astype(o_ref.dtype)


def _proj_call(x3, gain, scale3, shift3, w, segs, dtypes, *, tm, row0, rows, name):
    B, S, D = x3.shape
    nrb = rows // tm
    rb0 = row0 // tm
    per_row = scale3.shape[1] != 1
    if per_row:
        mod_spec = pl.BlockSpec((1, tm, D), lambda b, i: (b, rb0 + i, 0))
    else:
        mod_spec = pl.BlockSpec((1, 1, D), lambda b, i: (b, 0, 0))
    out_shape = [jax.ShapeDtypeStruct((B, rows, wd), dt) for (_, wd), dt in zip(segs, dtypes)]
    out_specs = [pl.BlockSpec((1, tm, wd), lambda b, i: (b, i, 0)) for (_, wd) in segs]
    return pl.pallas_call(
        functools.partial(_proj_kernel, segs=tuple(segs)),
        grid=(B, nrb),
        in_specs=[pl.BlockSpec((1, tm, D), lambda b, i: (b, rb0 + i, 0)),
                  pl.BlockSpec((1, D), lambda b, i: (0, 0)),
                  mod_spec, mod_spec,
                  pl.BlockSpec(w.shape, lambda b, i: (0, 0), pipeline_mode=pl.Buffered(1))],
        out_specs=out_specs,
        out_shape=out_shape,
        compiler_params=_params(("arbitrary", "arbitrary")),
        name=name,
    )(x3, gain, scale3, shift3, w)


def _attn_head_unit(q, kp, vp, kc, vc, bias_ref, prev_mask, o_ref, l_ref, rows):
    dn = (((1,), (1,)), ((), ()))
    for h in range(ATT_HEADS):
        cs = slice(h * ATT_HEAD_DIM, (h + 1) * ATT_HEAD_DIM)
        qh = q[:, cs]
        bias = bias_ref[h]
        s_p = lax.dot_general(qh, kp[:, cs], dn, preferred_element_type=F32) + bias[:, :WIN_STEPS]
        if prev_mask is not None:
            s_p = s_p + prev_mask
        s_c = lax.dot_general(qh, kc[:, cs], dn, preferred_element_type=F32) + bias[:, WIN_STEPS:]
        mx = jnp.maximum(jnp.max(s_p, -1, keepdims=True), jnp.max(s_c, -1, keepdims=True))
        p_p = jnp.exp(s_p - mx)
        p_c = jnp.exp(s_c - mx)
        ssum = jnp.sum(p_p, -1, keepdims=True) + jnp.sum(p_c, -1, keepdims=True)
        acc = (jnp.dot(p_p.astype(BF16), vp[:, cs], preferred_element_type=F32)
               + jnp.dot(p_c.astype(BF16), vc[:, cs], preferred_element_type=F32))
        o_ref[0, rows, cs] = (acc / ssum).astype(o_ref.dtype)
        l_ref[0, rows, cs] = jnp.broadcast_to(mx + jnp.log(ssum), (WIN_STEPS, ATT_HEAD_DIM))


def _attn_kernel(q_ref, kvc_ref, kvp_ref, bias_ref, o_ref, l_ref, *, ns):
    prev_mask = jnp.where(pl.program_id(1) == 0, NEG, 0.0).astype(F32)
    r0 = slice(0, WIN_STEPS)
    q0 = q_ref[0, r0, :] * ATT_SCALE
    _attn_head_unit(q0, kvp_ref[0, :, :ATT_WIDTH], kvp_ref[0, :, ATT_WIDTH:],
                    kvc_ref[0, r0, :ATT_WIDTH], kvc_ref[0, r0, ATT_WIDTH:],
                    bias_ref, prev_mask, o_ref, l_ref, r0)

    def body(j, carry):
        rc = pl.ds(pl.multiple_of(j * WIN_STEPS, WIN_STEPS), WIN_STEPS)
        rp = pl.ds(pl.multiple_of((j - 1) * WIN_STEPS, WIN_STEPS), WIN_STEPS)
        q = q_ref[0, rc, :] * ATT_SCALE
        _attn_head_unit(q, kvc_ref[0, rp, :ATT_WIDTH], kvc_ref[0, rp, ATT_WIDTH:],
                        kvc_ref[0, rc, :ATT_WIDTH], kvc_ref[0, rc, ATT_WIDTH:],
                        bias_ref, None, o_ref, l_ref, rc)
        return carry

    if ns > 1:
        lax.fori_loop(1, ns, body, 0)


def _attn_call(q, kv, bias, d, name):
    B, S, _ = q.shape
    U = S // d
    ns = ATT_TILE // (WIN_STEPS * d)
    rows = ns * WIN_STEPS
    qv = q.reshape(B, U, d * ATT_WIDTH)
    kvv = kv.reshape(B, U, d * 2 * ATT_WIDTH)
    o, l = pl.pallas_call(
        functools.partial(_attn_kernel, ns=ns),
        grid=(B, S // ATT_TILE, d),
        in_specs=[pl.BlockSpec((1, rows, ATT_WIDTH), lambda b, t, r: (b, t, r)),
                  pl.BlockSpec((1, rows, 2 * ATT_WIDTH), lambda b, t, r: (b, t, r)),
                  pl.BlockSpec((1, WIN_STEPS, 2 * ATT_WIDTH),
                               lambda b, t, r: (b, jnp.maximum(t * ns - 1, 0), r)),
                  pl.BlockSpec(bias.shape, lambda b, t, r: (0, 0, 0))],
        out_specs=[pl.BlockSpec((1, rows, ATT_WIDTH), lambda b, t, r: (b, t, r)),
                   pl.BlockSpec((1, rows, ATT_WIDTH), lambda b, t, r: (b, t, r))],
        out_shape=[jax.ShapeDtypeStruct((B, U, d * ATT_WIDTH), BF16),
                   jax.ShapeDtypeStruct((B, U, d * ATT_WIDTH), F32)],
        compiler_params=_params(("arbitrary", "arbitrary", "arbitrary")),
        name=name,
    )(qv, kvv, kvv, bias)
    return o.reshape(B, S, ATT_WIDTH), l.reshape(B, S, ATT_WIDTH)


def _prompt_bias(rel_table, g, d):
    qi = np.arange(WIN_STEPS)[:, None]
    kj = np.arange(2 * WIN_STEPS)[None, :]
    dist = qi - kj + WIN_STEPS
    in_band = (dist >= 0) & (dist <= WIN_STEPS)
    bucket = _t5_bucket(np.clip(dist, 0, WIN_STEPS) * d)
    tbl = rel_table[:, g * ATT_HEADS:(g + 1) * ATT_HEADS].astype(F32)
    bias = jnp.transpose(tbl[bucket], (2, 0, 1))
    return jnp.where(in_band[None], bias, NEG)


def _sattn_kernel(*refs, T):
    q_refs = refs[0:3]
    kvn_refs = refs[3:6]
    cache_refs = refs[6:9]
    bc_refs = refs[9:12]
    bn_refs = refs[12:15]
    o_ref = refs[15]
    co_refs = refs[16:19]
    HT = ATT_HEADS * T
    dn = (((1,), (1,)), ((), ()))
    row_head = lax.broadcasted_iota(jnp.int32, (HT, ATT_WIDTH), 0) // T
    col_head = lax.broadcasted_iota(jnp.int32, (HT, ATT_WIDTH), 1) // ATT_HEAD_DIM
    head_mask = row_head == col_head

    stats = []
    for g in range(N_GROUPS):
        L = cache_refs[g].shape[1]
        q = q_refs[g][0] * ATT_SCALE
        qexp = jnp.where(head_mask, jnp.concatenate([q] * ATT_HEADS, axis=0), 0.0).astype(BF16)
        kvn = kvn_refs[g][0]
        kvn_pad = jnp.concatenate([kvn, jnp.zeros((LANES - T, 2 * ATT_WIDTH), F32)], axis=0).astype(BF16)
        kc = cache_refs[g][0, :, :ATT_WIDTH].astype(BF16)
        lc = lax.dot_general(qexp, kc, dn, preferred_element_type=F32) + bc_refs[g][...]
        ln = lax.dot_general(qexp, kvn_pad[:, :ATT_WIDTH], dn, preferred_element_type=F32) + bn_refs[g][...]
        mx = jnp.maximum(jnp.max(lc, -1, keepdims=True), jnp.max(ln, -1, keepdims=True))
        pc = jnp.exp(lc - mx)
        pn = jnp.exp(ln - mx)
        ssum = jnp.sum(pc, -1, keepdims=True) + jnp.sum(pn, -1, keepdims=True)
        stats.append((pc, pn, ssum, mx + jnp.log(ssum), kvn_pad))
        co_refs[g][0, 0:L - T, :] = cache_refs[g][0, T:L, :]
        co_refs[g][0, L - T:L, :] = kvn

    lse_max = jnp.maximum(jnp.maximum(stats[0][3], stats[1][3]), stats[2][3])
    es = [jnp.exp(st[3] - lse_max) for st in stats]
    esum = es[0] + es[1] + es[2]
    acc = jnp.zeros((HT, ATT_WIDTH), F32)
    for g in range(N_GROUPS):
        pc, pn, ssum, _, kvn_pad = stats[g]
        w = es[g] / (esum * ssum)
        vc = cache_refs[g][0, :, ATT_WIDTH:].astype(BF16)
        acc = acc + jnp.dot((pc * w).astype(BF16), vc, preferred_element_type=F32)
        acc = acc + jnp.dot((pn * w).astype(BF16), kvn_pad[:, ATT_WIDTH:], preferred_element_type=F32)
    lane_head = lax.broadcasted_iota(jnp.int32, (T, ATT_WIDTH), 1) // ATT_HEAD_DIM
    o = jnp.zeros((T, ATT_WIDTH), F32)
    for h in range(ATT_HEADS):
        o = o + jnp.where(lane_head == h, acc[h * T:(h + 1) * T, :], 0.0)
    o_ref[0] = o


def _sample_bias(rel_table, g, W, d, Lb, T):
    tbl = rel_table[:, g * ATT_HEADS:(g + 1) * ATT_HEADS].astype(F32)
    t = np.arange(T)[:, None]

    def table(idx):
        dist = Lb + t - idx[None, :]
        valid = (dist >= 0) & (dist % d == 0) & (dist // d <= WIN_STEPS)
        bucket = _t5_bucket(np.clip(dist, 0, W))
        bias = jnp.transpose(tbl[bucket], (2, 0, 1))
        bias = jnp.where(valid[None], bias, NEG)
        return bias.reshape(ATT_HEADS * T, idx.shape[0])

    bc = table(np.arange(Lb))
    bn = table(Lb + np.arange(T))
    bn = jnp.concatenate([bn, jnp.full((ATT_HEADS * T, LANES - T), NEG, F32)], axis=1)
    return bc, bn


def _sattn_call(qs, kvns, caches, bcs, bns, T):
    DB = caches[0].shape[0]
    in_specs = ([pl.BlockSpec((1, T, ATT_WIDTH), lambda b: (b, 0, 0))] * 3
                + [pl.BlockSpec((1, T, 2 * ATT_WIDTH), lambda b: (b, 0, 0))] * 3
                + [pl.BlockSpec((1,) + c.shape[1:], lambda b: (b, 0, 0)) for c in caches]
                + [pl.BlockSpec(x.shape, lambda b: (0, 0)) for x in bcs]
                + [pl.BlockSpec(x.shape, lambda b: (0, 0)) for x in bns])
    out_specs = ([pl.BlockSpec((1, T, ATT_WIDTH), lambda b: (b, 0, 0))]
                 + [pl.BlockSpec((1,) + c.shape[1:], lambda b: (b, 0, 0)) for c in caches])
    out_shape = ([jax.ShapeDtypeStruct((DB, T, ATT_WIDTH), F32)]
                 + [jax.ShapeDtypeStruct(c.shape, F32) for c in caches])
    return pl.pallas_call(
        functools.partial(_sattn_kernel, T=T),
        grid=(DB,),
        in_specs=in_specs, out_specs=out_specs, out_shape=out_shape,
        compiler_params=_params(("arbitrary",)),
        name="sample_attn",
    )(*qs, *kvns, *caches, *bcs, *bns)


def _scan_rows(x, op, fill):
    n = x.shape[0]
    rowid = lax.broadcasted_iota(jnp.int32, x.shape, 0)
    s = 1
    while s < n:
        shifted = pltpu.roll(x, s, axis=0)
        x = op(x, jnp.where(rowid >= s, shifted, fill))
        s *= 2
    return x


def _pad_rows(x, n, fill=0.0):
    if x.shape[0] == n:
        return x
    return jnp.concatenate([x, jnp.full((n - x.shape[0],) + x.shape[1:], fill, x.dtype)], axis=0)


def _mlstm_kernel(xm_ref, zm_ref, om_ref, g_ref, cprev_ref, C0_ref, n0_ref, m0_ref,
                  convw_ref, convb_ref, wq_ref, wk_ref, mnorm_ref, mskip_ref, bif_ref,
                  mo_ref, C_ref, n_ref, m_ref, xc_ref, *, L):
    LS = max(L, LANES)
    DV = C_ref.shape[2]
    DK = C_ref.shape[3]
    c = pl.program_id(1)

    @pl.when(c == 0)
    def _():
        xc_ref[0:SUBLANES, :] = cprev_ref[0]
        C_ref[...] = C0_ref[...]
        n_ref[...] = n0_ref[...]
        m_ref[...] = m0_ref[...]

    xm_b = xm_ref[0].astype(BF16)
    xc_ref[SUBLANES:SUBLANES + L, :] = xm_ref[0].astype(F32)
    conv = convb_ref[...]
    for j in range(CONV_WIDTH):
        off = SUBLANES - (CONV_WIDTH - 1) + j
        conv = conv + convw_ref[j:j + 1, :] * xc_ref[off:off + L, :]
    xc_ref[0:SUBLANES, :] = xc_ref[L:L + SUBLANES, :]
    c_act = _silu(conv)
    cb = c_act.astype(BF16)

    gates = g_ref[0] + bif_ref[...]
    i_pre = gates[:, :LANES]
    logf = jax.nn.log_sigmoid(gates[:, LANES:])
    b = _scan_rows(logf, jnp.add, 0.0)
    a = i_pre - b
    ca = _scan_rows(a, jnp.maximum, NEG)
    m_prev = m_ref[0]
    mm = jnp.maximum(ca, m_prev)
    u = -mm
    w_inter = jnp.exp(u + m_prev)
    emt = jnp.exp(-(b + mm))
    bL = b[L - 1:L, :]
    m_new = bL + jnp.maximum(m_prev, ca[L - 1:L, :])
    wk = jnp.exp(bL + a - m_new)
    wC = jnp.exp(bL + m_prev - m_new)
    m_ref[0] = m_new

    t_id = lax.broadcasted_iota(jnp.int32, (L, LS), 0)
    s_id = lax.broadcasted_iota(jnp.int32, (L, LS), 1)
    causal = s_id <= t_id
    eye = s_id == t_id
    dn_t = (((1,), (1,)), ((), ()))

    for h in range(M_HEADS):
        vs = slice(h * DV, (h + 1) * DV)
        ch = cb[:, vs]
        qh = (jnp.dot(ch, wq_ref[h], preferred_element_type=F32) * (DK ** -0.5)).astype(BF16)
        kh = jnp.dot(ch, wk_ref[h], preferred_element_type=F32).astype(BF16)
        kh_s = _pad_rows(kh, LS)
        vh_s = _pad_rows(xm_b[:, vs], LS)
        a_col = a[:, h:h + 1]
        a_row = jnp.sum(jnp.where(eye, a_col, 0.0), axis=0, keepdims=True)
        dm = jnp.where(causal, u[:, h:h + 1] + a_row, NEG)
        w_intra = jnp.exp(dm)
        sc = lax.dot_general(qh, kh_s, dn_t, preferred_element_type=F32) * w_intra
        C_h = C_ref[0, h]
        n_h = n_ref[0, h:h + 1, :]
        wi = w_inter[:, h:h + 1]
        num = (jnp.dot(sc.astype(BF16), vh_s, preferred_element_type=F32)
               + wi * lax.dot_general(qh, C_h.astype(BF16), dn_t, preferred_element_type=F32))
        den = (jnp.sum(sc, -1, keepdims=True)
               + wi * jnp.sum(qh.astype(F32) * n_h, -1, keepdims=True))
        hcell = num / jnp.maximum(jnp.abs(den), emt[:, h:h + 1])
        mu = jnp.mean(hcell, -1, keepdims=True)
        hc = hcell - mu
        hn = hc * lax.rsqrt(jnp.mean(hc * hc, -1, keepdims=True) + EPS)
        out = ((jax.nn.sigmoid(om_ref[0, :, vs].astype(F32)) * (hn * mnorm_ref[:, vs])
                + mskip_ref[:, vs] * c_act[:, vs]) * _silu(zm_ref[0, :, vs].astype(F32)))
        mo_ref[0, :, vs] = out.astype(mo_ref.dtype)
        wk_s = _pad_rows(wk[:, h:h + 1], LS)
        vw = (vh_s.astype(F32) * wk_s).astype(BF16)
        wc = wC[:, h:h + 1]
        C_ref[0, h] = wc * C_h + lax.dot_general(vw, kh_s, (((0,), (0,)), ((), ())),
                                                 preferred_element_type=F32)
        n_ref[0, h:h + 1, :] = wc * n_h + jnp.sum(kh_s.astype(F32) * wk_s, axis=0, keepdims=True)


def _mlstm_call(xm, zm, om, gates, conv_prev, C0, n0, m0, convw, convb, wq, wk, mnorm, mskip, bif,
                *, L, out_dtype, name):
    N, S, M = xm.shape
    nc = S // L
    H, DV, DK = C0.shape[1:]
    seq = lambda b, c: (b, c, 0)
    per_b3 = lambda b, c: (b, 0, 0)
    per_b4 = lambda b, c: (b, 0, 0, 0)
    const2 = lambda b, c: (0, 0)
    const3 = lambda b, c: (0, 0, 0)
    return pl.pallas_call(
        functools.partial(_mlstm_kernel, L=L),
        grid=(N, nc),
        in_specs=[pl.BlockSpec((1, L, M), seq), pl.BlockSpec((1, L, M), seq), pl.BlockSpec((1, L, M), seq),
                  pl.BlockSpec((1, L, 2 * LANES), seq),
                  pl.BlockSpec((1, SUBLANES, M), per_b3),
                  pl.BlockSpec((1, H, DV, DK), per_b4),
                  pl.BlockSpec((1, H, DK), per_b3),
                  pl.BlockSpec((1, 1, LANES), per_b3),
                  pl.BlockSpec(convw.shape, const2), pl.BlockSpec(convb.shape, const2),
                  pl.BlockSpec(wq.shape, const3), pl.BlockSpec(wk.shape, const3),
                  pl.BlockSpec(mnorm.shape, const2), pl.BlockSpec(mskip.shape, const2),
                  pl.BlockSpec(bif.shape, const2)],
        out_specs=[pl.BlockSpec((1, L, M), seq),
                   pl.BlockSpec((1, H, DV, DK), per_b4),
                   pl.BlockSpec((1, H, DK), per_b3),
                   pl.BlockSpec((1, 1, LANES), per_b3)],
        out_shape=[jax.ShapeDtypeStruct((N, S, M), out_dtype),
                   jax.ShapeDtypeStruct((N, H, DV, DK), F32),
                   jax.ShapeDtypeStruct((N, H, DK), F32),
                   jax.ShapeDtypeStruct((N, 1, LANES), F32)],
        scratch_shapes=[pltpu.VMEM((L + 2 * SUBLANES, M), F32)],
        compiler_params=_params(("arbitrary", "arbitrary")),
        name=name,
    )(xm, zm, om, gates, conv_prev, C0, n0, m0, convw, convb, wq, wk, mnorm, mskip, bif)


def _post_kernel(*refs, merge):
    if merge:
        (x_ref, gate_ref, o0, o1, o2, l0, l1, l2, za_ref, mo_ref, ga_ref, gm_ref,
         wpa_ref, wpm_ref, wout_ref, fg_ref, y_ref) = refs
        lmax = jnp.maximum(jnp.maximum(l0[0], l1[0]), l2[0])
        e0 = jnp.exp(l0[0] - lmax)
        e1 = jnp.exp(l1[0] - lmax)
        e2 = jnp.exp(l2[0] - lmax)
        o_att = (e0 * o0[0].astype(F32) + e1 * o1[0].astype(F32) + e2 * o2[0].astype(F32)) / (e0 + e1 + e2)
    else:
        (x_ref, gate_ref, oa_ref, za_ref, mo_ref, ga_ref, gm_ref,
         wpa_ref, wpm_ref, wout_ref, fg_ref, y_ref) = refs
        o_att = oa_ref[0]
    a_in = (o_att * _silu(za_ref[0].astype(F32))).astype(BF16)
    a_br = jnp.dot(a_in, wpa_ref[...], preferred_element_type=F32)
    m_br = jnp.dot(mo_ref[0].astype(BF16), wpm_ref[...], preferred_element_type=F32)
    merged = (jax.nn.sigmoid(ga_ref[0].astype(F32)) * a_br
              + jax.nn.sigmoid(gm_ref[0].astype(F32)) * m_br)
    y = x_ref[0] + gate_ref[0] * jnp.dot(merged.astype(BF16), wout_ref[...], preferred_element_type=F32)
    ms = jnp.mean(y * y, axis=-1, keepdims=True)
    y_ref[0] = y * lax.rsqrt(ms + EPS) * fg_ref[...]


def _post_call(x3, gate3, att_inputs, za, mo, ga, gm, wpa, wpm, wout, fgain, *, tm, merge, name):
    B, S, D = x3.shape
    row = lambda b, i: (b, i, 0)
    const2 = lambda b, i: (0, 0)
    if gate3.shape[1] == 1:
        gate_spec = pl.BlockSpec((1, 1, D), lambda b, i: (b, 0, 0))
    else:
        gate_spec = pl.BlockSpec((1, tm, D), row)
    blk = lambda a: pl.BlockSpec((1, tm, a.shape[2]), row)
    in_specs = ([pl.BlockSpec((1, tm, D), row), gate_spec]
                + [blk(a) for a in att_inputs]
                + [blk(za), blk(mo), blk(ga), blk(gm),
                   pl.BlockSpec(wpa.shape, const2), pl.BlockSpec(wpm.shape, const2),
                   pl.BlockSpec(wout.shape, const2), pl.BlockSpec(fgain.shape, const2)])
    return pl.pallas_call(
        functools.partial(_post_kernel, merge=merge),
        grid=(B, S // tm),
        in_specs=in_specs,
        out_specs=pl.BlockSpec((1, tm, D), row),
        out_shape=jax.ShapeDtypeStruct((B, S, D), F32),
        compiler_params=_params(("arbitrary", "arbitrary")),
        name=name,
    )(x3, gate3, *att_inputs, za, mo, ga, gm, wpa, wpm, wout, fgain)


def _permuted_weights(w_in):
    D = w_in.shape[0]
    AW = ATT_WIDTH
    M = D
    off_q, off_k, off_v = 0, 3 * AW, 6 * AW
    off_za = 9 * AW
    off_xm = off_za + AW
    off_zm, off_om = off_xm + M, off_xm + 2 * M
    off_i = off_xm + 3 * M
    off_f = off_i + M_HEADS
    off_ga = off_f + M_HEADS
    off_gm = off_ga + D
    cols = [w_in[:, off_q:off_q + 3 * AW], w_in[:, off_za:off_za + AW]]
    for g in range(N_GROUPS):
        cols += [w_in[:, off_k + g * AW:off_k + (g + 1) * AW], w_in[:, off_v + g * AW:off_v + (g + 1) * AW]]
    cols += [w_in[:, off_xm:off_xm + 3 * M], w_in[:, off_ga:off_ga + 2 * D]]
    pad = jnp.zeros((D, LANES - M_HEADS), w_in.dtype)
    cols += [w_in[:, off_i:off_i + M_HEADS], pad, w_in[:, off_f:off_f + M_HEADS], pad]
    wp = jnp.concatenate(cols, axis=1).astype(BF16)
    segs = {}
    c = 0
    for name, wd in (("q0", AW), ("q1", AW), ("q2", AW), ("za", AW),
                     ("kv0", 2 * AW), ("kv1", 2 * AW), ("kv2", 2 * AW),
                     ("xm", M), ("zm", M), ("om", M), ("ga", D), ("gm", D), ("gates", 2 * LANES)):
        segs[name] = (c, wd)
        c += wd
    return wp, segs


def kernel(x_prompt, x_sample, cache_kv_w128, cache_kv_w512, cache_kv_w2048, state_conv, state_C, state_n, state_m, c_prompt, c_sample, rel_table, norm_gain, w_ada, b_ada, w_in, b_if, conv_w, conv_b, w_mq, w_mk, m_norm, m_skip, w_pa, w_pm, w_out, final_gain):
    B, S, D = x_prompt.shape
    DB, T, _ = x_sample.shape
    assert norm_gain.shape[0] == 1, "single-layer trunk"
    assert S % ATT_TILE == 0 and S % MLSTM_CHUNK == 0 and T == SUBLANES
    caches = (cache_kv_w128[0], cache_kv_w512[0], cache_kv_w2048[0])
    H = M_HEADS
    M = conv_w.shape[2]

    wp, segs = _permuted_weights(w_in[0])
    names = ("q0", "q1", "q2", "za", "kv0", "kv1", "kv2", "xm", "zm", "om", "ga", "gm", "gates")
    seg_list = [segs[n] for n in names]
    gain = norm_gain[0].reshape(1, D)
    fgain = final_gain.reshape(1, D)
    wpa, wpm, wout = w_pa[0].astype(BF16), w_pm[0].astype(BF16), w_out[0].astype(BF16)
    wq, wk = w_mq[0].astype(BF16), w_mk[0].astype(BF16)
    convw, convb = conv_w[0], conv_b[0].reshape(1, M)
    mnorm, mskip = m_norm[0].reshape(1, M), m_skip[0].reshape(1, M)
    zpad = jnp.zeros((LANES - H,), F32)
    bif = jnp.concatenate([b_if[0, :H], zpad, b_if[0, H:], zpad]).reshape(1, 2 * LANES)

    ada = _ada_call(jnp.concatenate([c_prompt, c_sample], axis=0), w_ada[0], b_ada[0])
    shift, scale, gate = ada[:, :D], ada[:, D:2 * D], ada[:, 2 * D:]

    p_shift, p_scale, p_gate = (t[:B].reshape(B, 1, D) for t in (shift, scale, gate))
    dts = [BF16] * 12 + [F32]
    pr = dict(zip(names, _proj_call(x_prompt, gain, p_scale, p_shift, wp, seg_list, dts,
                                    tm=PROJ_TM, row0=0, rows=S, name="proj_prompt")))
    att = []
    for g, (win, dil) in enumerate(ATT_GROUPS):
        att.append(_attn_call(pr[f"q{g}"], pr[f"kv{g}"], _prompt_bias(rel_table, g, dil), dil,
                              f"attn_prompt_g{g}"))
    mo_p, C_p, n_p, m_p = _mlstm_call(
        pr["xm"], pr["zm"], pr["om"], pr["gates"],
        jnp.zeros((B, SUBLANES, M), F32), jnp.zeros((B,) + state_C.shape[2:], F32),
        jnp.zeros((B,) + state_n.shape[2:], F32), jnp.zeros((B, 1, LANES), F32),
        convw, convb, wq, wk, mnorm, mskip, bif, L=MLSTM_CHUNK, out_dtype=BF16, name="mlstm_prompt")
    y_prompt = _post_call(x_prompt, p_gate, [a[0] for a in att] + [a[1] for a in att],
                          pr["za"], mo_p, pr["ga"], pr["gm"], wpa, wpm, wout, fgain,
                          tm=POST_TM, merge=True, name="post_prompt")
    kv_p = []
    w_max = min(ATT_GROUPS[-1][0], S)
    (kv2_t,) = _proj_call(x_prompt, gain, p_scale, p_shift, wp, [segs["kv2"]], [F32],
                          tm=PROJ_TM, row0=S - w_max, rows=w_max, name="tail_kv2")
    w_mid = min(ATT_GROUPS[1][0], S)
    kv0_t, kv1_t, xm_t = _proj_call(x_prompt, gain, p_scale, p_shift, wp,
                                    [segs["kv0"], segs["kv1"], segs["xm"]], [F32] * 3,
                                    tm=w_mid, row0=S - w_mid, rows=w_mid, name="tail_kv01")
    kshape = lambda L: (1, B, L, 2, ATT_HEADS, ATT_HEAD_DIM)
    w0 = min(ATT_GROUPS[0][0], S)
    kv_p = [kv0_t[:, w_mid - w0:].reshape(kshape(w0)), kv1_t.reshape(kshape(w_mid)), kv2_t.reshape(kshape(w_max))]
    conv_p = xm_t[:, w_mid - (CONV_WIDTH - 1):][None]

    R = DB * T
    rep = lambda t: jnp.repeat(t[B:], T, axis=0).reshape(1, R, D)
    s_shift, s_scale, s_gate = rep(shift), rep(scale), rep(gate)
    xs = x_sample.reshape(1, R, D)
    sr = dict(zip(names, _proj_call(xs, gain, s_scale, s_shift, wp, seg_list, [F32] * 13,
                                    tm=R, row0=0, rows=R, name="proj_sample")))
    bcs, bns = [], []
    for g, (win, dil) in enumerate(ATT_GROUPS):
        bc, bn = _sample_bias(rel_table, g, win, dil, caches[g].shape[1], T)
        bcs.append(bc)
        bns.append(bn)
    cache2d = [c.reshape(DB, c.shape[1], 2 * ATT_WIDTH) for c in caches]
    sa = _sattn_call([sr[f"q{g}"].reshape(DB, T, ATT_WIDTH) for g in range(3)],
                     [sr[f"kv{g}"].reshape(DB, T, 2 * ATT_WIDTH) for g in range(3)],
                     cache2d, bcs, bns, T)
    o_att_s = sa[0].reshape(1, R, ATT_WIDTH)
    kv_s = [c.reshape((1,) + caches[g].shape) for g, c in enumerate(sa[1:])]
    conv_prev_s = jnp.concatenate([jnp.zeros((DB, SUBLANES - (CONV_WIDTH - 1), M), F32), state_conv[0]], axis=1)
    m0_s = jnp.concatenate([state_m[0], jnp.zeros((DB, LANES - H), F32)], axis=1).reshape(DB, 1, LANES)
    seqv = lambda t: t.reshape(DB, T, t.shape[-1])
    mo_s, C_s, n_s, m_s = _mlstm_call(
        seqv(sr["xm"]), seqv(sr["zm"]), seqv(sr["om"]), seqv(sr["gates"]),
        conv_prev_s, state_C[0], state_n[0], m0_s,
        convw, convb, wq, wk, mnorm, mskip, bif, L=T, out_dtype=F32, name="mlstm_sample")
    y_sample = _post_call(xs, s_gate, [o_att_s], sr["za"], mo_s.reshape(1, R, M), sr["ga"], sr["gm"],
                          wpa, wpm, wout, fgain, tm=R, merge=False, name="post_sample")
    conv_s = seqv(sr["xm"])[:, T - (CONV_WIDTH - 1):][None]

    return (y_prompt, y_sample.reshape(DB, T, D),
            kv_p[0], kv_s[0], kv_p[1], kv_s[1], kv_p[2], kv_s[2],
            conv_p, conv_s, C_p[None], C_s[None], n_p[None], n_s[None],
            m_p[:, 0, :H][None], m_s[:, 0, :H][None])
```

```python
import functools

import numpy as np
import jax
import jax.numpy as jnp
from jax import lax
from jax.experimental import pallas as pl
from jax.experimental.pallas import tpu as pltpu

F32 = jnp.float32
BF16 = jnp.bfloat16

ATT_GROUPS = ((128, 1), (512, 4), (2048, 16))
N_GROUPS = len(ATT_GROUPS)
ATT_HEADS = 8
ATT_HEAD_DIM = 64
ATT_WIDTH = ATT_HEADS * ATT_HEAD_DIM
WIN_STEPS = 128
ATT_SCALE = ATT_HEAD_DIM ** -0.5
N_BUCKETS = 32
MAX_DISTANCE = 2048
M_HEADS = 4
CONV_WIDTH = 4
EPS = 1e-6
NEG = -1e30

LANES = 128
SUBLANES = 8
VMEM_LIMIT = 56 * 1024 * 1024

ATT_TILE = 2048
PROJ_TM = 512
POST_TM = 512
MLSTM_CHUNK = 256


def _params(sem, vmem=VMEM_LIMIT):
    return pltpu.CompilerParams(dimension_semantics=sem, vmem_limit_bytes=vmem)


def _t5_bucket(dist):
    n = np.asarray(dist).astype(np.int64)
    max_exact = N_BUCKETS // 2
    nf = np.maximum(n, 1).astype(np.float32)
    large = max_exact + (np.log(nf / max_exact) / np.log(np.float32(MAX_DISTANCE / max_exact))
                         * (N_BUCKETS - max_exact)).astype(np.int64)
    large = np.minimum(large, N_BUCKETS - 1)
    return np.where(n < max_exact, n, large).astype(np.int32)


def _silu(x):
    return x * jax.nn.sigmoid(x)


def _ada_kernel(c_ref, w_ref, b_ref, o_ref):
    s = _silu(c_ref[...])
    o_ref[...] = jnp.dot(s, w_ref[...], preferred_element_type=F32,
                         precision=lax.Precision.HIGHEST) + b_ref[...]


def _ada_call(c, w, b):
    n, d = c.shape
    width = w.shape[1]
    tn = 512
    return pl.pallas_call(
        _ada_kernel,
        grid=(width // tn,),
        in_specs=[pl.BlockSpec((n, d), lambda j: (0, 0)),
                  pl.BlockSpec((d, tn), lambda j: (0, j)),
                  pl.BlockSpec((1, tn), lambda j: (0, j))],
        out_specs=pl.BlockSpec((n, tn), lambda j: (0, j)),
        out_shape=jax.ShapeDtypeStruct((n, width), F32),
        compiler_params=_params(("arbitrary",)),
        name="ada",
    )(c, w, b.reshape(1, width))


def _proj_kernel(x_ref, gain_ref, scale_ref, shift_ref, w_ref, *out_refs, segs):
    x = x_ref[0]
    ms = jnp.mean(x * x, axis=-1, keepdims=True)
    h = x * lax.rsqrt(ms + EPS) * gain_ref[...] * (1.0 + scale_ref[0]) + shift_ref[0]
    hb = h.astype(BF16)
    for o_ref, (c0, width) in zip(out_refs, segs):
        o_ref[0] = jnp.dot(hb, w_ref[:, c0:c0 + width],
                           preferred_element_type=F32).astype(o_ref.dtype)


def _proj_call(x3, gain, scale3, shift3, w, segs, dtypes, *, tm, row0, rows, name):
    B, S, D = x3.shape
    nrb = rows // tm
    rb0 = row0 // tm
    per_row = scale3.shape[1] != 1
    if per_row:
        mod_spec = pl.BlockSpec((1, tm, D), lambda b, i: (b, rb0 + i, 0))
    else:
        mod_spec = pl.BlockSpec((1, 1, D), lambda b, i: (b, 0, 0))
    out_shape = [jax.ShapeDtypeStruct((B, rows, wd), dt) for (_, wd), dt in zip(segs, dtypes)]
    out_specs = [pl.BlockSpec((1, tm, wd), lambda b, i: (b, i, 0)) for (_, wd) in segs]
    return pl.pallas_call(
        functools.partial(_proj_kernel, segs=tuple(segs)),
        grid=(B, nrb),
        in_specs=[pl.BlockSpec((1, tm, D), lambda b, i: (b, rb0 + i, 0)),
                  pl.BlockSpec((1, D), lambda b, i: (0, 0)),
                  mod_spec, mod_spec,
                  pl.BlockSpec(w.shape, lambda b, i: (0, 0), pipeline_mode=pl.Buffered(1))],
        out_specs=out_specs,
        out_shape=out_shape,
        compiler_params=_params(("arbitrary", "arbitrary")),
        name=name,
    )(x3, gain, scale3, shift3, w)


HEADS_PER_SLAB = LANES // ATT_HEAD_DIM
N_SLABS = ATT_HEADS // HEADS_PER_SLAB


def _attn_unit(q, kv, bias_ref, prev_mask, o_ref, l_ref, rows):
    dn = (((1,), (1,)), ((), ()))
    nk = 2 * WIN_STEPS
    lane_q = lax.broadcasted_iota(jnp.int32, (WIN_STEPS, LANES), 1) < ATT_HEAD_DIM
    lane_k = lax.broadcasted_iota(jnp.int32, (nk, LANES), 1) < ATT_HEAD_DIM
    ones_lo = jnp.where(lane_k, 1.0, 0.0).astype(BF16)
    ones_hi = jnp.where(lane_k, 0.0, 1.0).astype(BF16)
    zero_q = jnp.zeros((WIN_STEPS, LANES), BF16)
    zero_k = jnp.zeros((nk, LANES), BF16)

    def scores(m):
        cs = slice(m * LANES, (m + 1) * LANES)
        qs = q[:, cs]
        ks = kv[:, cs]
        out = []
        for hh in range(HEADS_PER_SLAB):
            qm = jnp.where(lane_q, qs, zero_q) if hh == 0 else jnp.where(lane_q, zero_q, qs)
            s = lax.dot_general(qm, ks, dn, preferred_element_type=F32) + bias_ref[m * HEADS_PER_SLAB + hh]
            if prev_mask is not None:
                s = jnp.concatenate([s[:, :WIN_STEPS] + prev_mask, s[:, WIN_STEPS:]], axis=1)
            out.append(s)
        return out

    def finish(m, ss):
        cs = slice(m * LANES, (m + 1) * LANES)
        vs = kv[:, ATT_WIDTH + m * LANES:ATT_WIDTH + (m + 1) * LANES]
        ps, mxs = [], []
        for s in ss:
            mx = jnp.max(jnp.maximum(s[:, :WIN_STEPS], s[:, WIN_STEPS:]), -1, keepdims=True)
            ps.append(jnp.exp(s - mx).astype(BF16))
            mxs.append(mx)
        pcat = jnp.concatenate(ps, axis=1)
        vpair = jnp.concatenate(
            [jnp.concatenate([jnp.where(lane_k, vs, zero_k), ones_lo], axis=1),
             jnp.concatenate([jnp.where(lane_k, zero_k, vs), ones_hi], axis=1)], axis=0)
        acc = jnp.dot(pcat, vpair, preferred_element_type=F32)
        den = acc[:, LANES:]
        o_ref[0, rows, cs] = (acc[:, :LANES] / den).astype(o_ref.dtype)
        l_ref[0, rows, cs] = jnp.where(lane_q, mxs[0], mxs[1]) + jnp.log(den)

    pending = scores(0)
    for m in range(N_SLABS):
        nxt = scores(m + 1) if m + 1 < N_SLABS else None
        finish(m, pending)
        pending = nxt


def _attn_kernel(q_ref, kvc_ref, kvp_ref, bias_ref, o_ref, l_ref, *, ns):
    prev_mask = jnp.where(pl.program_id(1) == 0, NEG, 0.0).astype(F32)
    r0 = slice(0, WIN_STEPS)
    q0 = q_ref[0, r0, :] * ATT_SCALE
    kv0 = jnp.concatenate([kvp_ref[0], kvc_ref[0, r0, :]], axis=0)
    _attn_unit(q0, kv0, bias_ref, prev_mask, o_ref, l_ref, r0)

    def body(j, carry):
        rc = pl.ds(pl.multiple_of(j * WIN_STEPS, WIN_STEPS), WIN_STEPS)
        rk = pl.ds(pl.multiple_of((j - 1) * WIN_STEPS, WIN_STEPS), 2 * WIN_STEPS)
        q = q_ref[0, rc, :] * ATT_SCALE
        _attn_unit(q, kvc_ref[0, rk, :], bias_ref, None, o_ref, l_ref, rc)
        return carry

    if ns > 1:
        lax.fori_loop(1, ns, body, 0)


def _attn_call(q, kv, bias, d, name):
    B, S, _ = q.shape
    U = S // d
    ns = ATT_TILE // (WIN_STEPS * d)
    rows = ns * WIN_STEPS
    qv = q.reshape(B, U, d * ATT_WIDTH)
    kvv = kv.reshape(B, U, d * 2 * ATT_WIDTH)
    o, l = pl.pallas_call(
        functools.partial(_attn_kernel, ns=ns),
        grid=(B, S // ATT_TILE, d),
        in_specs=[pl.BlockSpec((1, rows, ATT_WIDTH), lambda b, t, r: (b, t, r)),
                  pl.BlockSpec((1, rows, 2 * ATT_WIDTH), lambda b, t, r: (b, t, r)),
                  pl.BlockSpec((1, WIN_STEPS, 2 * ATT_WIDTH),
                               lambda b, t, r: (b, jnp.maximum(t * ns - 1, 0), r)),
                  pl.BlockSpec(bias.shape, lambda b, t, r: (0, 0, 0))],
        out_specs=[pl.BlockSpec((1, rows, ATT_WIDTH), lambda b, t, r: (b, t, r)),
                   pl.BlockSpec((1, rows, ATT_WIDTH), lambda b, t, r: (b, t, r))],
        out_shape=[jax.ShapeDtypeStruct((B, U, d * ATT_WIDTH), BF16),
                   jax.ShapeDtypeStruct((B, U, d * ATT_WIDTH), F32)],
        compiler_params=_params(("arbitrary", "arbitrary", "arbitrary")),
        name=name,
    )(qv, kvv, kvv, bias)
    return o.reshape(B, S, ATT_WIDTH), l.reshape(B, S, ATT_WIDTH)


def _stride_bias(rel_table, g, d):
    bucket = _t5_bucket(np.arange(WIN_STEPS + 1) * d)
    onehot = jnp.asarray(np.eye(N_BUCKETS, dtype=np.float32)[bucket])
    tbl = rel_table[:, g * ATT_HEADS:(g + 1) * ATT_HEADS].astype(F32)
    return jnp.dot(onehot, tbl, precision=lax.Precision.HIGHEST).T


def _prompt_bias(rel_table, g, d):
    vals = _stride_bias(rel_table, g, d)
    n = WIN_STEPS
    period = 3 * n
    wp = jnp.concatenate([jnp.full((ATT_HEADS, n - 1), NEG, F32), vals[:, ::-1],
                          jnp.full((ATT_HEADS, n), NEG, F32)], axis=1)
    flat = jnp.tile(wp, (1, n))[:, :n * (period - 1)]
    return flat.reshape(ATT_HEADS, n, period - 1)[:, :, n - 1:n - 1 + 2 * n]


SHIFT_ROWS = 64


def _sattn_kernel(*refs, T):
    q_refs = refs[0:3]
    kvn_refs = refs[3:6]
    cache_refs = refs[6:9]
    bc_refs = refs[9:12]
    bn_refs = refs[12:15]
    o_ref = refs[15]
    co_refs = refs[16:19]
    HT = ATT_HEADS * T
    dn = (((1,), (1,)), ((), ()))
    row_head = lax.broadcasted_iota(jnp.int32, (HT, ATT_WIDTH), 0) // T
    col_head = lax.broadcasted_iota(jnp.int32, (HT, ATT_WIDTH), 1) // ATT_HEAD_DIM
    head_mask = row_head == col_head

    stats = []
    for g in range(N_GROUPS):
        q = q_refs[g][0] * ATT_SCALE
        qexp = jnp.where(head_mask, jnp.concatenate([q] * ATT_HEADS, axis=0), 0.0).astype(BF16)
        kn = kvn_refs[g][0, :ATT_WIDTH, :].astype(BF16)
        kc = cache_refs[g][0, :ATT_WIDTH, :].astype(BF16)
        lc = jnp.dot(qexp, kc, preferred_element_type=F32) + bc_refs[g][...]
        ln = jnp.dot(qexp, kn, preferred_element_type=F32) + bn_refs[g][...]
        mx = jnp.maximum(jnp.max(lc, -1, keepdims=True), jnp.max(ln, -1, keepdims=True))
        pc = jnp.exp(lc - mx)
        pn = jnp.exp(ln - mx)
        ssum = jnp.sum(pc, -1, keepdims=True) + jnp.sum(pn, -1, keepdims=True)
        stats.append((pc, pn, ssum, mx + jnp.log(ssum)))

    lse_max = jnp.maximum(jnp.maximum(stats[0][3], stats[1][3]), stats[2][3])
    es = [jnp.exp(st[3] - lse_max) for st in stats]
    esum = es[0] + es[1] + es[2]
    acc = jnp.zeros((HT, ATT_WIDTH), F32)
    for g in range(N_GROUPS):
        pc, pn, ssum, _ = stats[g]
        w = es[g] / (esum * ssum)
        vc = cache_refs[g][0, ATT_WIDTH:, :].astype(BF16)
        vn = kvn_refs[g][0, ATT_WIDTH:, :].astype(BF16)
        acc = acc + lax.dot_general((pc * w).astype(BF16), vc, dn, preferred_element_type=F32)
        acc = acc + lax.dot_general((pn * w).astype(BF16), vn, dn, preferred_element_type=F32)
    lane_head = lax.broadcasted_iota(jnp.int32, (T, ATT_WIDTH), 1) // ATT_HEAD_DIM
    o = jnp.zeros((T, ATT_WIDTH), F32)
    for h in range(ATT_HEADS):
        o = o + jnp.where(lane_head == h, acc[h * T:(h + 1) * T, :], 0.0)
    o_ref[0] = o

    for g in range(N_GROUPS):
        L = cache_refs[g].shape[2]
        nrows = cache_refs[g].shape[1]
        is_new = lax.broadcasted_iota(jnp.int32, (SHIFT_ROWS, LANES), 1) >= LANES - T

        def shift_rows(i, carry, g=g, L=L, is_new=is_new):
            rs = pl.ds(pl.multiple_of(i * SHIFT_ROWS, SHIFT_ROWS), SHIFT_ROWS)
            rolled = pltpu.roll(cache_refs[g][0, rs, :], L - T, axis=1)
            if L > LANES:
                co_refs[g][0, rs, 0:L - LANES] = rolled[:, 0:L - LANES]
            co_refs[g][0, rs, L - LANES:L] = jnp.where(is_new, kvn_refs[g][0, rs, :], rolled[:, L - LANES:])
            return carry

        lax.fori_loop(0, nrows // SHIFT_ROWS, shift_rows, 0)


def _sample_bias(rel_table, g, W, d, Lb, T):
    vals = _stride_bias(rel_table, g, d)
    n = Lb + T
    by_dist = jnp.concatenate([vals[:, :, None], jnp.full((ATT_HEADS, WIN_STEPS + 1, d - 1), NEG, F32)],
                              axis=2).reshape(ATT_HEADS, (WIN_STEPS + 1) * d)
    if by_dist.shape[1] < n:
        by_dist = jnp.concatenate([by_dist, jnp.full((ATT_HEADS, n - by_dist.shape[1]), NEG, F32)], axis=1)
    rev = jnp.concatenate([by_dist[:, :n][:, ::-1], jnp.full((ATT_HEADS, T), NEG, F32)], axis=1)
    rows = jnp.stack([rev[:, T - 1 - t:T - 1 - t + n] for t in range(T)], axis=1)
    rows = rows.reshape(ATT_HEADS * T, n)
    bc = rows[:, :Lb]
    bn = jnp.concatenate([jnp.full((ATT_HEADS * T, LANES - T), NEG, F32), rows[:, Lb:]], axis=1)
    return bc, bn


def _sattn_call(qs, kvns, caches, bcs, bns, T):
    DB = caches[0].shape[0]
    in_specs = ([pl.BlockSpec((1, T, ATT_WIDTH), lambda b: (b, 0, 0))] * 3
                + [pl.BlockSpec((1,) + x.shape[1:], lambda b: (b, 0, 0)) for x in kvns]
                + [pl.BlockSpec((1,) + c.shape[1:], lambda b: (b, 0, 0)) for c in caches]
                + [pl.BlockSpec(x.shape, lambda b: (0, 0)) for x in bcs]
                + [pl.BlockSpec(x.shape, lambda b: (0, 0)) for x in bns])
    out_specs = ([pl.BlockSpec((1, T, ATT_WIDTH), lambda b: (b, 0, 0))]
                 + [pl.BlockSpec((1,) + c.shape[1:], lambda b: (b, 0, 0)) for c in caches])
    out_shape = ([jax.ShapeDtypeStruct((DB, T, ATT_WIDTH), F32)]
                 + [jax.ShapeDtypeStruct(c.shape, F32) for c in caches])
    return pl.pallas_call(
        functools.partial(_sattn_kernel, T=T),
        grid=(DB,),
        in_specs=in_specs, out_specs=out_specs, out_shape=out_shape,
        compiler_params=_params(("arbitrary",)),
        name="sample_attn",
    )(*qs, *kvns, *caches, *bcs, *bns)


def _scan_rows(x, op, fill):
    n = x.shape[0]
    rowid = lax.broadcasted_iota(jnp.int32, x.shape, 0)
    s = 1
    while s < n:
        shifted = pltpu.roll(x, s, axis=0)
        x = op(x, jnp.where(rowid >= s, shifted, fill))
        s *= 2
    return x


def _pad_rows(x, n, fill=0.0):
    if x.shape[0] == n:
        return x
    return jnp.concatenate([x, jnp.full((n - x.shape[0],) + x.shape[1:], fill, x.dtype)], axis=0)


def _mlstm_kernel(xm_ref, zm_ref, om_ref, g_ref, cprev_ref, C0_ref, n0_ref, m0_ref,
                  convw_ref, convb_ref, wq_ref, wk_ref, mnorm_ref, mskip_ref, bif_ref,
                  mo_ref, C_ref, n_ref, m_ref, xc_ref, *, L):
    LS = max(L, LANES)
    DV = C_ref.shape[2]
    DK = C_ref.shape[3]
    c = pl.program_id(1)

    @pl.when(c == 0)
    def _():
        xc_ref[0:SUBLANES, :] = cprev_ref[0]
        C_ref[...] = C0_ref[...]
        n_ref[...] = n0_ref[...]
        m_ref[...] = m0_ref[...]

    xm_b = xm_ref[0].astype(BF16)
    xc_ref[SUBLANES:SUBLANES + L, :] = xm_ref[0].astype(F32)
    conv = convb_ref[...]
    for j in range(CONV_WIDTH):
        off = SUBLANES - (CONV_WIDTH - 1) + j
        conv = conv + convw_ref[j:j + 1, :] * xc_ref[off:off + L, :]
    xc_ref[0:SUBLANES, :] = xc_ref[L:L + SUBLANES, :]
    c_act = _silu(conv)
    cb = c_act.astype(BF16)

    gates = g_ref[0] + bif_ref[...]
    i_pre = gates[:, :LANES]
    logf = jax.nn.log_sigmoid(gates[:, LANES:])
    b = _scan_rows(logf, jnp.add, 0.0)
    a = i_pre - b
    ca = _scan_rows(a, jnp.maximum, NEG)
    m_prev = m_ref[0]
    mm = jnp.maximum(ca, m_prev)
    u = -mm
    w_inter = jnp.exp(u + m_prev)
    emt = jnp.exp(-(b + mm))
    bL = b[L - 1:L, :]
    m_new = bL + jnp.maximum(m_prev, ca[L - 1:L, :])
    wk = jnp.exp(bL + a - m_new)
    wC = jnp.exp(bL + m_prev - m_new)
    m_ref[0] = m_new

    t_id = lax.broadcasted_iota(jnp.int32, (L, LS), 0)
    s_id = lax.broadcasted_iota(jnp.int32, (L, LS), 1)
    causal = s_id <= t_id
    eye = s_id == t_id
    dn_t = (((1,), (1,)), ((), ()))

    for h in range(M_HEADS):
        vs = slice(h * DV, (h + 1) * DV)
        ch = cb[:, vs]
        qh = (jnp.dot(ch, wq_ref[h], preferred_element_type=F32) * (DK ** -0.5)).astype(BF16)
        kh = jnp.dot(ch, wk_ref[h], preferred_element_type=F32).astype(BF16)
        kh_s = _pad_rows(kh, LS)
        vh_s = _pad_rows(xm_b[:, vs], LS)
        a_col = a[:, h:h + 1]
        a_row = jnp.sum(jnp.where(eye, a_col, 0.0), axis=0, keepdims=True)
        dm = jnp.where(causal, u[:, h:h + 1] + a_row, NEG)
        w_intra = jnp.exp(dm)
        sc = lax.dot_general(qh, kh_s, dn_t, preferred_element_type=F32) * w_intra
        C_h = C_ref[0, h]
        n_h = n_ref[0, h:h + 1, :]
        wi = w_inter[:, h:h + 1]
        num = (jnp.dot(sc.astype(BF16), vh_s, preferred_element_type=F32)
               + wi * lax.dot_general(qh, C_h.astype(BF16), dn_t, preferred_element_type=F32))
        den = (jnp.sum(sc, -1, keepdims=True)
               + wi * jnp.sum(qh.astype(F32) * n_h, -1, keepdims=True))
        hcell = num / jnp.maximum(jnp.abs(den), emt[:, h:h + 1])
        mu = jnp.mean(hcell, -1, keepdims=True)
        hc = hcell - mu
        hn = hc * lax.rsqrt(jnp.mean(hc * hc, -1, keepdims=True) + EPS)
        out = ((jax.nn.sigmoid(om_ref[0, :, vs].astype(F32)) * (hn * mnorm_ref[:, vs])
                + mskip_ref[:, vs] * c_act[:, vs]) * _silu(zm_ref[0, :, vs].astype(F32)))
        mo_ref[0, :, vs] = out.astype(mo_ref.dtype)
        wk_s = _pad_rows(wk[:, h:h + 1], LS)
        vw = (vh_s.astype(F32) * wk_s).astype(BF16)
        wc = wC[:, h:h + 1]
        C_ref[0, h] = wc * C_h + lax.dot_general(vw, kh_s, (((0,), (0,)), ((), ())),
                                                 preferred_element_type=F32)
        n_ref[0, h:h + 1, :] = wc * n_h + jnp.sum(kh_s.astype(F32) * wk_s, axis=0, keepdims=True)


def _mlstm_call(xm, zm, om, gates, conv_prev, C0, n0, m0, convw, convb, wq, wk, mnorm, mskip, bif,
                *, L, out_dtype, name):
    N, S, M = xm.shape
    nc = S // L
    H, DV, DK = C0.shape[1:]
    seq = lambda b, c: (b, c, 0)
    per_b3 = lambda b, c: (b, 0, 0)
    per_b4 = lambda b, c: (b, 0, 0, 0)
    const2 = lambda b, c: (0, 0)
    const3 = lambda b, c: (0, 0, 0)
    return pl.pallas_call(
        functools.partial(_mlstm_kernel, L=L),
        grid=(N, nc),
        in_specs=[pl.BlockSpec((1, L, M), seq), pl.BlockSpec((1, L, M), seq), pl.BlockSpec((1, L, M), seq),
                  pl.BlockSpec((1, L, 2 * LANES), seq),
                  pl.BlockSpec((1, SUBLANES, M), per_b3),
                  pl.BlockSpec((1, H, DV, DK), per_b4),
                  pl.BlockSpec((1, H, DK), per_b3),
                  pl.BlockSpec((1, 1, LANES), per_b3),
                  pl.BlockSpec(convw.shape, const2), pl.BlockSpec(convb.shape, const2),
                  pl.BlockSpec(wq.shape, const3), pl.BlockSpec(wk.shape, const3),
                  pl.BlockSpec(mnorm.shape, const2), pl.BlockSpec(mskip.shape, const2),
                  pl.BlockSpec(bif.shape, const2)],
        out_specs=[pl.BlockSpec((1, L, M), seq),
                   pl.BlockSpec((1, H, DV, DK), per_b4),
                   pl.BlockSpec((1, H, DK), per_b3),
                   pl.BlockSpec((1, 1, LANES), per_b3)],
        out_shape=[jax.ShapeDtypeStruct((N, S, M), out_dtype),
                   jax.ShapeDtypeStruct((N, H, DV, DK), F32),
                   jax.ShapeDtypeStruct((N, H, DK), F32),
                   jax.ShapeDtypeStruct((N, 1, LANES), F32)],
        scratch_shapes=[pltpu.VMEM((L + 2 * SUBLANES, M), F32)],
        compiler_params=_params(("arbitrary", "arbitrary")),
        name=name,
    )(xm, zm, om, gates, conv_prev, C0, n0, m0, convw, convb, wq, wk, mnorm, mskip, bif)


def _post_kernel(*refs, merge):
    if merge:
        (x_ref, gate_ref, o0, o1, o2, l0, l1, l2, za_ref, mo_ref, ga_ref, gm_ref,
         wpa_ref, wpm_ref, wout_ref, fg_ref, y_ref) = refs
        lmax = jnp.maximum(jnp.maximum(l0[0], l1[0]), l2[0])
        e0 = jnp.exp(l0[0] - lmax)
        e1 = jnp.exp(l1[0] - lmax)
        e2 = jnp.exp(l2[0] - lmax)
        o_att = (e0 * o0[0].astype(F32) + e1 * o1[0].astype(F32) + e2 * o2[0].astype(F32)) / (e0 + e1 + e2)
    else:
        (x_ref, gate_ref, oa_ref, za_ref, mo_ref, ga_ref, gm_ref,
         wpa_ref, wpm_ref, wout_ref, fg_ref, y_ref) = refs
        o_att = oa_ref[0]
    a_in = (o_att * _silu(za_ref[0].astype(F32))).astype(BF16)
    a_br = jnp.dot(a_in, wpa_ref[...], preferred_element_type=F32)
    m_br = jnp.dot(mo_ref[0].astype(BF16), wpm_ref[...], preferred_element_type=F32)
    merged = (jax.nn.sigmoid(ga_ref[0].astype(F32)) * a_br
              + jax.nn.sigmoid(gm_ref[0].astype(F32)) * m_br)
    y = x_ref[0] + gate_ref[0] * jnp.dot(merged.astype(BF16), wout_ref[...], preferred_element_type=F32)
    ms = jnp.mean(y * y, axis=-1, keepdims=True)
    y_ref[0] = y * lax.rsqrt(ms + EPS) * fg_ref[...]


def _post_call(x3, gate3, att_inputs, za, mo, ga, gm, wpa, wpm, wout, fgain, *, tm, merge, name):
    B, S, D = x3.shape
    row = lambda b, i: (b, i, 0)
    const2 = lambda b, i: (0, 0)
    if gate3.shape[1] == 1:
        gate_spec = pl.BlockSpec((1, 1, D), lambda b, i: (b, 0, 0))
    else:
        gate_spec = pl.BlockSpec((1, tm, D), row)
    blk = lambda a: pl.BlockSpec((1, tm, a.shape[2]), row)
    in_specs = ([pl.BlockSpec((1, tm, D), row), gate_spec]
                + [blk(a) for a in att_inputs]
                + [blk(za), blk(mo), blk(ga), blk(gm),
                   pl.BlockSpec(wpa.shape, const2), pl.BlockSpec(wpm.shape, const2),
                   pl.BlockSpec(wout.shape, const2), pl.BlockSpec(fgain.shape, const2)])
    return pl.pallas_call(
        functools.partial(_post_kernel, merge=merge),
        grid=(B, S // tm),
        in_specs=in_specs,
        out_specs=pl.BlockSpec((1, tm, D), row),
        out_shape=jax.ShapeDtypeStruct((B, S, D), F32),
        compiler_params=_params(("arbitrary", "arbitrary")),
        name=name,
    )(x3, gate3, *att_inputs, za, mo, ga, gm, wpa, wpm, wout, fgain)


def _permuted_weights(w_in):
    D = w_in.shape[0]
    AW = ATT_WIDTH
    M = D
    off_q, off_k, off_v = 0, 3 * AW, 6 * AW
    off_za = 9 * AW
    off_xm = off_za + AW
    off_zm, off_om = off_xm + M, off_xm + 2 * M
    off_i = off_xm + 3 * M
    off_f = off_i + M_HEADS
    off_ga = off_f + M_HEADS
    off_gm = off_ga + D
    cols = [w_in[:, off_q:off_q + 3 * AW], w_in[:, off_za:off_za + AW]]
    for g in range(N_GROUPS):
        cols += [w_in[:, off_k + g * AW:off_k + (g + 1) * AW], w_in[:, off_v + g * AW:off_v + (g + 1) * AW]]
    cols += [w_in[:, off_xm:off_xm + 3 * M], w_in[:, off_ga:off_ga + 2 * D]]
    pad = jnp.zeros((D, LANES - M_HEADS), w_in.dtype)
    cols += [w_in[:, off_i:off_i + M_HEADS], pad, w_in[:, off_f:off_f + M_HEADS], pad]
    wp = jnp.concatenate(cols, axis=1).astype(BF16)
    segs = {}
    c = 0
    for name, wd in (("q0", AW), ("q1", AW), ("q2", AW), ("za", AW),
                     ("kv0", 2 * AW), ("kv1", 2 * AW), ("kv2", 2 * AW),
                     ("xm", M), ("zm", M), ("om", M), ("ga", D), ("gm", D), ("gates", 2 * LANES)):
        segs[name] = (c, wd)
        c += wd
    return wp, segs


def kernel(x_prompt, x_sample, cache_kv_w128, cache_kv_w512, cache_kv_w2048, state_conv, state_C, state_n, state_m, c_prompt, c_sample, rel_table, norm_gain, w_ada, b_ada, w_in, b_if, conv_w, conv_b, w_mq, w_mk, m_norm, m_skip, w_pa, w_pm, w_out, final_gain):
    B, S, D = x_prompt.shape
    DB, T, _ = x_sample.shape
    assert norm_gain.shape[0] == 1, "single-layer trunk"
    assert S % ATT_TILE == 0 and S % MLSTM_CHUNK == 0 and T == SUBLANES
    caches = (cache_kv_w128[0], cache_kv_w512[0], cache_kv_w2048[0])
    H = M_HEADS
    M = conv_w.shape[2]

    wp, segs = _permuted_weights(w_in[0])
    names = ("q0", "q1", "q2", "za", "kv0", "kv1", "kv2", "xm", "zm", "om", "ga", "gm", "gates")
    seg_list = [segs[n] for n in names]
    gain = norm_gain[0].reshape(1, D)
    fgain = final_gain.reshape(1, D)
    wpa, wpm, wout = w_pa[0].astype(BF16), w_pm[0].astype(BF16), w_out[0].astype(BF16)
    wq, wk = w_mq[0].astype(BF16), w_mk[0].astype(BF16)
    convw, convb = conv_w[0], conv_b[0].reshape(1, M)
    mnorm, mskip = m_norm[0].reshape(1, M), m_skip[0].reshape(1, M)
    zpad = jnp.zeros((LANES - H,), F32)
    bif = jnp.concatenate([b_if[0, :H], zpad, b_if[0, H:], zpad]).reshape(1, 2 * LANES)

    ada = _ada_call(jnp.concatenate([c_prompt, c_sample], axis=0), w_ada[0], b_ada[0])
    shift, scale, gate = ada[:, :D], ada[:, D:2 * D], ada[:, 2 * D:]

    p_shift, p_scale, p_gate = (t[:B].reshape(B, 1, D) for t in (shift, scale, gate))
    dts = [BF16] * 12 + [F32]
    pr = dict(zip(names, _proj_call(x_prompt, gain, p_scale, p_shift, wp, seg_list, dts,
                                    tm=PROJ_TM, row0=0, rows=S, name="proj_prompt")))
    att = []
    for g, (win, dil) in enumerate(ATT_GROUPS):
        att.append(_attn_call(pr[f"q{g}"], pr[f"kv{g}"], _prompt_bias(rel_table, g, dil), dil,
                              f"attn_prompt_g{g}"))
    mo_p, C_p, n_p, m_p = _mlstm_call(
        pr["xm"], pr["zm"], pr["om"], pr["gates"],
        jnp.zeros((B, SUBLANES, M), F32), jnp.zeros((B,) + state_C.shape[2:], F32),
        jnp.zeros((B,) + state_n.shape[2:], F32), jnp.zeros((B, 1, LANES), F32),
        convw, convb, wq, wk, mnorm, mskip, bif, L=MLSTM_CHUNK, out_dtype=BF16, name="mlstm_prompt")
    y_prompt = _post_call(x_prompt, p_gate, [a[0] for a in att] + [a[1] for a in att],
                          pr["za"], mo_p, pr["ga"], pr["gm"], wpa, wpm, wout, fgain,
                          tm=POST_TM, merge=True, name="post_prompt")
    kv_p = []
    w_max = min(ATT_GROUPS[-1][0], S)
    (kv2_t,) = _proj_call(x_prompt, gain, p_scale, p_shift, wp, [segs["kv2"]], [F32],
                          tm=PROJ_TM, row0=S - w_max, rows=w_max, name="tail_kv2")
    w_mid = min(ATT_GROUPS[1][0], S)
    kv0_t, kv1_t, xm_t = _proj_call(x_prompt, gain, p_scale, p_shift, wp,
                                    [segs["kv0"], segs["kv1"], segs["xm"]], [F32] * 3,
                                    tm=w_mid, row0=S - w_mid, rows=w_mid, name="tail_kv01")
    kshape = lambda L: (1, B, L, 2, ATT_HEADS, ATT_HEAD_DIM)
    w0 = min(ATT_GROUPS[0][0], S)
    kv_p = [kv0_t[:, w_mid - w0:].reshape(kshape(w0)), kv1_t.reshape(kshape(w_mid)), kv2_t.reshape(kshape(w_max))]
    conv_p = xm_t[:, w_mid - (CONV_WIDTH - 1):][None]

    R = DB * T
    rep = lambda t: jnp.repeat(t[B:], T, axis=0).reshape(1, R, D)
    s_shift, s_scale, s_gate = rep(shift), rep(scale), rep(gate)
    xs = x_sample.reshape(1, R, D)
    sr = dict(zip(names, _proj_call(xs, gain, s_scale, s_shift, wp, seg_list, [F32] * 13,
                                    tm=R, row0=0, rows=R, name="proj_sample")))
    bcs, bns = [], []
    for g, (win, dil) in enumerate(ATT_GROUPS):
        bc, bn = _sample_bias(rel_table, g, win, dil, caches[g].shape[1], T)
        bcs.append(bc)
        bns.append(bn)
    cache_t = [jnp.transpose(c, (0, 2, 3, 4, 1)).reshape(DB, 2 * ATT_WIDTH, c.shape[1]) for c in caches]
    kvn_t = [jnp.pad(jnp.transpose(sr[f"kv{g}"].reshape(DB, T, 2 * ATT_WIDTH), (0, 2, 1)),
                     ((0, 0), (0, 0), (LANES - T, 0))) for g in range(3)]
    sa = _sattn_call([sr[f"q{g}"].reshape(DB, T, ATT_WIDTH) for g in range(3)],
                     kvn_t, cache_t, bcs, bns, T)
    o_att_s = sa[0].reshape(1, R, ATT_WIDTH)
    kv_s = [jnp.transpose(c.reshape(DB, 2, ATT_HEADS, ATT_HEAD_DIM, c.shape[2]), (0, 4, 1, 2, 3))[None]
            for c in sa[1:]]
    conv_prev_s = jnp.concatenate([jnp.zeros((DB, SUBLANES - (CONV_WIDTH - 1), M), F32), state_conv[0]], axis=1)
    m0_s = jnp.concatenate([state_m[0], jnp.zeros((DB, LANES - H), F32)], axis=1).reshape(DB, 1, LANES)
    seqv = lambda t: t.reshape(DB, T, t.shape[-1])
    mo_s, C_s, n_s, m_s = _mlstm_call(
        seqv(sr["xm"]), seqv(sr["zm"]), seqv(sr["om"]), seqv(sr["gates"]),
        conv_prev_s, state_C[0], state_n[0], m0_s,
        convw, convb, wq, wk, mnorm, mskip, bif, L=T, out_dtype=F32, name="mlstm_sample")
    y_sample = _post_call(xs, s_gate, [o_att_s], sr["za"], mo_s.reshape(1, R, M), sr["ga"], sr["gm"],
                          wpa, wpm, wout, fgain, tm=R, merge=False, name="post_sample")
    conv_s = seqv(sr["xm"])[:, T - (CONV_WIDTH - 1):][None]

    return (y_prompt, y_sample.reshape(DB, T, D),
            kv_p[0], kv_s[0], kv_p[1], kv_s[1], kv_p[2], kv_s[2],
            conv_p, conv_s, C_p[None], C_s[None], n_p[None], n_s[None],
            m_p[:, 0, :H][None], m_s[:, 0, :H][None])
```

```python
import functools

import numpy as np
import jax
import jax.numpy as jnp
from jax import lax
from jax.experimental import pallas as pl
from jax.experimental.pallas import tpu as pltpu

F32 = jnp.float32
BF16 = jnp.bfloat16

ATT_GROUPS = ((128, 1), (512, 4), (2048, 16))
N_GROUPS = len(ATT_GROUPS)
ATT_HEADS = 8
ATT_HEAD_DIM = 64
ATT_WIDTH = ATT_HEADS * ATT_HEAD_DIM
WIN_STEPS = 128
ATT_SCALE = ATT_HEAD_DIM ** -0.5
N_BUCKETS = 32
MAX_DISTANCE = 2048
M_HEADS = 4
CONV_WIDTH = 4
EPS = 1e-6
NEG = -1e30

LANES = 128
SUBLANES = 8
VMEM_LIMIT = 56 * 1024 * 1024

ATT_TILE = 2048
PROJ_TM = 512
POST_TM = 512
MLSTM_CHUNK = 256


def _params(sem, vmem=VMEM_LIMIT):
    return pltpu.CompilerParams(dimension_semantics=sem, vmem_limit_bytes=vmem)


def _t5_bucket(dist):
    n = np.asarray(dist).astype(np.int64)
    max_exact = N_BUCKETS // 2
    nf = np.maximum(n, 1).astype(np.float32)
    large = max_exact + (np.log(nf / max_exact) / np.log(np.float32(MAX_DISTANCE / max_exact))
                         * (N_BUCKETS - max_exact)).astype(np.int64)
    large = np.minimum(large, N_BUCKETS - 1)
    return np.where(n < max_exact, n, large).astype(np.int32)


def _silu(x):
    return x * jax.nn.sigmoid(x)


def _ada_kernel(c_ref, w_ref, b_ref, o_ref):
    s = _silu(c_ref[...])
    o_ref[...] = jnp.dot(s, w_ref[...], preferred_element_type=F32,
                         precision=lax.Precision.HIGHEST) + b_ref[...]


def _ada_call(c, w, b):
    n, d = c.shape
    width = w.shape[1]
    tn = 512
    return pl.pallas_call(
        _ada_kernel,
        grid=(width // tn,),
        in_specs=[pl.BlockSpec((n, d), lambda j: (0, 0)),
                  pl.BlockSpec((d, tn), lambda j: (0, j)),
                  pl.BlockSpec((1, tn), lambda j: (0, j))],
        out_specs=pl.BlockSpec((n, tn), lambda j: (0, j)),
        out_shape=jax.ShapeDtypeStruct((n, width), F32),
        compiler_params=_params(("arbitrary",)),
        name="ada",
    )(c, w, b.reshape(1, width))


def _proj_kernel(x_ref, gain_ref, scale_ref, shift_ref, w_ref, *rest, segs, dils):
    out_refs = rest[:len(segs)]
    x = x_ref[0]
    tm, D = x.shape
    ms = jnp.mean(x * x, axis=-1, keepdims=True)
    h = x * lax.rsqrt(ms + EPS) * gain_ref[...] * (1.0 + scale_ref[0]) + shift_ref[0]
    lhs = {1: h.astype(BF16)}
    strides = sorted(set(dils) - {1})
    if strides:
        hs_ref = rest[len(segs)]
        n_tiles = D // LANES
        for s in range(n_tiles):
            hs_ref[s] = h[:, s * LANES:(s + 1) * LANES]
        for d in strides:
            n = tm // d
            lhs[d] = jnp.concatenate(
                [jnp.concatenate([hs_ref[s, pl.ds(r, n, stride=d), :] for r in range(d)], axis=0)
                 for s in range(n_tiles)], axis=1).astype(BF16)
    for o_ref, (c0, width), d in zip(out_refs, segs, dils):
        res = jnp.dot(lhs[d], w_ref[:, c0:c0 + width], preferred_element_type=F32).astype(o_ref.dtype)
        if d == 1:
            o_ref[0] = res
        else:
            n = tm // d
            for r in range(d):
                o_ref[0, :, r * width:(r + 1) * width] = res[r * n:(r + 1) * n, :]


def _proj_call(x3, gain, scale3, shift3, w, segs, dtypes, *, tm, row0, rows, name, dils=None):
    B, S, D = x3.shape
    nrb = rows // tm
    rb0 = row0 // tm
    dils = tuple(dils) if dils is not None else (1,) * len(segs)
    per_row = scale3.shape[1] != 1
    if per_row:
        mod_spec = pl.BlockSpec((1, tm, D), lambda b, i: (b, rb0 + i, 0))
    else:
        mod_spec = pl.BlockSpec((1, 1, D), lambda b, i: (b, 0, 0))
    out_shape = [jax.ShapeDtypeStruct((B, rows // d, d * wd), dt) for (_, wd), dt, d in zip(segs, dtypes, dils)]
    out_specs = [pl.BlockSpec((1, tm // d, d * wd), lambda b, i: (b, i, 0)) for (_, wd), d in zip(segs, dils)]
    scratch = [pltpu.VMEM((D // LANES, tm, LANES), F32)] if any(d > 1 for d in dils) else []
    return pl.pallas_call(
        functools.partial(_proj_kernel, segs=tuple(segs), dils=dils),
        grid=(B, nrb),
        in_specs=[pl.BlockSpec((1, tm, D), lambda b, i: (b, rb0 + i, 0)),
                  pl.BlockSpec((1, D), lambda b, i: (0, 0)),
                  mod_spec, mod_spec,
                  pl.BlockSpec(w.shape, lambda b, i: (0, 0), pipeline_mode=pl.Buffered(1))],
        out_specs=out_specs,
        out_shape=out_shape,
        scratch_shapes=scratch,
        compiler_params=_params(("arbitrary", "arbitrary")),
        name=name,
    )(x3, gain, scale3, shift3, w)


HEADS_PER_SLAB = LANES // ATT_HEAD_DIM
N_SLABS = ATT_HEADS // HEADS_PER_SLAB
LSE_LANES = LANES // ATT_HEADS


def _attn_unit(q, kv, bias_ref, prev_mask, o_ref, l_ref, rows):
    dn = (((1,), (1,)), ((), ()))
    nk = 2 * WIN_STEPS
    lane_q = lax.broadcasted_iota(jnp.int32, (WIN_STEPS, LANES), 1) < ATT_HEAD_DIM
    lane_k = lax.broadcasted_iota(jnp.int32, (nk, LANES), 1) < ATT_HEAD_DIM
    ones_lo = jnp.where(lane_k, 1.0, 0.0).astype(BF16)
    ones_hi = jnp.where(lane_k, 0.0, 1.0).astype(BF16)
    zero_q = jnp.zeros((WIN_STEPS, LANES), BF16)
    zero_k = jnp.zeros((nk, LANES), BF16)

    def scores(m):
        cs = slice(m * LANES, (m + 1) * LANES)
        qs = q[:, cs]
        ks = kv[:, cs]
        out = []
        for hh in range(HEADS_PER_SLAB):
            qm = jnp.where(lane_q, qs, zero_q) if hh == 0 else jnp.where(lane_q, zero_q, qs)
            s = lax.dot_general(qm, ks, dn, preferred_element_type=F32) + bias_ref[m * HEADS_PER_SLAB + hh]
            if prev_mask is not None:
                s = jnp.concatenate([s[:, :WIN_STEPS] + prev_mask, s[:, WIN_STEPS:]], axis=1)
            out.append(s)
        return out

    def finish(m, ss):
        cs = slice(m * LANES, (m + 1) * LANES)
        vs = kv[:, ATT_WIDTH + m * LANES:ATT_WIDTH + (m + 1) * LANES]
        ps, mxs = [], []
        for s in ss:
            mx = jnp.max(jnp.maximum(s[:, :WIN_STEPS], s[:, WIN_STEPS:]), -1, keepdims=True)
            ps.append(jnp.exp(s - mx).astype(BF16))
            mxs.append(mx)
        pcat = jnp.concatenate(ps, axis=1)
        vpair = jnp.concatenate(
            [jnp.concatenate([jnp.where(lane_k, vs, zero_k), ones_lo], axis=1),
             jnp.concatenate([jnp.where(lane_k, zero_k, vs), ones_hi], axis=1)], axis=0)
        acc = jnp.dot(pcat, vpair, preferred_element_type=F32)
        den = acc[:, LANES:]
        o_ref[0, rows, cs] = (acc[:, :LANES] / den).astype(o_ref.dtype)
        lse = jnp.where(lane_q, mxs[0], mxs[1]) + jnp.log(den)
        return pltpu.roll(lse, (LSE_LANES * HEADS_PER_SLAB * m - 48) % LANES, axis=1)

    lane = lax.broadcasted_iota(jnp.int32, (WIN_STEPS, LANES), 1)
    lse_c = None
    pending = scores(0)
    for m in range(N_SLABS):
        nxt = scores(m + 1) if m + 1 < N_SLABS else None
        part = finish(m, pending)
        lse_c = part if lse_c is None else jnp.where(lane >= LSE_LANES * HEADS_PER_SLAB * m, part, lse_c)
        pending = nxt
    l_ref[0, rows, :] = lse_c


def _attn_kernel(q_ref, kvc_ref, kvp_ref, bias_ref, o_ref, l_ref, *, ns):
    prev_mask = jnp.where(pl.program_id(1) == 0, NEG, 0.0).astype(F32)
    r0 = slice(0, WIN_STEPS)
    q0 = q_ref[0, r0, :] * ATT_SCALE
    kv0 = jnp.concatenate([kvp_ref[0], kvc_ref[0, r0, :]], axis=0)
    _attn_unit(q0, kv0, bias_ref, prev_mask, o_ref, l_ref, r0)

    def body(j, carry):
        rc = pl.ds(pl.multiple_of(j * WIN_STEPS, WIN_STEPS), WIN_STEPS)
        rk = pl.ds(pl.multiple_of((j - 1) * WIN_STEPS, WIN_STEPS), 2 * WIN_STEPS)
        q = q_ref[0, rc, :] * ATT_SCALE
        _attn_unit(q, kvc_ref[0, rk, :], bias_ref, None, o_ref, l_ref, rc)
        return carry

    if ns > 1:
        lax.fori_loop(1, ns, body, 0)


def _attn_call(qv, kvv, bias, d, name):
    B, U, _ = qv.shape
    S = U * d
    ns = ATT_TILE // (WIN_STEPS * d)
    rows = ns * WIN_STEPS
    o, l = pl.pallas_call(
        functools.partial(_attn_kernel, ns=ns),
        grid=(B, S // ATT_TILE, d),
        in_specs=[pl.BlockSpec((1, rows, ATT_WIDTH), lambda b, t, r: (b, t, r)),
                  pl.BlockSpec((1, rows, 2 * ATT_WIDTH), lambda b, t, r: (b, t, r)),
                  pl.BlockSpec((1, WIN_STEPS, 2 * ATT_WIDTH),
                               lambda b, t, r: (b, jnp.maximum(t * ns - 1, 0), r)),
                  pl.BlockSpec(bias.shape, lambda b, t, r: (0, 0, 0))],
        out_specs=[pl.BlockSpec((1, rows, ATT_WIDTH), lambda b, t, r: (b, t, r)),
                   pl.BlockSpec((1, rows, LANES), lambda b, t, r: (b, t, r))],
        out_shape=[jax.ShapeDtypeStruct((B, U, d * ATT_WIDTH), BF16),
                   jax.ShapeDtypeStruct((B, U, d * LANES), F32)],
        compiler_params=_params(("arbitrary", "arbitrary", "arbitrary")),
        name=name,
    )(qv, kvv, kvv, bias)
    return o.reshape(B, S, ATT_WIDTH), l.reshape(B, S, LANES)


def _stride_bias(rel_table, g, d):
    bucket = _t5_bucket(np.arange(WIN_STEPS + 1) * d)
    onehot = jnp.asarray(np.eye(N_BUCKETS, dtype=np.float32)[bucket])
    tbl = rel_table[:, g * ATT_HEADS:(g + 1) * ATT_HEADS].astype(F32)
    return jnp.dot(onehot, tbl, precision=lax.Precision.HIGHEST).T


def _prompt_bias(rel_table, g, d):
    vals = _stride_bias(rel_table, g, d)
    n = WIN_STEPS
    period = 3 * n
    wp = jnp.concatenate([jnp.full((ATT_HEADS, n - 1), NEG, F32), vals[:, ::-1],
                          jnp.full((ATT_HEADS, n), NEG, F32)], axis=1)
    flat = jnp.tile(wp, (1, n))[:, :n * (period - 1)]
    return flat.reshape(ATT_HEADS, n, period - 1)[:, :, n - 1:n - 1 + 2 * n]


SHIFT_ROWS = 64


def _sattn_kernel(*refs, T):
    q_refs = refs[0:3]
    kvn_refs = refs[3:6]
    cache_refs = refs[6:9]
    bc_refs = refs[9:12]
    bn_refs = refs[12:15]
    o_ref = refs[15]
    co_refs = refs[16:19]
    HT = ATT_HEADS * T
    dn = (((1,), (1,)), ((), ()))
    row_head = lax.broadcasted_iota(jnp.int32, (HT, ATT_WIDTH), 0) // T
    col_head = lax.broadcasted_iota(jnp.int32, (HT, ATT_WIDTH), 1) // ATT_HEAD_DIM
    head_mask = row_head == col_head

    stats = []
    for g in range(N_GROUPS):
        q = q_refs[g][0] * ATT_SCALE
        qexp = jnp.where(head_mask, jnp.concatenate([q] * ATT_HEADS, axis=0), 0.0).astype(BF16)
        kn = kvn_refs[g][0, :ATT_WIDTH, :].astype(BF16)
        kc = cache_refs[g][0, :ATT_WIDTH, :].astype(BF16)
        lc = jnp.dot(qexp, kc, preferred_element_type=F32) + bc_refs[g][...]
        ln = jnp.dot(qexp, kn, preferred_element_type=F32) + bn_refs[g][...]
        mx = jnp.maximum(jnp.max(lc, -1, keepdims=True), jnp.max(ln, -1, keepdims=True))
        pc = jnp.exp(lc - mx)
        pn = jnp.exp(ln - mx)
        ssum = jnp.sum(pc, -1, keepdims=True) + jnp.sum(pn, -1, keepdims=True)
        stats.append((pc, pn, ssum, mx + jnp.log(ssum)))

    lse_max = jnp.maximum(jnp.maximum(stats[0][3], stats[1][3]), stats[2][3])
    es = [jnp.exp(st[3] - lse_max) for st in stats]
    esum = es[0] + es[1] + es[2]
    acc = jnp.zeros((HT, ATT_WIDTH), F32)
    for g in range(N_GROUPS):
        pc, pn, ssum, _ = stats[g]
        w = es[g] / (esum * ssum)
        vc = cache_refs[g][0, ATT_WIDTH:, :].astype(BF16)
        vn = kvn_refs[g][0, ATT_WIDTH:, :].astype(BF16)
        acc = acc + lax.dot_general((pc * w).astype(BF16), vc, dn, preferred_element_type=F32)
        acc = acc + lax.dot_general((pn * w).astype(BF16), vn, dn, preferred_element_type=F32)
    lane_head = lax.broadcasted_iota(jnp.int32, (T, ATT_WIDTH), 1) // ATT_HEAD_DIM
    o = jnp.zeros((T, ATT_WIDTH), F32)
    for h in range(ATT_HEADS):
        o = o + jnp.where(lane_head == h, acc[h * T:(h + 1) * T, :], 0.0)
    o_ref[0] = o

    for g in range(N_GROUPS):
        L = cache_refs[g].shape[2]
        nrows = cache_refs[g].shape[1]
        is_new = lax.broadcasted_iota(jnp.int32, (SHIFT_ROWS, LANES), 1) >= LANES - T

        def shift_rows(i, carry, g=g, L=L, is_new=is_new):
            rs = pl.ds(pl.multiple_of(i * SHIFT_ROWS, SHIFT_ROWS), SHIFT_ROWS)
            rolled = pltpu.roll(cache_refs[g][0, rs, :], L - T, axis=1)
            if L > LANES:
                co_refs[g][0, rs, 0:L - LANES] = rolled[:, 0:L - LANES]
            co_refs[g][0, rs, L - LANES:L] = jnp.where(is_new, kvn_refs[g][0, rs, :], rolled[:, L - LANES:])
            return carry

        lax.fori_loop(0, nrows // SHIFT_ROWS, shift_rows, 0)


def _sample_bias(rel_table, g, W, d, Lb, T):
    vals = _stride_bias(rel_table, g, d)
    n = Lb + T
    by_dist = jnp.concatenate([vals[:, :, None], jnp.full((ATT_HEADS, WIN_STEPS + 1, d - 1), NEG, F32)],
                              axis=2).reshape(ATT_HEADS, (WIN_STEPS + 1) * d)
    if by_dist.shape[1] < n:
        by_dist = jnp.concatenate([by_dist, jnp.full((ATT_HEADS, n - by_dist.shape[1]), NEG, F32)], axis=1)
    rev = jnp.concatenate([by_dist[:, :n][:, ::-1], jnp.full((ATT_HEADS, T), NEG, F32)], axis=1)
    rows = jnp.stack([rev[:, T - 1 - t:T - 1 - t + n] for t in range(T)], axis=1)
    rows = rows.reshape(ATT_HEADS * T, n)
    bc = rows[:, :Lb]
    bn = jnp.concatenate([jnp.full((ATT_HEADS * T, LANES - T), NEG, F32), rows[:, Lb:]], axis=1)
    return bc, bn


def _sattn_call(qs, kvns, caches, bcs, bns, T):
    DB = caches[0].shape[0]
    in_specs = ([pl.BlockSpec((1, T, ATT_WIDTH), lambda b: (b, 0, 0))] * 3
                + [pl.BlockSpec((1,) + x.shape[1:], lambda b: (b, 0, 0)) for x in kvns]
                + [pl.BlockSpec((1,) + c.shape[1:], lambda b: (b, 0, 0)) for c in caches]
                + [pl.BlockSpec(x.shape, lambda b: (0, 0)) for x in bcs]
                + [pl.BlockSpec(x.shape, lambda b: (0, 0)) for x in bns])
    out_specs = ([pl.BlockSpec((1, T, ATT_WIDTH), lambda b: (b, 0, 0))]
                 + [pl.BlockSpec((1,) + c.shape[1:], lambda b: (b, 0, 0)) for c in caches])
    out_shape = ([jax.ShapeDtypeStruct((DB, T, ATT_WIDTH), F32)]
                 + [jax.ShapeDtypeStruct(c.shape, F32) for c in caches])
    return pl.pallas_call(
        functools.partial(_sattn_kernel, T=T),
        grid=(DB,),
        in_specs=in_specs, out_specs=out_specs, out_shape=out_shape,
        compiler_params=_params(("arbitrary",)),
        name="sample_attn",
    )(*qs, *kvns, *caches, *bcs, *bns)


def _scan_rows(x, op, fill):
    n = x.shape[0]
    rowid = lax.broadcasted_iota(jnp.int32, x.shape, 0)
    s = 1
    while s < n:
        shifted = pltpu.roll(x, s, axis=0)
        x = op(x, jnp.where(rowid >= s, shifted, fill))
        s *= 2
    return x


def _pad_rows(x, n, fill=0.0):
    if x.shape[0] == n:
        return x
    return jnp.concatenate([x, jnp.full((n - x.shape[0],) + x.shape[1:], fill, x.dtype)], axis=0)


def _mlstm_kernel(xm_ref, zm_ref, om_ref, g_ref, cprev_ref, C0_ref, n0_ref, m0_ref,
                  convw_ref, convb_ref, wq_ref, wk_ref, mnorm_ref, mskip_ref, bif_ref,
                  mo_ref, C_ref, n_ref, m_ref, xc_ref, *, L):
    LS = max(L, LANES)
    DV = C_ref.shape[2]
    DK = C_ref.shape[3]
    c = pl.program_id(1)

    @pl.when(c == 0)
    def _():
        xc_ref[0:SUBLANES, :] = cprev_ref[0]
        C_ref[...] = C0_ref[...]
        n_ref[...] = n0_ref[...]
        m_ref[...] = m0_ref[...]

    xm_b = xm_ref[0].astype(BF16)
    xc_ref[SUBLANES:SUBLANES + L, :] = xm_ref[0].astype(F32)
    conv = convb_ref[...]
    for j in range(CONV_WIDTH):
        off = SUBLANES - (CONV_WIDTH - 1) + j
        conv = conv + convw_ref[j:j + 1, :] * xc_ref[off:off + L, :]
    xc_ref[0:SUBLANES, :] = xc_ref[L:L + SUBLANES, :]
    c_act = _silu(conv)
    cb = c_act.astype(BF16)

    gates = g_ref[0] + bif_ref[...]
    i_pre = gates[:, :LANES]
    logf = jax.nn.log_sigmoid(gates[:, LANES:])
    b = _scan_rows(logf, jnp.add, 0.0)
    a = i_pre - b
    ca = _scan_rows(a, jnp.maximum, NEG)
    m_prev = m_ref[0]
    mm = jnp.maximum(ca, m_prev)
    u = -mm
    w_inter = jnp.exp(u + m_prev)
    emt = jnp.exp(-(b + mm))
    bL = b[L - 1:L, :]
    m_new = bL + jnp.maximum(m_prev, ca[L - 1:L, :])
    wk = jnp.exp(bL + a - m_new)
    wC = jnp.exp(bL + m_prev - m_new)
    m_ref[0] = m_new

    t_id = lax.broadcasted_iota(jnp.int32, (L, LS), 0)
    s_id = lax.broadcasted_iota(jnp.int32, (L, LS), 1)
    causal = s_id <= t_id
    eye = s_id == t_id
    dn_t = (((1,), (1,)), ((), ()))

    for h in range(M_HEADS):
        vs = slice(h * DV, (h + 1) * DV)
        ch = cb[:, vs]
        qh = (jnp.dot(ch, wq_ref[h], preferred_element_type=F32) * (DK ** -0.5)).astype(BF16)
        kh = jnp.dot(ch, wk_ref[h], preferred_element_type=F32).astype(BF16)
        kh_s = _pad_rows(kh, LS)
        vh_s = _pad_rows(xm_b[:, vs], LS)
        a_col = a[:, h:h + 1]
        a_row = jnp.sum(jnp.where(eye, a_col, 0.0), axis=0, keepdims=True)
        dm = jnp.where(causal, u[:, h:h + 1] + a_row, NEG)
        w_intra = jnp.exp(dm)
        sc = lax.dot_general(qh, kh_s, dn_t, preferred_element_type=F32) * w_intra
        C_h = C_ref[0, h]
        n_h = n_ref[0, h:h + 1, :]
        wi = w_inter[:, h:h + 1]
        num = (jnp.dot(sc.astype(BF16), vh_s, preferred_element_type=F32)
               + wi * lax.dot_general(qh, C_h.astype(BF16), dn_t, preferred_element_type=F32))
        den = (jnp.sum(sc, -1, keepdims=True)
               + wi * jnp.sum(qh.astype(F32) * n_h, -1, keepdims=True))
        hcell = num / jnp.maximum(jnp.abs(den), emt[:, h:h + 1])
        mu = jnp.mean(hcell, -1, keepdims=True)
        hc = hcell - mu
        hn = hc * lax.rsqrt(jnp.mean(hc * hc, -1, keepdims=True) + EPS)
        out = ((jax.nn.sigmoid(om_ref[0, :, vs].astype(F32)) * (hn * mnorm_ref[:, vs])
                + mskip_ref[:, vs] * c_act[:, vs]) * _silu(zm_ref[0, :, vs].astype(F32)))
        mo_ref[0, :, vs] = out.astype(mo_ref.dtype)
        wk_s = _pad_rows(wk[:, h:h + 1], LS)
        vw = (vh_s.astype(F32) * wk_s).astype(BF16)
        wc = wC[:, h:h + 1]
        C_ref[0, h] = wc * C_h + lax.dot_general(vw, kh_s, (((0,), (0,)), ((), ())),
                                                 preferred_element_type=F32)
        n_ref[0, h:h + 1, :] = wc * n_h + jnp.sum(kh_s.astype(F32) * wk_s, axis=0, keepdims=True)


def _mlstm_call(xm, zm, om, gates, conv_prev, C0, n0, m0, convw, convb, wq, wk, mnorm, mskip, bif,
                *, L, out_dtype, name):
    N, S, M = xm.shape
    nc = S // L
    H, DV, DK = C0.shape[1:]
    seq = lambda b, c: (b, c, 0)
    per_b3 = lambda b, c: (b, 0, 0)
    per_b4 = lambda b, c: (b, 0, 0, 0)
    const2 = lambda b, c: (0, 0)
    const3 = lambda b, c: (0, 0, 0)
    return pl.pallas_call(
        functools.partial(_mlstm_kernel, L=L),
        grid=(N, nc),
        in_specs=[pl.BlockSpec((1, L, M), seq), pl.BlockSpec((1, L, M), seq), pl.BlockSpec((1, L, M), seq),
                  pl.BlockSpec((1, L, 2 * LANES), seq),
                  pl.BlockSpec((1, SUBLANES, M), per_b3),
                  pl.BlockSpec((1, H, DV, DK), per_b4),
                  pl.BlockSpec((1, H, DK), per_b3),
                  pl.BlockSpec((1, 1, LANES), per_b3),
                  pl.BlockSpec(convw.shape, const2), pl.BlockSpec(convb.shape, const2),
                  pl.BlockSpec(wq.shape, const3), pl.BlockSpec(wk.shape, const3),
                  pl.BlockSpec(mnorm.shape, const2), pl.BlockSpec(mskip.shape, const2),
                  pl.BlockSpec(bif.shape, const2)],
        out_specs=[pl.BlockSpec((1, L, M), seq),
                   pl.BlockSpec((1, H, DV, DK), per_b4),
                   pl.BlockSpec((1, H, DK), per_b3),
                   pl.BlockSpec((1, 1, LANES), per_b3)],
        out_shape=[jax.ShapeDtypeStruct((N, S, M), out_dtype),
                   jax.ShapeDtypeStruct((N, H, DV, DK), F32),
                   jax.ShapeDtypeStruct((N, H, DK), F32),
                   jax.ShapeDtypeStruct((N, 1, LANES), F32)],
        scratch_shapes=[pltpu.VMEM((L + 2 * SUBLANES, M), F32)],
        compiler_params=_params(("arbitrary", "arbitrary")),
        name=name,
    )(xm, zm, om, gates, conv_prev, C0, n0, m0, convw, convb, wq, wk, mnorm, mskip, bif)


def _post_kernel(*refs, merge):
    if merge:
        (x_ref, gate_ref, o0, o1, o2, l0, l1, l2, expand_ref, za_ref, mo_ref, ga_ref, gm_ref,
         wpa_ref, wpm_ref, wout_ref, fg_ref, y_ref) = refs
        lmax = jnp.maximum(jnp.maximum(l0[0], l1[0]), l2[0])
        es = [jnp.exp(l[0] - lmax) for l in (l0, l1, l2)]
        inv = 1.0 / (es[0] + es[1] + es[2])
        o_att = None
        for e, o in zip(es, (o0, o1, o2)):
            a = e * inv
            hi = a.astype(BF16)
            lo = (a - hi.astype(F32)).astype(BF16)
            a_wide = jnp.dot(jnp.concatenate([hi, lo], axis=1), expand_ref[...],
                             preferred_element_type=F32)
            term = a_wide * o[0].astype(F32)
            o_att = term if o_att is None else o_att + term
    else:
        (x_ref, gate_ref, oa_ref, za_ref, mo_ref, ga_ref, gm_ref,
         wpa_ref, wpm_ref, wout_ref, fg_ref, y_ref) = refs
        o_att = oa_ref[0]
    a_in = (o_att * _silu(za_ref[0].astype(F32))).astype(BF16)
    a_br = jnp.dot(a_in, wpa_ref[...], preferred_element_type=F32)
    m_br = jnp.dot(mo_ref[0].astype(BF16), wpm_ref[...], preferred_element_type=F32)
    merged = (jax.nn.sigmoid(ga_ref[0].astype(F32)) * a_br
              + jax.nn.sigmoid(gm_ref[0].astype(F32)) * m_br)
    y = x_ref[0] + gate_ref[0] * jnp.dot(merged.astype(BF16), wout_ref[...], preferred_element_type=F32)
    ms = jnp.mean(y * y, axis=-1, keepdims=True)
    y_ref[0] = y * lax.rsqrt(ms + EPS) * fg_ref[...]


def _post_call(x3, gate3, att_inputs, za, mo, ga, gm, wpa, wpm, wout, fgain, *, tm, merge, name):
    B, S, D = x3.shape
    row = lambda b, i: (b, i, 0)
    const2 = lambda b, i: (0, 0)
    if gate3.shape[1] == 1:
        gate_spec = pl.BlockSpec((1, 1, D), lambda b, i: (b, 0, 0))
    else:
        gate_spec = pl.BlockSpec((1, tm, D), row)
    blk = lambda a: pl.BlockSpec((1, tm, a.shape[2]), row) if a.ndim == 3 else pl.BlockSpec(a.shape, const2)
    in_specs = ([pl.BlockSpec((1, tm, D), row), gate_spec]
                + [blk(a) for a in att_inputs]
                + [blk(za), blk(mo), blk(ga), blk(gm),
                   pl.BlockSpec(wpa.shape, const2), pl.BlockSpec(wpm.shape, const2),
                   pl.BlockSpec(wout.shape, const2), pl.BlockSpec(fgain.shape, const2)])
    return pl.pallas_call(
        functools.partial(_post_kernel, merge=merge),
        grid=(B, S // tm),
        in_specs=in_specs,
        out_specs=pl.BlockSpec((1, tm, D), row),
        out_shape=jax.ShapeDtypeStruct((B, S, D), F32),
        compiler_params=_params(("arbitrary", "arbitrary")),
        name=name,
    )(x3, gate3, *att_inputs, za, mo, ga, gm, wpa, wpm, wout, fgain)


def _permuted_weights(w_in):
    D = w_in.shape[0]
    AW = ATT_WIDTH
    M = D
    off_q, off_k, off_v = 0, 3 * AW, 6 * AW
    off_za = 9 * AW
    off_xm = off_za + AW
    off_zm, off_om = off_xm + M, off_xm + 2 * M
    off_i = off_xm + 3 * M
    off_f = off_i + M_HEADS
    off_ga = off_f + M_HEADS
    off_gm = off_ga + D
    cols = [w_in[:, off_q:off_q + 3 * AW], w_in[:, off_za:off_za + AW]]
    for g in range(N_GROUPS):
        cols += [w_in[:, off_k + g * AW:off_k + (g + 1) * AW], w_in[:, off_v + g * AW:off_v + (g + 1) * AW]]
    cols += [w_in[:, off_xm:off_xm + 3 * M], w_in[:, off_ga:off_ga + 2 * D]]
    pad = jnp.zeros((D, LANES - M_HEADS), w_in.dtype)
    cols += [w_in[:, off_i:off_i + M_HEADS], pad, w_in[:, off_f:off_f + M_HEADS], pad]
    wp = jnp.concatenate(cols, axis=1).astype(BF16)
    segs = {}
    c = 0
    for name, wd in (("q0", AW), ("q1", AW), ("q2", AW), ("za", AW),
                     ("kv0", 2 * AW), ("kv1", 2 * AW), ("kv2", 2 * AW),
                     ("xm", M), ("zm", M), ("om", M), ("ga", D), ("gm", D), ("gates", 2 * LANES)):
        segs[name] = (c, wd)
        c += wd
    return wp, segs


def kernel(x_prompt, x_sample, cache_kv_w128, cache_kv_w512, cache_kv_w2048, state_conv, state_C, state_n, state_m, c_prompt, c_sample, rel_table, norm_gain, w_ada, b_ada, w_in, b_if, conv_w, conv_b, w_mq, w_mk, m_norm, m_skip, w_pa, w_pm, w_out, final_gain):
    B, S, D = x_prompt.shape
    DB, T, _ = x_sample.shape
    assert norm_gain.shape[0] == 1, "single-layer trunk"
    assert S % ATT_TILE == 0 and S % MLSTM_CHUNK == 0 and T == SUBLANES
    caches = (cache_kv_w128[0], cache_kv_w512[0], cache_kv_w2048[0])
    H = M_HEADS
    M = conv_w.shape[2]

    wp, segs = _permuted_weights(w_in[0])
    names = ("q0", "q1", "q2", "za", "kv0", "kv1", "kv2", "xm", "zm", "om", "ga", "gm", "gates")
    seg_list = [segs[n] for n in names]
    gain = norm_gain[0].reshape(1, D)
    fgain = final_gain.reshape(1, D)
    wpa, wpm, wout = w_pa[0].astype(BF16), w_pm[0].astype(BF16), w_out[0].astype(BF16)
    wq, wk = w_mq[0].astype(BF16), w_mk[0].astype(BF16)
    convw, convb = conv_w[0], conv_b[0].reshape(1, M)
    mnorm, mskip = m_norm[0].reshape(1, M), m_skip[0].reshape(1, M)
    zpad = jnp.zeros((LANES - H,), F32)
    bif = jnp.concatenate([b_if[0, :H], zpad, b_if[0, H:], zpad]).reshape(1, 2 * LANES)

    ada = _ada_call(jnp.concatenate([c_prompt, c_sample], axis=0), w_ada[0], b_ada[0])
    shift, scale, gate = ada[:, :D], ada[:, D:2 * D], ada[:, 2 * D:]

    p_shift, p_scale, p_gate = (t[:B].reshape(B, 1, D) for t in (shift, scale, gate))
    dts = [BF16] * 12 + [F32]
    group_dil = {f"{p}{g}": dil for g, (_, dil) in enumerate(ATT_GROUPS) for p in ("q", "kv")}
    pr = dict(zip(names, _proj_call(x_prompt, gain, p_scale, p_shift, wp, seg_list, dts,
                                    tm=PROJ_TM, row0=0, rows=S, name="proj_prompt",
                                    dils=[group_dil.get(n, 1) for n in names])))
    att = []
    for g, (win, dil) in enumerate(ATT_GROUPS):
        att.append(_attn_call(pr[f"q{g}"], pr[f"kv{g}"], _prompt_bias(rel_table, g, dil), dil,
                              f"attn_prompt_g{g}"))
    expand = np.zeros((2 * LANES, ATT_WIDTH), np.float32)
    for h in range(ATT_HEADS):
        expand[[LSE_LANES * h, LANES + LSE_LANES * h], h * ATT_HEAD_DIM:(h + 1) * ATT_HEAD_DIM] = 1.0
    expand = jnp.asarray(expand, BF16)
    mo_p, C_p, n_p, m_p = _mlstm_call(
        pr["xm"], pr["zm"], pr["om"], pr["gates"],
        jnp.zeros((B, SUBLANES, M), F32), jnp.zeros((B,) + state_C.shape[2:], F32),
        jnp.zeros((B,) + state_n.shape[2:], F32), jnp.zeros((B, 1, LANES), F32),
        convw, convb, wq, wk, mnorm, mskip, bif, L=MLSTM_CHUNK, out_dtype=BF16, name="mlstm_prompt")
    y_prompt = _post_call(x_prompt, p_gate, [a[0] for a in att] + [a[1] for a in att] + [expand],
                          pr["za"], mo_p, pr["ga"], pr["gm"], wpa, wpm, wout, fgain,
                          tm=POST_TM, merge=True, name="post_prompt")
    kv_p = []
    w_max = min(ATT_GROUPS[-1][0], S)
    (kv2_t,) = _proj_call(x_prompt, gain, p_scale, p_shift, wp, [segs["kv2"]], [F32],
                          tm=PROJ_TM, row0=S - w_max, rows=w_max, name="tail_kv2")
    w_mid = min(ATT_GROUPS[1][0], S)
    kv0_t, kv1_t, xm_t = _proj_call(x_prompt, gain, p_scale, p_shift, wp,
                                    [segs["kv0"], segs["kv1"], segs["xm"]], [F32] * 3,
                                    tm=w_mid, row0=S - w_mid, rows=w_mid, name="tail_kv01")
    kshape = lambda L: (1, B, L, 2, ATT_HEADS, ATT_HEAD_DIM)
    w0 = min(ATT_GROUPS[0][0], S)
    kv_p = [kv0_t[:, w_mid - w0:].reshape(kshape(w0)), kv1_t.reshape(kshape(w_mid)), kv2_t.reshape(kshape(w_max))]
    conv_p = xm_t[:, w_mid - (CONV_WIDTH - 1):][None]

    R = DB * T
    rep = lambda t: jnp.repeat(t[B:], T, axis=0).reshape(1, R, D)
    s_shift, s_scale, s_gate = rep(shift), rep(scale), rep(gate)
    xs = x_sample.reshape(1, R, D)
    sr = dict(zip(names, _proj_call(xs, gain, s_scale, s_shift, wp, seg_list, [F32] * 13,
                                    tm=R, row0=0, rows=R, name="proj_sample")))
    bcs, bns = [], []
    for g, (win, dil) in enumerate(ATT_GROUPS):
        bc, bn = _sample_bias(rel_table, g, win, dil, caches[g].shape[1], T)
        bcs.append(bc)
        bns.append(bn)
    cache_t = [jnp.transpose(c, (0, 2, 3, 4, 1)).reshape(DB, 2 * ATT_WIDTH, c.shape[1]) for c in caches]
    kvn_t = [jnp.pad(jnp.transpose(sr[f"kv{g}"].reshape(DB, T, 2 * ATT_WIDTH), (0, 2, 1)),
                     ((0, 0), (0, 0), (LANES - T, 0))) for g in range(3)]
    sa = _sattn_call([sr[f"q{g}"].reshape(DB, T, ATT_WIDTH) for g in range(3)],
                     kvn_t, cache_t, bcs, bns, T)
    o_att_s = sa[0].reshape(1, R, ATT_WIDTH)
    kv_s = [jnp.transpose(c.reshape(DB, 2, ATT_HEADS, ATT_HEAD_DIM, c.shape[2]), (0, 4, 1, 2, 3))[None]
            for c in sa[1:]]
    conv_prev_s = jnp.concatenate([jnp.zeros((DB, SUBLANES - (CONV_WIDTH - 1), M), F32), state_conv[0]], axis=1)
    m0_s = jnp.concatenate([state_m[0], jnp.zeros((DB, LANES - H), F32)], axis=1).reshape(DB, 1, LANES)
    seqv = lambda t: t.reshape(DB, T, t.shape[-1])
    mo_s, C_s, n_s, m_s = _mlstm_call(
        seqv(sr["xm"]), seqv(sr["zm"]), seqv(sr["om"]), seqv(sr["gates"]),
        conv_prev_s, state_C[0], state_n[0], m0_s,
        convw, convb, wq, wk, mnorm, mskip, bif, L=T, out_dtype=F32, name="mlstm_sample")
    y_sample = _post_call(xs, s_gate, [o_att_s], sr["za"], mo_s.reshape(1, R, M), sr["ga"], sr["gm"],
                          wpa, wpm, wout, fgain, tm=R, merge=False, name="post_sample")
    conv_s = seqv(sr["xm"])[:, T - (CONV_WIDTH - 1):][None]

    return (y_prompt, y_sample.reshape(DB, T, D),
            kv_p[0], kv_s[0], kv_p[1], kv_s[1], kv_p[2], kv_s[2],
            conv_p, conv_s, C_p[None], C_s[None], n_p[None], n_s[None],
            m_p[:, 0, :H][None], m_s[:, 0, :H][None])
```

```python
import functools

import numpy as np
import jax
import jax.numpy as jnp
from jax import lax
from jax.experimental import pallas as pl
from jax.experimental.pallas import tpu as pltpu

F32 = jnp.float32
BF16 = jnp.bfloat16

ATT_GROUPS = ((128, 1), (512, 4), (2048, 16))
N_GROUPS = len(ATT_GROUPS)
ATT_HEADS = 8
ATT_HEAD_DIM = 64
ATT_WIDTH = ATT_HEADS * ATT_HEAD_DIM
WIN_STEPS = 128
ATT_SCALE = ATT_HEAD_DIM ** -0.5
N_BUCKETS = 32
MAX_DISTANCE = 2048
M_HEADS = 4
CONV_WIDTH = 4
EPS = 1e-6
NEG = -1e30

LANES = 128
SUBLANES = 8
VMEM_LIMIT = 56 * 1024 * 1024

ATT_TILE = 2048
PROJ_TM = 512
POST_TM = 512
MLSTM_CHUNK = 256


def _params(sem, vmem=VMEM_LIMIT):
    return pltpu.CompilerParams(dimension_semantics=sem, vmem_limit_bytes=vmem)


def _t5_bucket(dist):
    n = np.asarray(dist).astype(np.int64)
    max_exact = N_BUCKETS // 2
    nf = np.maximum(n, 1).astype(np.float32)
    large = max_exact + (np.log(nf / max_exact) / np.log(np.float32(MAX_DISTANCE / max_exact))
                         * (N_BUCKETS - max_exact)).astype(np.int64)
    large = np.minimum(large, N_BUCKETS - 1)
    return np.where(n < max_exact, n, large).astype(np.int32)


def _silu(x):
    return x * jax.nn.sigmoid(x)


def _ada_kernel(c_ref, w_ref, b_ref, o_ref):
    s = _silu(c_ref[...])
    o_ref[...] = jnp.dot(s, w_ref[...], preferred_element_type=F32,
                         precision=lax.Precision.HIGHEST) + b_ref[...]


def _ada_call(c, w, b):
    n, d = c.shape
    width = w.shape[1]
    tn = 512
    return pl.pallas_call(
        _ada_kernel,
        grid=(width // tn,),
        in_specs=[pl.BlockSpec((n, d), lambda j: (0, 0)),
                  pl.BlockSpec((d, tn), lambda j: (0, j)),
                  pl.BlockSpec((1, tn), lambda j: (0, j))],
        out_specs=pl.BlockSpec((n, tn), lambda j: (0, j)),
        out_shape=jax.ShapeDtypeStruct((n, width), F32),
        compiler_params=_params(("arbitrary",)),
        name="ada",
    )(c, w, b.reshape(1, width))


def _proj_kernel(x_ref, gain_ref, scale_ref, shift_ref, w_ref, *rest, segs, dils):
    out_refs = rest[:len(segs)]
    x = x_ref[0]
    tm, D = x.shape
    ms = jnp.mean(x * x, axis=-1, keepdims=True)
    h = x * lax.rsqrt(ms + EPS) * gain_ref[...] * (1.0 + scale_ref[0]) + shift_ref[0]
    lhs = {1: h.astype(BF16)}
    strides = sorted(set(dils) - {1})
    if strides:
        hs_ref = rest[len(segs)]
        n_tiles = D // LANES
        for s in range(n_tiles):
            hs_ref[s] = h[:, s * LANES:(s + 1) * LANES]
        for d in strides:
            n = tm // d
            lhs[d] = jnp.concatenate(
                [jnp.concatenate([hs_ref[s, pl.ds(r, n, stride=d), :] for r in range(d)], axis=0)
                 for s in range(n_tiles)], axis=1).astype(BF16)
    for o_ref, (c0, width), d in zip(out_refs, segs, dils):
        res = jnp.dot(lhs[d], w_ref[:, c0:c0 + width], preferred_element_type=F32).astype(o_ref.dtype)
        if d == 1:
            o_ref[0] = res
        else:
            n = tm // d
            for r in range(d):
                o_ref[0, r] = res[r * n:(r + 1) * n, :]


def _proj_call(x3, gain, scale3, shift3, w, segs, dtypes, *, tm, row0, rows, name, dils=None):
    B, S, D = x3.shape
    nrb = rows // tm
    rb0 = row0 // tm
    dils = tuple(dils) if dils is not None else (1,) * len(segs)
    per_row = scale3.shape[1] != 1
    if per_row:
        mod_spec = pl.BlockSpec((1, tm, D), lambda b, i: (b, rb0 + i, 0))
    else:
        mod_spec = pl.BlockSpec((1, 1, D), lambda b, i: (b, 0, 0))
    out_shape, out_specs = [], []
    for (_, wd), dt, d in zip(segs, dtypes, dils):
        if d == 1:
            out_shape.append(jax.ShapeDtypeStruct((B, rows, wd), dt))
            out_specs.append(pl.BlockSpec((1, tm, wd), lambda b, i: (b, i, 0)))
        else:
            out_shape.append(jax.ShapeDtypeStruct((B, d, rows // d, wd), dt))
            out_specs.append(pl.BlockSpec((1, d, tm // d, wd), lambda b, i: (b, 0, i, 0)))
    scratch = [pltpu.VMEM((D // LANES, tm, LANES), F32)] if any(d > 1 for d in dils) else []
    return pl.pallas_call(
        functools.partial(_proj_kernel, segs=tuple(segs), dils=dils),
        grid=(B, nrb),
        in_specs=[pl.BlockSpec((1, tm, D), lambda b, i: (b, rb0 + i, 0)),
                  pl.BlockSpec((1, D), lambda b, i: (0, 0)),
                  mod_spec, mod_spec,
                  pl.BlockSpec(w.shape, lambda b, i: (0, 0), pipeline_mode=pl.Buffered(1))],
        out_specs=out_specs,
        out_shape=out_shape,
        scratch_shapes=scratch,
        compiler_params=_params(("arbitrary", "arbitrary")),
        name=name,
    )(x3, gain, scale3, shift3, w)


HEADS_PER_SLAB = LANES // ATT_HEAD_DIM
N_SLABS = ATT_HEADS // HEADS_PER_SLAB
LSE_LANES = LANES // ATT_HEADS


def _attn_unit(q, kv, bias_ref, prev_mask, o_ref, l_ref, at):
    dn = (((1,), (1,)), ((), ()))
    nk = 2 * WIN_STEPS
    lane_q = lax.broadcasted_iota(jnp.int32, (WIN_STEPS, LANES), 1) < ATT_HEAD_DIM
    lane_k = lax.broadcasted_iota(jnp.int32, (nk, LANES), 1) < ATT_HEAD_DIM
    ones_lo = jnp.where(lane_k, 1.0, 0.0).astype(BF16)
    ones_hi = jnp.where(lane_k, 0.0, 1.0).astype(BF16)
    zero_q = jnp.zeros((WIN_STEPS, LANES), BF16)
    zero_k = jnp.zeros((nk, LANES), BF16)

    def scores(m):
        cs = slice(m * LANES, (m + 1) * LANES)
        qs = q[:, cs]
        ks = kv[:, cs]
        out = []
        for hh in range(HEADS_PER_SLAB):
            qm = jnp.where(lane_q, qs, zero_q) if hh == 0 else jnp.where(lane_q, zero_q, qs)
            s = lax.dot_general(qm, ks, dn, preferred_element_type=F32) + bias_ref[m * HEADS_PER_SLAB + hh]
            out.append(jnp.concatenate([s[:, :WIN_STEPS] + prev_mask, s[:, WIN_STEPS:]], axis=1))
        return out

    def finish(m, ss):
        cs = slice(m * LANES, (m + 1) * LANES)
        vs = kv[:, ATT_WIDTH + m * LANES:ATT_WIDTH + (m + 1) * LANES]
        ps, mxs = [], []
        for s in ss:
            mx = jnp.max(jnp.maximum(s[:, :WIN_STEPS], s[:, WIN_STEPS:]), -1, keepdims=True)
            ps.append(jnp.exp(s - mx).astype(BF16))
            mxs.append(mx)
        pcat = jnp.concatenate(ps, axis=1)
        vpair = jnp.concatenate(
            [jnp.concatenate([jnp.where(lane_k, vs, zero_k), ones_lo], axis=1),
             jnp.concatenate([jnp.where(lane_k, zero_k, vs), ones_hi], axis=1)], axis=0)
        acc = jnp.dot(pcat, vpair, preferred_element_type=F32)
        den = acc[:, LANES:]
        o_ref[at + (cs,)] = (acc[:, :LANES] / den).astype(o_ref.dtype)
        lse = jnp.where(lane_q, mxs[0], mxs[1]) + jnp.log(den)
        return pltpu.roll(lse, (LSE_LANES * HEADS_PER_SLAB * m - 48) % LANES, axis=1)

    lane = lax.broadcasted_iota(jnp.int32, (WIN_STEPS, LANES), 1)
    lse_c = None
    pending = scores(0)
    for m in range(N_SLABS):
        nxt = scores(m + 1) if m + 1 < N_SLABS else None
        part = finish(m, pending)
        lse_c = part if lse_c is None else jnp.where(lane >= LSE_LANES * HEADS_PER_SLAB * m, part, lse_c)
        pending = nxt
    l_ref[at + (slice(None),)] = lse_c


def _attn_kernel(q_ref, kvc_ref, kvp_ref, bias_ref, o_ref, l_ref, *, ns):
    d = q_ref.shape[1]
    first_tile = pl.program_id(1) == 0

    def body(idx, carry):
        rr = idx // ns
        j = idx % ns
        rc = pl.ds(pl.multiple_of(j * WIN_STEPS, WIN_STEPS), WIN_STEPS)
        q = q_ref[0, rr, rc, :] * ATT_SCALE
        kv_prev = kvp_ref[0, rr]
        if ns > 1:
            rp = pl.ds(pl.multiple_of(jnp.maximum(j - 1, 0) * WIN_STEPS, WIN_STEPS), WIN_STEPS)
            kv_prev = jnp.where(j == 0, kv_prev, kvc_ref[0, rr, rp, :])
        kv = jnp.concatenate([kv_prev, kvc_ref[0, rr, rc, :]], axis=0)
        prev_mask = jnp.where(first_tile & (j == 0), NEG, 0.0).astype(F32)
        _attn_unit(q, kv, bias_ref, prev_mask, o_ref, l_ref, (0, rr, rc))
        return carry

    lax.fori_loop(0, d * ns, body, 0)


def _attn_call(q, kv, bias, name):
    B, d, U, _ = q.shape
    ns = ATT_TILE // (WIN_STEPS * d)
    rows = ns * WIN_STEPS
    blk = lambda width: pl.BlockSpec((1, d, rows, width), lambda b, t: (b, 0, t, 0))
    return pl.pallas_call(
        functools.partial(_attn_kernel, ns=ns),
        grid=(B, U // rows),
        in_specs=[blk(ATT_WIDTH), blk(2 * ATT_WIDTH),
                  pl.BlockSpec((1, d, WIN_STEPS, 2 * ATT_WIDTH),
                               lambda b, t: (b, 0, jnp.maximum(t * ns - 1, 0), 0)),
                  pl.BlockSpec(bias.shape, lambda b, t: (0, 0, 0))],
        out_specs=[blk(ATT_WIDTH), blk(LANES)],
        out_shape=[jax.ShapeDtypeStruct((B, d, U, ATT_WIDTH), BF16),
                   jax.ShapeDtypeStruct((B, d, U, LANES), F32)],
        compiler_params=_params(("arbitrary", "arbitrary")),
        name=name,
    )(q, kv, kv, bias)


def _stride_bias(rel_table, g, d):
    bucket = _t5_bucket(np.arange(WIN_STEPS + 1) * d)
    onehot = jnp.asarray(np.eye(N_BUCKETS, dtype=np.float32)[bucket])
    tbl = rel_table[:, g * ATT_HEADS:(g + 1) * ATT_HEADS].astype(F32)
    return jnp.dot(onehot, tbl, precision=lax.Precision.HIGHEST).T


def _prompt_bias(rel_table, g, d):
    vals = _stride_bias(rel_table, g, d)
    n = WIN_STEPS
    period = 3 * n
    wp = jnp.concatenate([jnp.full((ATT_HEADS, n - 1), NEG, F32), vals[:, ::-1],
                          jnp.full((ATT_HEADS, n), NEG, F32)], axis=1)
    flat = jnp.tile(wp, (1, n))[:, :n * (period - 1)]
    return flat.reshape(ATT_HEADS, n, period - 1)[:, :, n - 1:n - 1 + 2 * n]


SHIFT_ROWS = 64


def _sattn_kernel(*refs, T):
    q_refs = refs[0:3]
    kvnew_refs = refs[3:6]
    cache_refs = refs[6:9]
    bc_refs = refs[9:12]
    bn_refs = refs[12:15]
    o_ref = refs[15]
    co_refs = refs[16:19]
    kvn_refs = refs[19:22]
    HT = ATT_HEADS * T
    dn = (((1,), (1,)), ((), ()))
    row_head = lax.broadcasted_iota(jnp.int32, (HT, ATT_WIDTH), 0) // T
    col_head = lax.broadcasted_iota(jnp.int32, (HT, ATT_WIDTH), 1) // ATT_HEAD_DIM
    head_mask = row_head == col_head

    for g in range(N_GROUPS):
        rows = jnp.concatenate([jnp.zeros((LANES - T, 2 * ATT_WIDTH), F32), kvnew_refs[g][0]], axis=0)
        kvn_refs[g][0] = rows.T

    stats = []
    for g in range(N_GROUPS):
        q = q_refs[g][0] * ATT_SCALE
        qexp = jnp.where(head_mask, jnp.concatenate([q] * ATT_HEADS, axis=0), 0.0).astype(BF16)
        kn = kvn_refs[g][0, :ATT_WIDTH, :].astype(BF16)
        kc = cache_refs[g][0, :ATT_WIDTH, :].astype(BF16)
        lc = jnp.dot(qexp, kc, preferred_element_type=F32) + bc_refs[g][...]
        ln = jnp.dot(qexp, kn, preferred_element_type=F32) + bn_refs[g][...]
        mx = jnp.maximum(jnp.max(lc, -1, keepdims=True), jnp.max(ln, -1, keepdims=True))
        pc = jnp.exp(lc - mx)
        pn = jnp.exp(ln - mx)
        ssum = jnp.sum(pc, -1, keepdims=True) + jnp.sum(pn, -1, keepdims=True)
        stats.append((pc, pn, ssum, mx + jnp.log(ssum)))

    lse_max = jnp.maximum(jnp.maximum(stats[0][3], stats[1][3]), stats[2][3])
    es = [jnp.exp(st[3] - lse_max) for st in stats]
    esum = es[0] + es[1] + es[2]
    acc = jnp.zeros((HT, ATT_WIDTH), F32)
    for g in range(N_GROUPS):
        pc, pn, ssum, _ = stats[g]
        w = es[g] / (esum * ssum)
        vc = cache_refs[g][0, ATT_WIDTH:, :].astype(BF16)
        vn = kvn_refs[g][0, ATT_WIDTH:, :].astype(BF16)
        acc = acc + lax.dot_general((pc * w).astype(BF16), vc, dn, preferred_element_type=F32)
        acc = acc + lax.dot_general((pn * w).astype(BF16), vn, dn, preferred_element_type=F32)
    lane_head = lax.broadcasted_iota(jnp.int32, (T, ATT_WIDTH), 1) // ATT_HEAD_DIM
    o = jnp.zeros((T, ATT_WIDTH), F32)
    for h in range(ATT_HEADS):
        o = o + jnp.where(lane_head == h, acc[h * T:(h + 1) * T, :], 0.0)
    o_ref[0] = o

    for g in range(N_GROUPS):
        L = cache_refs[g].shape[2]
        nrows = cache_refs[g].shape[1]
        is_new = lax.broadcasted_iota(jnp.int32, (SHIFT_ROWS, LANES), 1) >= LANES - T

        def shift_rows(i, carry, g=g, L=L, is_new=is_new):
            rs = pl.ds(pl.multiple_of(i * SHIFT_ROWS, SHIFT_ROWS), SHIFT_ROWS)
            rolled = pltpu.roll(cache_refs[g][0, rs, :], L - T, axis=1)
            if L > LANES:
                co_refs[g][0, rs, 0:L - LANES] = rolled[:, 0:L - LANES]
            co_refs[g][0, rs, L - LANES:L] = jnp.where(is_new, kvn_refs[g][0, rs, :], rolled[:, L - LANES:])
            return carry

        lax.fori_loop(0, nrows // SHIFT_ROWS, shift_rows, 0)


def _sample_bias(rel_table, g, W, d, Lb, T):
    vals = _stride_bias(rel_table, g, d)
    n = Lb + T
    by_dist = jnp.concatenate([vals[:, :, None], jnp.full((ATT_HEADS, WIN_STEPS + 1, d - 1), NEG, F32)],
                              axis=2).reshape(ATT_HEADS, (WIN_STEPS + 1) * d)
    if by_dist.shape[1] < n:
        by_dist = jnp.concatenate([by_dist, jnp.full((ATT_HEADS, n - by_dist.shape[1]), NEG, F32)], axis=1)
    rev = jnp.concatenate([by_dist[:, :n][:, ::-1], jnp.full((ATT_HEADS, T), NEG, F32)], axis=1)
    rows = jnp.stack([rev[:, T - 1 - t:T - 1 - t + n] for t in range(T)], axis=1)
    rows = rows.reshape(ATT_HEADS * T, n)
    bc = rows[:, :Lb]
    bn = jnp.concatenate([jnp.full((ATT_HEADS * T, LANES - T), NEG, F32), rows[:, Lb:]], axis=1)
    return bc, bn


def _sattn_call(qs, kvns, caches, bcs, bns, T):
    DB = caches[0].shape[0]
    rows = caches[0].shape[1]
    in_specs = ([pl.BlockSpec((1, T, ATT_WIDTH), lambda b: (b, 0, 0))] * 3
                + [pl.BlockSpec((1,) + x.shape[1:], lambda b: (b, 0, 0)) for x in kvns]
                + [pl.BlockSpec((1,) + c.shape[1:], lambda b: (b, 0, 0)) for c in caches]
                + [pl.BlockSpec(x.shape, lambda b: (0, 0)) for x in bcs]
                + [pl.BlockSpec(x.shape, lambda b: (0, 0)) for x in bns])
    out_specs = ([pl.BlockSpec((1, T, ATT_WIDTH), lambda b: (b, 0, 0))]
                 + [pl.BlockSpec((1,) + c.shape[1:], lambda b: (b, 0, 0)) for c in caches])
    out_shape = ([jax.ShapeDtypeStruct((DB, T, ATT_WIDTH), F32)]
                 + [jax.ShapeDtypeStruct(c.shape, F32) for c in caches])
    return pl.pallas_call(
        functools.partial(_sattn_kernel, T=T),
        grid=(DB,),
        in_specs=in_specs, out_specs=out_specs, out_shape=out_shape,
        scratch_shapes=[pltpu.VMEM((1, rows, LANES), F32)] * N_GROUPS,
        compiler_params=_params(("arbitrary",)),
        name="sample_attn",
    )(*qs, *kvns, *caches, *bcs, *bns)


def _scan_rows(x, op, fill):
    n = x.shape[0]
    rowid = lax.broadcasted_iota(jnp.int32, x.shape, 0)
    s = 1
    while s < n:
        shifted = pltpu.roll(x, s, axis=0)
        x = op(x, jnp.where(rowid >= s, shifted, fill))
        s *= 2
    return x


def _pad_rows(x, n, fill=0.0):
    if x.shape[0] == n:
        return x
    return jnp.concatenate([x, jnp.full((n - x.shape[0],) + x.shape[1:], fill, x.dtype)], axis=0)


def _mlstm_kernel(xm_ref, zm_ref, om_ref, g_ref, cprev_ref, C0_ref, n0_ref, m0_ref,
                  convw_ref, convb_ref, wq_ref, wk_ref, mnorm_ref, mskip_ref, bif_ref,
                  mo_ref, C_ref, n_ref, m_ref, xc_ref, *, L):
    LS = max(L, LANES)
    DV = C_ref.shape[2]
    DK = C_ref.shape[3]
    c = pl.program_id(1)

    @pl.when(c == 0)
    def _():
        xc_ref[0:SUBLANES, :] = cprev_ref[0]
        C_ref[...] = C0_ref[...]
        n_ref[...] = n0_ref[...]
        m_ref[...] = m0_ref[...]

    xm_b = xm_ref[0].astype(BF16)
    xc_ref[SUBLANES:SUBLANES + L, :] = xm_ref[0].astype(F32)
    conv = convb_ref[...]
    for j in range(CONV_WIDTH):
        off = SUBLANES - (CONV_WIDTH - 1) + j
        conv = conv + convw_ref[j:j + 1, :] * xc_ref[off:off + L, :]
    xc_ref[0:SUBLANES, :] = xc_ref[L:L + SUBLANES, :]
    c_act = _silu(conv)
    cb = c_act.astype(BF16)

    gates = g_ref[0] + bif_ref[...]
    i_pre = gates[:, :LANES]
    logf = jax.nn.log_sigmoid(gates[:, LANES:])
    b = _scan_rows(logf, jnp.add, 0.0)
    a = i_pre - b
    ca = _scan_rows(a, jnp.maximum, NEG)
    m_prev = m_ref[0]
    mm = jnp.maximum(ca, m_prev)
    u = -mm
    w_inter = jnp.exp(u + m_prev)
    emt = jnp.exp(-(b + mm))
    bL = b[L - 1:L, :]
    m_new = bL + jnp.maximum(m_prev, ca[L - 1:L, :])
    wk = jnp.exp(bL + a - m_new)
    wC = jnp.exp(bL + m_prev - m_new)
    m_ref[0] = m_new

    t_id = lax.broadcasted_iota(jnp.int32, (L, LS), 0)
    s_id = lax.broadcasted_iota(jnp.int32, (L, LS), 1)
    causal = s_id <= t_id
    eye = s_id == t_id
    dn_t = (((1,), (1,)), ((), ()))

    for h in range(M_HEADS):
        vs = slice(h * DV, (h + 1) * DV)
        ch = cb[:, vs]
        qh = (jnp.dot(ch, wq_ref[h], preferred_element_type=F32) * (DK ** -0.5)).astype(BF16)
        kh = jnp.dot(ch, wk_ref[h], preferred_element_type=F32).astype(BF16)
        kh_s = _pad_rows(kh, LS)
        vh_s = _pad_rows(xm_b[:, vs], LS)
        a_col = a[:, h:h + 1]
        a_row = jnp.sum(jnp.where(eye, a_col, 0.0), axis=0, keepdims=True)
        dm = jnp.where(causal, u[:, h:h + 1] + a_row, NEG)
        w_intra = jnp.exp(dm)
        sc = lax.dot_general(qh, kh_s, dn_t, preferred_element_type=F32) * w_intra
        C_h = C_ref[0, h]
        n_h = n_ref[0, h:h + 1, :]
        wi = w_inter[:, h:h + 1]
        num = (jnp.dot(sc.astype(BF16), vh_s, preferred_element_type=F32)
               + wi * lax.dot_general(qh, C_h.astype(BF16), dn_t, preferred_element_type=F32))
        den = (jnp.sum(sc, -1, keepdims=True)
               + wi * jnp.sum(qh.astype(F32) * n_h, -1, keepdims=True))
        hcell = num / jnp.maximum(jnp.abs(den), emt[:, h:h + 1])
        mu = jnp.mean(hcell, -1, keepdims=True)
        hc = hcell - mu
        hn = hc * lax.rsqrt(jnp.mean(hc * hc, -1, keepdims=True) + EPS)
        out = ((jax.nn.sigmoid(om_ref[0, :, vs].astype(F32)) * (hn * mnorm_ref[:, vs])
                + mskip_ref[:, vs] * c_act[:, vs]) * _silu(zm_ref[0, :, vs].astype(F32)))
        mo_ref[0, :, vs] = out.astype(mo_ref.dtype)
        wk_s = _pad_rows(wk[:, h:h + 1], LS)
        vw = (vh_s.astype(F32) * wk_s).astype(BF16)
        wc = wC[:, h:h + 1]
        C_ref[0, h] = wc * C_h + lax.dot_general(vw, kh_s, (((0,), (0,)), ((), ())),
                                                 preferred_element_type=F32)
        n_ref[0, h:h + 1, :] = wc * n_h + jnp.sum(kh_s.astype(F32) * wk_s, axis=0, keepdims=True)


def _mlstm_call(xm, zm, om, gates, conv_prev, C0, n0, m0, convw, convb, wq, wk, mnorm, mskip, bif,
                *, L, out_dtype, name):
    N, S, M = xm.shape
    nc = S // L
    H, DV, DK = C0.shape[1:]
    seq = lambda b, c: (b, c, 0)
    per_b3 = lambda b, c: (b, 0, 0)
    per_b4 = lambda b, c: (b, 0, 0, 0)
    const2 = lambda b, c: (0, 0)
    const3 = lambda b, c: (0, 0, 0)
    return pl.pallas_call(
        functools.partial(_mlstm_kernel, L=L),
        grid=(N, nc),
        in_specs=[pl.BlockSpec((1, L, M), seq), pl.BlockSpec((1, L, M), seq), pl.BlockSpec((1, L, M), seq),
                  pl.BlockSpec((1, L, 2 * LANES), seq),
                  pl.BlockSpec((1, SUBLANES, M), per_b3),
                  pl.BlockSpec((1, H, DV, DK), per_b4),
                  pl.BlockSpec((1, H, DK), per_b3),
                  pl.BlockSpec((1, 1, LANES), per_b3),
                  pl.BlockSpec(convw.shape, const2), pl.BlockSpec(convb.shape, const2),
                  pl.BlockSpec(wq.shape, const3), pl.BlockSpec(wk.shape, const3),
                  pl.BlockSpec(mnorm.shape, const2), pl.BlockSpec(mskip.shape, const2),
                  pl.BlockSpec(bif.shape, const2)],
        out_specs=[pl.BlockSpec((1, L, M), seq),
                   pl.BlockSpec((1, H, DV, DK), per_b4),
                   pl.BlockSpec((1, H, DK), per_b3),
                   pl.BlockSpec((1, 1, LANES), per_b3)],
        out_shape=[jax.ShapeDtypeStruct((N, S, M), out_dtype),
                   jax.ShapeDtypeStruct((N, H, DV, DK), F32),
                   jax.ShapeDtypeStruct((N, H, DK), F32),
                   jax.ShapeDtypeStruct((N, 1, LANES), F32)],
        scratch_shapes=[pltpu.VMEM((L + 2 * SUBLANES, M), F32)],
        compiler_params=_params(("arbitrary", "arbitrary")),
        name=name,
    )(xm, zm, om, gates, conv_prev, C0, n0, m0, convw, convb, wq, wk, mnorm, mskip, bif)


def _token_order(src_ref, scr_ref):
    _, d, n, width = src_ref.shape
    if d == 1:
        return src_ref[0, 0].astype(F32)
    n_tiles = width // LANES
    for r in range(d):
        for s in range(n_tiles):
            scr_ref[s, pl.ds(r, n, stride=d), :] = src_ref[0, r, :, s * LANES:(s + 1) * LANES].astype(F32)
    return jnp.concatenate([scr_ref[s] for s in range(n_tiles)], axis=1)


def _post_kernel(*refs, merge):
    if merge:
        (x_ref, gate_ref, o0, o1, o2, l0, l1, l2, expand_ref, za_ref, mo_ref, ga_ref, gm_ref,
         wpa_ref, wpm_ref, wout_ref, fg_ref, y_ref, scr_ref) = refs
        ls = [_token_order(l, scr_ref) for l in (l0, l1, l2)]
        lmax = jnp.maximum(jnp.maximum(ls[0], ls[1]), ls[2])
        es = [jnp.exp(l - lmax) for l in ls]
        inv = 1.0 / (es[0] + es[1] + es[2])
        o_att = None
        for e, o in zip(es, (o0, o1, o2)):
            a = e * inv
            hi = a.astype(BF16)
            lo = (a - hi.astype(F32)).astype(BF16)
            a_wide = jnp.dot(jnp.concatenate([hi, lo], axis=1), expand_ref[...],
                             preferred_element_type=F32)
            term = a_wide * _token_order(o, scr_ref)
            o_att = term if o_att is None else o_att + term
    else:
        (x_ref, gate_ref, oa_ref, za_ref, mo_ref, ga_ref, gm_ref,
         wpa_ref, wpm_ref, wout_ref, fg_ref, y_ref) = refs
        o_att = oa_ref[0]
    a_in = (o_att * _silu(za_ref[0].astype(F32))).astype(BF16)
    a_br = jnp.dot(a_in, wpa_ref[...], preferred_element_type=F32)
    m_br = jnp.dot(mo_ref[0].astype(BF16), wpm_ref[...], preferred_element_type=F32)
    merged = (jax.nn.sigmoid(ga_ref[0].astype(F32)) * a_br
              + jax.nn.sigmoid(gm_ref[0].astype(F32)) * m_br)
    y = x_ref[0] + gate_ref[0] * jnp.dot(merged.astype(BF16), wout_ref[...], preferred_element_type=F32)
    ms = jnp.mean(y * y, axis=-1, keepdims=True)
    y_ref[0] = y * lax.rsqrt(ms + EPS) * fg_ref[...]


def _post_call(x3, gate3, att_inputs, za, mo, ga, gm, wpa, wpm, wout, fgain, *, tm, merge, name):
    B, S, D = x3.shape
    row = lambda b, i: (b, i, 0)
    const2 = lambda b, i: (0, 0)
    if gate3.shape[1] == 1:
        gate_spec = pl.BlockSpec((1, 1, D), lambda b, i: (b, 0, 0))
    else:
        gate_spec = pl.BlockSpec((1, tm, D), row)
    def blk(a):
        if a.ndim == 2:
            return pl.BlockSpec(a.shape, const2)
        if a.ndim == 4:
            d = a.shape[1]
            return pl.BlockSpec((1, d, tm // d, a.shape[3]), lambda b, i: (b, 0, i, 0))
        return pl.BlockSpec((1, tm, a.shape[2]), row)

    in_specs = ([pl.BlockSpec((1, tm, D), row), gate_spec]
                + [blk(a) for a in att_inputs]
                + [blk(za), blk(mo), blk(ga), blk(gm),
                   pl.BlockSpec(wpa.shape, const2), pl.BlockSpec(wpm.shape, const2),
                   pl.BlockSpec(wout.shape, const2), pl.BlockSpec(fgain.shape, const2)])
    scratch = [pltpu.VMEM((ATT_WIDTH // LANES, tm, LANES), F32)] if merge else []
    return pl.pallas_call(
        functools.partial(_post_kernel, merge=merge),
        grid=(B, S // tm),
        in_specs=in_specs,
        out_specs=pl.BlockSpec((1, tm, D), row),
        out_shape=jax.ShapeDtypeStruct((B, S, D), F32),
        scratch_shapes=scratch,
        compiler_params=_params(("arbitrary", "arbitrary")),
        name=name,
    )(x3, gate3, *att_inputs, za, mo, ga, gm, wpa, wpm, wout, fgain)


def _permuted_weights(w_in):
    D = w_in.shape[0]
    AW = ATT_WIDTH
    M = D
    off_q, off_k, off_v = 0, 3 * AW, 6 * AW
    off_za = 9 * AW
    off_xm = off_za + AW
    off_zm, off_om = off_xm + M, off_xm + 2 * M
    off_i = off_xm + 3 * M
    off_f = off_i + M_HEADS
    off_ga = off_f + M_HEADS
    off_gm = off_ga + D
    cols = [w_in[:, off_q:off_q + 3 * AW], w_in[:, off_za:off_za + AW]]
    for g in range(N_GROUPS):
        cols += [w_in[:, off_k + g * AW:off_k + (g + 1) * AW], w_in[:, off_v + g * AW:off_v + (g + 1) * AW]]
    cols += [w_in[:, off_xm:off_xm + 3 * M], w_in[:, off_ga:off_ga + 2 * D]]
    pad = jnp.zeros((D, LANES - M_HEADS), w_in.dtype)
    cols += [w_in[:, off_i:off_i + M_HEADS], pad, w_in[:, off_f:off_f + M_HEADS], pad]
    wp = jnp.concatenate([c.astype(BF16) for c in cols], axis=1)
    segs = {}
    c = 0
    for name, wd in (("q0", AW), ("q1", AW), ("q2", AW), ("za", AW),
                     ("kv0", 2 * AW), ("kv1", 2 * AW), ("kv2", 2 * AW),
                     ("xm", M), ("zm", M), ("om", M), ("ga", D), ("gm", D), ("gates", 2 * LANES)):
        segs[name] = (c, wd)
        c += wd
    return wp, segs


def kernel(x_prompt, x_sample, cache_kv_w128, cache_kv_w512, cache_kv_w2048, state_conv, state_C, state_n, state_m, c_prompt, c_sample, rel_table, norm_gain, w_ada, b_ada, w_in, b_if, conv_w, conv_b, w_mq, w_mk, m_norm, m_skip, w_pa, w_pm, w_out, final_gain):
    B, S, D = x_prompt.shape
    DB, T, _ = x_sample.shape
    assert norm_gain.shape[0] == 1, "single-layer trunk"
    assert S % ATT_TILE == 0 and S % MLSTM_CHUNK == 0 and T == SUBLANES
    caches = (cache_kv_w128[0], cache_kv_w512[0], cache_kv_w2048[0])
    H = M_HEADS
    M = conv_w.shape[2]

    wp, segs = _permuted_weights(w_in[0])
    names = ("q0", "q1", "q2", "za", "kv0", "kv1", "kv2", "xm", "zm", "om", "ga", "gm", "gates")
    seg_list = [segs[n] for n in names]
    gain = norm_gain[0].reshape(1, D)
    fgain = final_gain.reshape(1, D)
    wpa, wpm, wout = w_pa[0].astype(BF16), w_pm[0].astype(BF16), w_out[0].astype(BF16)
    wq, wk = w_mq[0].astype(BF16), w_mk[0].astype(BF16)
    convw, convb = conv_w[0], conv_b[0].reshape(1, M)
    mnorm, mskip = m_norm[0].reshape(1, M), m_skip[0].reshape(1, M)
    zpad = jnp.zeros((LANES - H,), F32)
    bif = jnp.concatenate([b_if[0, :H], zpad, b_if[0, H:], zpad]).reshape(1, 2 * LANES)

    ada = _ada_call(jnp.concatenate([c_prompt, c_sample], axis=0), w_ada[0], b_ada[0])
    shift, scale, gate = ada[:, :D], ada[:, D:2 * D], ada[:, 2 * D:]

    p_shift, p_scale, p_gate = (t[:B].reshape(B, 1, D) for t in (shift, scale, gate))
    dts = [BF16] * 12 + [F32]
    group_dil = {f"{p}{g}": dil for g, (_, dil) in enumerate(ATT_GROUPS) for p in ("q", "kv")}
    pr = dict(zip(names, _proj_call(x_prompt, gain, p_scale, p_shift, wp, seg_list, dts,
                                    tm=PROJ_TM, row0=0, rows=S, name="proj_prompt",
                                    dils=[group_dil.get(n, 1) for n in names])))
    att = []
    for g, (win, dil) in enumerate(ATT_GROUPS):
        planes = lambda a: a if a.ndim == 4 else a[:, None]
        att.append(_attn_call(planes(pr[f"q{g}"]), planes(pr[f"kv{g}"]), _prompt_bias(rel_table, g, dil),
                              f"attn_prompt_g{g}"))
    expand = np.zeros((2 * LANES, ATT_WIDTH), np.float32)
    for h in range(ATT_HEADS):
        expand[[LSE_LANES * h, LANES + LSE_LANES * h], h * ATT_HEAD_DIM:(h + 1) * ATT_HEAD_DIM] = 1.0
    expand = jnp.asarray(expand, BF16)
    mo_p, C_p, n_p, m_p = _mlstm_call(
        pr["xm"], pr["zm"], pr["om"], pr["gates"],
        jnp.zeros((B, SUBLANES, M), F32), jnp.zeros((B,) + state_C.shape[2:], F32),
        jnp.zeros((B,) + state_n.shape[2:], F32), jnp.zeros((B, 1, LANES), F32),
        convw, convb, wq, wk, mnorm, mskip, bif, L=MLSTM_CHUNK, out_dtype=BF16, name="mlstm_prompt")
    y_prompt = _post_call(x_prompt, p_gate, [a[0] for a in att] + [a[1] for a in att] + [expand],
                          pr["za"], mo_p, pr["ga"], pr["gm"], wpa, wpm, wout, fgain,
                          tm=POST_TM, merge=True, name="post_prompt")
    kv_p = []
    w_max = min(ATT_GROUPS[-1][0], S)
    (kv2_t,) = _proj_call(x_prompt, gain, p_scale, p_shift, wp, [segs["kv2"]], [F32],
                          tm=PROJ_TM, row0=S - w_max, rows=w_max, name="tail_kv2")
    w_mid = min(ATT_GROUPS[1][0], S)
    kv0_t, kv1_t, xm_t = _proj_call(x_prompt, gain, p_scale, p_shift, wp,
                                    [segs["kv0"], segs["kv1"], segs["xm"]], [F32] * 3,
                                    tm=w_mid, row0=S - w_mid, rows=w_mid, name="tail_kv01")
    kshape = lambda L: (1, B, L, 2, ATT_HEADS, ATT_HEAD_DIM)
    w0 = min(ATT_GROUPS[0][0], S)
    kv_p = [kv0_t[:, w_mid - w0:].reshape(kshape(w0)), kv1_t.reshape(kshape(w_mid)), kv2_t.reshape(kshape(w_max))]
    conv_p = xm_t[:, w_mid - (CONV_WIDTH - 1):][None]

    R = DB * T
    rep = lambda t: jnp.repeat(t[B:], T, axis=0).reshape(1, R, D)
    s_shift, s_scale, s_gate = rep(shift), rep(scale), rep(gate)
    xs = x_sample.reshape(1, R, D)
    sr = dict(zip(names, _proj_call(xs, gain, s_scale, s_shift, wp, seg_list, [F32] * 13,
                                    tm=R, row0=0, rows=R, name="proj_sample")))
    bcs, bns = [], []
    for g, (win, dil) in enumerate(ATT_GROUPS):
        bc, bn = _sample_bias(rel_table, g, win, dil, caches[g].shape[1], T)
        bcs.append(bc)
        bns.append(bn)
    cache_t = [jnp.transpose(c, (0, 2, 3, 4, 1)).reshape(DB, 2 * ATT_WIDTH, c.shape[1]) for c in caches]
    kvn_t = [sr[f"kv{g}"].reshape(DB, T, 2 * ATT_WIDTH) for g in range(3)]
    sa = _sattn_call([sr[f"q{g}"].reshape(DB, T, ATT_WIDTH) for g in range(3)],
                     kvn_t, cache_t, bcs, bns, T)
    o_att_s = sa[0].reshape(1, R, ATT_WIDTH)
    kv_s = [jnp.transpose(c.reshape(DB, 2, ATT_HEADS, ATT_HEAD_DIM, c.shape[2]), (0, 4, 1, 2, 3))[None]
            for c in sa[1:]]
    conv_prev_s = jnp.concatenate([jnp.zeros((DB, SUBLANES - (CONV_WIDTH - 1), M), F32), state_conv[0]], axis=1)
    m0_s = jnp.concatenate([state_m[0], jnp.zeros((DB, LANES - H), F32)], axis=1).reshape(DB, 1, LANES)
    seqv = lambda t: t.reshape(DB, T, t.shape[-1])
    mo_s, C_s, n_s, m_s = _mlstm_call(
        seqv(sr["xm"]), seqv(sr["zm"]), seqv(sr["om"]), seqv(sr["gates"]),
        conv_prev_s, state_C[0], state_n[0], m0_s,
        convw, convb, wq, wk, mnorm, mskip, bif, L=T, out_dtype=F32, name="mlstm_sample")
    y_sample = _post_call(xs, s_gate, [o_att_s], sr["za"], mo_s.reshape(1, R, M), sr["ga"], sr["gm"],
                          wpa, wpm, wout, fgain, tm=R, merge=False, name="post_sample")
    conv_s = seqv(sr["xm"])[:, T - (CONV_WIDTH - 1):][None]

    return (y_prompt, y_sample.reshape(DB, T, D),
            kv_p[0], kv_s[0], kv_p[1], kv_s[1], kv_p[2], kv_s[2],
            conv_p, conv_s, C_p[None], C_s[None], n_p[None], n_s[None],
            m_p[:, 0, :H][None], m_s[:, 0, :H][None])
```

```python
import functools

import numpy as np
import jax
import jax.numpy as jnp
from jax import lax
from jax.experimental import pallas as pl
from jax.experimental.pallas import tpu as pltpu

F32 = jnp.float32
BF16 = jnp.bfloat16

ATT_GROUPS = ((128, 1), (512, 4), (2048, 16))
N_GROUPS = len(ATT_GROUPS)
ATT_HEADS = 8
ATT_HEAD_DIM = 64
ATT_WIDTH = ATT_HEADS * ATT_HEAD_DIM
WIN_STEPS = 128
ATT_SCALE = ATT_HEAD_DIM ** -0.5
N_BUCKETS = 32
MAX_DISTANCE = 2048
M_HEADS = 4
CONV_WIDTH = 4
EPS = 1e-6
NEG = -1e30

LANES = 128
SUBLANES = 8
VMEM_LIMIT = 56 * 1024 * 1024

ATT_TILE = 2048
PROJ_TM = 512
POST_TM = 512
MLSTM_CHUNK = 256


def _params(sem, vmem=VMEM_LIMIT):
    return pltpu.CompilerParams(dimension_semantics=sem, vmem_limit_bytes=vmem)


def _t5_bucket(dist):
    n = np.asarray(dist).astype(np.int64)
    max_exact = N_BUCKETS // 2
    nf = np.maximum(n, 1).astype(np.float32)
    large = max_exact + (np.log(nf / max_exact) / np.log(np.float32(MAX_DISTANCE / max_exact))
                         * (N_BUCKETS - max_exact)).astype(np.int64)
    large = np.minimum(large, N_BUCKETS - 1)
    return np.where(n < max_exact, n, large).astype(np.int32)


def _silu(x):
    return x * jax.nn.sigmoid(x)


def _ada_kernel(c_ref, w_ref, b_ref, o_ref):
    s = _silu(c_ref[...])
    o_ref[...] = jnp.dot(s, w_ref[...], preferred_element_type=F32,
                         precision=lax.Precision.HIGHEST) + b_ref[...]


def _ada_call(c, w, b):
    n, d = c.shape
    width = w.shape[1]
    tn = 512
    return pl.pallas_call(
        _ada_kernel,
        grid=(width // tn,),
        in_specs=[pl.BlockSpec((n, d), lambda j: (0, 0)),
                  pl.BlockSpec((d, tn), lambda j: (0, j)),
                  pl.BlockSpec((1, tn), lambda j: (0, j))],
        out_specs=pl.BlockSpec((n, tn), lambda j: (0, j)),
        out_shape=jax.ShapeDtypeStruct((n, width), F32),
        compiler_params=_params(("arbitrary",)),
        name="ada",
    )(c, w, b.reshape(1, width))


_ACTIVATIONS = {None: lambda v: v, "silu": _silu, "sigmoid": jax.nn.sigmoid}


def _proj_kernel(x_ref, gain_ref, scale_ref, shift_ref, *rest, n_w, segs, dils, acts):
    w_refs = rest[:n_w]
    rest = rest[n_w:]
    out_refs = rest[:len(segs)]
    x = x_ref[0]
    tm, D = x.shape
    ms = jnp.mean(x * x, axis=-1, keepdims=True)
    h = x * lax.rsqrt(ms + EPS) * gain_ref[...] * (1.0 + scale_ref[0]) + shift_ref[0]
    lhs = {1: h.astype(BF16)}
    strides = sorted(set(dils) - {1})
    if strides:
        hs_ref = rest[len(segs)]
        n_tiles = D // LANES
        for s in range(n_tiles):
            hs_ref[s] = h[:, s * LANES:(s + 1) * LANES]
        for d in strides:
            n = tm // d
            lhs[d] = jnp.concatenate(
                [jnp.concatenate([hs_ref[s, pl.ds(r, n, stride=d), :] for r in range(d)], axis=0)
                 for s in range(n_tiles)], axis=1).astype(BF16)
    for o_ref, pieces, d, act in zip(out_refs, segs, dils, acts):
        parts = [_ACTIVATIONS[act](jnp.dot(lhs[d], w_refs[wi][:, c0:c0 + width],
                                           preferred_element_type=F32)).astype(o_ref.dtype)
                 for wi, c0, width in pieces]
        res = parts[0] if len(parts) == 1 else jnp.concatenate(parts, axis=1)
        if d == 1:
            o_ref[0] = res
        else:
            n = tm // d
            for r in range(d):
                o_ref[0, r] = res[r * n:(r + 1) * n, :]


def _proj_call(x3, gain, scale3, shift3, ws, segs, dtypes, *, tm, row0, rows, name, dils=None, acts=None):
    B, S, D = x3.shape
    nrb = rows // tm
    rb0 = row0 // tm
    dils = tuple(dils) if dils is not None else (1,) * len(segs)
    acts = tuple(acts) if acts is not None else (None,) * len(segs)
    per_row = scale3.shape[1] != 1
    if per_row:
        mod_spec = pl.BlockSpec((1, tm, D), lambda b, i: (b, rb0 + i, 0))
    else:
        mod_spec = pl.BlockSpec((1, 1, D), lambda b, i: (b, 0, 0))
    out_shape, out_specs = [], []
    for pieces, dt, d in zip(segs, dtypes, dils):
        wd = sum(p[2] for p in pieces)
        if d == 1:
            out_shape.append(jax.ShapeDtypeStruct((B, rows, wd), dt))
            out_specs.append(pl.BlockSpec((1, tm, wd), lambda b, i: (b, i, 0)))
        else:
            out_shape.append(jax.ShapeDtypeStruct((B, d, rows // d, wd), dt))
            out_specs.append(pl.BlockSpec((1, d, tm // d, wd), lambda b, i: (b, 0, i, 0)))
    scratch = [pltpu.VMEM((D // LANES, tm, LANES), F32)] if any(d > 1 for d in dils) else []
    return pl.pallas_call(
        functools.partial(_proj_kernel, n_w=len(ws), segs=tuple(segs), dils=dils, acts=acts),
        grid=(B, nrb),
        in_specs=[pl.BlockSpec((1, tm, D), lambda b, i: (b, rb0 + i, 0)),
                  pl.BlockSpec((1, D), lambda b, i: (0, 0)),
                  mod_spec, mod_spec]
                 + [pl.BlockSpec(w.shape, lambda b, i: (0, 0), pipeline_mode=pl.Buffered(1)) for w in ws],
        out_specs=out_specs,
        out_shape=out_shape,
        scratch_shapes=scratch,
        compiler_params=_params(("arbitrary", "arbitrary")),
        name=name,
    )(x3, gain, scale3, shift3, *ws)


HEADS_PER_SLAB = LANES // ATT_HEAD_DIM
N_SLABS = ATT_HEADS // HEADS_PER_SLAB
LSE_LANES = LANES // ATT_HEADS


def _attn_unit(q, kv, bias_ref, prev_mask, o_ref, l_ref, at):
    dn = (((1,), (1,)), ((), ()))
    nk = 2 * WIN_STEPS
    lane_q = lax.broadcasted_iota(jnp.int32, (WIN_STEPS, LANES), 1) < ATT_HEAD_DIM
    lane_k = lax.broadcasted_iota(jnp.int32, (nk, LANES), 1) < ATT_HEAD_DIM
    ones_lo = jnp.where(lane_k, 1.0, 0.0).astype(BF16)
    ones_hi = jnp.where(lane_k, 0.0, 1.0).astype(BF16)
    zero_q = jnp.zeros((WIN_STEPS, LANES), BF16)
    zero_k = jnp.zeros((nk, LANES), BF16)

    def scores(m):
        cs = slice(m * LANES, (m + 1) * LANES)
        qs = q[:, cs]
        ks = kv[:, cs]
        out = []
        for hh in range(HEADS_PER_SLAB):
            qm = jnp.where(lane_q, qs, zero_q) if hh == 0 else jnp.where(lane_q, zero_q, qs)
            s = lax.dot_general(qm, ks, dn, preferred_element_type=F32) + bias_ref[m * HEADS_PER_SLAB + hh]
            out.append(jnp.concatenate([s[:, :WIN_STEPS] + prev_mask, s[:, WIN_STEPS:]], axis=1))
        return out

    def finish(m, ss):
        cs = slice(m * LANES, (m + 1) * LANES)
        vs = kv[:, ATT_WIDTH + m * LANES:ATT_WIDTH + (m + 1) * LANES]
        ps, mxs = [], []
        for s in ss:
            mx = jnp.max(jnp.maximum(s[:, :WIN_STEPS], s[:, WIN_STEPS:]), -1, keepdims=True)
            ps.append(jnp.exp(s - mx).astype(BF16))
            mxs.append(mx)
        pcat = jnp.concatenate(ps, axis=1)
        vpair = jnp.concatenate(
            [jnp.concatenate([jnp.where(lane_k, vs, zero_k), ones_lo], axis=1),
             jnp.concatenate([jnp.where(lane_k, zero_k, vs), ones_hi], axis=1)], axis=0)
        acc = jnp.dot(pcat, vpair, preferred_element_type=F32)
        den = acc[:, LANES:]
        o_ref[at + (cs,)] = (acc[:, :LANES] / den).astype(o_ref.dtype)
        lse = jnp.where(lane_q, mxs[0], mxs[1]) + jnp.log(den)
        return pltpu.roll(lse, (LSE_LANES * HEADS_PER_SLAB * m - 48) % LANES, axis=1)

    lane = lax.broadcasted_iota(jnp.int32, (WIN_STEPS, LANES), 1)
    lse_c = None
    pending = {0: scores(0), 1: scores(1)}
    for m in range(N_SLABS):
        if m + 2 < N_SLABS:
            pending[m + 2] = scores(m + 2)
        part = finish(m, pending.pop(m))
        lse_c = part if lse_c is None else jnp.where(lane >= LSE_LANES * HEADS_PER_SLAB * m, part, lse_c)
    l_ref[at + (slice(None),)] = lse_c


def _attn_kernel(q_ref, kvc_ref, kvp_ref, bias_ref, o_ref, l_ref, *, ns):
    d = q_ref.shape[1]
    first_tile = pl.program_id(1) == 0

    def body(idx, carry):
        rr = idx // ns
        j = idx % ns
        rc = pl.ds(pl.multiple_of(j * WIN_STEPS, WIN_STEPS), WIN_STEPS)
        q = q_ref[0, rr, rc, :] * ATT_SCALE
        kv_prev = kvp_ref[0, rr]
        if ns > 1:
            rp = pl.ds(pl.multiple_of(jnp.maximum(j - 1, 0) * WIN_STEPS, WIN_STEPS), WIN_STEPS)
            kv_prev = jnp.where(j == 0, kv_prev, kvc_ref[0, rr, rp, :])
        kv = jnp.concatenate([kv_prev, kvc_ref[0, rr, rc, :]], axis=0)
        prev_mask = jnp.where(first_tile & (j == 0), NEG, 0.0).astype(F32)
        _attn_unit(q, kv, bias_ref, prev_mask, o_ref, l_ref, (0, rr, rc))
        return carry

    lax.fori_loop(0, d * ns, body, 0, unroll=2)


def _attn_call(q, kv, bias, name):
    B, d, U, _ = q.shape
    ns = ATT_TILE // (WIN_STEPS * d)
    rows = ns * WIN_STEPS
    blk = lambda width: pl.BlockSpec((1, d, rows, width), lambda b, t: (b, 0, t, 0))
    return pl.pallas_call(
        functools.partial(_attn_kernel, ns=ns),
        grid=(B, U // rows),
        in_specs=[blk(ATT_WIDTH), blk(2 * ATT_WIDTH),
                  pl.BlockSpec((1, d, WIN_STEPS, 2 * ATT_WIDTH),
                               lambda b, t: (b, 0, jnp.maximum(t * ns - 1, 0), 0)),
                  pl.BlockSpec(bias.shape, lambda b, t: (0, 0, 0))],
        out_specs=[blk(ATT_WIDTH), blk(LANES)],
        out_shape=[jax.ShapeDtypeStruct((B, d, U, ATT_WIDTH), BF16),
                   jax.ShapeDtypeStruct((B, d, U, LANES), F32)],
        compiler_params=_params(("arbitrary", "arbitrary")),
        name=name,
    )(q, kv, kv, bias)


def _stride_bias(rel_table, g, d):
    bucket = _t5_bucket(np.arange(WIN_STEPS + 1) * d)
    onehot = jnp.asarray(np.eye(N_BUCKETS, dtype=np.float32)[bucket])
    tbl = rel_table[:, g * ATT_HEADS:(g + 1) * ATT_HEADS].astype(F32)
    return jnp.dot(onehot, tbl, precision=lax.Precision.HIGHEST).T


def _prompt_bias(rel_table, g, d):
    vals = _stride_bias(rel_table, g, d)
    n = WIN_STEPS
    period = 3 * n
    wp = jnp.concatenate([jnp.full((ATT_HEADS, n - 1), NEG, F32), vals[:, ::-1],
                          jnp.full((ATT_HEADS, n), NEG, F32)], axis=1)
    flat = jnp.tile(wp, (1, n))[:, :n * (period - 1)]
    return flat.reshape(ATT_HEADS, n, period - 1)[:, :, n - 1:n - 1 + 2 * n]


SHIFT_ROWS = 64


def _sattn_kernel(*refs, T):
    q_refs = refs[0:3]
    kvnew_refs = refs[3:6]
    cache_refs = refs[6:9]
    bc_refs = refs[9:12]
    bn_refs = refs[12:15]
    o_ref = refs[15]
    co_refs = refs[16:19]
    kvn_refs = refs[19:22]
    HT = ATT_HEADS * T
    dn = (((1,), (1,)), ((), ()))
    row_head = lax.broadcasted_iota(jnp.int32, (HT, ATT_WIDTH), 0) // T
    col_head = lax.broadcasted_iota(jnp.int32, (HT, ATT_WIDTH), 1) // ATT_HEAD_DIM
    head_mask = row_head == col_head

    for g in range(N_GROUPS):
        rows = jnp.concatenate([jnp.zeros((LANES - T, 2 * ATT_WIDTH), F32), kvnew_refs[g][0]], axis=0)
        kvn_refs[g][0] = rows.T

    stats = []
    for g in range(N_GROUPS):
        q = q_refs[g][0] * ATT_SCALE
        qexp = jnp.where(head_mask, jnp.concatenate([q] * ATT_HEADS, axis=0), 0.0).astype(BF16)
        kn = kvn_refs[g][0, :ATT_WIDTH, :].astype(BF16)
        kc = cache_refs[g][0, :ATT_WIDTH, :].astype(BF16)
        lc = jnp.dot(qexp, kc, preferred_element_type=F32) + bc_refs[g][...]
        ln = jnp.dot(qexp, kn, preferred_element_type=F32) + bn_refs[g][...]
        mx = jnp.maximum(jnp.max(lc, -1, keepdims=True), jnp.max(ln, -1, keepdims=True))
        pc = jnp.exp(lc - mx)
        pn = jnp.exp(ln - mx)
        ssum = jnp.sum(pc, -1, keepdims=True) + jnp.sum(pn, -1, keepdims=True)
        stats.append((pc, pn, ssum, mx + jnp.log(ssum)))

    lse_max = jnp.maximum(jnp.maximum(stats[0][3], stats[1][3]), stats[2][3])
    es = [jnp.exp(st[3] - lse_max) for st in stats]
    esum = es[0] + es[1] + es[2]
    acc = jnp.zeros((HT, ATT_WIDTH), F32)
    for g in range(N_GROUPS):
        pc, pn, ssum, _ = stats[g]
        w = es[g] / (esum * ssum)
        vc = cache_refs[g][0, ATT_WIDTH:, :].astype(BF16)
        vn = kvn_refs[g][0, ATT_WIDTH:, :].astype(BF16)
        acc = acc + lax.dot_general((pc * w).astype(BF16), vc, dn, preferred_element_type=F32)
        acc = acc + lax.dot_general((pn * w).astype(BF16), vn, dn, preferred_element_type=F32)
    lane_head = lax.broadcasted_iota(jnp.int32, (T, ATT_WIDTH), 1) // ATT_HEAD_DIM
    o = jnp.zeros((T, ATT_WIDTH), F32)
    for h in range(ATT_HEADS):
        o = o + jnp.where(lane_head == h, acc[h * T:(h + 1) * T, :], 0.0)
    o_ref[0] = o

    for g in range(N_GROUPS):
        L = cache_refs[g].shape[2]
        nrows = cache_refs[g].shape[1]
        is_new = lax.broadcasted_iota(jnp.int32, (SHIFT_ROWS, LANES), 1) >= LANES - T

        def shift_rows(i, carry, g=g, L=L, is_new=is_new):
            rs = pl.ds(pl.multiple_of(i * SHIFT_ROWS, SHIFT_ROWS), SHIFT_ROWS)
            rolled = pltpu.roll(cache_refs[g][0, rs, :], L - T, axis=1)
            if L > LANES:
                co_refs[g][0, rs, 0:L - LANES] = rolled[:, 0:L - LANES]
            co_refs[g][0, rs, L - LANES:L] = jnp.where(is_new, kvn_refs[g][0, rs, :], rolled[:, L - LANES:])
            return carry

        lax.fori_loop(0, nrows // SHIFT_ROWS, shift_rows, 0)


def _sample_bias(rel_table, g, W, d, Lb, T):
    vals = _stride_bias(rel_table, g, d)
    n = Lb + T
    by_dist = jnp.concatenate([vals[:, :, None], jnp.full((ATT_HEADS, WIN_STEPS + 1, d - 1), NEG, F32)],
                              axis=2).reshape(ATT_HEADS, (WIN_STEPS + 1) * d)
    if by_dist.shape[1] < n:
        by_dist = jnp.concatenate([by_dist, jnp.full((ATT_HEADS, n - by_dist.shape[1]), NEG, F32)], axis=1)
    rev = jnp.concatenate([by_dist[:, :n][:, ::-1], jnp.full((ATT_HEADS, T), NEG, F32)], axis=1)
    rows = jnp.stack([rev[:, T - 1 - t:T - 1 - t + n] for t in range(T)], axis=1)
    rows = rows.reshape(ATT_HEADS * T, n)
    bc = rows[:, :Lb]
    bn = jnp.concatenate([jnp.full((ATT_HEADS * T, LANES - T), NEG, F32), rows[:, Lb:]], axis=1)
    return bc, bn


def _sattn_call(qs, kvns, caches, bcs, bns, T):
    DB = caches[0].shape[0]
    rows = caches[0].shape[1]
    in_specs = ([pl.BlockSpec((1, T, ATT_WIDTH), lambda b: (b, 0, 0))] * 3
                + [pl.BlockSpec((1,) + x.shape[1:], lambda b: (b, 0, 0)) for x in kvns]
                + [pl.BlockSpec((1,) + c.shape[1:], lambda b: (b, 0, 0)) for c in caches]
                + [pl.BlockSpec(x.shape, lambda b: (0, 0)) for x in bcs]
                + [pl.BlockSpec(x.shape, lambda b: (0, 0)) for x in bns])
    out_specs = ([pl.BlockSpec((1, T, ATT_WIDTH), lambda b: (b, 0, 0))]
                 + [pl.BlockSpec((1,) + c.shape[1:], lambda b: (b, 0, 0)) for c in caches])
    out_shape = ([jax.ShapeDtypeStruct((DB, T, ATT_WIDTH), F32)]
                 + [jax.ShapeDtypeStruct(c.shape, F32) for c in caches])
    return pl.pallas_call(
        functools.partial(_sattn_kernel, T=T),
        grid=(DB,),
        in_specs=in_specs, out_specs=out_specs, out_shape=out_shape,
        scratch_shapes=[pltpu.VMEM((1, rows, LANES), F32)] * N_GROUPS,
        compiler_params=_params(("arbitrary",)),
        name="sample_attn",
    )(*qs, *kvns, *caches, *bcs, *bns)


def _scan_rows(x, op, fill):
    n = x.shape[0]
    rowid = lax.broadcasted_iota(jnp.int32, x.shape, 0)
    s = 1
    while s < n:
        shifted = pltpu.roll(x, s, axis=0)
        x = op(x, jnp.where(rowid >= s, shifted, fill))
        s *= 2
    return x


def _pad_rows(x, n, fill=0.0):
    if x.shape[0] == n:
        return x
    return jnp.concatenate([x, jnp.full((n - x.shape[0],) + x.shape[1:], fill, x.dtype)], axis=0)


def _mlstm_kernel(xm_ref, zm_ref, om_ref, g_ref, cprev_ref, C0_ref, n0_ref, m0_ref,
                  convw_ref, convb_ref, wq_ref, wk_ref, mnorm_ref, mskip_ref, bif_ref,
                  mo_ref, C_ref, n_ref, m_ref, xc_ref, *, L):
    LS = max(L, LANES)
    DV = C_ref.shape[2]
    DK = C_ref.shape[3]
    c = pl.program_id(1)

    @pl.when(c == 0)
    def _():
        xc_ref[0:SUBLANES, :] = cprev_ref[0]
        C_ref[...] = C0_ref[...]
        n_ref[...] = n0_ref[...]
        m_ref[...] = m0_ref[...]

    xm_b = xm_ref[0].astype(BF16)
    xc_ref[SUBLANES:SUBLANES + L, :] = xm_ref[0].astype(F32)
    conv = convb_ref[...]
    for j in range(CONV_WIDTH):
        off = SUBLANES - (CONV_WIDTH - 1) + j
        conv = conv + convw_ref[j:j + 1, :] * xc_ref[off:off + L, :]
    xc_ref[0:SUBLANES, :] = xc_ref[L:L + SUBLANES, :]
    c_act = _silu(conv)
    cb = c_act.astype(BF16)

    gates = g_ref[0] + bif_ref[...]
    i_pre = gates[:, :LANES]
    logf = jax.nn.log_sigmoid(gates[:, LANES:])
    b = _scan_rows(logf, jnp.add, 0.0)
    a = i_pre - b
    ca = _scan_rows(a, jnp.maximum, NEG)
    m_prev = m_ref[0]
    mm = jnp.maximum(ca, m_prev)
    u = -mm
    w_inter = jnp.exp(u + m_prev)
    emt = jnp.exp(-(b + mm))
    bL = b[L - 1:L, :]
    m_new = bL + jnp.maximum(m_prev, ca[L - 1:L, :])
    wk = jnp.exp(bL + a - m_new)
    wC = jnp.exp(bL + m_prev - m_new)
    m_ref[0] = m_new

    t_id = lax.broadcasted_iota(jnp.int32, (L, LS), 0)
    s_id = lax.broadcasted_iota(jnp.int32, (L, LS), 1)
    causal = s_id <= t_id
    eye = s_id == t_id
    dn_t = (((1,), (1,)), ((), ()))

    heads = range(M_HEADS)
    vsl = [slice(h * DV, (h + 1) * DV) for h in heads]
    qs, ks, vss = [], [], []
    for h in heads:
        ch = cb[:, vsl[h]]
        qs.append((jnp.dot(ch, wq_ref[h], preferred_element_type=F32) * (DK ** -0.5)).astype(BF16))
        ks.append(_pad_rows(jnp.dot(ch, wk_ref[h], preferred_element_type=F32).astype(BF16), LS))
        vss.append(_pad_rows(xm_b[:, vsl[h]], LS))
    C_old = [C_ref[0, h] for h in heads]
    n_old = [n_ref[0, h:h + 1, :] for h in heads]
    scs, inters = [], []
    for h in heads:
        a_row = jnp.sum(jnp.where(eye, a[:, h:h + 1], 0.0), axis=0, keepdims=True)
        w_intra = jnp.exp(jnp.where(causal, u[:, h:h + 1] + a_row, NEG))
        scs.append(lax.dot_general(qs[h], ks[h], dn_t, preferred_element_type=F32) * w_intra)
        inters.append(lax.dot_general(qs[h], C_old[h].astype(BF16), dn_t, preferred_element_type=F32))
    for h in heads:
        wi = w_inter[:, h:h + 1]
        num = jnp.dot(scs[h].astype(BF16), vss[h], preferred_element_type=F32) + wi * inters[h]
        den = (jnp.sum(scs[h], -1, keepdims=True)
               + wi * jnp.sum(qs[h].astype(F32) * n_old[h], -1, keepdims=True))
        hcell = num / jnp.maximum(jnp.abs(den), emt[:, h:h + 1])
        mu = jnp.mean(hcell, -1, keepdims=True)
        hc = hcell - mu
        hn = hc * lax.rsqrt(jnp.mean(hc * hc, -1, keepdims=True) + EPS)
        out = ((om_ref[0, :, vsl[h]].astype(F32) * (hn * mnorm_ref[:, vsl[h]])
                + mskip_ref[:, vsl[h]] * c_act[:, vsl[h]]) * zm_ref[0, :, vsl[h]].astype(F32))
        mo_ref[0, :, vsl[h]] = out.astype(mo_ref.dtype)
    for h in heads:
        wk_s = _pad_rows(wk[:, h:h + 1], LS)
        vw = (vss[h].astype(F32) * wk_s).astype(BF16)
        wc = wC[:, h:h + 1]
        C_ref[0, h] = wc * C_old[h] + lax.dot_general(vw, ks[h], (((0,), (0,)), ((), ())),
                                                      preferred_element_type=F32)
        n_ref[0, h:h + 1, :] = wc * n_old[h] + jnp.sum(ks[h].astype(F32) * wk_s, axis=0, keepdims=True)


def _mlstm_call(xm, zm, om, gates, conv_prev, C0, n0, m0, convw, convb, wq, wk, mnorm, mskip, bif,
                *, L, out_dtype, name):
    N, S, M = xm.shape
    nc = S // L
    H, DV, DK = C0.shape[1:]
    seq = lambda b, c: (b, c, 0)
    per_b3 = lambda b, c: (b, 0, 0)
    per_b4 = lambda b, c: (b, 0, 0, 0)
    const2 = lambda b, c: (0, 0)
    const3 = lambda b, c: (0, 0, 0)
    return pl.pallas_call(
        functools.partial(_mlstm_kernel, L=L),
        grid=(N, nc),
        in_specs=[pl.BlockSpec((1, L, M), seq), pl.BlockSpec((1, L, M), seq), pl.BlockSpec((1, L, M), seq),
                  pl.BlockSpec((1, L, 2 * LANES), seq),
                  pl.BlockSpec((1, SUBLANES, M), per_b3),
                  pl.BlockSpec((1, H, DV, DK), per_b4),
                  pl.BlockSpec((1, H, DK), per_b3),
                  pl.BlockSpec((1, 1, LANES), per_b3),
                  pl.BlockSpec(convw.shape, const2), pl.BlockSpec(convb.shape, const2),
                  pl.BlockSpec(wq.shape, const3), pl.BlockSpec(wk.shape, const3),
                  pl.BlockSpec(mnorm.shape, const2), pl.BlockSpec(mskip.shape, const2),
                  pl.BlockSpec(bif.shape, const2)],
        out_specs=[pl.BlockSpec((1, L, M), seq),
                   pl.BlockSpec((1, H, DV, DK), per_b4),
                   pl.BlockSpec((1, H, DK), per_b3),
                   pl.BlockSpec((1, 1, LANES), per_b3)],
        out_shape=[jax.ShapeDtypeStruct((N, S, M), out_dtype),
                   jax.ShapeDtypeStruct((N, H, DV, DK), F32),
                   jax.ShapeDtypeStruct((N, H, DK), F32),
                   jax.ShapeDtypeStruct((N, 1, LANES), F32)],
        scratch_shapes=[pltpu.VMEM((L + 2 * SUBLANES, M), F32)],
        compiler_params=_params(("arbitrary", "arbitrary")),
        name=name,
    )(xm, zm, om, gates, conv_prev, C0, n0, m0, convw, convb, wq, wk, mnorm, mskip, bif)


def _token_order(src_ref, scr_ref):
    _, d, n, width = src_ref.shape
    if d == 1:
        return src_ref[0, 0].astype(F32)
    n_tiles = width // LANES
    for r in range(d):
        for s in range(n_tiles):
            scr_ref[s, pl.ds(r, n, stride=d), :] = src_ref[0, r, :, s * LANES:(s + 1) * LANES].astype(F32)
    return jnp.concatenate([scr_ref[s] for s in range(n_tiles)], axis=1)


def _post_kernel(*refs, merge):
    if merge:
        (x_ref, gate_ref, o0, o1, o2, l0, l1, l2, expand_ref, za_ref, mo_ref, ga_ref, gm_ref,
         wpa_ref, wpm_ref, wout_ref, fg_ref, y_ref, scr_ref) = refs
        ls = [_token_order(l, scr_ref) for l in (l0, l1, l2)]
        lmax = jnp.maximum(jnp.maximum(ls[0], ls[1]), ls[2])
        es = [jnp.exp(l - lmax) for l in ls]
        inv = 1.0 / (es[0] + es[1] + es[2])
        o_att = None
        for e, o in zip(es, (o0, o1, o2)):
            a = e * inv
            hi = a.astype(BF16)
            lo = (a - hi.astype(F32)).astype(BF16)
            a_wide = jnp.dot(jnp.concatenate([hi, lo], axis=1), expand_ref[...],
                             preferred_element_type=F32)
            term = a_wide * _token_order(o, scr_ref)
            o_att = term if o_att is None else o_att + term
    else:
        (x_ref, gate_ref, oa_ref, za_ref, mo_ref, ga_ref, gm_ref,
         wpa_ref, wpm_ref, wout_ref, fg_ref, y_ref) = refs
        o_att = oa_ref[0]
    a_in = (o_att * za_ref[0].astype(F32)).astype(BF16)
    a_br = jnp.dot(a_in, wpa_ref[...], preferred_element_type=F32)
    m_br = jnp.dot(mo_ref[0].astype(BF16), wpm_ref[...], preferred_element_type=F32)
    merged = ga_ref[0].astype(F32) * a_br + gm_ref[0].astype(F32) * m_br
    y = x_ref[0] + gate_ref[0] * jnp.dot(merged.astype(BF16), wout_ref[...], preferred_element_type=F32)
    ms = jnp.mean(y * y, axis=-1, keepdims=True)
    y_ref[0] = y * lax.rsqrt(ms + EPS) * fg_ref[...]


def _post_call(x3, gate3, att_inputs, za, mo, ga, gm, wpa, wpm, wout, fgain, *, tm, merge, name):
    B, S, D = x3.shape
    row = lambda b, i: (b, i, 0)
    const2 = lambda b, i: (0, 0)
    if gate3.shape[1] == 1:
        gate_spec = pl.BlockSpec((1, 1, D), lambda b, i: (b, 0, 0))
    else:
        gate_spec = pl.BlockSpec((1, tm, D), row)
    def blk(a):
        if a.ndim == 2:
            return pl.BlockSpec(a.shape, const2)
        if a.ndim == 4:
            d = a.shape[1]
            return pl.BlockSpec((1, d, tm // d, a.shape[3]), lambda b, i: (b, 0, i, 0))
        return pl.BlockSpec((1, tm, a.shape[2]), row)

    in_specs = ([pl.BlockSpec((1, tm, D), row), gate_spec]
                + [blk(a) for a in att_inputs]
                + [blk(za), blk(mo), blk(ga), blk(gm),
                   pl.BlockSpec(wpa.shape, const2), pl.BlockSpec(wpm.shape, const2),
                   pl.BlockSpec(wout.shape, const2), pl.BlockSpec(fgain.shape, const2)])
    scratch = [pltpu.VMEM((ATT_WIDTH // LANES, tm, LANES), F32)] if merge else []
    return pl.pallas_call(
        functools.partial(_post_kernel, merge=merge),
        grid=(B, S // tm),
        in_specs=in_specs,
        out_specs=pl.BlockSpec((1, tm, D), row),
        out_shape=jax.ShapeDtypeStruct((B, S, D), F32),
        scratch_shapes=scratch,
        compiler_params=_params(("arbitrary", "arbitrary")),
        name=name,
    )(x3, gate3, *att_inputs, za, mo, ga, gm, wpa, wpm, wout, fgain)


def _projection_weights(w_in):
    D = w_in.shape[0]
    AW = ATT_WIDTH
    M = D
    off_k, off_v, off_za = 3 * AW, 6 * AW, 9 * AW
    off_xm = off_za + AW
    off_i = off_xm + 3 * M
    off_f = off_i + M_HEADS
    off_ga = off_f + M_HEADS
    w_main = w_in[:, :off_i].astype(BF16)
    pad = jnp.zeros((D, LANES - M_HEADS), BF16)
    w_tail = jnp.concatenate([w_in[:, off_ga:off_ga + 2 * D].astype(BF16),
                              w_in[:, off_i:off_f].astype(BF16), pad,
                              w_in[:, off_f:off_ga].astype(BF16), pad], axis=1)
    segs = {"za": ((0, off_za, AW),), "xm": ((0, off_xm, M),), "zm": ((0, off_xm + M, M),),
            "om": ((0, off_xm + 2 * M, M),), "ga": ((1, 0, D),), "gm": ((1, D, D),),
            "gates": ((1, 2 * D, 2 * LANES),)}
    for g in range(N_GROUPS):
        segs[f"q{g}"] = ((0, g * AW, AW),)
        segs[f"kv{g}"] = ((0, off_k + g * AW, AW), (0, off_v + g * AW, AW))
    return (w_main, w_tail), segs


def kernel(x_prompt, x_sample, cache_kv_w128, cache_kv_w512, cache_kv_w2048, state_conv, state_C, state_n, state_m, c_prompt, c_sample, rel_table, norm_gain, w_ada, b_ada, w_in, b_if, conv_w, conv_b, w_mq, w_mk, m_norm, m_skip, w_pa, w_pm, w_out, final_gain):
    B, S, D = x_prompt.shape
    DB, T, _ = x_sample.shape
    assert norm_gain.shape[0] == 1, "single-layer trunk"
    assert S % ATT_TILE == 0 and S % MLSTM_CHUNK == 0 and T == SUBLANES
    caches = (cache_kv_w128[0], cache_kv_w512[0], cache_kv_w2048[0])
    H = M_HEADS
    M = conv_w.shape[2]

    wp, segs = _projection_weights(w_in[0])
    names = ("q0", "q1", "q2", "za", "kv0", "kv1", "kv2", "xm", "zm", "om", "ga", "gm", "gates")
    seg_list = [segs[n] for n in names]
    gain = norm_gain[0].reshape(1, D)
    fgain = final_gain.reshape(1, D)
    wpa, wpm, wout = w_pa[0].astype(BF16), w_pm[0].astype(BF16), w_out[0].astype(BF16)
    wq, wk = w_mq[0].astype(BF16), w_mk[0].astype(BF16)
    convw, convb = conv_w[0], conv_b[0].reshape(1, M)
    mnorm, mskip = m_norm[0].reshape(1, M), m_skip[0].reshape(1, M)
    zpad = jnp.zeros((LANES - H,), F32)
    bif = jnp.concatenate([b_if[0, :H], zpad, b_if[0, H:], zpad]).reshape(1, 2 * LANES)

    ada = _ada_call(jnp.concatenate([c_prompt, c_sample], axis=0), w_ada[0], b_ada[0])
    shift, scale, gate = ada[:, :D], ada[:, D:2 * D], ada[:, 2 * D:]

    p_shift, p_scale, p_gate = (t[:B].reshape(B, 1, D) for t in (shift, scale, gate))
    dts = [BF16] * 12 + [F32]
    group_dil = {f"{p}{g}": dil for g, (_, dil) in enumerate(ATT_GROUPS) for p in ("q", "kv")}
    gate_act = {"za": "silu", "zm": "silu", "om": "sigmoid", "ga": "sigmoid", "gm": "sigmoid"}
    acts = [gate_act.get(n) for n in names]
    pr = dict(zip(names, _proj_call(x_prompt, gain, p_scale, p_shift, wp, seg_list, dts,
                                    tm=PROJ_TM, row0=0, rows=S, name="proj_prompt",
                                    dils=[group_dil.get(n, 1) for n in names], acts=acts)))
    att = []
    for g, (win, dil) in enumerate(ATT_GROUPS):
        planes = lambda a: a if a.ndim == 4 else a[:, None]
        att.append(_attn_call(planes(pr[f"q{g}"]), planes(pr[f"kv{g}"]), _prompt_bias(rel_table, g, dil),
                              f"attn_prompt_g{g}"))
    expand = np.zeros((2 * LANES, ATT_WIDTH), np.float32)
    for h in range(ATT_HEADS):
        expand[[LSE_LANES * h, LANES + LSE_LANES * h], h * ATT_HEAD_DIM:(h + 1) * ATT_HEAD_DIM] = 1.0
    expand = jnp.asarray(expand, BF16)
    mo_p, C_p, n_p, m_p = _mlstm_call(
        pr["xm"], pr["zm"], pr["om"], pr["gates"],
        jnp.zeros((B, SUBLANES, M), F32), jnp.zeros((B,) + state_C.shape[2:], F32),
        jnp.zeros((B,) + state_n.shape[2:], F32), jnp.zeros((B, 1, LANES), F32),
        convw, convb, wq, wk, mnorm, mskip, bif, L=MLSTM_CHUNK, out_dtype=BF16, name="mlstm_prompt")
    y_prompt = _post_call(x_prompt, p_gate, [a[0] for a in att] + [a[1] for a in att] + [expand],
                          pr["za"], mo_p, pr["ga"], pr["gm"], wpa, wpm, wout, fgain,
                          tm=POST_TM, merge=True, name="post_prompt")
    kv_p = []
    w_max = min(ATT_GROUPS[-1][0], S)
    (kv2_t,) = _proj_call(x_prompt, gain, p_scale, p_shift, wp[:1], [segs["kv2"]], [F32],
                          tm=PROJ_TM, row0=S - w_max, rows=w_max, name="tail_kv2")
    w_mid = min(ATT_GROUPS[1][0], S)
    kv0_t, kv1_t, xm_t = _proj_call(x_prompt, gain, p_scale, p_shift, wp[:1],
                                    [segs["kv0"], segs["kv1"], segs["xm"]], [F32] * 3,
                                    tm=w_mid, row0=S - w_mid, rows=w_mid, name="tail_kv01")
    kshape = lambda L: (1, B, L, 2, ATT_HEADS, ATT_HEAD_DIM)
    w0 = min(ATT_GROUPS[0][0], S)
    kv_p = [kv0_t[:, w_mid - w0:].reshape(kshape(w0)), kv1_t.reshape(kshape(w_mid)), kv2_t.reshape(kshape(w_max))]
    conv_p = xm_t[:, w_mid - (CONV_WIDTH - 1):][None]

    R = DB * T
    rep = lambda t: jnp.repeat(t[B:], T, axis=0).reshape(1, R, D)
    s_shift, s_scale, s_gate = rep(shift), rep(scale), rep(gate)
    xs = x_sample.reshape(1, R, D)
    sr = dict(zip(names, _proj_call(xs, gain, s_scale, s_shift, wp, seg_list, [F32] * 13,
                                    tm=R, row0=0, rows=R, name="proj_sample", acts=acts)))
    bcs, bns = [], []
    for g, (win, dil) in enumerate(ATT_GROUPS):
        bc, bn = _sample_bias(rel_table, g, win, dil, caches[g].shape[1], T)
        bcs.append(bc)
        bns.append(bn)
    cache_t = [jnp.transpose(c, (0, 2, 3, 4, 1)).reshape(DB, 2 * ATT_WIDTH, c.shape[1]) for c in caches]
    kvn_t = [sr[f"kv{g}"].reshape(DB, T, 2 * ATT_WIDTH) for g in range(3)]
    sa = _sattn_call([sr[f"q{g}"].reshape(DB, T, ATT_WIDTH) for g in range(3)],
                     kvn_t, cache_t, bcs, bns, T)
    o_att_s = sa[0].reshape(1, R, ATT_WIDTH)
    kv_s = [jnp.transpose(c.reshape(DB, 2, ATT_HEADS, ATT_HEAD_DIM, c.shape[2]), (0, 4, 1, 2, 3))[None]
            for c in sa[1:]]
    conv_prev_s = jnp.concatenate([jnp.zeros((DB, SUBLANES - (CONV_WIDTH - 1), M), F32), state_conv[0]], axis=1)
    m0_s = jnp.concatenate([state_m[0], jnp.zeros((DB, LANES - H), F32)], axis=1).reshape(DB, 1, LANES)
    seqv = lambda t: t.reshape(DB, T, t.shape[-1])
    mo_s, C_s, n_s, m_s = _mlstm_call(
        seqv(sr["xm"]), seqv(sr["zm"]), seqv(sr["om"]), seqv(sr["gates"]),
        conv_prev_s, state_C[0], state_n[0], m0_s,
        convw, convb, wq, wk, mnorm, mskip, bif, L=T, out_dtype=F32, name="mlstm_sample")
    y_sample = _post_call(xs, s_gate, [o_att_s], sr["za"], mo_s.reshape(1, R, M), sr["ga"], sr["gm"],
                          wpa, wpm, wout, fgain, tm=R, merge=False, name="post_sample")
    conv_s = seqv(sr["xm"])[:, T - (CONV_WIDTH - 1):][None]

    return (y_prompt, y_sample.reshape(DB, T, D),
            kv_p[0], kv_s[0], kv_p[1], kv_s[1], kv_p[2], kv_s[2],
            conv_p, conv_s, C_p[None], C_s[None], n_p[None], n_s[None],
            m_p[:, 0, :H][None], m_s[:, 0, :H][None])
```

```python
import functools

import numpy as np
import jax
import jax.numpy as jnp
from jax import lax
from jax.experimental import pallas as pl
from jax.experimental.pallas import tpu as pltpu

F32 = jnp.float32
BF16 = jnp.bfloat16

ATT_GROUPS = ((128, 1), (512, 4), (2048, 16))
N_GROUPS = len(ATT_GROUPS)
ATT_HEADS = 8
ATT_HEAD_DIM = 64
ATT_WIDTH = ATT_HEADS * ATT_HEAD_DIM
WIN_STEPS = 128
ATT_SCALE = ATT_HEAD_DIM ** -0.5
N_BUCKETS = 32
MAX_DISTANCE = 2048
M_HEADS = 4
CONV_WIDTH = 4
EPS = 1e-6
NEG = -1e30

LANES = 128
SUBLANES = 8
VMEM_LIMIT = 56 * 1024 * 1024

ATT_TILE = 2048
PROJ_TM = 512
POST_TM = 512
MLSTM_CHUNK = 256


def _params(sem, vmem=VMEM_LIMIT):
    return pltpu.CompilerParams(dimension_semantics=sem, vmem_limit_bytes=vmem)


def _t5_bucket(dist):
    n = np.asarray(dist).astype(np.int64)
    max_exact = N_BUCKETS // 2
    nf = np.maximum(n, 1).astype(np.float32)
    large = max_exact + (np.log(nf / max_exact) / np.log(np.float32(MAX_DISTANCE / max_exact))
                         * (N_BUCKETS - max_exact)).astype(np.int64)
    large = np.minimum(large, N_BUCKETS - 1)
    return np.where(n < max_exact, n, large).astype(np.int32)


def _silu(x):
    return x * jax.nn.sigmoid(x)


def _ada_kernel(c_ref, w_ref, b_ref, o_ref):
    s = _silu(c_ref[...])
    o_ref[...] = jnp.dot(s, w_ref[...], preferred_element_type=F32,
                         precision=lax.Precision.HIGHEST) + b_ref[...]


def _ada_call(c, w, b):
    n, d = c.shape
    width = w.shape[1]
    tn = 512
    return pl.pallas_call(
        _ada_kernel,
        grid=(width // tn,),
        in_specs=[pl.BlockSpec((n, d), lambda j: (0, 0)),
                  pl.BlockSpec((d, tn), lambda j: (0, j)),
                  pl.BlockSpec((1, tn), lambda j: (0, j))],
        out_specs=pl.BlockSpec((n, tn), lambda j: (0, j)),
        out_shape=jax.ShapeDtypeStruct((n, width), F32),
        compiler_params=_params(("arbitrary",)),
        name="ada",
    )(c, w, b.reshape(1, width))


_ACTIVATIONS = {None: lambda v: v, "silu": _silu, "sigmoid": jax.nn.sigmoid}


def _proj_kernel(x_ref, gain_ref, scale_ref, shift_ref, *rest, n_w, segs, dils, acts):
    w_refs = rest[:n_w]
    rest = rest[n_w:]
    out_refs = rest[:len(segs)]
    x = x_ref[0]
    tm, D = x.shape
    ms = jnp.mean(x * x, axis=-1, keepdims=True)
    h = x * lax.rsqrt(ms + EPS) * gain_ref[...] * (1.0 + scale_ref[0]) + shift_ref[0]
    lhs = {1: h.astype(BF16)}
    strides = sorted(set(dils) - {1})
    if strides:
        hs_ref = rest[len(segs)]
        n_tiles = D // LANES
        for s in range(n_tiles):
            hs_ref[s] = h[:, s * LANES:(s + 1) * LANES]
        for d in strides:
            n = tm // d
            lhs[d] = jnp.concatenate(
                [jnp.concatenate([hs_ref[s, pl.ds(r, n, stride=d), :] for r in range(d)], axis=0)
                 for s in range(n_tiles)], axis=1).astype(BF16)
    for o_ref, pieces, d, act in zip(out_refs, segs, dils, acts):
        parts = [_ACTIVATIONS[act](jnp.dot(lhs[d], w_refs[wi][:, c0:c0 + width],
                                           preferred_element_type=F32)).astype(o_ref.dtype)
                 for wi, c0, width in pieces]
        res = parts[0] if len(parts) == 1 else jnp.concatenate(parts, axis=1)
        if d == 1:
            o_ref[0] = res
        else:
            n = tm // d
            for r in range(d):
                o_ref[0, r] = res[r * n:(r + 1) * n, :]


def _proj_call(x3, gain, scale3, shift3, ws, segs, dtypes, *, tm, row0, rows, name, dils=None, acts=None):
    B, S, D = x3.shape
    nrb = rows // tm
    rb0 = row0 // tm
    dils = tuple(dils) if dils is not None else (1,) * len(segs)
    acts = tuple(acts) if acts is not None else (None,) * len(segs)
    per_row = scale3.shape[1] != 1
    if per_row:
        mod_spec = pl.BlockSpec((1, tm, D), lambda b, i: (b, rb0 + i, 0))
    else:
        mod_spec = pl.BlockSpec((1, 1, D), lambda b, i: (b, 0, 0))
    out_shape, out_specs = [], []
    for pieces, dt, d in zip(segs, dtypes, dils):
        wd = sum(p[2] for p in pieces)
        if d == 1:
            out_shape.append(jax.ShapeDtypeStruct((B, rows, wd), dt))
            out_specs.append(pl.BlockSpec((1, tm, wd), lambda b, i: (b, i, 0)))
        else:
            out_shape.append(jax.ShapeDtypeStruct((B, d, rows // d, wd), dt))
            out_specs.append(pl.BlockSpec((1, d, tm // d, wd), lambda b, i: (b, 0, i, 0)))
    scratch = [pltpu.VMEM((D // LANES, tm, LANES), F32)] if any(d > 1 for d in dils) else []
    return pl.pallas_call(
        functools.partial(_proj_kernel, n_w=len(ws), segs=tuple(segs), dils=dils, acts=acts),
        grid=(B, nrb),
        in_specs=[pl.BlockSpec((1, tm, D), lambda b, i: (b, rb0 + i, 0)),
                  pl.BlockSpec((1, D), lambda b, i: (0, 0)),
                  mod_spec, mod_spec]
                 + [pl.BlockSpec(w.shape, lambda b, i: (0, 0), pipeline_mode=pl.Buffered(1)) for w in ws],
        out_specs=out_specs,
        out_shape=out_shape,
        scratch_shapes=scratch,
        compiler_params=_params(("arbitrary", "arbitrary")),
        name=name,
    )(x3, gain, scale3, shift3, *ws)


HEADS_PER_SLAB = LANES // ATT_HEAD_DIM
N_SLABS = ATT_HEADS // HEADS_PER_SLAB
LSE_LANES = LANES // ATT_HEADS


def _attn_unit(q, kv, bias_ref, prev_mask, o_ref, l_ref, at):
    dn = (((1,), (1,)), ((), ()))
    nk = 2 * WIN_STEPS
    lane_q = lax.broadcasted_iota(jnp.int32, (WIN_STEPS, LANES), 1) < ATT_HEAD_DIM
    lane_k = lax.broadcasted_iota(jnp.int32, (nk, LANES), 1) < ATT_HEAD_DIM
    ones_lo = jnp.where(lane_k, 1.0, 0.0).astype(BF16)
    ones_hi = jnp.where(lane_k, 0.0, 1.0).astype(BF16)
    zero_q = jnp.zeros((WIN_STEPS, LANES), BF16)
    zero_k = jnp.zeros((nk, LANES), BF16)

    def scores(m):
        cs = slice(m * LANES, (m + 1) * LANES)
        qs = q[:, cs]
        ks = kv[:, cs]
        out = []
        for hh in range(HEADS_PER_SLAB):
            qm = jnp.where(lane_q, qs, zero_q) if hh == 0 else jnp.where(lane_q, zero_q, qs)
            s = lax.dot_general(qm, ks, dn, preferred_element_type=F32) + bias_ref[m * HEADS_PER_SLAB + hh]
            out.append(jnp.concatenate([s[:, :WIN_STEPS] + prev_mask, s[:, WIN_STEPS:]], axis=1))
        return out

    def finish(m, ss):
        cs = slice(m * LANES, (m + 1) * LANES)
        vs = kv[:, ATT_WIDTH + m * LANES:ATT_WIDTH + (m + 1) * LANES]
        ps, mxs = [], []
        for s in ss:
            mx = jnp.max(jnp.maximum(s[:, :WIN_STEPS], s[:, WIN_STEPS:]), -1, keepdims=True)
            ps.append(jnp.exp(s - mx).astype(BF16))
            mxs.append(mx)
        pcat = jnp.concatenate(ps, axis=1)
        vpair = jnp.concatenate(
            [jnp.concatenate([jnp.where(lane_k, vs, zero_k), ones_lo], axis=1),
             jnp.concatenate([jnp.where(lane_k, zero_k, vs), ones_hi], axis=1)], axis=0)
        acc = jnp.dot(pcat, vpair, preferred_element_type=F32)
        den = acc[:, LANES:]
        o_ref[at + (cs,)] = (acc[:, :LANES] / den).astype(o_ref.dtype)
        lse = jnp.where(lane_q, mxs[0], mxs[1]) + jnp.log(den)
        return pltpu.roll(lse, (LSE_LANES * HEADS_PER_SLAB * m - 48) % LANES, axis=1)

    lane = lax.broadcasted_iota(jnp.int32, (WIN_STEPS, LANES), 1)
    lse_c = None
    pending = {0: scores(0), 1: scores(1)}
    for m in range(N_SLABS):
        if m + 2 < N_SLABS:
            pending[m + 2] = scores(m + 2)
        part = finish(m, pending.pop(m))
        lse_c = part if lse_c is None else jnp.where(lane >= LSE_LANES * HEADS_PER_SLAB * m, part, lse_c)
    l_ref[at + (slice(None),)] = lse_c


def _attn_kernel(q_ref, kvc_ref, kvp_ref, bias_ref, o_ref, l_ref, *, ns):
    d = q_ref.shape[1]
    first_tile = pl.program_id(1) == 0

    def body(idx, carry):
        rr = idx // ns
        j = idx % ns
        rc = pl.ds(pl.multiple_of(j * WIN_STEPS, WIN_STEPS), WIN_STEPS)
        q = q_ref[0, rr, rc, :] * ATT_SCALE
        kv_prev = kvp_ref[0, rr]
        if ns > 1:
            rp = pl.ds(pl.multiple_of(jnp.maximum(j - 1, 0) * WIN_STEPS, WIN_STEPS), WIN_STEPS)
            kv_prev = jnp.where(j == 0, kv_prev, kvc_ref[0, rr, rp, :])
        kv = jnp.concatenate([kv_prev, kvc_ref[0, rr, rc, :]], axis=0)
        prev_mask = jnp.where(first_tile & (j == 0), NEG, 0.0).astype(F32)
        _attn_unit(q, kv, bias_ref, prev_mask, o_ref, l_ref, (0, rr, rc))
        return carry

    lax.fori_loop(0, d * ns, body, 0, unroll=2)


def _attn_call(q, kv, bias, name):
    B, d, U, _ = q.shape
    ns = ATT_TILE // (WIN_STEPS * d)
    rows = ns * WIN_STEPS
    blk = lambda width: pl.BlockSpec((1, d, rows, width), lambda b, t: (b, 0, t, 0))
    return pl.pallas_call(
        functools.partial(_attn_kernel, ns=ns),
        grid=(B, U // rows),
        in_specs=[blk(ATT_WIDTH), blk(2 * ATT_WIDTH),
                  pl.BlockSpec((1, d, WIN_STEPS, 2 * ATT_WIDTH),
                               lambda b, t: (b, 0, jnp.maximum(t * ns - 1, 0), 0)),
                  pl.BlockSpec(bias.shape, lambda b, t: (0, 0, 0))],
        out_specs=[blk(ATT_WIDTH), blk(LANES)],
        out_shape=[jax.ShapeDtypeStruct((B, d, U, ATT_WIDTH), BF16),
                   jax.ShapeDtypeStruct((B, d, U, LANES), F32)],
        compiler_params=_params(("arbitrary", "arbitrary")),
        name=name,
    )(q, kv, kv, bias)


def _stride_bias(rel_table, g, d):
    bucket = _t5_bucket(np.arange(WIN_STEPS + 1) * d)
    onehot = jnp.asarray(np.eye(N_BUCKETS, dtype=np.float32)[bucket])
    tbl = rel_table[:, g * ATT_HEADS:(g + 1) * ATT_HEADS].astype(F32)
    return jnp.dot(onehot, tbl, precision=lax.Precision.HIGHEST).T


def _prompt_bias(rel_table, g, d):
    vals = _stride_bias(rel_table, g, d)
    n = WIN_STEPS
    period = 3 * n
    wp = jnp.concatenate([jnp.full((ATT_HEADS, n - 1), NEG, F32), vals[:, ::-1],
                          jnp.full((ATT_HEADS, n), NEG, F32)], axis=1)
    flat = jnp.tile(wp, (1, n))[:, :n * (period - 1)]
    return flat.reshape(ATT_HEADS, n, period - 1)[:, :, n - 1:n - 1 + 2 * n]


SHIFT_ROWS = 64


def _sattn_kernel(*refs, T):
    q_refs = refs[0:3]
    kvnew_refs = refs[3:6]
    cache_refs = refs[6:9]
    bc_refs = refs[9:12]
    bn_refs = refs[12:15]
    o_ref = refs[15]
    co_refs = refs[16:19]
    kvn_refs = refs[19:22]
    HT = ATT_HEADS * T
    dn = (((1,), (1,)), ((), ()))
    row_head = lax.broadcasted_iota(jnp.int32, (HT, ATT_WIDTH), 0) // T
    col_head = lax.broadcasted_iota(jnp.int32, (HT, ATT_WIDTH), 1) // ATT_HEAD_DIM
    head_mask = row_head == col_head

    for g in range(N_GROUPS):
        rows = jnp.concatenate([jnp.zeros((LANES - T, 2 * ATT_WIDTH), F32), kvnew_refs[g][0]], axis=0)
        kvn_refs[g][0] = rows.T

    stats = []
    for g in range(N_GROUPS):
        q = q_refs[g][0] * ATT_SCALE
        qexp = jnp.where(head_mask, jnp.concatenate([q] * ATT_HEADS, axis=0), 0.0).astype(BF16)
        kn = kvn_refs[g][0, :ATT_WIDTH, :].astype(BF16)
        kc = cache_refs[g][0, :ATT_WIDTH, :].astype(BF16)
        lc = jnp.dot(qexp, kc, preferred_element_type=F32) + bc_refs[g][...]
        ln = jnp.dot(qexp, kn, preferred_element_type=F32) + bn_refs[g][...]
        mx = jnp.maximum(jnp.max(lc, -1, keepdims=True), jnp.max(ln, -1, keepdims=True))
        pc = jnp.exp(lc - mx)
        pn = jnp.exp(ln - mx)
        ssum = jnp.sum(pc, -1, keepdims=True) + jnp.sum(pn, -1, keepdims=True)
        stats.append((pc, pn, ssum, mx + jnp.log(ssum)))

    lse_max = jnp.maximum(jnp.maximum(stats[0][3], stats[1][3]), stats[2][3])
    es = [jnp.exp(st[3] - lse_max) for st in stats]
    esum = es[0] + es[1] + es[2]
    acc = jnp.zeros((HT, ATT_WIDTH), F32)
    for g in range(N_GROUPS):
        pc, pn, ssum, _ = stats[g]
        w = es[g] / (esum * ssum)
        vc = cache_refs[g][0, ATT_WIDTH:, :].astype(BF16)
        vn = kvn_refs[g][0, ATT_WIDTH:, :].astype(BF16)
        acc = acc + lax.dot_general((pc * w).astype(BF16), vc, dn, preferred_element_type=F32)
        acc = acc + lax.dot_general((pn * w).astype(BF16), vn, dn, preferred_element_type=F32)
    lane_head = lax.broadcasted_iota(jnp.int32, (T, ATT_WIDTH), 1) // ATT_HEAD_DIM
    o = jnp.zeros((T, ATT_WIDTH), F32)
    for h in range(ATT_HEADS):
        o = o + jnp.where(lane_head == h, acc[h * T:(h + 1) * T, :], 0.0)
    o_ref[0] = o

    for g in range(N_GROUPS):
        L = cache_refs[g].shape[2]
        nrows = cache_refs[g].shape[1]
        is_new = lax.broadcasted_iota(jnp.int32, (SHIFT_ROWS, LANES), 1) >= LANES - T

        def shift_rows(i, carry, g=g, L=L, is_new=is_new):
            rs = pl.ds(pl.multiple_of(i * SHIFT_ROWS, SHIFT_ROWS), SHIFT_ROWS)
            rolled = pltpu.roll(cache_refs[g][0, rs, :], L - T, axis=1)
            if L > LANES:
                co_refs[g][0, rs, 0:L - LANES] = rolled[:, 0:L - LANES]
            co_refs[g][0, rs, L - LANES:L] = jnp.where(is_new, kvn_refs[g][0, rs, :], rolled[:, L - LANES:])
            return carry

        lax.fori_loop(0, nrows // SHIFT_ROWS, shift_rows, 0)


def _sample_bias(rel_table, g, W, d, Lb, T):
    vals = _stride_bias(rel_table, g, d)
    n = Lb + T
    by_dist = jnp.concatenate([vals[:, :, None], jnp.full((ATT_HEADS, WIN_STEPS + 1, d - 1), NEG, F32)],
                              axis=2).reshape(ATT_HEADS, (WIN_STEPS + 1) * d)
    if by_dist.shape[1] < n:
        by_dist = jnp.concatenate([by_dist, jnp.full((ATT_HEADS, n - by_dist.shape[1]), NEG, F32)], axis=1)
    rev = jnp.concatenate([by_dist[:, :n][:, ::-1], jnp.full((ATT_HEADS, T), NEG, F32)], axis=1)
    rows = jnp.stack([rev[:, T - 1 - t:T - 1 - t + n] for t in range(T)], axis=1)
    rows = rows.reshape(ATT_HEADS * T, n)
    bc = rows[:, :Lb]
    bn = jnp.concatenate([jnp.full((ATT_HEADS * T, LANES - T), NEG, F32), rows[:, Lb:]], axis=1)
    return bc, bn


def _sattn_specs(qs, kvns, caches, bcs, bns, T, seq, out_buffering=None):
    DB = caches[0].shape[0]
    rows = caches[0].shape[1]
    per_seq = lambda *g: (seq(*g), 0, 0)
    const2 = lambda *g: (0, 0)
    in_specs = ([pl.BlockSpec((1, T, ATT_WIDTH), per_seq)] * 3
                + [pl.BlockSpec((1,) + x.shape[1:], per_seq) for x in kvns]
                + [pl.BlockSpec((1,) + c.shape[1:], per_seq) for c in caches]
                + [pl.BlockSpec(x.shape, const2) for x in bcs]
                + [pl.BlockSpec(x.shape, const2) for x in bns])
    out_specs = ([pl.BlockSpec((1, T, ATT_WIDTH), per_seq)]
                 + [pl.BlockSpec((1,) + c.shape[1:], per_seq, pipeline_mode=out_buffering) for c in caches])
    out_shape = ([jax.ShapeDtypeStruct((DB, T, ATT_WIDTH), F32)]
                 + [jax.ShapeDtypeStruct(c.shape, F32) for c in caches])
    scratch = [pltpu.VMEM((1, rows, LANES), F32)] * N_GROUPS
    return in_specs, out_specs, out_shape, scratch, (*qs, *kvns, *caches, *bcs, *bns)


def _sattn_call(qs, kvns, caches, bcs, bns, T):
    in_specs, out_specs, out_shape, scratch, args = _sattn_specs(qs, kvns, caches, bcs, bns, T, lambda b: b)
    return pl.pallas_call(
        functools.partial(_sattn_kernel, T=T),
        grid=(caches[0].shape[0],),
        in_specs=in_specs, out_specs=out_specs, out_shape=out_shape,
        scratch_shapes=scratch,
        compiler_params=_params(("arbitrary",)),
        name="sample_attn",
    )(*args)


def _scan_rows(x, op, fill):
    n = x.shape[0]
    rowid = lax.broadcasted_iota(jnp.int32, x.shape, 0)
    s = 1
    while s < n:
        shifted = pltpu.roll(x, s, axis=0)
        x = op(x, jnp.where(rowid >= s, shifted, fill))
        s *= 2
    return x


def _pad_rows(x, n, fill=0.0):
    if x.shape[0] == n:
        return x
    return jnp.concatenate([x, jnp.full((n - x.shape[0],) + x.shape[1:], fill, x.dtype)], axis=0)


def _mlstm_kernel(xm_ref, zm_ref, om_ref, g_ref, cprev_ref, C0_ref, n0_ref, m0_ref,
                  convw_ref, convb_ref, wq_ref, wk_ref, mnorm_ref, mskip_ref, bif_ref,
                  mo_ref, C_ref, n_ref, m_ref, xc_ref, *, L):
    LS = max(L, LANES)
    DV = C_ref.shape[2]
    DK = C_ref.shape[3]
    c = pl.program_id(1)

    @pl.when(c == 0)
    def _():
        xc_ref[0:SUBLANES, :] = cprev_ref[0]
        C_ref[...] = C0_ref[...]
        n_ref[...] = n0_ref[...]
        m_ref[...] = m0_ref[...]

    xm_b = xm_ref[0].astype(BF16)
    xc_ref[SUBLANES:SUBLANES + L, :] = xm_ref[0].astype(F32)
    conv = convb_ref[...]
    for j in range(CONV_WIDTH):
        off = SUBLANES - (CONV_WIDTH - 1) + j
        conv = conv + convw_ref[j:j + 1, :] * xc_ref[off:off + L, :]
    xc_ref[0:SUBLANES, :] = xc_ref[L:L + SUBLANES, :]
    c_act = _silu(conv)
    cb = c_act.astype(BF16)

    gates = g_ref[0] + bif_ref[...]
    i_pre = gates[:, :LANES]
    logf = jax.nn.log_sigmoid(gates[:, LANES:])
    b = _scan_rows(logf, jnp.add, 0.0)
    a = i_pre - b
    ca = _scan_rows(a, jnp.maximum, NEG)
    m_prev = m_ref[0]
    mm = jnp.maximum(ca, m_prev)
    u = -mm
    w_inter = jnp.exp(u + m_prev)
    emt = jnp.exp(-(b + mm))
    bL = b[L - 1:L, :]
    m_new = bL + jnp.maximum(m_prev, ca[L - 1:L, :])
    wk = jnp.exp(bL + a - m_new)
    wC = jnp.exp(bL + m_prev - m_new)
    m_ref[0] = m_new

    t_id = lax.broadcasted_iota(jnp.int32, (L, LS), 0)
    s_id = lax.broadcasted_iota(jnp.int32, (L, LS), 1)
    causal = s_id <= t_id
    eye = s_id == t_id
    dn_t = (((1,), (1,)), ((), ()))

    heads = range(M_HEADS)
    vsl = [slice(h * DV, (h + 1) * DV) for h in heads]
    qs, ks, vss = [], [], []
    for h in heads:
        ch = cb[:, vsl[h]]
        qs.append((jnp.dot(ch, wq_ref[h], preferred_element_type=F32) * (DK ** -0.5)).astype(BF16))
        ks.append(_pad_rows(jnp.dot(ch, wk_ref[h], preferred_element_type=F32).astype(BF16), LS))
        vss.append(_pad_rows(xm_b[:, vsl[h]], LS))
    C_old = [C_ref[0, h] for h in heads]
    n_old = [n_ref[0, h:h + 1, :] for h in heads]
    scs, inters = [], []
    for h in heads:
        a_row = jnp.sum(jnp.where(eye, a[:, h:h + 1], 0.0), axis=0, keepdims=True)
        w_intra = jnp.exp(jnp.where(causal, u[:, h:h + 1] + a_row, NEG))
        scs.append(lax.dot_general(qs[h], ks[h], dn_t, preferred_element_type=F32) * w_intra)
        inters.append(lax.dot_general(qs[h], C_old[h].astype(BF16), dn_t, preferred_element_type=F32))
    for h in heads:
        wi = w_inter[:, h:h + 1]
        num = jnp.dot(scs[h].astype(BF16), vss[h], preferred_element_type=F32) + wi * inters[h]
        den = (jnp.sum(scs[h], -1, keepdims=True)
               + wi * jnp.sum(qs[h].astype(F32) * n_old[h], -1, keepdims=True))
        hcell = num / jnp.maximum(jnp.abs(den), emt[:, h:h + 1])
        mu = jnp.mean(hcell, -1, keepdims=True)
        hc = hcell - mu
        hn = hc * lax.rsqrt(jnp.mean(hc * hc, -1, keepdims=True) + EPS)
        out = ((om_ref[0, :, vsl[h]].astype(F32) * (hn * mnorm_ref[:, vsl[h]])
                + mskip_ref[:, vsl[h]] * c_act[:, vsl[h]]) * zm_ref[0, :, vsl[h]].astype(F32))
        mo_ref[0, :, vsl[h]] = out.astype(mo_ref.dtype)
    for h in heads:
        wk_s = _pad_rows(wk[:, h:h + 1], LS)
        vw = (vss[h].astype(F32) * wk_s).astype(BF16)
        wc = wC[:, h:h + 1]
        C_ref[0, h] = wc * C_old[h] + lax.dot_general(vw, ks[h], (((0,), (0,)), ((), ())),
                                                      preferred_element_type=F32)
        n_ref[0, h:h + 1, :] = wc * n_old[h] + jnp.sum(ks[h].astype(F32) * wk_s, axis=0, keepdims=True)


def _mlstm_specs(xm, zm, om, gates, conv_prev, C0, n0, m0, convw, convb, wq, wk, mnorm, mskip, bif,
                 *, L, out_dtype):
    N, S, M = xm.shape
    H, DV, DK = C0.shape[1:]
    seq = lambda b, c: (b, c, 0)
    per_b3 = lambda b, c: (b, 0, 0)
    per_b4 = lambda b, c: (b, 0, 0, 0)
    const2 = lambda b, c: (0, 0)
    const3 = lambda b, c: (0, 0, 0)
    in_specs = [pl.BlockSpec((1, L, M), seq), pl.BlockSpec((1, L, M), seq), pl.BlockSpec((1, L, M), seq),
                pl.BlockSpec((1, L, 2 * LANES), seq),
                pl.BlockSpec((1, SUBLANES, M), per_b3),
                pl.BlockSpec((1, H, DV, DK), per_b4),
                pl.BlockSpec((1, H, DK), per_b3),
                pl.BlockSpec((1, 1, LANES), per_b3),
                pl.BlockSpec(convw.shape, const2), pl.BlockSpec(convb.shape, const2),
                pl.BlockSpec(wq.shape, const3), pl.BlockSpec(wk.shape, const3),
                pl.BlockSpec(mnorm.shape, const2), pl.BlockSpec(mskip.shape, const2),
                pl.BlockSpec(bif.shape, const2)]
    out_specs = [pl.BlockSpec((1, L, M), seq),
                 pl.BlockSpec((1, H, DV, DK), per_b4),
                 pl.BlockSpec((1, H, DK), per_b3),
                 pl.BlockSpec((1, 1, LANES), per_b3)]
    out_shape = [jax.ShapeDtypeStruct((N, S, M), out_dtype),
                 jax.ShapeDtypeStruct((N, H, DV, DK), F32),
                 jax.ShapeDtypeStruct((N, H, DK), F32),
                 jax.ShapeDtypeStruct((N, 1, LANES), F32)]
    scratch = [pltpu.VMEM((L + 2 * SUBLANES, M), F32)]
    args = (xm, zm, om, gates, conv_prev, C0, n0, m0, convw, convb, wq, wk, mnorm, mskip, bif)
    return in_specs, out_specs, out_shape, scratch, args


def _mlstm_call(*args, L, out_dtype, name):
    in_specs, out_specs, out_shape, scratch, args = _mlstm_specs(*args, L=L, out_dtype=out_dtype)
    N, S, _ = args[0].shape
    return pl.pallas_call(
        functools.partial(_mlstm_kernel, L=L),
        grid=(N, S // L),
        in_specs=in_specs, out_specs=out_specs, out_shape=out_shape, scratch_shapes=scratch,
        compiler_params=_params(("arbitrary", "arbitrary")),
        name=name,
    )(*args)


def _mlstm_sattn_kernel(*refs, L, T, counts, steps_per_seq):
    parts, i = [], 0
    for n in counts:
        parts.append(refs[i:i + n])
        i += n
    ml_in, sa_in, ml_out, sa_out, ml_scr, sa_scr = parts
    _mlstm_kernel(*ml_in, *ml_out, *ml_scr, L=L)
    step = pl.program_id(0) * pl.num_programs(1) + pl.program_id(1)

    @pl.when(step % steps_per_seq == 0)
    def _():
        _sattn_kernel(*sa_in, *sa_out, *sa_scr, T=T)


def _mlstm_sattn_call(mlstm_args, sattn_args, *, L, T, out_dtype, name):
    N, S, _ = mlstm_args[0].shape
    nc = S // L
    DB = sattn_args[2][0].shape[0]
    steps_per_seq = (N * nc) // DB
    ml = _mlstm_specs(*mlstm_args, L=L, out_dtype=out_dtype)
    sa = _sattn_specs(*sattn_args, T, lambda b, c: (b * nc + c) // steps_per_seq,
                      out_buffering=pl.Buffered(1) if steps_per_seq > 1 else None)
    counts = (len(ml[0]), len(sa[0]), len(ml[1]), len(sa[1]), len(ml[3]), len(sa[3]))
    outs = pl.pallas_call(
        functools.partial(_mlstm_sattn_kernel, L=L, T=T, counts=counts, steps_per_seq=steps_per_seq),
        grid=(N, nc),
        in_specs=ml[0] + sa[0], out_specs=ml[1] + sa[1], out_shape=ml[2] + sa[2],
        scratch_shapes=ml[3] + sa[3],
        compiler_params=_params(("arbitrary", "arbitrary")),
        name=name,
    )(*ml[4], *sa[4])
    return outs[:len(ml[1])], outs[len(ml[1]):]


def _token_order(src_ref, scr_ref):
    _, d, n, width = src_ref.shape
    if d == 1:
        return src_ref[0, 0].astype(F32)
    n_tiles = width // LANES
    for r in range(d):
        for s in range(n_tiles):
            scr_ref[s, pl.ds(r, n, stride=d), :] = src_ref[0, r, :, s * LANES:(s + 1) * LANES].astype(F32)
    return jnp.concatenate([scr_ref[s] for s in range(n_tiles)], axis=1)


def _post_kernel(*refs, merge):
    if merge:
        (x_ref, gate_ref, o0, o1, o2, l0, l1, l2, expand_ref, za_ref, mo_ref, ga_ref, gm_ref,
         wpa_ref, wpm_ref, wout_ref, fg_ref, y_ref, scr_ref) = refs
        ls = [_token_order(l, scr_ref) for l in (l0, l1, l2)]
        lmax = jnp.maximum(jnp.maximum(ls[0], ls[1]), ls[2])
        es = [jnp.exp(l - lmax) for l in ls]
        inv = 1.0 / (es[0] + es[1] + es[2])
        o_att = None
        for e, o in zip(es, (o0, o1, o2)):
            a = e * inv
            hi = a.astype(BF16)
            lo = (a - hi.astype(F32)).astype(BF16)
            a_wide = jnp.dot(jnp.concatenate([hi, lo], axis=1), expand_ref[...],
                             preferred_element_type=F32)
            term = a_wide * _token_order(o, scr_ref)
            o_att = term if o_att is None else o_att + term
    else:
        (x_ref, gate_ref, oa_ref, za_ref, mo_ref, ga_ref, gm_ref,
         wpa_ref, wpm_ref, wout_ref, fg_ref, y_ref) = refs
        o_att = oa_ref[0]
    a_in = (o_att * za_ref[0].astype(F32)).astype(BF16)
    a_br = jnp.dot(a_in, wpa_ref[...], preferred_element_type=F32)
    m_br = jnp.dot(mo_ref[0].astype(BF16), wpm_ref[...], preferred_element_type=F32)
    merged = ga_ref[0].astype(F32) * a_br + gm_ref[0].astype(F32) * m_br
    y = x_ref[0] + gate_ref[0] * jnp.dot(merged.astype(BF16), wout_ref[...], preferred_element_type=F32)
    ms = jnp.mean(y * y, axis=-1, keepdims=True)
    y_ref[0] = y * lax.rsqrt(ms + EPS) * fg_ref[...]


def _post_call(x3, gate3, att_inputs, za, mo, ga, gm, wpa, wpm, wout, fgain, *, tm, merge, name):
    B, S, D = x3.shape
    row = lambda b, i: (b, i, 0)
    const2 = lambda b, i: (0, 0)
    if gate3.shape[1] == 1:
        gate_spec = pl.BlockSpec((1, 1, D), lambda b, i: (b, 0, 0))
    else:
        gate_spec = pl.BlockSpec((1, tm, D), row)
    def blk(a):
        if a.ndim == 2:
            return pl.BlockSpec(a.shape, const2)
        if a.ndim == 4:
            d = a.shape[1]
            return pl.BlockSpec((1, d, tm // d, a.shape[3]), lambda b, i: (b, 0, i, 0))
        return pl.BlockSpec((1, tm, a.shape[2]), row)

    in_specs = ([pl.BlockSpec((1, tm, D), row), gate_spec]
                + [blk(a) for a in att_inputs]
                + [blk(za), blk(mo), blk(ga), blk(gm),
                   pl.BlockSpec(wpa.shape, const2), pl.BlockSpec(wpm.shape, const2),
                   pl.BlockSpec(wout.shape, const2), pl.BlockSpec(fgain.shape, const2)])
    scratch = [pltpu.VMEM((ATT_WIDTH // LANES, tm, LANES), F32)] if merge else []
    return pl.pallas_call(
        functools.partial(_post_kernel, merge=merge),
        grid=(B, S // tm),
        in_specs=in_specs,
        out_specs=pl.BlockSpec((1, tm, D), row),
        out_shape=jax.ShapeDtypeStruct((B, S, D), F32),
        scratch_shapes=scratch,
        compiler_params=_params(("arbitrary", "arbitrary")),
        name=name,
    )(x3, gate3, *att_inputs, za, mo, ga, gm, wpa, wpm, wout, fgain)


def _projection_weights(w_in):
    D = w_in.shape[0]
    AW = ATT_WIDTH
    M = D
    off_k, off_v, off_za = 3 * AW, 6 * AW, 9 * AW
    off_xm = off_za + AW
    off_i = off_xm + 3 * M
    off_f = off_i + M_HEADS
    off_ga = off_f + M_HEADS
    w_main = w_in[:, :off_i].astype(BF16)
    pad = jnp.zeros((D, LANES - M_HEADS), BF16)
    w_tail = jnp.concatenate([w_in[:, off_ga:off_ga + 2 * D].astype(BF16),
                              w_in[:, off_i:off_f].astype(BF16), pad,
                              w_in[:, off_f:off_ga].astype(BF16), pad], axis=1)
    segs = {"za": ((0, off_za, AW),), "xm": ((0, off_xm, M),), "zm": ((0, off_xm + M, M),),
            "om": ((0, off_xm + 2 * M, M),), "ga": ((1, 0, D),), "gm": ((1, D, D),),
            "gates": ((1, 2 * D, 2 * LANES),)}
    for g in range(N_GROUPS):
        segs[f"q{g}"] = ((0, g * AW, AW),)
        segs[f"kv{g}"] = ((0, off_k + g * AW, AW), (0, off_v + g * AW, AW))
    return (w_main, w_tail), segs


def kernel(x_prompt, x_sample, cache_kv_w128, cache_kv_w512, cache_kv_w2048, state_conv, state_C, state_n, state_m, c_prompt, c_sample, rel_table, norm_gain, w_ada, b_ada, w_in, b_if, conv_w, conv_b, w_mq, w_mk, m_norm, m_skip, w_pa, w_pm, w_out, final_gain):
    B, S, D = x_prompt.shape
    DB, T, _ = x_sample.shape
    assert norm_gain.shape[0] == 1, "single-layer trunk"
    assert S % ATT_TILE == 0 and S % MLSTM_CHUNK == 0 and T == SUBLANES
    caches = (cache_kv_w128[0], cache_kv_w512[0], cache_kv_w2048[0])
    H = M_HEADS
    M = conv_w.shape[2]

    wp, segs = _projection_weights(w_in[0])
    names = ("q0", "q1", "q2", "za", "kv0", "kv1", "kv2", "xm", "zm", "om", "ga", "gm", "gates")
    seg_list = [segs[n] for n in names]
    gain = norm_gain[0].reshape(1, D)
    fgain = final_gain.reshape(1, D)
    wpa, wpm, wout = w_pa[0].astype(BF16), w_pm[0].astype(BF16), w_out[0].astype(BF16)
    wq, wk = w_mq[0].astype(BF16), w_mk[0].astype(BF16)
    convw, convb = conv_w[0], conv_b[0].reshape(1, M)
    mnorm, mskip = m_norm[0].reshape(1, M), m_skip[0].reshape(1, M)
    zpad = jnp.zeros((LANES - H,), F32)
    bif = jnp.concatenate([b_if[0, :H], zpad, b_if[0, H:], zpad]).reshape(1, 2 * LANES)

    ada = _ada_call(jnp.concatenate([c_prompt, c_sample], axis=0), w_ada[0], b_ada[0])
    shift, scale, gate = ada[:, :D], ada[:, D:2 * D], ada[:, 2 * D:]

    gate_act = {"za": "silu", "zm": "silu", "om": "sigmoid", "ga": "sigmoid", "gm": "sigmoid"}
    acts = [gate_act.get(n) for n in names]

    R = DB * T
    rep = lambda t: jnp.repeat(t[B:], T, axis=0).reshape(1, R, D)
    s_shift, s_scale, s_gate = rep(shift), rep(scale), rep(gate)
    xs = x_sample.reshape(1, R, D)
    sr = dict(zip(names, _proj_call(xs, gain, s_scale, s_shift, wp, seg_list, [F32] * 13,
                                    tm=R, row0=0, rows=R, name="proj_sample", acts=acts)))
    bcs, bns = [], []
    for g, (win, dil) in enumerate(ATT_GROUPS):
        bc, bn = _sample_bias(rel_table, g, win, dil, caches[g].shape[1], T)
        bcs.append(bc)
        bns.append(bn)
    cache_t = [jnp.transpose(c, (0, 2, 3, 4, 1)).reshape(DB, 2 * ATT_WIDTH, c.shape[1]) for c in caches]
    kvn_t = [sr[f"kv{g}"].reshape(DB, T, 2 * ATT_WIDTH) for g in range(3)]
    sattn_args = ([sr[f"q{g}"].reshape(DB, T, ATT_WIDTH) for g in range(3)], kvn_t, cache_t, bcs, bns)

    p_shift, p_scale, p_gate = (t[:B].reshape(B, 1, D) for t in (shift, scale, gate))
    dts = [BF16] * 12 + [F32]
    group_dil = {f"{p}{g}": dil for g, (_, dil) in enumerate(ATT_GROUPS) for p in ("q", "kv")}
    pr = dict(zip(names, _proj_call(x_prompt, gain, p_scale, p_shift, wp, seg_list, dts,
                                    tm=PROJ_TM, row0=0, rows=S, name="proj_prompt",
                                    dils=[group_dil.get(n, 1) for n in names], acts=acts)))
    att = []
    for g, (win, dil) in enumerate(ATT_GROUPS):
        planes = lambda a: a if a.ndim == 4 else a[:, None]
        att.append(_attn_call(planes(pr[f"q{g}"]), planes(pr[f"kv{g}"]), _prompt_bias(rel_table, g, dil),
                              f"attn_prompt_g{g}"))
    expand = np.zeros((2 * LANES, ATT_WIDTH), np.float32)
    for h in range(ATT_HEADS):
        expand[[LSE_LANES * h, LANES + LSE_LANES * h], h * ATT_HEAD_DIM:(h + 1) * ATT_HEAD_DIM] = 1.0
    expand = jnp.asarray(expand, BF16)
    mlstm_p_args = (pr["xm"], pr["zm"], pr["om"], pr["gates"],
                    jnp.zeros((B, SUBLANES, M), F32), jnp.zeros((B,) + state_C.shape[2:], F32),
                    jnp.zeros((B,) + state_n.shape[2:], F32), jnp.zeros((B, 1, LANES), F32),
                    convw, convb, wq, wk, mnorm, mskip, bif)
    n_steps = B * (S // MLSTM_CHUNK)
    if n_steps % DB == 0:
        (mo_p, C_p, n_p, m_p), sa = _mlstm_sattn_call(mlstm_p_args, sattn_args, L=MLSTM_CHUNK, T=T,
                                                        out_dtype=BF16, name="mlstm_prompt_sample_attn")
    else:
        mo_p, C_p, n_p, m_p = _mlstm_call(*mlstm_p_args, L=MLSTM_CHUNK, out_dtype=BF16, name="mlstm_prompt")
        sa = _sattn_call(*sattn_args, T)
    y_prompt = _post_call(x_prompt, p_gate, [a[0] for a in att] + [a[1] for a in att] + [expand],
                          pr["za"], mo_p, pr["ga"], pr["gm"], wpa, wpm, wout, fgain,
                          tm=POST_TM, merge=True, name="post_prompt")
    kv_p = []
    w_max = min(ATT_GROUPS[-1][0], S)
    (kv2_t,) = _proj_call(x_prompt, gain, p_scale, p_shift, wp[:1], [segs["kv2"]], [F32],
                          tm=PROJ_TM, row0=S - w_max, rows=w_max, name="tail_kv2")
    w_mid = min(ATT_GROUPS[1][0], S)
    kv0_t, kv1_t, xm_t = _proj_call(x_prompt, gain, p_scale, p_shift, wp[:1],
                                    [segs["kv0"], segs["kv1"], segs["xm"]], [F32] * 3,
                                    tm=w_mid, row0=S - w_mid, rows=w_mid, name="tail_kv01")
    kshape = lambda L: (1, B, L, 2, ATT_HEADS, ATT_HEAD_DIM)
    w0 = min(ATT_GROUPS[0][0], S)
    kv_p = [kv0_t[:, w_mid - w0:].reshape(kshape(w0)), kv1_t.reshape(kshape(w_mid)), kv2_t.reshape(kshape(w_max))]
    conv_p = xm_t[:, w_mid - (CONV_WIDTH - 1):][None]

    o_att_s = sa[0].reshape(1, R, ATT_WIDTH)
    kv_s = [jnp.transpose(c.reshape(DB, 2, ATT_HEADS, ATT_HEAD_DIM, c.shape[2]), (0, 4, 1, 2, 3))[None]
            for c in sa[1:]]
    conv_prev_s = jnp.concatenate([jnp.zeros((DB, SUBLANES - (CONV_WIDTH - 1), M), F32), state_conv[0]], axis=1)
    m0_s = jnp.concatenate([state_m[0], jnp.zeros((DB, LANES - H), F32)], axis=1).reshape(DB, 1, LANES)
    seqv = lambda t: t.reshape(DB, T, t.shape[-1])
    mo_s, C_s, n_s, m_s = _mlstm_call(
        seqv(sr["xm"]), seqv(sr["zm"]), seqv(sr["om"]), seqv(sr["gates"]),
        conv_prev_s, state_C[0], state_n[0], m0_s,
        convw, convb, wq, wk, mnorm, mskip, bif, L=T, out_dtype=F32, name="mlstm_sample")
    y_sample = _post_call(xs, s_gate, [o_att_s], sr["za"], mo_s.reshape(1, R, M), sr["ga"], sr["gm"],
                          wpa, wpm, wout, fgain, tm=R, merge=False, name="post_sample")
    conv_s = seqv(sr["xm"])[:, T - (CONV_WIDTH - 1):][None]

    return (y_prompt, y_sample.reshape(DB, T, D),
            kv_p[0], kv_s[0], kv_p[1], kv_s[1], kv_p[2], kv_s[2],
            conv_p, conv_s, C_p[None], C_s[None], n_p[None], n_s[None],
            m_p[:, 0, :H][None], m_s[:, 0, :H][None])
```

```python
import functools

import numpy as np
import jax
import jax.numpy as jnp
from jax import lax
from jax.experimental import pallas as pl
from jax.experimental.pallas import tpu as pltpu

F32 = jnp.float32
BF16 = jnp.bfloat16

ATT_GROUPS = ((128, 1), (512, 4), (2048, 16))
N_GROUPS = len(ATT_GROUPS)
ATT_HEADS = 8
ATT_HEAD_DIM = 64
ATT_WIDTH = ATT_HEADS * ATT_HEAD_DIM
WIN_STEPS = 128
ATT_SCALE = ATT_HEAD_DIM ** -0.5
N_BUCKETS = 32
MAX_DISTANCE = 2048
M_HEADS = 4
CONV_WIDTH = 4
EPS = 1e-6
NEG = -1e30

LANES = 128
SUBLANES = 8
VMEM_LIMIT = 56 * 1024 * 1024

ATT_TILE = 2048
PROJ_TM = 512
POST_TM = 512
MLSTM_CHUNK = 256


def _params(sem, vmem=VMEM_LIMIT):
    return pltpu.CompilerParams(dimension_semantics=sem, vmem_limit_bytes=vmem)


def _t5_bucket(dist):
    n = np.asarray(dist).astype(np.int64)
    max_exact = N_BUCKETS // 2
    nf = np.maximum(n, 1).astype(np.float32)
    large = max_exact + (np.log(nf / max_exact) / np.log(np.float32(MAX_DISTANCE / max_exact))
                         * (N_BUCKETS - max_exact)).astype(np.int64)
    large = np.minimum(large, N_BUCKETS - 1)
    return np.where(n < max_exact, n, large).astype(np.int32)


def _silu(x):
    return x * jax.nn.sigmoid(x)


def _ada_kernel(c_ref, w_ref, b_ref, o_ref):
    s = _silu(c_ref[...])
    o_ref[...] = jnp.dot(s, w_ref[...], preferred_element_type=F32,
                         precision=lax.Precision.HIGHEST) + b_ref[...]


def _ada_call(c, w, b):
    n, d = c.shape
    width = w.shape[1]
    tn = 512
    return pl.pallas_call(
        _ada_kernel,
        grid=(width // tn,),
        in_specs=[pl.BlockSpec((n, d), lambda j: (0, 0)),
                  pl.BlockSpec((d, tn), lambda j: (0, j)),
                  pl.BlockSpec((1, tn), lambda j: (0, j))],
        out_specs=pl.BlockSpec((n, tn), lambda j: (0, j)),
        out_shape=jax.ShapeDtypeStruct((n, width), F32),
        compiler_params=_params(("arbitrary",)),
        name="ada",
    )(c, w, b.reshape(1, width))


_ACTIVATIONS = {None: lambda v: v, "silu": _silu, "sigmoid": jax.nn.sigmoid}


def _proj_kernel(x_ref, gain_ref, scale_ref, shift_ref, *rest, n_w, segs, dils, acts):
    w_refs = rest[:n_w]
    rest = rest[n_w:]
    out_refs = rest[:len(segs)]
    x = x_ref[0]
    tm, D = x.shape
    ms = jnp.mean(x * x, axis=-1, keepdims=True)
    h = x * lax.rsqrt(ms + EPS) * gain_ref[...] * (1.0 + scale_ref[0]) + shift_ref[0]
    lhs = {1: h.astype(BF16)}
    strides = sorted(set(dils) - {1})
    if strides:
        hs_ref = rest[len(segs)]
        n_tiles = D // LANES
        for s in range(n_tiles):
            hs_ref[s] = h[:, s * LANES:(s + 1) * LANES]
        for d in strides:
            n = tm // d
            lhs[d] = jnp.concatenate(
                [jnp.concatenate([hs_ref[s, pl.ds(r, n, stride=d), :] for r in range(d)], axis=0)
                 for s in range(n_tiles)], axis=1).astype(BF16)
    for o_ref, pieces, d, act in zip(out_refs, segs, dils, acts):
        parts = [_ACTIVATIONS[act](jnp.dot(lhs[d], w_refs[wi][:, c0:c0 + width],
                                           preferred_element_type=F32)).astype(o_ref.dtype)
                 for wi, c0, width in pieces]
        res = parts[0] if len(parts) == 1 else jnp.concatenate(parts, axis=1)
        if d == 1:
            o_ref[0] = res
        else:
            n = tm // d
            for r in range(d):
                o_ref[0, r] = res[r * n:(r + 1) * n, :]


def _proj_call(x3, gain, scale3, shift3, ws, segs, dtypes, *, tm, row0, rows, name, dils=None, acts=None):
    B, S, D = x3.shape
    nrb = rows // tm
    rb0 = row0 // tm
    dils = tuple(dils) if dils is not None else (1,) * len(segs)
    acts = tuple(acts) if acts is not None else (None,) * len(segs)
    per_row = scale3.shape[1] != 1
    if per_row:
        mod_spec = pl.BlockSpec((1, tm, D), lambda b, i: (b, rb0 + i, 0))
    else:
        mod_spec = pl.BlockSpec((1, 1, D), lambda b, i: (b, 0, 0))
    out_shape, out_specs = [], []
    for pieces, dt, d in zip(segs, dtypes, dils):
        wd = sum(p[2] for p in pieces)
        if d == 1:
            out_shape.append(jax.ShapeDtypeStruct((B, rows, wd), dt))
            out_specs.append(pl.BlockSpec((1, tm, wd), lambda b, i: (b, i, 0)))
        else:
            out_shape.append(jax.ShapeDtypeStruct((B, d, rows // d, wd), dt))
            out_specs.append(pl.BlockSpec((1, d, tm // d, wd), lambda b, i: (b, 0, i, 0)))
    scratch = [pltpu.VMEM((D // LANES, tm, LANES), F32)] if any(d > 1 for d in dils) else []
    return pl.pallas_call(
        functools.partial(_proj_kernel, n_w=len(ws), segs=tuple(segs), dils=dils, acts=acts),
        grid=(B, nrb),
        in_specs=[pl.BlockSpec((1, tm, D), lambda b, i: (b, rb0 + i, 0)),
                  pl.BlockSpec((1, D), lambda b, i: (0, 0)),
                  mod_spec, mod_spec]
                 + [pl.BlockSpec(w.shape, lambda b, i: (0, 0), pipeline_mode=pl.Buffered(1)) for w in ws],
        out_specs=out_specs,
        out_shape=out_shape,
        scratch_shapes=scratch,
        compiler_params=_params(("arbitrary", "arbitrary")),
        name=name,
    )(x3, gain, scale3, shift3, *ws)


HEADS_PER_SLAB = LANES // ATT_HEAD_DIM
N_SLABS = ATT_HEADS // HEADS_PER_SLAB
LSE_LANES = LANES // ATT_HEADS


def _attn_unit(q, kv, bias_ref, prev_mask, o_ref, l_ref, at):
    dn = (((1,), (1,)), ((), ()))
    nk = 2 * WIN_STEPS
    lane_q = lax.broadcasted_iota(jnp.int32, (WIN_STEPS, LANES), 1) < ATT_HEAD_DIM
    lane_k = lax.broadcasted_iota(jnp.int32, (nk, LANES), 1) < ATT_HEAD_DIM
    ones_lo = jnp.where(lane_k, 1.0, 0.0).astype(BF16)
    ones_hi = jnp.where(lane_k, 0.0, 1.0).astype(BF16)
    zero_q = jnp.zeros((WIN_STEPS, LANES), BF16)
    zero_k = jnp.zeros((nk, LANES), BF16)

    def scores(m):
        cs = slice(m * LANES, (m + 1) * LANES)
        qs = q[:, cs]
        ks = kv[:, cs]
        out = []
        for hh in range(HEADS_PER_SLAB):
            qm = jnp.where(lane_q, qs, zero_q) if hh == 0 else jnp.where(lane_q, zero_q, qs)
            s = lax.dot_general(qm, ks, dn, preferred_element_type=F32) + bias_ref[m * HEADS_PER_SLAB + hh]
            out.append(jnp.concatenate([s[:, :WIN_STEPS] + prev_mask, s[:, WIN_STEPS:]], axis=1))
        return out

    def finish(m, ss):
        cs = slice(m * LANES, (m + 1) * LANES)
        vs = kv[:, ATT_WIDTH + m * LANES:ATT_WIDTH + (m + 1) * LANES]
        ps, mxs = [], []
        for s in ss:
            mx = jnp.max(jnp.maximum(s[:, :WIN_STEPS], s[:, WIN_STEPS:]), -1, keepdims=True)
            ps.append(jnp.exp(s - mx).astype(BF16))
            mxs.append(mx)
        pcat = jnp.concatenate(ps, axis=1)
        vpair = jnp.concatenate(
            [jnp.concatenate([jnp.where(lane_k, vs, zero_k), ones_lo], axis=1),
             jnp.concatenate([jnp.where(lane_k, zero_k, vs), ones_hi], axis=1)], axis=0)
        acc = jnp.dot(pcat, vpair, preferred_element_type=F32)
        den = acc[:, LANES:]
        o_ref[at + (cs,)] = (acc[:, :LANES] / den).astype(o_ref.dtype)
        lse = jnp.where(lane_q, mxs[0], mxs[1]) + jnp.log(den)
        return pltpu.roll(lse, (LSE_LANES * HEADS_PER_SLAB * m - 48) % LANES, axis=1)

    lane = lax.broadcasted_iota(jnp.int32, (WIN_STEPS, LANES), 1)
    lse_c = None
    pending = {0: scores(0), 1: scores(1)}
    for m in range(N_SLABS):
        if m + 2 < N_SLABS:
            pending[m + 2] = scores(m + 2)
        part = finish(m, pending.pop(m))
        lse_c = part if lse_c is None else jnp.where(lane >= LSE_LANES * HEADS_PER_SLAB * m, part, lse_c)
    l_ref[at + (slice(None),)] = lse_c


def _attn_kernel(q_ref, kvc_ref, kvp_ref, bias_ref, o_ref, l_ref, *, ns):
    d = q_ref.shape[1]
    first_tile = pl.program_id(1) == 0

    def body(idx, carry):
        rr = idx // ns
        j = idx % ns
        rc = pl.ds(pl.multiple_of(j * WIN_STEPS, WIN_STEPS), WIN_STEPS)
        q = q_ref[0, rr, rc, :] * ATT_SCALE
        kv_prev = kvp_ref[0, rr]
        if ns > 1:
            rp = pl.ds(pl.multiple_of(jnp.maximum(j - 1, 0) * WIN_STEPS, WIN_STEPS), WIN_STEPS)
            kv_prev = jnp.where(j == 0, kv_prev, kvc_ref[0, rr, rp, :])
        kv = jnp.concatenate([kv_prev, kvc_ref[0, rr, rc, :]], axis=0)
        prev_mask = jnp.where(first_tile & (j == 0), NEG, 0.0).astype(F32)
        _attn_unit(q, kv, bias_ref, prev_mask, o_ref, l_ref, (0, rr, rc))
        return carry

    lax.fori_loop(0, d * ns, body, 0, unroll=2)


def _attn_call(q, kv, bias, name):
    B, d, U, _ = q.shape
    ns = ATT_TILE // (WIN_STEPS * d)
    rows = ns * WIN_STEPS
    blk = lambda width: pl.BlockSpec((1, d, rows, width), lambda b, t: (b, 0, t, 0))
    return pl.pallas_call(
        functools.partial(_attn_kernel, ns=ns),
        grid=(B, U // rows),
        in_specs=[blk(ATT_WIDTH), blk(2 * ATT_WIDTH),
                  pl.BlockSpec((1, d, WIN_STEPS, 2 * ATT_WIDTH),
                               lambda b, t: (b, 0, jnp.maximum(t * ns - 1, 0), 0)),
                  pl.BlockSpec(bias.shape, lambda b, t: (0, 0, 0))],
        out_specs=[blk(ATT_WIDTH), blk(LANES)],
        out_shape=[jax.ShapeDtypeStruct((B, d, U, ATT_WIDTH), BF16),
                   jax.ShapeDtypeStruct((B, d, U, LANES), F32)],
        compiler_params=_params(("arbitrary", "arbitrary")),
        name=name,
    )(q, kv, kv, bias)


def _stride_bias(rel_table, g, d):
    bucket = _t5_bucket(np.arange(WIN_STEPS + 1) * d)
    onehot = jnp.asarray(np.eye(N_BUCKETS, dtype=np.float32)[bucket])
    tbl = rel_table[:, g * ATT_HEADS:(g + 1) * ATT_HEADS].astype(F32)
    return jnp.dot(onehot, tbl, precision=lax.Precision.HIGHEST).T


def _prompt_bias(rel_table, g, d):
    vals = _stride_bias(rel_table, g, d)
    n = WIN_STEPS
    period = 3 * n
    wp = jnp.concatenate([jnp.full((ATT_HEADS, n - 1), NEG, F32), vals[:, ::-1],
                          jnp.full((ATT_HEADS, n), NEG, F32)], axis=1)
    flat = jnp.tile(wp, (1, n))[:, :n * (period - 1)]
    return flat.reshape(ATT_HEADS, n, period - 1)[:, :, n - 1:n - 1 + 2 * n]


SHIFT_VREGS = 128


def _sattn_kernel(*refs, T):
    q_refs = refs[0:3]
    kvnew_refs = refs[3:6]
    cache_refs = refs[6:9]
    bc_refs = refs[9:12]
    bn_refs = refs[12:15]
    o_ref = refs[15]
    co_refs = refs[16:19]
    kvn_refs = refs[19:22]
    HT = ATT_HEADS * T
    dn = (((1,), (1,)), ((), ()))
    row_head = lax.broadcasted_iota(jnp.int32, (HT, ATT_WIDTH), 0) // T
    col_head = lax.broadcasted_iota(jnp.int32, (HT, ATT_WIDTH), 1) // ATT_HEAD_DIM
    head_mask = row_head == col_head

    for g in range(N_GROUPS):
        rows = jnp.concatenate([jnp.zeros((LANES - T, 2 * ATT_WIDTH), F32), kvnew_refs[g][0]], axis=0)
        kvn_refs[g][0] = rows.T

    stats = []
    for g in range(N_GROUPS):
        q = q_refs[g][0] * ATT_SCALE
        qexp = jnp.where(head_mask, jnp.concatenate([q] * ATT_HEADS, axis=0), 0.0).astype(BF16)
        kn = kvn_refs[g][0, :ATT_WIDTH, :].astype(BF16)
        kc = cache_refs[g][0, :ATT_WIDTH, :].astype(BF16)
        lc = jnp.dot(qexp, kc, preferred_element_type=F32) + bc_refs[g][...]
        ln = jnp.dot(qexp, kn, preferred_element_type=F32) + bn_refs[g][...]
        mx = jnp.maximum(jnp.max(lc, -1, keepdims=True), jnp.max(ln, -1, keepdims=True))
        pc = jnp.exp(lc - mx)
        pn = jnp.exp(ln - mx)
        ssum = jnp.sum(pc, -1, keepdims=True) + jnp.sum(pn, -1, keepdims=True)
        stats.append((pc, pn, ssum, mx + jnp.log(ssum)))

    lse_max = jnp.maximum(jnp.maximum(stats[0][3], stats[1][3]), stats[2][3])
    es = [jnp.exp(st[3] - lse_max) for st in stats]
    esum = es[0] + es[1] + es[2]
    acc = jnp.zeros((HT, ATT_WIDTH), F32)
    for g in range(N_GROUPS):
        pc, pn, ssum, _ = stats[g]
        w = es[g] / (esum * ssum)
        vc = cache_refs[g][0, ATT_WIDTH:, :].astype(BF16)
        vn = kvn_refs[g][0, ATT_WIDTH:, :].astype(BF16)
        acc = acc + lax.dot_general((pc * w).astype(BF16), vc, dn, preferred_element_type=F32)
        acc = acc + lax.dot_general((pn * w).astype(BF16), vn, dn, preferred_element_type=F32)
    lane_head = lax.broadcasted_iota(jnp.int32, (T, ATT_WIDTH), 1) // ATT_HEAD_DIM
    o = jnp.zeros((T, ATT_WIDTH), F32)
    for h in range(ATT_HEADS):
        o = o + jnp.where(lane_head == h, acc[h * T:(h + 1) * T, :], 0.0)
    o_ref[0] = o

    for g in range(N_GROUPS):
        L = cache_refs[g].shape[2]
        nrows = cache_refs[g].shape[1]
        chunk = min(nrows, SHIFT_VREGS * SUBLANES * LANES // L)
        is_new = lax.broadcasted_iota(jnp.int32, (chunk, LANES), 1) >= LANES - T

        def shift_rows(i, carry, g=g, L=L, chunk=chunk, is_new=is_new):
            rs = pl.ds(pl.multiple_of(i * chunk, chunk), chunk)
            rolled = pltpu.roll(cache_refs[g][0, rs, :], L - T, axis=1)
            if L > LANES:
                co_refs[g][0, rs, 0:L - LANES] = rolled[:, 0:L - LANES]
            co_refs[g][0, rs, L - LANES:L] = jnp.where(is_new, kvn_refs[g][0, rs, :], rolled[:, L - LANES:])
            return carry

        lax.fori_loop(0, nrows // chunk, shift_rows, 0)


def _sample_bias(rel_table, g, W, d, Lb, T):
    vals = _stride_bias(rel_table, g, d)
    n = Lb + T
    by_dist = jnp.concatenate([vals[:, :, None], jnp.full((ATT_HEADS, WIN_STEPS + 1, d - 1), NEG, F32)],
                              axis=2).reshape(ATT_HEADS, (WIN_STEPS + 1) * d)
    if by_dist.shape[1] < n:
        by_dist = jnp.concatenate([by_dist, jnp.full((ATT_HEADS, n - by_dist.shape[1]), NEG, F32)], axis=1)
    rev = jnp.concatenate([by_dist[:, :n][:, ::-1], jnp.full((ATT_HEADS, T), NEG, F32)], axis=1)
    rows = jnp.stack([rev[:, T - 1 - t:T - 1 - t + n] for t in range(T)], axis=1)
    rows = rows.reshape(ATT_HEADS * T, n)
    bc = rows[:, :Lb]
    bn = jnp.concatenate([jnp.full((ATT_HEADS * T, LANES - T), NEG, F32), rows[:, Lb:]], axis=1)
    return bc, bn


def _sattn_call(qs, kvns, caches, bcs, bns, T):
    DB = caches[0].shape[0]
    rows = caches[0].shape[1]
    per_seq = lambda b: (b, 0, 0)
    const2 = lambda b: (0, 0)
    in_specs = ([pl.BlockSpec((1, T, ATT_WIDTH), per_seq)] * 3
                + [pl.BlockSpec((1,) + x.shape[1:], per_seq) for x in kvns]
                + [pl.BlockSpec((1,) + c.shape[1:], per_seq) for c in caches]
                + [pl.BlockSpec(x.shape, const2) for x in bcs]
                + [pl.BlockSpec(x.shape, const2) for x in bns])
    out_specs = ([pl.BlockSpec((1, T, ATT_WIDTH), per_seq)]
                 + [pl.BlockSpec((1,) + c.shape[1:], per_seq) for c in caches])
    out_shape = ([jax.ShapeDtypeStruct((DB, T, ATT_WIDTH), F32)]
                 + [jax.ShapeDtypeStruct(c.shape, F32) for c in caches])
    return pl.pallas_call(
        functools.partial(_sattn_kernel, T=T),
        grid=(DB,),
        in_specs=in_specs, out_specs=out_specs, out_shape=out_shape,
        scratch_shapes=[pltpu.VMEM((1, rows, LANES), F32)] * N_GROUPS,
        compiler_params=_params(("arbitrary",)),
        name="sample_attn",
    )(*qs, *kvns, *caches, *bcs, *bns)


def _scan_rows(x, op, fill):
    n = x.shape[0]
    rowid = lax.broadcasted_iota(jnp.int32, x.shape, 0)
    s = 1
    while s < n:
        shifted = pltpu.roll(x, s, axis=0)
        x = op(x, jnp.where(rowid >= s, shifted, fill))
        s *= 2
    return x


def _pad_rows(x, n, fill=0.0):
    if x.shape[0] == n:
        return x
    return jnp.concatenate([x, jnp.full((n - x.shape[0],) + x.shape[1:], fill, x.dtype)], axis=0)


def _mlstm_kernel(xm_ref, zm_ref, om_ref, g_ref, cprev_ref, C0_ref, n0_ref, m0_ref,
                  convw_ref, convb_ref, wq_ref, wk_ref, mnorm_ref, mskip_ref, bif_ref,
                  mo_ref, C_ref, n_ref, m_ref, xc_ref, *, L):
    LS = max(L, LANES)
    DV = C_ref.shape[2]
    DK = C_ref.shape[3]
    c = pl.program_id(1)

    @pl.when(c == 0)
    def _():
        xc_ref[0:SUBLANES, :] = cprev_ref[0]
        C_ref[...] = C0_ref[...]
        n_ref[...] = n0_ref[...]
        m_ref[...] = m0_ref[...]

    xm_b = xm_ref[0].astype(BF16)
    xm_f = xm_ref[0].astype(F32)
    t_id = lax.broadcasted_iota(jnp.int32, (L, LS), 0)
    s_id = lax.broadcasted_iota(jnp.int32, (L, LS), 1)
    xc_ref[SUBLANES:SUBLANES + L, :] = xm_f
    conv = convb_ref[...]
    for j in range(CONV_WIDTH):
        off = SUBLANES - (CONV_WIDTH - 1) + j
        conv = conv + convw_ref[j:j + 1, :] * xc_ref[off:off + L, :]
    xc_ref[0:SUBLANES, :] = xc_ref[L:L + SUBLANES, :]
    c_act = _silu(conv)
    cb = c_act.astype(BF16)

    gates = g_ref[0] + bif_ref[...]
    i_pre = gates[:, :LANES]
    logf = jax.nn.log_sigmoid(gates[:, LANES:])
    b = _scan_rows(logf, jnp.add, 0.0)
    a = i_pre - b
    ca = _scan_rows(a, jnp.maximum, NEG)
    m_prev = m_ref[0]
    mm = jnp.maximum(ca, m_prev)
    u = -mm
    w_inter = jnp.exp(u + m_prev)
    emt = jnp.exp(-(b + mm))
    bL = b[L - 1:L, :]
    m_new = bL + jnp.maximum(m_prev, ca[L - 1:L, :])
    wk = jnp.exp(bL + a - m_new)
    wC = jnp.exp(bL + m_prev - m_new)
    m_ref[0] = m_new

    causal = s_id <= t_id
    eye = s_id == t_id
    dn_t = (((1,), (1,)), ((), ()))

    heads = range(M_HEADS)
    vsl = [slice(h * DV, (h + 1) * DV) for h in heads]
    qs, ks, vss = [], [], []
    for h in heads:
        ch = cb[:, vsl[h]]
        qs.append((jnp.dot(ch, wq_ref[h], preferred_element_type=F32) * (DK ** -0.5)).astype(BF16))
        ks.append(_pad_rows(jnp.dot(ch, wk_ref[h], preferred_element_type=F32).astype(BF16), LS))
        vss.append(_pad_rows(xm_b[:, vsl[h]], LS))
    C_old = [C_ref[0, h] for h in heads]
    n_old = [n_ref[0, h:h + 1, :] for h in heads]
    scs, inters = [], []
    for h in heads:
        a_row = jnp.sum(jnp.where(eye, a[:, h:h + 1], 0.0), axis=0, keepdims=True)
        w_intra = jnp.exp(jnp.where(causal, u[:, h:h + 1] + a_row, NEG))
        scs.append(lax.dot_general(qs[h], ks[h], dn_t, preferred_element_type=F32) * w_intra)
        inters.append(lax.dot_general(qs[h], C_old[h].astype(BF16), dn_t, preferred_element_type=F32))
    for h in heads:
        wi = w_inter[:, h:h + 1]
        num = jnp.dot(scs[h].astype(BF16), vss[h], preferred_element_type=F32) + wi * inters[h]
        den = (jnp.sum(scs[h], -1, keepdims=True)
               + wi * jnp.sum(qs[h].astype(F32) * n_old[h], -1, keepdims=True))
        hcell = num / jnp.maximum(jnp.abs(den), emt[:, h:h + 1])
        mu = jnp.mean(hcell, -1, keepdims=True)
        hc = hcell - mu
        hn = hc * lax.rsqrt(jnp.mean(hc * hc, -1, keepdims=True) + EPS)
        out = ((om_ref[0, :, vsl[h]].astype(F32) * (hn * mnorm_ref[:, vsl[h]])
                + mskip_ref[:, vsl[h]] * c_act[:, vsl[h]]) * zm_ref[0, :, vsl[h]].astype(F32))
        mo_ref[0, :, vsl[h]] = out.astype(mo_ref.dtype)
    for h in heads:
        wk_s = _pad_rows(wk[:, h:h + 1], LS)
        vw = (vss[h].astype(F32) * wk_s).astype(BF16)
        wc = wC[:, h:h + 1]
        C_ref[0, h] = wc * C_old[h] + lax.dot_general(vw, ks[h], (((0,), (0,)), ((), ())),
                                                      preferred_element_type=F32)
        n_ref[0, h:h + 1, :] = wc * n_old[h] + jnp.sum(ks[h].astype(F32) * wk_s, axis=0, keepdims=True)


def _mlstm_specs(xm, zm, om, gates, conv_prev, C0, n0, m0, convw, convb, wq, wk, mnorm, mskip, bif,
                 *, L, out_dtype):
    N, S, M = xm.shape
    H, DV, DK = C0.shape[1:]
    seq = lambda b, c: (b, c, 0)
    per_b3 = lambda b, c: (b, 0, 0)
    per_b4 = lambda b, c: (b, 0, 0, 0)
    const2 = lambda b, c: (0, 0)
    const3 = lambda b, c: (0, 0, 0)
    in_specs = [pl.BlockSpec((1, L, M), seq), pl.BlockSpec((1, L, M), seq), pl.BlockSpec((1, L, M), seq),
                pl.BlockSpec((1, L, 2 * LANES), seq),
                pl.BlockSpec((1, SUBLANES, M), per_b3),
                pl.BlockSpec((1, H, DV, DK), per_b4),
                pl.BlockSpec((1, H, DK), per_b3),
                pl.BlockSpec((1, 1, LANES), per_b3),
                pl.BlockSpec(convw.shape, const2), pl.BlockSpec(convb.shape, const2),
                pl.BlockSpec(wq.shape, const3), pl.BlockSpec(wk.shape, const3),
                pl.BlockSpec(mnorm.shape, const2), pl.BlockSpec(mskip.shape, const2),
                pl.BlockSpec(bif.shape, const2)]
    out_specs = [pl.BlockSpec((1, L, M), seq),
                 pl.BlockSpec((1, H, DV, DK), per_b4),
                 pl.BlockSpec((1, H, DK), per_b3),
                 pl.BlockSpec((1, 1, LANES), per_b3)]
    out_shape = [jax.ShapeDtypeStruct((N, S, M), out_dtype),
                 jax.ShapeDtypeStruct((N, H, DV, DK), F32),
                 jax.ShapeDtypeStruct((N, H, DK), F32),
                 jax.ShapeDtypeStruct((N, 1, LANES), F32)]
    scratch = [pltpu.VMEM((L + 2 * SUBLANES, M), F32)]
    args = (xm, zm, om, gates, conv_prev, C0, n0, m0, convw, convb, wq, wk, mnorm, mskip, bif)
    return in_specs, out_specs, out_shape, scratch, args


def _mlstm_call(*args, L, out_dtype, name):
    in_specs, out_specs, out_shape, scratch, args = _mlstm_specs(*args, L=L, out_dtype=out_dtype)
    N, S, _ = args[0].shape
    return pl.pallas_call(
        functools.partial(_mlstm_kernel, L=L),
        grid=(N, S // L),
        in_specs=in_specs, out_specs=out_specs, out_shape=out_shape, scratch_shapes=scratch,
        compiler_params=_params(("arbitrary", "arbitrary")),
        name=name,
    )(*args)


def _token_order(src_ref, scr_ref):
    _, d, n, width = src_ref.shape
    if d == 1:
        return src_ref[0, 0].astype(F32)
    n_tiles = width // LANES
    for r in range(d):
        for s in range(n_tiles):
            scr_ref[s, pl.ds(r, n, stride=d), :] = src_ref[0, r, :, s * LANES:(s + 1) * LANES].astype(F32)
    return jnp.concatenate([scr_ref[s] for s in range(n_tiles)], axis=1)


def _post_kernel(*refs, merge):
    if merge:
        (x_ref, gate_ref, o0, o1, o2, l0, l1, l2, expand_ref, za_ref, mo_ref, ga_ref, gm_ref,
         wpa_ref, wpm_ref, wout_ref, fg_ref, y_ref, scr_ref) = refs
        ls = [_token_order(l, scr_ref) for l in (l0, l1, l2)]
        lmax = jnp.maximum(jnp.maximum(ls[0], ls[1]), ls[2])
        es = [jnp.exp(l - lmax) for l in ls]
        inv = 1.0 / (es[0] + es[1] + es[2])
        o_att = None
        for e, o in zip(es, (o0, o1, o2)):
            a = e * inv
            hi = a.astype(BF16)
            lo = (a - hi.astype(F32)).astype(BF16)
            a_wide = jnp.dot(jnp.concatenate([hi, lo], axis=1), expand_ref[...],
                             preferred_element_type=F32)
            term = a_wide * _token_order(o, scr_ref)
            o_att = term if o_att is None else o_att + term
    else:
        (x_ref, gate_ref, oa_ref, za_ref, mo_ref, ga_ref, gm_ref,
         wpa_ref, wpm_ref, wout_ref, fg_ref, y_ref) = refs
        o_att = oa_ref[0]
    a_in = (o_att * za_ref[0].astype(F32)).astype(BF16)
    a_br = jnp.dot(a_in, wpa_ref[...], preferred_element_type=F32)
    m_br = jnp.dot(mo_ref[0].astype(BF16), wpm_ref[...], preferred_element_type=F32)
    merged = ga_ref[0].astype(F32) * a_br + gm_ref[0].astype(F32) * m_br
    y = x_ref[0] + gate_ref[0] * jnp.dot(merged.astype(BF16), wout_ref[...], preferred_element_type=F32)
    ms = jnp.mean(y * y, axis=-1, keepdims=True)
    y_ref[0] = y * lax.rsqrt(ms + EPS) * fg_ref[...]


def _post_call(x3, gate3, att_inputs, za, mo, ga, gm, wpa, wpm, wout, fgain, *, tm, merge, name):
    B, S, D = x3.shape
    row = lambda b, i: (b, i, 0)
    const2 = lambda b, i: (0, 0)
    if gate3.shape[1] == 1:
        gate_spec = pl.BlockSpec((1, 1, D), lambda b, i: (b, 0, 0))
    else:
        gate_spec = pl.BlockSpec((1, tm, D), row)
    def blk(a):
        if a.ndim == 2:
            return pl.BlockSpec(a.shape, const2)
        if a.ndim == 4:
            d = a.shape[1]
            return pl.BlockSpec((1, d, tm // d, a.shape[3]), lambda b, i: (b, 0, i, 0))
        return pl.BlockSpec((1, tm, a.shape[2]), row)

    in_specs = ([pl.BlockSpec((1, tm, D), row), gate_spec]
                + [blk(a) for a in att_inputs]
                + [blk(za), blk(mo), blk(ga), blk(gm),
                   pl.BlockSpec(wpa.shape, const2), pl.BlockSpec(wpm.shape, const2),
                   pl.BlockSpec(wout.shape, const2), pl.BlockSpec(fgain.shape, const2)])
    scratch = [pltpu.VMEM((ATT_WIDTH // LANES, tm, LANES), F32)] if merge else []
    return pl.pallas_call(
        functools.partial(_post_kernel, merge=merge),
        grid=(B, S // tm),
        in_specs=in_specs,
        out_specs=pl.BlockSpec((1, tm, D), row),
        out_shape=jax.ShapeDtypeStruct((B, S, D), F32),
        scratch_shapes=scratch,
        compiler_params=_params(("arbitrary", "arbitrary")),
        name=name,
    )(x3, gate3, *att_inputs, za, mo, ga, gm, wpa, wpm, wout, fgain)


def _projection_weights(w_in):
    D = w_in.shape[0]
    AW = ATT_WIDTH
    M = D
    off_k, off_v, off_za = 3 * AW, 6 * AW, 9 * AW
    off_xm = off_za + AW
    off_i = off_xm + 3 * M
    off_f = off_i + M_HEADS
    off_ga = off_f + M_HEADS
    w_main = w_in[:, :off_i].astype(BF16)
    pad = jnp.zeros((D, LANES - M_HEADS), BF16)
    w_tail = jnp.concatenate([w_in[:, off_ga:off_ga + 2 * D].astype(BF16),
                              w_in[:, off_i:off_f].astype(BF16), pad,
                              w_in[:, off_f:off_ga].astype(BF16), pad], axis=1)
    segs = {"za": ((0, off_za, AW),), "xm": ((0, off_xm, M),), "zm": ((0, off_xm + M, M),),
            "om": ((0, off_xm + 2 * M, M),), "ga": ((1, 0, D),), "gm": ((1, D, D),),
            "gates": ((1, 2 * D, 2 * LANES),)}
    for g in range(N_GROUPS):
        segs[f"q{g}"] = ((0, g * AW, AW),)
        segs[f"kv{g}"] = ((0, off_k + g * AW, AW), (0, off_v + g * AW, AW))
    return (w_main, w_tail), segs


def kernel(x_prompt, x_sample, cache_kv_w128, cache_kv_w512, cache_kv_w2048, state_conv, state_C, state_n, state_m, c_prompt, c_sample, rel_table, norm_gain, w_ada, b_ada, w_in, b_if, conv_w, conv_b, w_mq, w_mk, m_norm, m_skip, w_pa, w_pm, w_out, final_gain):
    B, S, D = x_prompt.shape
    DB, T, _ = x_sample.shape
    assert norm_gain.shape[0] == 1, "single-layer trunk"
    assert S % ATT_TILE == 0 and S % MLSTM_CHUNK == 0 and T == SUBLANES
    caches = (cache_kv_w128[0], cache_kv_w512[0], cache_kv_w2048[0])
    H = M_HEADS
    M = conv_w.shape[2]

    wp, segs = _projection_weights(w_in[0])
    names = ("q0", "q1", "q2", "za", "kv0", "kv1", "kv2", "xm", "zm", "om", "ga", "gm", "gates")
    seg_list = [segs[n] for n in names]
    gain = norm_gain[0].reshape(1, D)
    fgain = final_gain.reshape(1, D)
    wpa, wpm, wout = w_pa[0].astype(BF16), w_pm[0].astype(BF16), w_out[0].astype(BF16)
    wq, wk = w_mq[0].astype(BF16), w_mk[0].astype(BF16)
    convw, convb = conv_w[0], conv_b[0].reshape(1, M)
    mnorm, mskip = m_norm[0].reshape(1, M), m_skip[0].reshape(1, M)
    zpad = jnp.zeros((LANES - H,), F32)
    bif = jnp.concatenate([b_if[0, :H], zpad, b_if[0, H:], zpad]).reshape(1, 2 * LANES)

    ada = _ada_call(jnp.concatenate([c_prompt, c_sample], axis=0), w_ada[0], b_ada[0])
    shift, scale, gate = ada[:, :D], ada[:, D:2 * D], ada[:, 2 * D:]

    gate_act = {"za": "silu", "zm": "silu", "om": "sigmoid", "ga": "sigmoid", "gm": "sigmoid"}
    acts = [gate_act.get(n) for n in names]

    R = DB * T
    rep = lambda t: jnp.repeat(t[B:], T, axis=0).reshape(1, R, D)
    s_shift, s_scale, s_gate = rep(shift), rep(scale), rep(gate)
    xs = x_sample.reshape(1, R, D)
    sr = dict(zip(names, _proj_call(xs, gain, s_scale, s_shift, wp, seg_list, [F32] * 13,
                                    tm=R, row0=0, rows=R, name="proj_sample", acts=acts)))
    bcs, bns = [], []
    for g, (win, dil) in enumerate(ATT_GROUPS):
        bc, bn = _sample_bias(rel_table, g, win, dil, caches[g].shape[1], T)
        bcs.append(bc)
        bns.append(bn)
    cache_t = [jnp.transpose(c, (0, 2, 3, 4, 1)).reshape(DB, 2 * ATT_WIDTH, c.shape[1]) for c in caches]
    kvn_t = [sr[f"kv{g}"].reshape(DB, T, 2 * ATT_WIDTH) for g in range(3)]
    sattn_args = ([sr[f"q{g}"].reshape(DB, T, ATT_WIDTH) for g in range(3)], kvn_t, cache_t, bcs, bns)

    p_shift, p_scale, p_gate = (t[:B].reshape(B, 1, D) for t in (shift, scale, gate))
    dts = [BF16] * 12 + [F32]
    group_dil = {f"{p}{g}": dil for g, (_, dil) in enumerate(ATT_GROUPS) for p in ("q", "kv")}
    pr = dict(zip(names, _proj_call(x_prompt, gain, p_scale, p_shift, wp, seg_list, dts,
                                    tm=PROJ_TM, row0=0, rows=S, name="proj_prompt",
                                    dils=[group_dil.get(n, 1) for n in names], acts=acts)))
    att = []
    for g, (win, dil) in enumerate(ATT_GROUPS):
        planes = lambda a: a if a.ndim == 4 else a[:, None]
        att.append(_attn_call(planes(pr[f"q{g}"]), planes(pr[f"kv{g}"]), _prompt_bias(rel_table, g, dil),
                              f"attn_prompt_g{g}"))
    expand = np.zeros((2 * LANES, ATT_WIDTH), np.float32)
    for h in range(ATT_HEADS):
        expand[[LSE_LANES * h, LANES + LSE_LANES * h], h * ATT_HEAD_DIM:(h + 1) * ATT_HEAD_DIM] = 1.0
    expand = jnp.asarray(expand, BF16)
    mlstm_p_args = (pr["xm"], pr["zm"], pr["om"], pr["gates"],
                    jnp.zeros((B, SUBLANES, M), F32), jnp.zeros((B,) + state_C.shape[2:], F32),
                    jnp.zeros((B,) + state_n.shape[2:], F32), jnp.zeros((B, 1, LANES), F32),
                    convw, convb, wq, wk, mnorm, mskip, bif)
    mo_p, C_p, n_p, m_p = _mlstm_call(*mlstm_p_args, L=MLSTM_CHUNK, out_dtype=BF16, name="mlstm_prompt")
    sa = _sattn_call(*sattn_args, T)
    y_prompt = _post_call(x_prompt, p_gate, [a[0] for a in att] + [a[1] for a in att] + [expand],
                          pr["za"], mo_p, pr["ga"], pr["gm"], wpa, wpm, wout, fgain,
                          tm=POST_TM, merge=True, name="post_prompt")
    kv_p = []
    w_max = min(ATT_GROUPS[-1][0], S)
    (kv2_t,) = _proj_call(x_prompt, gain, p_scale, p_shift, wp[:1], [segs["kv2"]], [F32],
                          tm=PROJ_TM, row0=S - w_max, rows=w_max, name="tail_kv2")
    w_mid = min(ATT_GROUPS[1][0], S)
    kv0_t, kv1_t, xm_t = _proj_call(x_prompt, gain, p_scale, p_shift, wp[:1],
                                    [segs["kv0"], segs["kv1"], segs["xm"]], [F32] * 3,
                                    tm=w_mid, row0=S - w_mid, rows=w_mid, name="tail_kv01")
    kshape = lambda L: (1, B, L, 2, ATT_HEADS, ATT_HEAD_DIM)
    w0 = min(ATT_GROUPS[0][0], S)
    kv_p = [kv0_t[:, w_mid - w0:].reshape(kshape(w0)), kv1_t.reshape(kshape(w_mid)), kv2_t.reshape(kshape(w_max))]
    conv_p = xm_t[:, w_mid - (CONV_WIDTH - 1):][None]

    o_att_s = sa[0].reshape(1, R, ATT_WIDTH)
    kv_s = [jnp.transpose(c.reshape(DB, 2, ATT_HEADS, ATT_HEAD_DIM, c.shape[2]), (0, 4, 1, 2, 3))[None]
            for c in sa[1:]]
    conv_prev_s = jnp.concatenate([jnp.zeros((DB, SUBLANES - (CONV_WIDTH - 1), M), F32), state_conv[0]], axis=1)
    m0_s = jnp.concatenate([state_m[0], jnp.zeros((DB, LANES - H), F32)], axis=1).reshape(DB, 1, LANES)
    seqv = lambda t: t.reshape(DB, T, t.shape[-1])
    mo_s, C_s, n_s, m_s = _mlstm_call(
        seqv(sr["xm"]), seqv(sr["zm"]), seqv(sr["om"]), seqv(sr["gates"]),
        conv_prev_s, state_C[0], state_n[0], m0_s,
        convw, convb, wq, wk, mnorm, mskip, bif, L=T, out_dtype=F32, name="mlstm_sample")
    y_sample = _post_call(xs, s_gate, [o_att_s], sr["za"], mo_s.reshape(1, R, M), sr["ga"], sr["gm"],
                          wpa, wpm, wout, fgain, tm=R, merge=False, name="post_sample")
    conv_s = seqv(sr["xm"])[:, T - (CONV_WIDTH - 1):][None]

    return (y_prompt, y_sample.reshape(DB, T, D),
            kv_p[0], kv_s[0], kv_p[1], kv_s[1], kv_p[2], kv_s[2],
            conv_p, conv_s, C_p[None], C_s[None], n_p[None], n_s[None],
            m_p[:, 0, :H][None], m_s[:, 0, :H][None])
```

```python
import functools

import numpy as np
import jax
import jax.numpy as jnp
from jax import lax
from jax.experimental import pallas as pl
from jax.experimental.pallas import tpu as pltpu

F32 = jnp.float32
BF16 = jnp.bfloat16

ATT_GROUPS = ((128, 1), (512, 4), (2048, 16))
N_GROUPS = len(ATT_GROUPS)
ATT_HEADS = 8
ATT_HEAD_DIM = 64
ATT_WIDTH = ATT_HEADS * ATT_HEAD_DIM
WIN_STEPS = 128
ATT_SCALE = ATT_HEAD_DIM ** -0.5
N_BUCKETS = 32
MAX_DISTANCE = 2048
M_HEADS = 4
CONV_WIDTH = 4
EPS = 1e-6
NEG = -1e30

LANES = 128
SUBLANES = 8
VMEM_LIMIT = 56 * 1024 * 1024

ATT_TILE = 2048
PROJ_TM = 512
POST_TM = 512
MLSTM_CHUNK = 256


def _params(sem, vmem=VMEM_LIMIT):
    return pltpu.CompilerParams(dimension_semantics=sem, vmem_limit_bytes=vmem)


def _t5_bucket(dist):
    n = np.asarray(dist).astype(np.int64)
    max_exact = N_BUCKETS // 2
    nf = np.maximum(n, 1).astype(np.float32)
    large = max_exact + (np.log(nf / max_exact) / np.log(np.float32(MAX_DISTANCE / max_exact))
                         * (N_BUCKETS - max_exact)).astype(np.int64)
    large = np.minimum(large, N_BUCKETS - 1)
    return np.where(n < max_exact, n, large).astype(np.int32)


def _silu(x):
    return x * jax.nn.sigmoid(x)


def _ada_kernel(c_ref, w_ref, b_ref, o_ref):
    s = _silu(c_ref[...])
    o_ref[...] = jnp.dot(s, w_ref[...], preferred_element_type=F32,
                         precision=lax.Precision.HIGHEST) + b_ref[...]


def _ada_call(c, w, b):
    n, d = c.shape
    width = w.shape[1]
    tn = 512
    return pl.pallas_call(
        _ada_kernel,
        grid=(width // tn,),
        in_specs=[pl.BlockSpec((n, d), lambda j: (0, 0)),
                  pl.BlockSpec((d, tn), lambda j: (0, j)),
                  pl.BlockSpec((1, tn), lambda j: (0, j))],
        out_specs=pl.BlockSpec((n, tn), lambda j: (0, j)),
        out_shape=jax.ShapeDtypeStruct((n, width), F32),
        compiler_params=_params(("arbitrary",)),
        name="ada",
    )(c, w, b.reshape(1, width))


_ACTIVATIONS = {None: lambda v: v, "silu": _silu, "sigmoid": jax.nn.sigmoid}


def _proj_kernel(x_ref, gain_ref, scale_ref, shift_ref, *rest, n_w, segs, dils, acts):
    w_refs = rest[:n_w]
    rest = rest[n_w:]
    out_refs = rest[:len(segs)]
    x = x_ref[0]
    tm, D = x.shape
    ms = jnp.mean(x * x, axis=-1, keepdims=True)
    h = x * lax.rsqrt(ms + EPS) * gain_ref[...] * (1.0 + scale_ref[0]) + shift_ref[0]
    lhs = {1: h.astype(BF16)}
    lhs[0] = lhs[1]
    strides = sorted(set(dils) - {0, 1})
    if strides:
        hs_ref = rest[len(segs)]
        n_tiles = D // LANES
        for s in range(n_tiles):
            hs_ref[s] = h[:, s * LANES:(s + 1) * LANES]
        for d in strides:
            n = tm // d
            lhs[d] = jnp.concatenate(
                [jnp.concatenate([hs_ref[s, pl.ds(r, n, stride=d), :] for r in range(d)], axis=0)
                 for s in range(n_tiles)], axis=1).astype(BF16)
    for o_ref, pieces, d, act in zip(out_refs, segs, dils, acts):
        parts = [_ACTIVATIONS[act](jnp.dot(lhs[d], w_refs[wi][:, c0:c0 + width],
                                           preferred_element_type=F32)).astype(o_ref.dtype)
                 for wi, c0, width in pieces]
        res = parts[0] if len(parts) == 1 else jnp.concatenate(parts, axis=1)
        if d == 0:
            o_ref[0] = res.T
        elif d == 1:
            o_ref[0] = res
        else:
            n = tm // d
            for r in range(d):
                o_ref[0, r] = res[r * n:(r + 1) * n, :]


def _proj_call(x3, gain, scale3, shift3, ws, segs, dtypes, *, tm, row0, rows, name, dils=None, acts=None):
    B, S, D = x3.shape
    nrb = rows // tm
    rb0 = row0 // tm
    dils = tuple(dils) if dils is not None else (1,) * len(segs)
    acts = tuple(acts) if acts is not None else (None,) * len(segs)
    per_row = scale3.shape[1] != 1
    if per_row:
        mod_spec = pl.BlockSpec((1, tm, D), lambda b, i: (b, rb0 + i, 0))
    else:
        mod_spec = pl.BlockSpec((1, 1, D), lambda b, i: (b, 0, 0))
    out_shape, out_specs = [], []
    for pieces, dt, d in zip(segs, dtypes, dils):
        wd = sum(p[2] for p in pieces)
        if d == 0:
            out_shape.append(jax.ShapeDtypeStruct((B, wd, rows), dt))
            out_specs.append(pl.BlockSpec((1, wd, tm), lambda b, i: (b, 0, i)))
        elif d == 1:
            out_shape.append(jax.ShapeDtypeStruct((B, rows, wd), dt))
            out_specs.append(pl.BlockSpec((1, tm, wd), lambda b, i: (b, i, 0)))
        else:
            out_shape.append(jax.ShapeDtypeStruct((B, d, rows // d, wd), dt))
            out_specs.append(pl.BlockSpec((1, d, tm // d, wd), lambda b, i: (b, 0, i, 0)))
    scratch = [pltpu.VMEM((D // LANES, tm, LANES), F32)] if any(d > 1 for d in dils) else []
    return pl.pallas_call(
        functools.partial(_proj_kernel, n_w=len(ws), segs=tuple(segs), dils=dils, acts=acts),
        grid=(B, nrb),
        in_specs=[pl.BlockSpec((1, tm, D), lambda b, i: (b, rb0 + i, 0)),
                  pl.BlockSpec((1, D), lambda b, i: (0, 0)),
                  mod_spec, mod_spec]
                 + [pl.BlockSpec(w.shape, lambda b, i: (0, 0), pipeline_mode=pl.Buffered(1)) for w in ws],
        out_specs=out_specs,
        out_shape=out_shape,
        scratch_shapes=scratch,
        compiler_params=_params(("arbitrary", "arbitrary")),
        name=name,
    )(x3, gain, scale3, shift3, *ws)


HEADS_PER_SLAB = LANES // ATT_HEAD_DIM
N_SLABS = ATT_HEADS // HEADS_PER_SLAB
LSE_LANES = LANES // ATT_HEADS


def _attn_unit(q, kv, bias_ref, prev_mask, o_ref, l_ref, at):
    dn = (((1,), (1,)), ((), ()))
    nk = 2 * WIN_STEPS
    lane_q = lax.broadcasted_iota(jnp.int32, (WIN_STEPS, LANES), 1) < ATT_HEAD_DIM
    lane_k = lax.broadcasted_iota(jnp.int32, (nk, LANES), 1) < ATT_HEAD_DIM
    ones_lo = jnp.where(lane_k, 1.0, 0.0).astype(BF16)
    ones_hi = jnp.where(lane_k, 0.0, 1.0).astype(BF16)
    zero_q = jnp.zeros((WIN_STEPS, LANES), BF16)
    zero_k = jnp.zeros((nk, LANES), BF16)

    def scores(m):
        cs = slice(m * LANES, (m + 1) * LANES)
        qs = q[:, cs]
        ks = kv[:, cs]
        out = []
        for hh in range(HEADS_PER_SLAB):
            qm = jnp.where(lane_q, qs, zero_q) if hh == 0 else jnp.where(lane_q, zero_q, qs)
            s = lax.dot_general(qm, ks, dn, preferred_element_type=F32) + bias_ref[m * HEADS_PER_SLAB + hh]
            out.append(jnp.concatenate([s[:, :WIN_STEPS] + prev_mask, s[:, WIN_STEPS:]], axis=1))
        return out

    def finish(m, ss):
        cs = slice(m * LANES, (m + 1) * LANES)
        vs = kv[:, ATT_WIDTH + m * LANES:ATT_WIDTH + (m + 1) * LANES]
        ps, mxs = [], []
        for s in ss:
            mx = jnp.max(jnp.maximum(s[:, :WIN_STEPS], s[:, WIN_STEPS:]), -1, keepdims=True)
            ps.append(jnp.exp(s - mx).astype(BF16))
            mxs.append(mx)
        pcat = jnp.concatenate(ps, axis=1)
        vpair = jnp.concatenate(
            [jnp.concatenate([jnp.where(lane_k, vs, zero_k), ones_lo], axis=1),
             jnp.concatenate([jnp.where(lane_k, zero_k, vs), ones_hi], axis=1)], axis=0)
        acc = jnp.dot(pcat, vpair, preferred_element_type=F32)
        den = acc[:, LANES:]
        o_ref[at + (cs,)] = (acc[:, :LANES] / den).astype(o_ref.dtype)
        lse = jnp.where(lane_q, mxs[0], mxs[1]) + jnp.log(den)
        return pltpu.roll(lse, (LSE_LANES * HEADS_PER_SLAB * m - 48) % LANES, axis=1)

    lane = lax.broadcasted_iota(jnp.int32, (WIN_STEPS, LANES), 1)
    lse_c = None
    pending = {0: scores(0), 1: scores(1)}
    for m in range(N_SLABS):
        if m + 2 < N_SLABS:
            pending[m + 2] = scores(m + 2)
        part = finish(m, pending.pop(m))
        lse_c = part if lse_c is None else jnp.where(lane >= LSE_LANES * HEADS_PER_SLAB * m, part, lse_c)
    l_ref[at + (slice(None),)] = lse_c


def _attn_kernel(q_ref, kvc_ref, kvp_ref, bias_ref, o_ref, l_ref, *, ns):
    d = q_ref.shape[1]
    first_tile = pl.program_id(1) == 0

    def body(idx, carry):
        rr = idx // ns
        j = idx % ns
        rc = pl.ds(pl.multiple_of(j * WIN_STEPS, WIN_STEPS), WIN_STEPS)
        q = q_ref[0, rr, rc, :] * ATT_SCALE
        kv_prev = kvp_ref[0, rr]
        if ns > 1:
            rp = pl.ds(pl.multiple_of(jnp.maximum(j - 1, 0) * WIN_STEPS, WIN_STEPS), WIN_STEPS)
            kv_prev = jnp.where(j == 0, kv_prev, kvc_ref[0, rr, rp, :])
        kv = jnp.concatenate([kv_prev, kvc_ref[0, rr, rc, :]], axis=0)
        prev_mask = jnp.where(first_tile & (j == 0), NEG, 0.0).astype(F32)
        _attn_unit(q, kv, bias_ref, prev_mask, o_ref, l_ref, (0, rr, rc))
        return carry

    lax.fori_loop(0, d * ns, body, 0, unroll=2)


def _attn_call(q, kv, bias, name):
    B, d, U, _ = q.shape
    ns = ATT_TILE // (WIN_STEPS * d)
    rows = ns * WIN_STEPS
    blk = lambda width: pl.BlockSpec((1, d, rows, width), lambda b, t: (b, 0, t, 0))
    return pl.pallas_call(
        functools.partial(_attn_kernel, ns=ns),
        grid=(B, U // rows),
        in_specs=[blk(ATT_WIDTH), blk(2 * ATT_WIDTH),
                  pl.BlockSpec((1, d, WIN_STEPS, 2 * ATT_WIDTH),
                               lambda b, t: (b, 0, jnp.maximum(t * ns - 1, 0), 0)),
                  pl.BlockSpec(bias.shape, lambda b, t: (0, 0, 0))],
        out_specs=[blk(ATT_WIDTH), blk(LANES)],
        out_shape=[jax.ShapeDtypeStruct((B, d, U, ATT_WIDTH), BF16),
                   jax.ShapeDtypeStruct((B, d, U, LANES), F32)],
        compiler_params=_params(("arbitrary", "arbitrary")),
        name=name,
    )(q, kv, kv, bias)


def _stride_bias(rel_table, g, d):
    bucket = _t5_bucket(np.arange(WIN_STEPS + 1) * d)
    onehot = jnp.asarray(np.eye(N_BUCKETS, dtype=np.float32)[bucket])
    tbl = rel_table[:, g * ATT_HEADS:(g + 1) * ATT_HEADS].astype(F32)
    return jnp.dot(onehot, tbl, precision=lax.Precision.HIGHEST).T


def _prompt_bias(rel_table, g, d):
    vals = _stride_bias(rel_table, g, d)
    n = WIN_STEPS
    period = 3 * n
    wp = jnp.concatenate([jnp.full((ATT_HEADS, n - 1), NEG, F32), vals[:, ::-1],
                          jnp.full((ATT_HEADS, n), NEG, F32)], axis=1)
    flat = jnp.tile(wp, (1, n))[:, :n * (period - 1)]
    return flat.reshape(ATT_HEADS, n, period - 1)[:, :, n - 1:n - 1 + 2 * n]


SHIFT_VREGS = 128


def _sattn_kernel(*refs, T):
    q_refs = refs[0:3]
    kvnew_refs = refs[3:6]
    cache_refs = refs[6:9]
    bc_refs = refs[9:12]
    bn_refs = refs[12:15]
    o_ref = refs[15]
    co_refs = refs[16:19]
    kvn_refs = refs[19:22]
    HT = ATT_HEADS * T
    dn = (((1,), (1,)), ((), ()))
    row_head = lax.broadcasted_iota(jnp.int32, (HT, ATT_WIDTH), 0) // T
    col_head = lax.broadcasted_iota(jnp.int32, (HT, ATT_WIDTH), 1) // ATT_HEAD_DIM
    head_mask = row_head == col_head

    for g in range(N_GROUPS):
        rows = jnp.concatenate([jnp.zeros((LANES - T, 2 * ATT_WIDTH), F32), kvnew_refs[g][0]], axis=0)
        kvn_refs[g][0] = rows.T

    stats = []
    for g in range(N_GROUPS):
        q = q_refs[g][0] * ATT_SCALE
        qexp = jnp.where(head_mask, jnp.concatenate([q] * ATT_HEADS, axis=0), 0.0).astype(BF16)
        kn = kvn_refs[g][0, :ATT_WIDTH, :].astype(BF16)
        kc = cache_refs[g][0, :ATT_WIDTH, :].astype(BF16)
        lc = jnp.dot(qexp, kc, preferred_element_type=F32) + bc_refs[g][...]
        ln = jnp.dot(qexp, kn, preferred_element_type=F32) + bn_refs[g][...]
        mx = jnp.maximum(jnp.max(lc, -1, keepdims=True), jnp.max(ln, -1, keepdims=True))
        pc = jnp.exp(lc - mx)
        pn = jnp.exp(ln - mx)
        ssum = jnp.sum(pc, -1, keepdims=True) + jnp.sum(pn, -1, keepdims=True)
        stats.append((pc, pn, ssum, mx + jnp.log(ssum)))

    lse_max = jnp.maximum(jnp.maximum(stats[0][3], stats[1][3]), stats[2][3])
    es = [jnp.exp(st[3] - lse_max) for st in stats]
    esum = es[0] + es[1] + es[2]
    acc = jnp.zeros((HT, ATT_WIDTH), F32)
    for g in range(N_GROUPS):
        pc, pn, ssum, _ = stats[g]
        w = es[g] / (esum * ssum)
        vc = cache_refs[g][0, ATT_WIDTH:, :].astype(BF16)
        vn = kvn_refs[g][0, ATT_WIDTH:, :].astype(BF16)
        acc = acc + lax.dot_general((pc * w).astype(BF16), vc, dn, preferred_element_type=F32)
        acc = acc + lax.dot_general((pn * w).astype(BF16), vn, dn, preferred_element_type=F32)
    lane_head = lax.broadcasted_iota(jnp.int32, (T, ATT_WIDTH), 1) // ATT_HEAD_DIM
    o = jnp.zeros((T, ATT_WIDTH), F32)
    for h in range(ATT_HEADS):
        o = o + jnp.where(lane_head == h, acc[h * T:(h + 1) * T, :], 0.0)
    o_ref[0] = o

    for g in range(N_GROUPS):
        L = cache_refs[g].shape[2]
        nrows = cache_refs[g].shape[1]
        chunk = min(nrows, SHIFT_VREGS * SUBLANES * LANES // L)
        is_new = lax.broadcasted_iota(jnp.int32, (chunk, LANES), 1) >= LANES - T

        def shift_rows(i, carry, g=g, L=L, chunk=chunk, is_new=is_new):
            rs = pl.ds(pl.multiple_of(i * chunk, chunk), chunk)
            rolled = pltpu.roll(cache_refs[g][0, rs, :], L - T, axis=1)
            if L > LANES:
                co_refs[g][0, rs, 0:L - LANES] = rolled[:, 0:L - LANES]
            co_refs[g][0, rs, L - LANES:L] = jnp.where(is_new, kvn_refs[g][0, rs, :], rolled[:, L - LANES:])
            return carry

        lax.fori_loop(0, nrows // chunk, shift_rows, 0)


def _sample_bias(rel_table, g, W, d, Lb, T):
    vals = _stride_bias(rel_table, g, d)
    n = Lb + T
    by_dist = jnp.concatenate([vals[:, :, None], jnp.full((ATT_HEADS, WIN_STEPS + 1, d - 1), NEG, F32)],
                              axis=2).reshape(ATT_HEADS, (WIN_STEPS + 1) * d)
    if by_dist.shape[1] < n:
        by_dist = jnp.concatenate([by_dist, jnp.full((ATT_HEADS, n - by_dist.shape[1]), NEG, F32)], axis=1)
    rev = jnp.concatenate([by_dist[:, :n][:, ::-1], jnp.full((ATT_HEADS, T), NEG, F32)], axis=1)
    rows = jnp.stack([rev[:, T - 1 - t:T - 1 - t + n] for t in range(T)], axis=1)
    rows = rows.reshape(ATT_HEADS * T, n)
    bc = rows[:, :Lb]
    bn = jnp.concatenate([jnp.full((ATT_HEADS * T, LANES - T), NEG, F32), rows[:, Lb:]], axis=1)
    return bc, bn


def _sattn_call(qs, kvns, caches, bcs, bns, T):
    DB = caches[0].shape[0]
    rows = caches[0].shape[1]
    per_seq = lambda b: (b, 0, 0)
    const2 = lambda b: (0, 0)
    in_specs = ([pl.BlockSpec((1, T, ATT_WIDTH), per_seq)] * 3
                + [pl.BlockSpec((1,) + x.shape[1:], per_seq) for x in kvns]
                + [pl.BlockSpec((1,) + c.shape[1:], per_seq) for c in caches]
                + [pl.BlockSpec(x.shape, const2) for x in bcs]
                + [pl.BlockSpec(x.shape, const2) for x in bns])
    out_specs = ([pl.BlockSpec((1, T, ATT_WIDTH), per_seq)]
                 + [pl.BlockSpec((1,) + c.shape[1:], per_seq) for c in caches])
    out_shape = ([jax.ShapeDtypeStruct((DB, T, ATT_WIDTH), F32)]
                 + [jax.ShapeDtypeStruct(c.shape, F32) for c in caches])
    return pl.pallas_call(
        functools.partial(_sattn_kernel, T=T),
        grid=(DB,),
        in_specs=in_specs, out_specs=out_specs, out_shape=out_shape,
        scratch_shapes=[pltpu.VMEM((1, rows, LANES), F32)] * N_GROUPS,
        compiler_params=_params(("arbitrary",)),
        name="sample_attn",
    )(*qs, *kvns, *caches, *bcs, *bns)


def _scan_rows(x, op, fill):
    n = x.shape[0]
    rowid = lax.broadcasted_iota(jnp.int32, x.shape, 0)
    s = 1
    while s < n:
        shifted = pltpu.roll(x, s, axis=0)
        x = op(x, jnp.where(rowid >= s, shifted, fill))
        s *= 2
    return x


def _pad_rows(x, n, fill=0.0):
    if x.shape[0] == n:
        return x
    return jnp.concatenate([x, jnp.full((n - x.shape[0],) + x.shape[1:], fill, x.dtype)], axis=0)


def _mlstm_kernel(xm_ref, zm_ref, om_ref, g_ref, cprev_ref, C0_ref, n0_ref, m0_ref,
                  convw_ref, convb_ref, wq_ref, wk_ref, mnorm_ref, mskip_ref, bif_ref,
                  mo_ref, C_ref, n_ref, m_ref, xc_ref, *, L):
    LS = max(L, LANES)
    DV = C_ref.shape[2]
    DK = C_ref.shape[3]
    c = pl.program_id(1)

    @pl.when(c == 0)
    def _():
        xc_ref[0:SUBLANES, :] = cprev_ref[0]
        C_ref[...] = C0_ref[...]
        n_ref[...] = n0_ref[...]
        m_ref[...] = m0_ref[...]

    xm_b = xm_ref[0].astype(BF16)
    xm_f = xm_ref[0].astype(F32)
    t_id = lax.broadcasted_iota(jnp.int32, (L, LS), 0)
    s_id = lax.broadcasted_iota(jnp.int32, (L, LS), 1)
    xc_ref[SUBLANES:SUBLANES + L, :] = xm_f
    conv = convb_ref[...]
    for j in range(CONV_WIDTH):
        off = SUBLANES - (CONV_WIDTH - 1) + j
        conv = conv + convw_ref[j:j + 1, :] * xc_ref[off:off + L, :]
    xc_ref[0:SUBLANES, :] = xc_ref[L:L + SUBLANES, :]
    c_act = _silu(conv)
    cb = c_act.astype(BF16)

    i_pre = g_ref[0] + bif_ref[...]
    logf = jax.nn.log_sigmoid(pltpu.roll(i_pre, LANES - M_HEADS, axis=1))
    b = _scan_rows(logf, jnp.add, 0.0)
    a = i_pre - b
    ca = _scan_rows(a, jnp.maximum, NEG)
    m_prev = m_ref[0]
    mm = jnp.maximum(ca, m_prev)
    u = -mm
    w_inter = jnp.exp(u + m_prev)
    emt = jnp.exp(-(b + mm))
    bL = b[L - 1:L, :]
    m_new = bL + jnp.maximum(m_prev, ca[L - 1:L, :])
    wk = jnp.exp(bL + a - m_new)
    wC = jnp.exp(bL + m_prev - m_new)
    m_ref[0] = m_new

    causal = s_id <= t_id
    eye = s_id == t_id
    dn_t = (((1,), (1,)), ((), ()))

    heads = range(M_HEADS)
    vsl = [slice(h * DV, (h + 1) * DV) for h in heads]
    qs, ks, vss = [], [], []
    for h in heads:
        ch = cb[:, vsl[h]]
        qs.append((jnp.dot(ch, wq_ref[h], preferred_element_type=F32) * (DK ** -0.5)).astype(BF16))
        ks.append(_pad_rows(jnp.dot(ch, wk_ref[h], preferred_element_type=F32).astype(BF16), LS))
        vss.append(_pad_rows(xm_b[:, vsl[h]], LS))
    C_old = [C_ref[0, h] for h in heads]
    n_old = [n_ref[0, h:h + 1, :] for h in heads]
    scs, inters = [], []
    for h in heads:
        a_row = jnp.sum(jnp.where(eye, a[:, h:h + 1], 0.0), axis=0, keepdims=True)
        w_intra = jnp.exp(jnp.where(causal, u[:, h:h + 1] + a_row, NEG))
        scs.append(lax.dot_general(qs[h], ks[h], dn_t, preferred_element_type=F32) * w_intra)
        inters.append(lax.dot_general(qs[h], C_old[h].astype(BF16), dn_t, preferred_element_type=F32))
    for h in heads:
        wi = w_inter[:, h:h + 1]
        num = jnp.dot(scs[h].astype(BF16), vss[h], preferred_element_type=F32) + wi * inters[h]
        den = (jnp.sum(scs[h], -1, keepdims=True)
               + wi * jnp.sum(qs[h].astype(F32) * n_old[h], -1, keepdims=True))
        hcell = num / jnp.maximum(jnp.abs(den), emt[:, h:h + 1])
        mu = jnp.mean(hcell, -1, keepdims=True)
        hc = hcell - mu
        hn = hc * lax.rsqrt(jnp.mean(hc * hc, -1, keepdims=True) + EPS)
        out = ((om_ref[0, :, vsl[h]].astype(F32) * (hn * mnorm_ref[:, vsl[h]])
                + mskip_ref[:, vsl[h]] * c_act[:, vsl[h]]) * zm_ref[0, :, vsl[h]].astype(F32))
        mo_ref[0, :, vsl[h]] = out.astype(mo_ref.dtype)
    for h in heads:
        wk_s = _pad_rows(wk[:, h:h + 1], LS)
        vw = (vss[h].astype(F32) * wk_s).astype(BF16)
        wc = wC[:, h:h + 1]
        C_ref[0, h] = wc * C_old[h] + lax.dot_general(vw, ks[h], (((0,), (0,)), ((), ())),
                                                      preferred_element_type=F32)
        n_ref[0, h:h + 1, :] = wc * n_old[h] + jnp.sum(ks[h].astype(F32) * wk_s, axis=0, keepdims=True)


def _mlstm_specs(xm, zm, om, gates, conv_prev, C0, n0, m0, convw, convb, wq, wk, mnorm, mskip, bif,
                 *, L, out_dtype):
    N, S, M = xm.shape
    H, DV, DK = C0.shape[1:]
    seq = lambda b, c: (b, c, 0)
    per_b3 = lambda b, c: (b, 0, 0)
    per_b4 = lambda b, c: (b, 0, 0, 0)
    const2 = lambda b, c: (0, 0)
    const3 = lambda b, c: (0, 0, 0)
    in_specs = [pl.BlockSpec((1, L, M), seq), pl.BlockSpec((1, L, M), seq), pl.BlockSpec((1, L, M), seq),
                pl.BlockSpec((1, L, LANES), seq),
                pl.BlockSpec((1, SUBLANES, M), per_b3),
                pl.BlockSpec((1, H, DV, DK), per_b4),
                pl.BlockSpec((1, H, DK), per_b3),
                pl.BlockSpec((1, 1, LANES), per_b3),
                pl.BlockSpec(convw.shape, const2), pl.BlockSpec(convb.shape, const2),
                pl.BlockSpec(wq.shape, const3), pl.BlockSpec(wk.shape, const3),
                pl.BlockSpec(mnorm.shape, const2), pl.BlockSpec(mskip.shape, const2),
                pl.BlockSpec(bif.shape, const2)]
    out_specs = [pl.BlockSpec((1, L, M), seq),
                 pl.BlockSpec((1, H, DV, DK), per_b4),
                 pl.BlockSpec((1, H, DK), per_b3),
                 pl.BlockSpec((1, 1, LANES), per_b3)]
    out_shape = [jax.ShapeDtypeStruct((N, S, M), out_dtype),
                 jax.ShapeDtypeStruct((N, H, DV, DK), F32),
                 jax.ShapeDtypeStruct((N, H, DK), F32),
                 jax.ShapeDtypeStruct((N, 1, LANES), F32)]
    scratch = [pltpu.VMEM((L + 2 * SUBLANES, M), F32)]
    args = (xm, zm, om, gates, conv_prev, C0, n0, m0, convw, convb, wq, wk, mnorm, mskip, bif)
    return in_specs, out_specs, out_shape, scratch, args


def _mlstm_call(*args, L, out_dtype, name):
    in_specs, out_specs, out_shape, scratch, args = _mlstm_specs(*args, L=L, out_dtype=out_dtype)
    N, S, _ = args[0].shape
    return pl.pallas_call(
        functools.partial(_mlstm_kernel, L=L),
        grid=(N, S // L),
        in_specs=in_specs, out_specs=out_specs, out_shape=out_shape, scratch_shapes=scratch,
        compiler_params=_params(("arbitrary", "arbitrary")),
        name=name,
    )(*args)


def _token_order(src_ref, scr_ref):
    _, d, n, width = src_ref.shape
    if d == 1:
        return src_ref[0, 0].astype(F32)
    n_tiles = width // LANES
    for r in range(d):
        for s in range(n_tiles):
            scr_ref[s, pl.ds(r, n, stride=d), :] = src_ref[0, r, :, s * LANES:(s + 1) * LANES].astype(F32)
    return jnp.concatenate([scr_ref[s] for s in range(n_tiles)], axis=1)


def _post_kernel(*refs, merge):
    if merge:
        (x_ref, gate_ref, o0, o1, o2, l0, l1, l2, expand_ref, za_ref, mo_ref, ga_ref, gm_ref,
         wpa_ref, wpm_ref, wout_ref, fg_ref, y_ref, scr_ref) = refs
        ls = [_token_order(l, scr_ref) for l in (l0, l1, l2)]
        lmax = jnp.maximum(jnp.maximum(ls[0], ls[1]), ls[2])
        es = [jnp.exp(l - lmax) for l in ls]
        inv = 1.0 / (es[0] + es[1] + es[2])
        o_att = None
        for e, o in zip(es, (o0, o1, o2)):
            a = e * inv
            hi = a.astype(BF16)
            lo = (a - hi.astype(F32)).astype(BF16)
            a_wide = jnp.dot(jnp.concatenate([hi, lo], axis=1), expand_ref[...],
                             preferred_element_type=F32)
            term = a_wide * _token_order(o, scr_ref)
            o_att = term if o_att is None else o_att + term
    else:
        (x_ref, gate_ref, oa_ref, za_ref, mo_ref, ga_ref, gm_ref,
         wpa_ref, wpm_ref, wout_ref, fg_ref, y_ref) = refs
        o_att = oa_ref[0]
    a_in = (o_att * za_ref[0].astype(F32)).astype(BF16)
    a_br = jnp.dot(a_in, wpa_ref[...], preferred_element_type=F32)
    m_br = jnp.dot(mo_ref[0].astype(BF16), wpm_ref[...], preferred_element_type=F32)
    merged = ga_ref[0].astype(F32) * a_br + gm_ref[0].astype(F32) * m_br
    y = x_ref[0] + gate_ref[0] * jnp.dot(merged.astype(BF16), wout_ref[...], preferred_element_type=F32)
    ms = jnp.mean(y * y, axis=-1, keepdims=True)
    y_ref[0] = y * lax.rsqrt(ms + EPS) * fg_ref[...]


def _post_call(x3, gate3, att_inputs, za, mo, ga, gm, wpa, wpm, wout, fgain, *, tm, merge, name):
    B, S, D = x3.shape
    row = lambda b, i: (b, i, 0)
    const2 = lambda b, i: (0, 0)
    if gate3.shape[1] == 1:
        gate_spec = pl.BlockSpec((1, 1, D), lambda b, i: (b, 0, 0))
    else:
        gate_spec = pl.BlockSpec((1, tm, D), row)
    def blk(a):
        if a.ndim == 2:
            return pl.BlockSpec(a.shape, const2)
        if a.ndim == 4:
            d = a.shape[1]
            return pl.BlockSpec((1, d, tm // d, a.shape[3]), lambda b, i: (b, 0, i, 0))
        return pl.BlockSpec((1, tm, a.shape[2]), row)

    in_specs = ([pl.BlockSpec((1, tm, D), row), gate_spec]
                + [blk(a) for a in att_inputs]
                + [blk(za), blk(mo), blk(ga), blk(gm),
                   pl.BlockSpec(wpa.shape, const2), pl.BlockSpec(wpm.shape, const2),
                   pl.BlockSpec(wout.shape, const2), pl.BlockSpec(fgain.shape, const2)])
    scratch = [pltpu.VMEM((ATT_WIDTH // LANES, tm, LANES), F32)] if merge else []
    return pl.pallas_call(
        functools.partial(_post_kernel, merge=merge),
        grid=(B, S // tm),
        in_specs=in_specs,
        out_specs=pl.BlockSpec((1, tm, D), row),
        out_shape=jax.ShapeDtypeStruct((B, S, D), F32),
        scratch_shapes=scratch,
        compiler_params=_params(("arbitrary", "arbitrary")),
        name=name,
    )(x3, gate3, *att_inputs, za, mo, ga, gm, wpa, wpm, wout, fgain)


def _cast_kernel(x_ref, o_ref):
    o_ref[...] = x_ref[...].T.astype(o_ref.dtype)


def _cast_bf16_call(wt, ncols):
    k = wt.shape[1]
    tn = 1024
    return pl.pallas_call(
        _cast_kernel,
        grid=(ncols // tn,),
        in_specs=[pl.BlockSpec((tn, k), lambda j: (j, 0))],
        out_specs=pl.BlockSpec((k, tn), lambda j: (0, j)),
        out_shape=jax.ShapeDtypeStruct((k, ncols), BF16),
        compiler_params=_params(("arbitrary",)),
        name="cast_weights",
    )(wt)


def _tail_cast_kernel(a_ref, b_ref, c_ref, o_ref, *, n_gate, n_merge):
    rows = jnp.concatenate([a_ref[...], b_ref[...], c_ref[0:SUBLANES, :]], axis=0)
    gates = jnp.concatenate([rows[0:n_gate], jnp.zeros((LANES - n_gate, rows.shape[1]), F32)], axis=0)
    out = jnp.concatenate([rows[n_gate:n_gate + n_merge], gates], axis=0)
    o_ref[...] = out.T.astype(o_ref.dtype)


def _tail_cast_call(wt, first, n_gate, n_merge):
    k = wt.shape[1]
    tn = n_merge // 2
    j0 = first // tn
    assert first % tn == 0 and n_gate == SUBLANES
    return pl.pallas_call(
        functools.partial(_tail_cast_kernel, n_gate=n_gate, n_merge=n_merge),
        grid=(1,),
        in_specs=[pl.BlockSpec((tn, k), lambda j: (j0, 0), pipeline_mode=pl.Buffered(1)),
                  pl.BlockSpec((tn, k), lambda j: (j0 + 1, 0), pipeline_mode=pl.Buffered(1)),
                  pl.BlockSpec((tn, k), lambda j: (j0 + 2, 0), pipeline_mode=pl.Buffered(1))],
        out_specs=pl.BlockSpec((k, n_merge + LANES), lambda j: (0, 0)),
        out_shape=jax.ShapeDtypeStruct((k, n_merge + LANES), BF16),
        compiler_params=_params(("arbitrary",)),
        name="cast_tail_weights",
    )(wt, wt, wt)


def _projection_weights(w_in):
    D = w_in.shape[0]
    AW = ATT_WIDTH
    M = D
    off_k, off_v, off_za = 3 * AW, 6 * AW, 9 * AW
    off_xm = off_za + AW
    off_i = off_xm + 3 * M
    wt = jnp.transpose(w_in)
    w_main = _cast_bf16_call(wt, off_i)
    w_tail = _tail_cast_call(wt, off_i, 2 * M_HEADS, 2 * D)
    segs = {"za": ((0, off_za, AW),), "xm": ((0, off_xm, M),), "zm": ((0, off_xm + M, M),),
            "om": ((0, off_xm + 2 * M, M),), "ga": ((1, 0, D),), "gm": ((1, D, D),),
            "gates": ((1, 2 * D, LANES),)}
    for g in range(N_GROUPS):
        segs[f"q{g}"] = ((0, g * AW, AW),)
        segs[f"kv{g}"] = ((0, off_k + g * AW, AW), (0, off_v + g * AW, AW))
    return (w_main, w_tail), segs


def kernel(x_prompt, x_sample, cache_kv_w128, cache_kv_w512, cache_kv_w2048, state_conv, state_C, state_n, state_m, c_prompt, c_sample, rel_table, norm_gain, w_ada, b_ada, w_in, b_if, conv_w, conv_b, w_mq, w_mk, m_norm, m_skip, w_pa, w_pm, w_out, final_gain):
    B, S, D = x_prompt.shape
    DB, T, _ = x_sample.shape
    assert norm_gain.shape[0] == 1, "single-layer trunk"
    assert S % ATT_TILE == 0 and S % MLSTM_CHUNK == 0 and T == SUBLANES
    caches = (cache_kv_w128[0], cache_kv_w512[0], cache_kv_w2048[0])
    H = M_HEADS
    M = conv_w.shape[2]

    wp, segs = _projection_weights(w_in[0])
    names = ("q0", "q1", "q2", "za", "kv0", "kv1", "kv2", "xm", "zm", "om", "ga", "gm", "gates")
    seg_list = [segs[n] for n in names]
    gain = norm_gain[0].reshape(1, D)
    fgain = final_gain.reshape(1, D)
    wpa, wpm, wout = w_pa[0].astype(BF16), w_pm[0].astype(BF16), w_out[0].astype(BF16)
    wq, wk = w_mq[0].astype(BF16), w_mk[0].astype(BF16)
    convw, convb = conv_w[0], conv_b[0].reshape(1, M)
    mnorm, mskip = m_norm[0].reshape(1, M), m_skip[0].reshape(1, M)
    bif = jnp.concatenate([b_if[0], jnp.zeros((LANES - 2 * H,), F32)]).reshape(1, LANES)

    ada = _ada_call(jnp.concatenate([c_prompt, c_sample], axis=0), w_ada[0], b_ada[0])
    shift, scale, gate = ada[:, :D], ada[:, D:2 * D], ada[:, 2 * D:]

    gate_act = {"za": "silu", "zm": "silu", "om": "sigmoid", "ga": "sigmoid", "gm": "sigmoid"}
    acts = [gate_act.get(n) for n in names]

    R = DB * T
    rep = lambda t: jnp.repeat(t[B:], T, axis=0).reshape(1, R, D)
    s_shift, s_scale, s_gate = rep(shift), rep(scale), rep(gate)
    xs = x_sample.reshape(1, R, D)
    sr = dict(zip(names, _proj_call(xs, gain, s_scale, s_shift, wp, seg_list, [F32] * 13,
                                    tm=R, row0=0, rows=R, name="proj_sample", acts=acts)))
    bcs, bns = [], []
    for g, (win, dil) in enumerate(ATT_GROUPS):
        bc, bn = _sample_bias(rel_table, g, win, dil, caches[g].shape[1], T)
        bcs.append(bc)
        bns.append(bn)
    cache_t = [jnp.transpose(c, (0, 2, 3, 4, 1)).reshape(DB, 2 * ATT_WIDTH, c.shape[1]) for c in caches]
    kvn_t = [sr[f"kv{g}"].reshape(DB, T, 2 * ATT_WIDTH) for g in range(3)]
    sattn_args = ([sr[f"q{g}"].reshape(DB, T, ATT_WIDTH) for g in range(3)], kvn_t, cache_t, bcs, bns)

    p_shift, p_scale, p_gate = (t[:B].reshape(B, 1, D) for t in (shift, scale, gate))
    dts = [BF16] * 12 + [F32]
    group_dil = {f"{p}{g}": dil for g, (_, dil) in enumerate(ATT_GROUPS) for p in ("q", "kv")}
    pr = dict(zip(names, _proj_call(x_prompt, gain, p_scale, p_shift, wp, seg_list, dts,
                                    tm=PROJ_TM, row0=0, rows=S, name="proj_prompt",
                                    dils=[group_dil.get(n, 1) for n in names], acts=acts)))
    att = []
    for g, (win, dil) in enumerate(ATT_GROUPS):
        planes = lambda a: a if a.ndim == 4 else a[:, None]
        att.append(_attn_call(planes(pr[f"q{g}"]), planes(pr[f"kv{g}"]), _prompt_bias(rel_table, g, dil),
                              f"attn_prompt_g{g}"))
    expand = np.zeros((2 * LANES, ATT_WIDTH), np.float32)
    for h in range(ATT_HEADS):
        expand[[LSE_LANES * h, LANES + LSE_LANES * h], h * ATT_HEAD_DIM:(h + 1) * ATT_HEAD_DIM] = 1.0
    expand = jnp.asarray(expand, BF16)
    mlstm_p_args = (pr["xm"], pr["zm"], pr["om"], pr["gates"],
                    jnp.zeros((B, SUBLANES, M), F32), jnp.zeros((B,) + state_C.shape[2:], F32),
                    jnp.zeros((B,) + state_n.shape[2:], F32), jnp.zeros((B, 1, LANES), F32),
                    convw, convb, wq, wk, mnorm, mskip, bif)
    mo_p, C_p, n_p, m_p = _mlstm_call(*mlstm_p_args, L=MLSTM_CHUNK, out_dtype=BF16, name="mlstm_prompt")
    sa = _sattn_call(*sattn_args, T)
    y_prompt = _post_call(x_prompt, p_gate, [a[0] for a in att] + [a[1] for a in att] + [expand],
                          pr["za"], mo_p, pr["ga"], pr["gm"], wpa, wpm, wout, fgain,
                          tm=POST_TM, merge=True, name="post_prompt")
    def tail_weights(keys):
        cols, tsegs, c = [], [], 0
        for key in keys:
            pieces = []
            for wi, c0, wd in segs[key]:
                cols.append(wp[wi][:, c0:c0 + wd])
                pieces.append((0, c, wd))
                c += wd
            tsegs.append(tuple(pieces))
        return (jnp.concatenate(cols, axis=1),), tsegs

    w_max = min(ATT_GROUPS[-1][0], S)
    w2, segs2 = tail_weights(["kv2"])
    (kv2_t,) = _proj_call(x_prompt, gain, p_scale, p_shift, w2, segs2, [F32],
                          tm=PROJ_TM, row0=S - w_max, rows=w_max, name="tail_kv2", dils=[0])
    w_mid = min(ATT_GROUPS[1][0], S)
    w01, segs01 = tail_weights(["kv0", "kv1", "xm"])
    kv0_t, kv1_t, xm_t = _proj_call(x_prompt, gain, p_scale, p_shift, w01, segs01, [F32] * 3,
                                    tm=w_mid, row0=S - w_mid, rows=w_mid, name="tail_kv01", dils=[0, 0, 1])
    w0 = min(ATT_GROUPS[0][0], S)
    as_buffer = lambda t: jnp.transpose(t.reshape(B, 2, ATT_HEADS, ATT_HEAD_DIM, t.shape[2]), (0, 4, 1, 2, 3))[None]
    kv_p = [as_buffer(kv0_t[:, :, w_mid - w0:]), as_buffer(kv1_t), as_buffer(kv2_t)]
    conv_p = xm_t[:, w_mid - (CONV_WIDTH - 1):][None]

    o_att_s = sa[0].reshape(1, R, ATT_WIDTH)
    kv_s = [jnp.transpose(c.reshape(DB, 2, ATT_HEADS, ATT_HEAD_DIM, c.shape[2]), (0, 4, 1, 2, 3))[None]
            for c in sa[1:]]
    conv_prev_s = jnp.concatenate([jnp.zeros((DB, SUBLANES - (CONV_WIDTH - 1), M), F32), state_conv[0]], axis=1)
    m0_s = jnp.concatenate([state_m[0], jnp.zeros((DB, LANES - H), F32)], axis=1).reshape(DB, 1, LANES)
    seqv = lambda t: t.reshape(DB, T, t.shape[-1])
    mo_s, C_s, n_s, m_s = _mlstm_call(
        seqv(sr["xm"]), seqv(sr["zm"]), seqv(sr["om"]), seqv(sr["gates"]),
        conv_prev_s, state_C[0], state_n[0], m0_s,
        convw, convb, wq, wk, mnorm, mskip, bif, L=T, out_dtype=F32, name="mlstm_sample")
    y_sample = _post_call(xs, s_gate, [o_att_s], sr["za"], mo_s.reshape(1, R, M), sr["ga"], sr["gm"],
                          wpa, wpm, wout, fgain, tm=R, merge=False, name="post_sample")
    conv_s = seqv(sr["xm"])[:, T - (CONV_WIDTH - 1):][None]

    return (y_prompt, y_sample.reshape(DB, T, D),
            kv_p[0], kv_s[0], kv_p[1], kv_s[1], kv_p[2], kv_s[2],
            conv_p, conv_s, C_p[None], C_s[None], n_p[None], n_s[None],
            m_p[:, 0, :H][None], m_s[:, 0, :H][None])
```

```python
import functools

import numpy as np
import jax
import jax.numpy as jnp
from jax import lax
from jax.experimental import pallas as pl
from jax.experimental.pallas import tpu as pltpu

F32 = jnp.float32
BF16 = jnp.bfloat16

ATT_GROUPS = ((128, 1), (512, 4), (2048, 16))
N_GROUPS = len(ATT_GROUPS)
ATT_HEADS = 8
ATT_HEAD_DIM = 64
ATT_WIDTH = ATT_HEADS * ATT_HEAD_DIM
WIN_STEPS = 128
ATT_SCALE = ATT_HEAD_DIM ** -0.5
N_BUCKETS = 32
MAX_DISTANCE = 2048
M_HEADS = 4
CONV_WIDTH = 4
EPS = 1e-6
NEG = -1e30

LANES = 128
SUBLANES = 8
VMEM_LIMIT = 56 * 1024 * 1024

ATT_TILE = 2048
PROJ_TM = 512
POST_TM = 1024
MLSTM_CHUNK = 256
MLSTM_SEQS_PROMPT = 2
MLSTM_SEQS_SAMPLE = 4


def _largest_divisor(n, cap):
    return max(g for g in range(1, cap + 1) if n % g == 0)


def _params(sem, vmem=VMEM_LIMIT):
    return pltpu.CompilerParams(dimension_semantics=sem, vmem_limit_bytes=vmem)


def _t5_bucket(dist):
    n = np.asarray(dist).astype(np.int64)
    max_exact = N_BUCKETS // 2
    nf = np.maximum(n, 1).astype(np.float32)
    large = max_exact + (np.log(nf / max_exact) / np.log(np.float32(MAX_DISTANCE / max_exact))
                         * (N_BUCKETS - max_exact)).astype(np.int64)
    large = np.minimum(large, N_BUCKETS - 1)
    return np.where(n < max_exact, n, large).astype(np.int32)


def _silu(x):
    return x * jax.nn.sigmoid(x)


def _ada_kernel(c_ref, w_ref, b_ref, o_ref):
    s = _silu(c_ref[...])
    o_ref[...] = jnp.dot(s, w_ref[...], preferred_element_type=F32,
                         precision=lax.Precision.HIGHEST) + b_ref[...]


def _ada_call(c, w, b):
    n, d = c.shape
    width = w.shape[1]
    tn = 512
    return pl.pallas_call(
        _ada_kernel,
        grid=(width // tn,),
        in_specs=[pl.BlockSpec((n, d), lambda j: (0, 0)),
                  pl.BlockSpec((d, tn), lambda j: (0, j)),
                  pl.BlockSpec((1, tn), lambda j: (0, j))],
        out_specs=pl.BlockSpec((n, tn), lambda j: (0, j)),
        out_shape=jax.ShapeDtypeStruct((n, width), F32),
        compiler_params=_params(("arbitrary",)),
        name="ada",
    )(c, w, b.reshape(1, width))


_ACTIVATIONS = {None: lambda v: v, "silu": _silu, "sigmoid": jax.nn.sigmoid}


def _proj_kernel(x_ref, gain_ref, scale_ref, shift_ref, *rest, n_w, segs, dils, acts):
    w_refs = rest[:n_w]
    rest = rest[n_w:]
    out_refs = rest[:len(segs)]
    x = x_ref[0]
    tm, D = x.shape
    ms = jnp.mean(x * x, axis=-1, keepdims=True)
    h = x * lax.rsqrt(ms + EPS) * gain_ref[...] * (1.0 + scale_ref[0]) + shift_ref[0]
    lhs = {1: h.astype(BF16)}
    lhs[0] = lhs[1]
    strides = sorted(set(dils) - {0, 1})
    if strides:
        hs_ref = rest[len(segs)]
        n_tiles = D // LANES
        for s in range(n_tiles):
            hs_ref[s] = h[:, s * LANES:(s + 1) * LANES]
        for d in strides:
            n = tm // d
            lhs[d] = jnp.concatenate(
                [jnp.concatenate([hs_ref[s, pl.ds(r, n, stride=d), :] for r in range(d)], axis=0)
                 for s in range(n_tiles)], axis=1).astype(BF16)
    for o_ref, pieces, d, act in zip(out_refs, segs, dils, acts):
        parts = [_ACTIVATIONS[act](jnp.dot(lhs[d], w_refs[wi][:, c0:c0 + width],
                                           preferred_element_type=F32)).astype(o_ref.dtype)
                 for wi, c0, width in pieces]
        res = parts[0] if len(parts) == 1 else jnp.concatenate(parts, axis=1)
        if d == 0:
            o_ref[0] = res.T
        elif d == 1:
            o_ref[0] = res
        else:
            n = tm // d
            for r in range(d):
                o_ref[0, r] = res[r * n:(r + 1) * n, :]


def _proj_call(x3, gain, scale3, shift3, ws, segs, dtypes, *, tm, row0, rows, name, dils=None, acts=None):
    B, S, D = x3.shape
    nrb = rows // tm
    rb0 = row0 // tm
    dils = tuple(dils) if dils is not None else (1,) * len(segs)
    acts = tuple(acts) if acts is not None else (None,) * len(segs)
    per_row = scale3.shape[1] != 1
    if per_row:
        mod_spec = pl.BlockSpec((1, tm, D), lambda b, i: (b, rb0 + i, 0))
    else:
        mod_spec = pl.BlockSpec((1, 1, D), lambda b, i: (b, 0, 0))
    out_shape, out_specs = [], []
    for pieces, dt, d in zip(segs, dtypes, dils):
        wd = sum(p[2] for p in pieces)
        if d == 0:
            out_shape.append(jax.ShapeDtypeStruct((B, wd, rows), dt))
            out_specs.append(pl.BlockSpec((1, wd, tm), lambda b, i: (b, 0, i)))
        elif d == 1:
            out_shape.append(jax.ShapeDtypeStruct((B, rows, wd), dt))
            out_specs.append(pl.BlockSpec((1, tm, wd), lambda b, i: (b, i, 0)))
        else:
            out_shape.append(jax.ShapeDtypeStruct((B, d, rows // d, wd), dt))
            out_specs.append(pl.BlockSpec((1, d, tm // d, wd), lambda b, i: (b, 0, i, 0)))
    scratch = [pltpu.VMEM((D // LANES, tm, LANES), F32)] if any(d > 1 for d in dils) else []
    return pl.pallas_call(
        functools.partial(_proj_kernel, n_w=len(ws), segs=tuple(segs), dils=dils, acts=acts),
        grid=(B, nrb),
        in_specs=[pl.BlockSpec((1, tm, D), lambda b, i: (b, rb0 + i, 0)),
                  pl.BlockSpec((1, D), lambda b, i: (0, 0)),
                  mod_spec, mod_spec]
                 + [pl.BlockSpec(w.shape, lambda b, i: (0, 0), pipeline_mode=pl.Buffered(1)) for w in ws],
        out_specs=out_specs,
        out_shape=out_shape,
        scratch_shapes=scratch,
        compiler_params=_params(("arbitrary", "arbitrary")),
        name=name,
    )(x3, gain, scale3, shift3, *ws)


HEADS_PER_SLAB = LANES // ATT_HEAD_DIM
N_SLABS = ATT_HEADS // HEADS_PER_SLAB
LSE_LANES = LANES // ATT_HEADS


def _attn_unit(q, kv, bias_ref, prev_mask, o_ref, l_ref, at):
    dn = (((1,), (1,)), ((), ()))
    nk = 2 * WIN_STEPS
    lane_q = lax.broadcasted_iota(jnp.int32, (WIN_STEPS, LANES), 1) < ATT_HEAD_DIM
    lane_k = lax.broadcasted_iota(jnp.int32, (nk, LANES), 1) < ATT_HEAD_DIM
    ones_lo = jnp.where(lane_k, 1.0, 0.0).astype(BF16)
    ones_hi = jnp.where(lane_k, 0.0, 1.0).astype(BF16)
    zero_q = jnp.zeros((WIN_STEPS, LANES), BF16)
    zero_k = jnp.zeros((nk, LANES), BF16)

    def scores(m):
        cs = slice(m * LANES, (m + 1) * LANES)
        qs = q[:, cs]
        ks = kv[:, cs]
        out = []
        for hh in range(HEADS_PER_SLAB):
            qm = jnp.where(lane_q, qs, zero_q) if hh == 0 else jnp.where(lane_q, zero_q, qs)
            s = lax.dot_general(qm, ks, dn, preferred_element_type=F32) + bias_ref[m * HEADS_PER_SLAB + hh]
            out.append(jnp.concatenate([s[:, :WIN_STEPS] + prev_mask, s[:, WIN_STEPS:]], axis=1))
        return out

    def finish(m, ss):
        cs = slice(m * LANES, (m + 1) * LANES)
        vs = kv[:, ATT_WIDTH + m * LANES:ATT_WIDTH + (m + 1) * LANES]
        ps, mxs = [], []
        for s in ss:
            mx = jnp.max(jnp.maximum(s[:, :WIN_STEPS], s[:, WIN_STEPS:]), -1, keepdims=True)
            ps.append(jnp.exp(s - mx).astype(BF16))
            mxs.append(mx)
        pcat = jnp.concatenate(ps, axis=1)
        vpair = jnp.concatenate(
            [jnp.concatenate([jnp.where(lane_k, vs, zero_k), ones_lo], axis=1),
             jnp.concatenate([jnp.where(lane_k, zero_k, vs), ones_hi], axis=1)], axis=0)
        acc = jnp.dot(pcat, vpair, preferred_element_type=F32)
        den = acc[:, LANES:]
        o_ref[at + (cs,)] = (acc[:, :LANES] / den).astype(o_ref.dtype)
        lse = jnp.where(lane_q, mxs[0], mxs[1]) + jnp.log(den)
        return pltpu.roll(lse, (LSE_LANES * HEADS_PER_SLAB * m - 48) % LANES, axis=1)

    lane = lax.broadcasted_iota(jnp.int32, (WIN_STEPS, LANES), 1)
    lse_c = None
    pending = {0: scores(0), 1: scores(1)}
    for m in range(N_SLABS):
        if m + 2 < N_SLABS:
            pending[m + 2] = scores(m + 2)
        part = finish(m, pending.pop(m))
        lse_c = part if lse_c is None else jnp.where(lane >= LSE_LANES * HEADS_PER_SLAB * m, part, lse_c)
    l_ref[at + (slice(None),)] = lse_c


def _attn_kernel(q_ref, kvc_ref, kvp_ref, bias_ref, o_ref, l_ref, *, ns):
    d = q_ref.shape[1]
    first_tile = pl.program_id(1) == 0

    def body(idx, carry):
        rr = idx // ns
        j = idx % ns
        rc = pl.ds(pl.multiple_of(j * WIN_STEPS, WIN_STEPS), WIN_STEPS)
        q = q_ref[0, rr, rc, :] * ATT_SCALE
        kv_prev = kvp_ref[0, rr]
        if ns > 1:
            rp = pl.ds(pl.multiple_of(jnp.maximum(j - 1, 0) * WIN_STEPS, WIN_STEPS), WIN_STEPS)
            kv_prev = jnp.where(j == 0, kv_prev, kvc_ref[0, rr, rp, :])
        kv = jnp.concatenate([kv_prev, kvc_ref[0, rr, rc, :]], axis=0)
        prev_mask = jnp.where(first_tile & (j == 0), NEG, 0.0).astype(F32)
        _attn_unit(q, kv, bias_ref, prev_mask, o_ref, l_ref, (0, rr, rc))
        return carry

    lax.fori_loop(0, d * ns, body, 0, unroll=2)


def _attn_call(q, kv, bias, name):
    B, d, U, _ = q.shape
    ns = ATT_TILE // (WIN_STEPS * d)
    rows = ns * WIN_STEPS
    blk = lambda width: pl.BlockSpec((1, d, rows, width), lambda b, t: (b, 0, t, 0))
    return pl.pallas_call(
        functools.partial(_attn_kernel, ns=ns),
        grid=(B, U // rows),
        in_specs=[blk(ATT_WIDTH), blk(2 * ATT_WIDTH),
                  pl.BlockSpec((1, d, WIN_STEPS, 2 * ATT_WIDTH),
                               lambda b, t: (b, 0, jnp.maximum(t * ns - 1, 0), 0)),
                  pl.BlockSpec(bias.shape, lambda b, t: (0, 0, 0))],
        out_specs=[blk(ATT_WIDTH), blk(LANES)],
        out_shape=[jax.ShapeDtypeStruct((B, d, U, ATT_WIDTH), BF16),
                   jax.ShapeDtypeStruct((B, d, U, LANES), F32)],
        compiler_params=_params(("arbitrary", "arbitrary")),
        name=name,
    )(q, kv, kv, bias)


def _stride_bias(rel_table, g, d):
    bucket = _t5_bucket(np.arange(WIN_STEPS + 1) * d)
    onehot = jnp.asarray(np.eye(N_BUCKETS, dtype=np.float32)[bucket])
    tbl = rel_table[:, g * ATT_HEADS:(g + 1) * ATT_HEADS].astype(F32)
    return jnp.dot(onehot, tbl, precision=lax.Precision.HIGHEST).T


def _prompt_bias(rel_table, g, d):
    vals = _stride_bias(rel_table, g, d)
    n = WIN_STEPS
    period = 3 * n
    wp = jnp.concatenate([jnp.full((ATT_HEADS, n - 1), NEG, F32), vals[:, ::-1],
                          jnp.full((ATT_HEADS, n), NEG, F32)], axis=1)
    flat = jnp.tile(wp, (1, n))[:, :n * (period - 1)]
    return flat.reshape(ATT_HEADS, n, period - 1)[:, :, n - 1:n - 1 + 2 * n]


SHIFT_VREGS = 128


def _sattn_kernel(*refs, T):
    q_refs = refs[0:3]
    kvnew_refs = refs[3:6]
    cache_refs = refs[6:9]
    bc_refs = refs[9:12]
    bn_refs = refs[12:15]
    o_ref = refs[15]
    co_refs = refs[16:19]
    kvn_refs = refs[19:22]
    HT = ATT_HEADS * T
    dn = (((1,), (1,)), ((), ()))
    row_head = lax.broadcasted_iota(jnp.int32, (HT, ATT_WIDTH), 0) // T
    col_head = lax.broadcasted_iota(jnp.int32, (HT, ATT_WIDTH), 1) // ATT_HEAD_DIM
    head_mask = row_head == col_head

    for g in range(N_GROUPS):
        rows = jnp.concatenate([jnp.zeros((LANES - T, 2 * ATT_WIDTH), F32), kvnew_refs[g][0]], axis=0)
        kvn_refs[g][0] = rows.T

    stats = []
    for g in range(N_GROUPS):
        q = q_refs[g][0] * ATT_SCALE
        qexp = jnp.where(head_mask, jnp.concatenate([q] * ATT_HEADS, axis=0), 0.0).astype(BF16)
        kn = kvn_refs[g][0, :ATT_WIDTH, :].astype(BF16)
        kc = cache_refs[g][0, :ATT_WIDTH, :].astype(BF16)
        lc = jnp.dot(qexp, kc, preferred_element_type=F32) + bc_refs[g][...]
        ln = jnp.dot(qexp, kn, preferred_element_type=F32) + bn_refs[g][...]
        mx = jnp.maximum(jnp.max(lc, -1, keepdims=True), jnp.max(ln, -1, keepdims=True))
        pc = jnp.exp(lc - mx)
        pn = jnp.exp(ln - mx)
        ssum = jnp.sum(pc, -1, keepdims=True) + jnp.sum(pn, -1, keepdims=True)
        stats.append((pc, pn, ssum, mx + jnp.log(ssum)))

    lse_max = jnp.maximum(jnp.maximum(stats[0][3], stats[1][3]), stats[2][3])
    es = [jnp.exp(st[3] - lse_max) for st in stats]
    esum = es[0] + es[1] + es[2]
    acc = jnp.zeros((HT, ATT_WIDTH), F32)
    for g in range(N_GROUPS):
        pc, pn, ssum, _ = stats[g]
        w = es[g] / (esum * ssum)
        vc = cache_refs[g][0, ATT_WIDTH:, :].astype(BF16)
        vn = kvn_refs[g][0, ATT_WIDTH:, :].astype(BF16)
        acc = acc + lax.dot_general((pc * w).astype(BF16), vc, dn, preferred_element_type=F32)
        acc = acc + lax.dot_general((pn * w).astype(BF16), vn, dn, preferred_element_type=F32)
    lane_head = lax.broadcasted_iota(jnp.int32, (T, ATT_WIDTH), 1) // ATT_HEAD_DIM
    o = jnp.zeros((T, ATT_WIDTH), F32)
    for h in range(ATT_HEADS):
        o = o + jnp.where(lane_head == h, acc[h * T:(h + 1) * T, :], 0.0)
    o_ref[0] = o

    for g in range(N_GROUPS):
        L = cache_refs[g].shape[2]
        nrows = cache_refs[g].shape[1]
        chunk = min(nrows, SHIFT_VREGS * SUBLANES * LANES // L)
        is_new = lax.broadcasted_iota(jnp.int32, (chunk, LANES), 1) >= LANES - T

        def shift_rows(i, carry, g=g, L=L, chunk=chunk, is_new=is_new):
            rs = pl.ds(pl.multiple_of(i * chunk, chunk), chunk)
            rolled = pltpu.roll(cache_refs[g][0, rs, :], L - T, axis=1)
            if L > LANES:
                co_refs[g][0, rs, 0:L - LANES] = rolled[:, 0:L - LANES]
            co_refs[g][0, rs, L - LANES:L] = jnp.where(is_new, kvn_refs[g][0, rs, :], rolled[:, L - LANES:])
            return carry

        lax.fori_loop(0, nrows // chunk, shift_rows, 0)


def _sample_bias(rel_table, g, W, d, Lb, T):
    vals = _stride_bias(rel_table, g, d)
    n = Lb + T
    by_dist = jnp.concatenate([vals[:, :, None], jnp.full((ATT_HEADS, WIN_STEPS + 1, d - 1), NEG, F32)],
                              axis=2).reshape(ATT_HEADS, (WIN_STEPS + 1) * d)
    if by_dist.shape[1] < n:
        by_dist = jnp.concatenate([by_dist, jnp.full((ATT_HEADS, n - by_dist.shape[1]), NEG, F32)], axis=1)
    rev = jnp.concatenate([by_dist[:, :n][:, ::-1], jnp.full((ATT_HEADS, T), NEG, F32)], axis=1)
    rows = jnp.stack([rev[:, T - 1 - t:T - 1 - t + n] for t in range(T)], axis=1)
    rows = rows.reshape(ATT_HEADS * T, n)
    bc = rows[:, :Lb]
    bn = jnp.concatenate([jnp.full((ATT_HEADS * T, LANES - T), NEG, F32), rows[:, Lb:]], axis=1)
    return bc, bn


def _sattn_call(qs, kvns, caches, bcs, bns, T):
    DB = caches[0].shape[0]
    rows = caches[0].shape[1]
    per_seq = lambda b: (b, 0, 0)
    const2 = lambda b: (0, 0)
    in_specs = ([pl.BlockSpec((1, T, ATT_WIDTH), per_seq)] * 3
                + [pl.BlockSpec((1,) + x.shape[1:], per_seq) for x in kvns]
                + [pl.BlockSpec((1,) + c.shape[1:], per_seq) for c in caches]
                + [pl.BlockSpec(x.shape, const2) for x in bcs]
                + [pl.BlockSpec(x.shape, const2) for x in bns])
    out_specs = ([pl.BlockSpec((1, T, ATT_WIDTH), per_seq)]
                 + [pl.BlockSpec((1,) + c.shape[1:], per_seq) for c in caches])
    out_shape = ([jax.ShapeDtypeStruct((DB, T, ATT_WIDTH), F32)]
                 + [jax.ShapeDtypeStruct(c.shape, F32) for c in caches])
    return pl.pallas_call(
        functools.partial(_sattn_kernel, T=T),
        grid=(DB,),
        in_specs=in_specs, out_specs=out_specs, out_shape=out_shape,
        scratch_shapes=[pltpu.VMEM((1, rows, LANES), F32)] * N_GROUPS,
        compiler_params=_params(("arbitrary",)),
        name="sample_attn",
    )(*qs, *kvns, *caches, *bcs, *bns)


def _scan_rows(x, op, fill):
    n = x.shape[0]
    rowid = lax.broadcasted_iota(jnp.int32, x.shape, 0)
    s = 1
    while s < n:
        shifted = pltpu.roll(x, s, axis=0)
        x = op(x, jnp.where(rowid >= s, shifted, fill))
        s *= 2
    return x


def _pad_rows(x, n, fill=0.0):
    if x.shape[0] == n:
        return x
    return jnp.concatenate([x, jnp.full((n - x.shape[0],) + x.shape[1:], fill, x.dtype)], axis=0)


def _mlstm_kernel(xm_ref, zm_ref, om_ref, g_ref, cprev_ref, C0_ref, n0_ref, m0_ref,
                  convw_ref, convb_ref, wq_ref, wk_ref, mnorm_ref, mskip_ref, bif_ref,
                  mo_ref, C_ref, n_ref, m_ref, xc_ref, *, L):
    LS = max(L, LANES)
    G = xm_ref.shape[0]
    DV = C_ref.shape[2]
    DK = C_ref.shape[3]
    c = pl.program_id(1)

    @pl.when(c == 0)
    def _():
        xc_ref[:, 0:SUBLANES, :] = cprev_ref[...]
        C_ref[...] = C0_ref[...]
        n_ref[...] = n0_ref[...]
        m_ref[...] = m0_ref[...]

    t_id = lax.broadcasted_iota(jnp.int32, (L, LS), 0)
    s_id = lax.broadcasted_iota(jnp.int32, (L, LS), 1)
    causal = s_id <= t_id
    eye = s_id == t_id
    dn_t = (((1,), (1,)), ((), ()))

    seqs = []
    for g in range(G):
        xm_b = xm_ref[g].astype(BF16)
        xc_ref[g, SUBLANES:SUBLANES + L, :] = xm_ref[g].astype(F32)
        conv = convb_ref[...]
        for j in range(CONV_WIDTH):
            off = SUBLANES - (CONV_WIDTH - 1) + j
            conv = conv + convw_ref[j:j + 1, :] * xc_ref[g, off:off + L, :]
        xc_ref[g, 0:SUBLANES, :] = xc_ref[g, L:L + SUBLANES, :]
        c_act = _silu(conv)

        i_pre = g_ref[g] + bif_ref[...]
        logf = jax.nn.log_sigmoid(pltpu.roll(i_pre, LANES - M_HEADS, axis=1))
        b = _scan_rows(logf, jnp.add, 0.0)
        a = i_pre - b
        ca = _scan_rows(a, jnp.maximum, NEG)
        m_prev = m_ref[g]
        mm = jnp.maximum(ca, m_prev)
        u = -mm
        bL = b[L - 1:L, :]
        m_new = bL + jnp.maximum(m_prev, ca[L - 1:L, :])
        m_ref[g] = m_new
        seqs.append(dict(xm_b=xm_b, c_act=c_act, cb=c_act.astype(BF16), a=a, u=u,
                         w_inter=jnp.exp(u + m_prev), emt=jnp.exp(-(b + mm)),
                         wk=jnp.exp(bL + a - m_new), wC=jnp.exp(bL + m_prev - m_new)))

    units = [(g, h) for g in range(G) for h in range(M_HEADS)]
    vsl = [slice(h * DV, (h + 1) * DV) for h in range(M_HEADS)]
    qs, ks, kws, vss, C_old, n_old = {}, {}, {}, {}, {}, {}
    for g, h in units:
        ch = seqs[g]["cb"][:, vsl[h]]
        qs[g, h] = (jnp.dot(ch, wq_ref[h], preferred_element_type=F32) * (DK ** -0.5)).astype(BF16)
        k32 = jnp.dot(ch, wk_ref[h], preferred_element_type=F32)
        ks[g, h] = _pad_rows(k32.astype(BF16), LS)
        kws[g, h] = _pad_rows(k32 * seqs[g]["wk"][:, h:h + 1], LS)
        vss[g, h] = _pad_rows(seqs[g]["xm_b"][:, vsl[h]], LS)
        C_old[g, h] = C_ref[g, h]
        n_old[g, h] = n_ref[g, h:h + 1, :]
    scs, inters = {}, {}
    for g, h in units:
        a_row = jnp.sum(jnp.where(eye, seqs[g]["a"][:, h:h + 1], 0.0), axis=0, keepdims=True)
        w_intra = jnp.exp(jnp.where(causal, seqs[g]["u"][:, h:h + 1] + a_row, NEG))
        scs[g, h] = lax.dot_general(qs[g, h], ks[g, h], dn_t, preferred_element_type=F32) * w_intra
        inters[g, h] = lax.dot_general(qs[g, h], C_old[g, h].astype(BF16), dn_t, preferred_element_type=F32)
    for g, h in units:
        wi = seqs[g]["w_inter"][:, h:h + 1]
        num = jnp.dot(scs[g, h].astype(BF16), vss[g, h], preferred_element_type=F32) + wi * inters[g, h]
        den = (jnp.sum(scs[g, h], -1, keepdims=True)
               + wi * jnp.sum(qs[g, h].astype(F32) * n_old[g, h], -1, keepdims=True))
        hcell = num / jnp.maximum(jnp.abs(den), seqs[g]["emt"][:, h:h + 1])
        mu = jnp.mean(hcell, -1, keepdims=True)
        hc = hcell - mu
        hn = hc * lax.rsqrt(jnp.mean(hc * hc, -1, keepdims=True) + EPS)
        out = ((om_ref[g, :, vsl[h]].astype(F32) * (hn * mnorm_ref[:, vsl[h]])
                + mskip_ref[:, vsl[h]] * seqs[g]["c_act"][:, vsl[h]]) * zm_ref[g, :, vsl[h]].astype(F32))
        mo_ref[g, :, vsl[h]] = out.astype(mo_ref.dtype)
    for g, h in units:
        wc = seqs[g]["wC"][:, h:h + 1]
        C_ref[g, h] = wc * C_old[g, h] + lax.dot_general(vss[g, h], kws[g, h].astype(BF16),
                                                         (((0,), (0,)), ((), ())),
                                                         preferred_element_type=F32)
        n_ref[g, h:h + 1, :] = wc * n_old[g, h] + jnp.sum(kws[g, h], axis=0, keepdims=True)


def _mlstm_call(xm, zm, om, gates, conv_prev, C0, n0, m0, convw, convb, wq, wk, mnorm, mskip, bif,
                *, L, G, out_dtype, name):
    N, S, M = xm.shape
    H, DV, DK = C0.shape[1:]
    seq = lambda b, c: (b, c, 0)
    per_b3 = lambda b, c: (b, 0, 0)
    per_b4 = lambda b, c: (b, 0, 0, 0)
    const2 = lambda b, c: (0, 0)
    const3 = lambda b, c: (0, 0, 0)
    in_specs = [pl.BlockSpec((G, L, M), seq), pl.BlockSpec((G, L, M), seq), pl.BlockSpec((G, L, M), seq),
                pl.BlockSpec((G, L, LANES), seq),
                pl.BlockSpec((G, SUBLANES, M), per_b3),
                pl.BlockSpec((G, H, DV, DK), per_b4),
                pl.BlockSpec((G, H, DK), per_b3),
                pl.BlockSpec((G, 1, LANES), per_b3),
                pl.BlockSpec(convw.shape, const2), pl.BlockSpec(convb.shape, const2),
                pl.BlockSpec(wq.shape, const3), pl.BlockSpec(wk.shape, const3),
                pl.BlockSpec(mnorm.shape, const2), pl.BlockSpec(mskip.shape, const2),
                pl.BlockSpec(bif.shape, const2)]
    out_specs = [pl.BlockSpec((G, L, M), seq),
                 pl.BlockSpec((G, H, DV, DK), per_b4),
                 pl.BlockSpec((G, H, DK), per_b3),
                 pl.BlockSpec((G, 1, LANES), per_b3)]
    out_shape = [jax.ShapeDtypeStruct((N, S, M), out_dtype),
                 jax.ShapeDtypeStruct((N, H, DV, DK), F32),
                 jax.ShapeDtypeStruct((N, H, DK), F32),
                 jax.ShapeDtypeStruct((N, 1, LANES), F32)]
    return pl.pallas_call(
        functools.partial(_mlstm_kernel, L=L),
        grid=(N // G, S // L),
        in_specs=in_specs, out_specs=out_specs, out_shape=out_shape,
        scratch_shapes=[pltpu.VMEM((G, L + 2 * SUBLANES, M), F32)],
        compiler_params=_params(("arbitrary", "arbitrary")),
        name=name,
    )(xm, zm, om, gates, conv_prev, C0, n0, m0, convw, convb, wq, wk, mnorm, mskip, bif)


def _token_order(src_ref, scr_ref):
    _, d, n, width = src_ref.shape
    if d == 1:
        return src_ref[0, 0].astype(F32)
    n_tiles = width // LANES
    for r in range(d):
        for s in range(n_tiles):
            scr_ref[s, pl.ds(r, n, stride=d), :] = src_ref[0, r, :, s * LANES:(s + 1) * LANES].astype(F32)
    return jnp.concatenate([scr_ref[s] for s in range(n_tiles)], axis=1)


def _post_kernel(*refs, merge):
    if merge:
        (x_ref, gate_ref, o0, o1, o2, l0, l1, l2, expand_ref, za_ref, mo_ref, ga_ref, gm_ref,
         wpa_ref, wpm_ref, wout_ref, fg_ref, y_ref, scr_ref) = refs
        ls = [_token_order(l, scr_ref) for l in (l0, l1, l2)]
        lmax = jnp.maximum(jnp.maximum(ls[0], ls[1]), ls[2])
        es = [jnp.exp(l - lmax) for l in ls]
        inv = 1.0 / (es[0] + es[1] + es[2])
        o_att = None
        for e, o in zip(es, (o0, o1, o2)):
            a = e * inv
            hi = a.astype(BF16)
            lo = (a - hi.astype(F32)).astype(BF16)
            a_wide = jnp.dot(jnp.concatenate([hi, lo], axis=1), expand_ref[...],
                             preferred_element_type=F32)
            term = a_wide * _token_order(o, scr_ref)
            o_att = term if o_att is None else o_att + term
    else:
        (x_ref, gate_ref, oa_ref, za_ref, mo_ref, ga_ref, gm_ref,
         wpa_ref, wpm_ref, wout_ref, fg_ref, y_ref) = refs
        o_att = oa_ref[0]
    a_in = (o_att * za_ref[0].astype(F32)).astype(BF16)
    a_br = jnp.dot(a_in, wpa_ref[...], preferred_element_type=F32)
    m_br = jnp.dot(mo_ref[0].astype(BF16), wpm_ref[...], preferred_element_type=F32)
    merged = ga_ref[0].astype(F32) * a_br + gm_ref[0].astype(F32) * m_br
    y = x_ref[0] + gate_ref[0] * jnp.dot(merged.astype(BF16), wout_ref[...], preferred_element_type=F32)
    ms = jnp.mean(y * y, axis=-1, keepdims=True)
    y_ref[0] = y * lax.rsqrt(ms + EPS) * fg_ref[...]


def _post_call(x3, gate3, att_inputs, za, mo, ga, gm, wpa, wpm, wout, fgain, *, tm, merge, name):
    B, S, D = x3.shape
    row = lambda b, i: (b, i, 0)
    const2 = lambda b, i: (0, 0)
    if gate3.shape[1] == 1:
        gate_spec = pl.BlockSpec((1, 1, D), lambda b, i: (b, 0, 0))
    else:
        gate_spec = pl.BlockSpec((1, tm, D), row)
    def blk(a):
        if a.ndim == 2:
            return pl.BlockSpec(a.shape, const2)
        if a.ndim == 4:
            d = a.shape[1]
            return pl.BlockSpec((1, d, tm // d, a.shape[3]), lambda b, i: (b, 0, i, 0))
        return pl.BlockSpec((1, tm, a.shape[2]), row)

    in_specs = ([pl.BlockSpec((1, tm, D), row), gate_spec]
                + [blk(a) for a in att_inputs]
                + [blk(za), blk(mo), blk(ga), blk(gm),
                   pl.BlockSpec(wpa.shape, const2), pl.BlockSpec(wpm.shape, const2),
                   pl.BlockSpec(wout.shape, const2), pl.BlockSpec(fgain.shape, const2)])
    scratch = [pltpu.VMEM((ATT_WIDTH // LANES, tm, LANES), F32)] if merge else []
    return pl.pallas_call(
        functools.partial(_post_kernel, merge=merge),
        grid=(B, S // tm),
        in_specs=in_specs,
        out_specs=pl.BlockSpec((1, tm, D), row),
        out_shape=jax.ShapeDtypeStruct((B, S, D), F32),
        scratch_shapes=scratch,
        compiler_params=_params(("arbitrary", "arbitrary")),
        name=name,
    )(x3, gate3, *att_inputs, za, mo, ga, gm, wpa, wpm, wout, fgain)


def _cast_kernel(x_ref, o_ref):
    o_ref[...] = x_ref[...].T.astype(o_ref.dtype)


def _cast_bf16_call(wt, ncols):
    k = wt.shape[1]
    tn = 1024
    return pl.pallas_call(
        _cast_kernel,
        grid=(ncols // tn,),
        in_specs=[pl.BlockSpec((tn, k), lambda j: (j, 0))],
        out_specs=pl.BlockSpec((k, tn), lambda j: (0, j)),
        out_shape=jax.ShapeDtypeStruct((k, ncols), BF16),
        compiler_params=_params(("arbitrary",)),
        name="cast_weights",
    )(wt)


def _tail_cast_kernel(a_ref, b_ref, c_ref, o_ref, *, n_gate, n_merge):
    rows = jnp.concatenate([a_ref[...], b_ref[...], c_ref[0:SUBLANES, :]], axis=0)
    gates = jnp.concatenate([rows[0:n_gate], jnp.zeros((LANES - n_gate, rows.shape[1]), F32)], axis=0)
    out = jnp.concatenate([rows[n_gate:n_gate + n_merge], gates], axis=0)
    o_ref[...] = out.T.astype(o_ref.dtype)


def _tail_cast_call(wt, first, n_gate, n_merge):
    k = wt.shape[1]
    tn = n_merge // 2
    j0 = first // tn
    assert first % tn == 0 and n_gate == SUBLANES
    return pl.pallas_call(
        functools.partial(_tail_cast_kernel, n_gate=n_gate, n_merge=n_merge),
        grid=(1,),
        in_specs=[pl.BlockSpec((tn, k), lambda j: (j0, 0), pipeline_mode=pl.Buffered(1)),
                  pl.BlockSpec((tn, k), lambda j: (j0 + 1, 0), pipeline_mode=pl.Buffered(1)),
                  pl.BlockSpec((tn, k), lambda j: (j0 + 2, 0), pipeline_mode=pl.Buffered(1))],
        out_specs=pl.BlockSpec((k, n_merge + LANES), lambda j: (0, 0)),
        out_shape=jax.ShapeDtypeStruct((k, n_merge + LANES), BF16),
        compiler_params=_params(("arbitrary",)),
        name="cast_tail_weights",
    )(wt, wt, wt)


def _projection_weights(w_in):
    D = w_in.shape[0]
    AW = ATT_WIDTH
    M = D
    off_k, off_v, off_za = 3 * AW, 6 * AW, 9 * AW
    off_xm = off_za + AW
    off_i = off_xm + 3 * M
    wt = jnp.transpose(w_in)
    w_main = _cast_bf16_call(wt, off_i)
    w_tail = _tail_cast_call(wt, off_i, 2 * M_HEADS, 2 * D)
    segs = {"za": ((0, off_za, AW),), "xm": ((0, off_xm, M),), "zm": ((0, off_xm + M, M),),
            "om": ((0, off_xm + 2 * M, M),), "ga": ((1, 0, D),), "gm": ((1, D, D),),
            "gates": ((1, 2 * D, LANES),)}
    for g in range(N_GROUPS):
        segs[f"q{g}"] = ((0, g * AW, AW),)
        segs[f"kv{g}"] = ((0, off_k + g * AW, AW), (0, off_v + g * AW, AW))
    return (w_main, w_tail), segs


def kernel(x_prompt, x_sample, cache_kv_w128, cache_kv_w512, cache_kv_w2048, state_conv, state_C, state_n, state_m, c_prompt, c_sample, rel_table, norm_gain, w_ada, b_ada, w_in, b_if, conv_w, conv_b, w_mq, w_mk, m_norm, m_skip, w_pa, w_pm, w_out, final_gain):
    B, S, D = x_prompt.shape
    DB, T, _ = x_sample.shape
    assert norm_gain.shape[0] == 1, "single-layer trunk"
    assert S % ATT_TILE == 0 and S % MLSTM_CHUNK == 0 and T == SUBLANES
    caches = (cache_kv_w128[0], cache_kv_w512[0], cache_kv_w2048[0])
    H = M_HEADS
    M = conv_w.shape[2]

    wp, segs = _projection_weights(w_in[0])
    names = ("q0", "q1", "q2", "za", "kv0", "kv1", "kv2", "xm", "zm", "om", "ga", "gm", "gates")
    seg_list = [segs[n] for n in names]
    gain = norm_gain[0].reshape(1, D)
    fgain = final_gain.reshape(1, D)
    wpa, wpm, wout = w_pa[0].astype(BF16), w_pm[0].astype(BF16), w_out[0].astype(BF16)
    wq, wk = w_mq[0].astype(BF16), w_mk[0].astype(BF16)
    convw, convb = conv_w[0], conv_b[0].reshape(1, M)
    mnorm, mskip = m_norm[0].reshape(1, M), m_skip[0].reshape(1, M)
    bif = jnp.concatenate([b_if[0], jnp.zeros((LANES - 2 * H,), F32)]).reshape(1, LANES)

    ada = _ada_call(jnp.concatenate([c_prompt, c_sample], axis=0), w_ada[0], b_ada[0])
    shift, scale, gate = ada[:, :D], ada[:, D:2 * D], ada[:, 2 * D:]

    gate_act = {"za": "silu", "zm": "silu", "om": "sigmoid", "ga": "sigmoid", "gm": "sigmoid"}
    acts = [gate_act.get(n) for n in names]

    R = DB * T
    rep = lambda t: jnp.repeat(t[B:], T, axis=0).reshape(1, R, D)
    s_shift, s_scale, s_gate = rep(shift), rep(scale), rep(gate)
    xs = x_sample.reshape(1, R, D)
    sr = dict(zip(names, _proj_call(xs, gain, s_scale, s_shift, wp, seg_list, [F32] * 13,
                                    tm=R, row0=0, rows=R, name="proj_sample", acts=acts)))
    bcs, bns = [], []
    for g, (win, dil) in enumerate(ATT_GROUPS):
        bc, bn = _sample_bias(rel_table, g, win, dil, caches[g].shape[1], T)
        bcs.append(bc)
        bns.append(bn)
    cache_t = [jnp.transpose(c, (0, 2, 3, 4, 1)).reshape(DB, 2 * ATT_WIDTH, c.shape[1]) for c in caches]
    kvn_t = [sr[f"kv{g}"].reshape(DB, T, 2 * ATT_WIDTH) for g in range(3)]
    sattn_args = ([sr[f"q{g}"].reshape(DB, T, ATT_WIDTH) for g in range(3)], kvn_t, cache_t, bcs, bns)

    p_shift, p_scale, p_gate = (t[:B].reshape(B, 1, D) for t in (shift, scale, gate))
    dts = [BF16] * 12 + [F32]
    group_dil = {f"{p}{g}": dil for g, (_, dil) in enumerate(ATT_GROUPS) for p in ("q", "kv")}
    pr = dict(zip(names, _proj_call(x_prompt, gain, p_scale, p_shift, wp, seg_list, dts,
                                    tm=PROJ_TM, row0=0, rows=S, name="proj_prompt",
                                    dils=[group_dil.get(n, 1) for n in names], acts=acts)))
    att = []
    for g, (win, dil) in enumerate(ATT_GROUPS):
        planes = lambda a: a if a.ndim == 4 else a[:, None]
        att.append(_attn_call(planes(pr[f"q{g}"]), planes(pr[f"kv{g}"]), _prompt_bias(rel_table, g, dil),
                              f"attn_prompt_g{g}"))
    expand = np.zeros((2 * LANES, ATT_WIDTH), np.float32)
    for h in range(ATT_HEADS):
        expand[[LSE_LANES * h, LANES + LSE_LANES * h], h * ATT_HEAD_DIM:(h + 1) * ATT_HEAD_DIM] = 1.0
    expand = jnp.asarray(expand, BF16)
    mlstm_p_args = (pr["xm"], pr["zm"], pr["om"], pr["gates"],
                    jnp.zeros((B, SUBLANES, M), F32), jnp.zeros((B,) + state_C.shape[2:], F32),
                    jnp.zeros((B,) + state_n.shape[2:], F32), jnp.zeros((B, 1, LANES), F32),
                    convw, convb, wq, wk, mnorm, mskip, bif)
    mo_p, C_p, n_p, m_p = _mlstm_call(*mlstm_p_args, L=MLSTM_CHUNK, G=_largest_divisor(B, MLSTM_SEQS_PROMPT),
                                      out_dtype=BF16, name="mlstm_prompt")
    sa = _sattn_call(*sattn_args, T)
    y_prompt = _post_call(x_prompt, p_gate, [a[0] for a in att] + [a[1] for a in att] + [expand],
                          pr["za"], mo_p, pr["ga"], pr["gm"], wpa, wpm, wout, fgain,
                          tm=POST_TM, merge=True, name="post_prompt")
    def tail_weights(keys):
        cols, tsegs, c = [], [], 0
        for key in keys:
            pieces = []
            for wi, c0, wd in segs[key]:
                cols.append(wp[wi][:, c0:c0 + wd])
                pieces.append((0, c, wd))
                c += wd
            tsegs.append(tuple(pieces))
        return (jnp.concatenate(cols, axis=1),), tsegs

    w_max = min(ATT_GROUPS[-1][0], S)
    w2, segs2 = tail_weights(["kv2"])
    (kv2_t,) = _proj_call(x_prompt, gain, p_scale, p_shift, w2, segs2, [F32],
                          tm=PROJ_TM, row0=S - w_max, rows=w_max, name="tail_kv2", dils=[0])
    w_mid = min(ATT_GROUPS[1][0], S)
    w01, segs01 = tail_weights(["kv0", "kv1", "xm"])
    kv0_t, kv1_t, xm_t = _proj_call(x_prompt, gain, p_scale, p_shift, w01, segs01, [F32] * 3,
                                    tm=w_mid, row0=S - w_mid, rows=w_mid, name="tail_kv01", dils=[0, 0, 1])
    w0 = min(ATT_GROUPS[0][0], S)
    as_buffer = lambda t: jnp.transpose(t.reshape(B, 2, ATT_HEADS, ATT_HEAD_DIM, t.shape[2]), (0, 4, 1, 2, 3))[None]
    kv_p = [as_buffer(kv0_t[:, :, w_mid - w0:]), as_buffer(kv1_t), as_buffer(kv2_t)]
    conv_p = xm_t[:, w_mid - (CONV_WIDTH - 1):][None]

    o_att_s = sa[0].reshape(1, R, ATT_WIDTH)
    kv_s = [jnp.transpose(c.reshape(DB, 2, ATT_HEADS, ATT_HEAD_DIM, c.shape[2]), (0, 4, 1, 2, 3))[None]
            for c in sa[1:]]
    conv_prev_s = jnp.concatenate([jnp.zeros((DB, SUBLANES - (CONV_WIDTH - 1), M), F32), state_conv[0]], axis=1)
    m0_s = jnp.concatenate([state_m[0], jnp.zeros((DB, LANES - H), F32)], axis=1).reshape(DB, 1, LANES)
    seqv = lambda t: t.reshape(DB, T, t.shape[-1])
    mo_s, C_s, n_s, m_s = _mlstm_call(
        seqv(sr["xm"]), seqv(sr["zm"]), seqv(sr["om"]), seqv(sr["gates"]),
        conv_prev_s, state_C[0], state_n[0], m0_s,
        convw, convb, wq, wk, mnorm, mskip, bif, L=T, G=_largest_divisor(DB, MLSTM_SEQS_SAMPLE),
        out_dtype=F32, name="mlstm_sample")
    y_sample = _post_call(xs, s_gate, [o_att_s], sr["za"], mo_s.reshape(1, R, M), sr["ga"], sr["gm"],
                          wpa, wpm, wout, fgain, tm=R, merge=False, name="post_sample")
    conv_s = seqv(sr["xm"])[:, T - (CONV_WIDTH - 1):][None]

    return (y_prompt, y_sample.reshape(DB, T, D),
            kv_p[0], kv_s[0], kv_p[1], kv_s[1], kv_p[2], kv_s[2],
            conv_p, conv_s, C_p[None], C_s[None], n_p[None], n_s[None],
            m_p[:, 0, :H][None], m_s[:, 0, :H][None])
```

```python
import functools

import numpy as np
import jax
import jax.numpy as jnp
from jax import lax
from jax.experimental import pallas as pl
from jax.experimental.pallas import tpu as pltpu

F32 = jnp.float32
BF16 = jnp.bfloat16

ATT_GROUPS = ((128, 1), (512, 4), (2048, 16))
N_GROUPS = len(ATT_GROUPS)
ATT_HEADS = 8
ATT_HEAD_DIM = 64
ATT_WIDTH = ATT_HEADS * ATT_HEAD_DIM
WIN_STEPS = 128
ATT_SCALE = ATT_HEAD_DIM ** -0.5
N_BUCKETS = 32
MAX_DISTANCE = 2048
M_HEADS = 4
CONV_WIDTH = 4
EPS = 1e-6
NEG = -1e30

LANES = 128
SUBLANES = 8
VMEM_LIMIT = 56 * 1024 * 1024

ATT_TILE = 2048
PROJ_TM = 512
POST_TM = 1024
MLSTM_CHUNK = 256
MLSTM_SEQS_PROMPT = 2
MLSTM_SEQS_SAMPLE = 4


def _largest_divisor(n, cap):
    return max(g for g in range(1, cap + 1) if n % g == 0)


def _params(sem, vmem=VMEM_LIMIT):
    return pltpu.CompilerParams(dimension_semantics=sem, vmem_limit_bytes=vmem)


def _t5_bucket(dist):
    n = np.asarray(dist).astype(np.int64)
    max_exact = N_BUCKETS // 2
    nf = np.maximum(n, 1).astype(np.float32)
    large = max_exact + (np.log(nf / max_exact) / np.log(np.float32(MAX_DISTANCE / max_exact))
                         * (N_BUCKETS - max_exact)).astype(np.int64)
    large = np.minimum(large, N_BUCKETS - 1)
    return np.where(n < max_exact, n, large).astype(np.int32)


def _sigmoid_tanh(v):
    return 0.5 * jnp.tanh(0.5 * v) + 0.5


def _silu(x):
    return x * _sigmoid_tanh(x)


def _ada_kernel(c_ref, w_ref, b_ref, o_ref):
    s = _silu(c_ref[...])
    o_ref[...] = jnp.dot(s, w_ref[...], preferred_element_type=F32,
                         precision=lax.Precision.HIGHEST) + b_ref[...]


def _ada_call(c, w, b):
    n, d = c.shape
    width = w.shape[1]
    tn = 512
    return pl.pallas_call(
        _ada_kernel,
        grid=(width // tn,),
        in_specs=[pl.BlockSpec((n, d), lambda j: (0, 0)),
                  pl.BlockSpec((d, tn), lambda j: (0, j)),
                  pl.BlockSpec((1, tn), lambda j: (0, j))],
        out_specs=pl.BlockSpec((n, tn), lambda j: (0, j)),
        out_shape=jax.ShapeDtypeStruct((n, width), F32),
        compiler_params=_params(("arbitrary",)),
        name="ada",
    )(c, w, b.reshape(1, width))


_ACTIVATIONS = {None: lambda v: v, "silu": _silu, "sigmoid": _sigmoid_tanh}


def _proj_kernel(x_ref, gain_ref, scale_ref, shift_ref, *rest, n_w, segs, dils, acts):
    w_refs = rest[:n_w]
    rest = rest[n_w:]
    out_refs = rest[:len(segs)]
    x = x_ref[0]
    tm, D = x.shape
    ms = jnp.mean(x * x, axis=-1, keepdims=True)
    h = x * lax.rsqrt(ms + EPS) * gain_ref[...] * (1.0 + scale_ref[0]) + shift_ref[0]
    lhs = {1: h.astype(BF16)}
    lhs[0] = lhs[1]
    strides = sorted(set(dils) - {0, 1})
    if strides:
        hs_ref = rest[len(segs)]
        n_tiles = D // LANES
        for s in range(n_tiles):
            hs_ref[s] = h[:, s * LANES:(s + 1) * LANES]
        for d in strides:
            n = tm // d
            lhs[d] = jnp.concatenate(
                [jnp.concatenate([hs_ref[s, pl.ds(r, n, stride=d), :] for r in range(d)], axis=0)
                 for s in range(n_tiles)], axis=1).astype(BF16)
    for o_ref, pieces, d, act in zip(out_refs, segs, dils, acts):
        parts = [_ACTIVATIONS[act](jnp.dot(lhs[d], w_refs[wi][:, c0:c0 + width],
                                           preferred_element_type=F32)).astype(o_ref.dtype)
                 for wi, c0, width in pieces]
        res = parts[0] if len(parts) == 1 else jnp.concatenate(parts, axis=1)
        if d == 0:
            o_ref[0] = res.T
        elif d == 1:
            o_ref[0] = res
        else:
            n = tm // d
            for r in range(d):
                o_ref[0, r] = res[r * n:(r + 1) * n, :]


def _proj_call(x3, gain, scale3, shift3, ws, segs, dtypes, *, tm, row0, rows, name, dils=None, acts=None):
    B, S, D = x3.shape
    nrb = rows // tm
    rb0 = row0 // tm
    dils = tuple(dils) if dils is not None else (1,) * len(segs)
    acts = tuple(acts) if acts is not None else (None,) * len(segs)
    per_row = scale3.shape[1] != 1
    if per_row:
        mod_spec = pl.BlockSpec((1, tm, D), lambda b, i: (b, rb0 + i, 0))
    else:
        mod_spec = pl.BlockSpec((1, 1, D), lambda b, i: (b, 0, 0))
    out_shape, out_specs = [], []
    for pieces, dt, d in zip(segs, dtypes, dils):
        wd = sum(p[2] for p in pieces)
        if d == 0:
            out_shape.append(jax.ShapeDtypeStruct((B, wd, rows), dt))
            out_specs.append(pl.BlockSpec((1, wd, tm), lambda b, i: (b, 0, i)))
        elif d == 1:
            out_shape.append(jax.ShapeDtypeStruct((B, rows, wd), dt))
            out_specs.append(pl.BlockSpec((1, tm, wd), lambda b, i: (b, i, 0)))
        else:
            out_shape.append(jax.ShapeDtypeStruct((B, d, rows // d, wd), dt))
            out_specs.append(pl.BlockSpec((1, d, tm // d, wd), lambda b, i: (b, 0, i, 0)))
    scratch = [pltpu.VMEM((D // LANES, tm, LANES), F32)] if any(d > 1 for d in dils) else []
    return pl.pallas_call(
        functools.partial(_proj_kernel, n_w=len(ws), segs=tuple(segs), dils=dils, acts=acts),
        grid=(B, nrb),
        in_specs=[pl.BlockSpec((1, tm, D), lambda b, i: (b, rb0 + i, 0)),
                  pl.BlockSpec((1, D), lambda b, i: (0, 0)),
                  mod_spec, mod_spec]
                 + [pl.BlockSpec(w.shape, lambda b, i: (0, 0), pipeline_mode=pl.Buffered(1)) for w in ws],
        out_specs=out_specs,
        out_shape=out_shape,
        scratch_shapes=scratch,
        compiler_params=_params(("arbitrary", "arbitrary")),
        name=name,
    )(x3, gain, scale3, shift3, *ws)


HEADS_PER_SLAB = LANES // ATT_HEAD_DIM
N_SLABS = ATT_HEADS // HEADS_PER_SLAB
LSE_LANES = LANES // ATT_HEADS


def _attn_unit(q, kv, bias_ref, table, o_ref, l_ref, at):
    dn = (((1,), (1,)), ((), ()))
    nk = 2 * WIN_STEPS
    lane_q = lax.broadcasted_iota(jnp.int32, (WIN_STEPS, LANES), 1) < ATT_HEAD_DIM
    lane_k = lax.broadcasted_iota(jnp.int32, (nk, LANES), 1) < ATT_HEAD_DIM
    ones_lo = jnp.where(lane_k, 1.0, 0.0).astype(BF16)
    ones_hi = jnp.where(lane_k, 0.0, 1.0).astype(BF16)
    zero_q = jnp.zeros((WIN_STEPS, LANES), BF16)
    zero_k = jnp.zeros((nk, LANES), BF16)

    def scores(m):
        cs = slice(m * LANES, (m + 1) * LANES)
        qs = q[:, cs]
        ks = kv[:, cs]
        out = []
        for hh in range(HEADS_PER_SLAB):
            qm = jnp.where(lane_q, qs, zero_q) if hh == 0 else jnp.where(lane_q, zero_q, qs)
            out.append(lax.dot_general(qm, ks, dn, preferred_element_type=F32)
                       + bias_ref[table, m * HEADS_PER_SLAB + hh])
        return out

    def finish(m, ss):
        cs = slice(m * LANES, (m + 1) * LANES)
        vs = kv[:, ATT_WIDTH + m * LANES:ATT_WIDTH + (m + 1) * LANES]
        ps, mxs = [], []
        for s in ss:
            mx = jnp.max(jnp.maximum(s[:, :WIN_STEPS], s[:, WIN_STEPS:]), -1, keepdims=True)
            ps.append(jnp.exp(s - mx).astype(BF16))
            mxs.append(mx)
        pcat = jnp.concatenate(ps, axis=1)
        vpair = jnp.concatenate(
            [jnp.concatenate([jnp.where(lane_k, vs, zero_k), ones_lo], axis=1),
             jnp.concatenate([jnp.where(lane_k, zero_k, vs), ones_hi], axis=1)], axis=0)
        acc = jnp.dot(pcat, vpair, preferred_element_type=F32)
        den = acc[:, LANES:]
        o_ref[at + (cs,)] = (acc[:, :LANES] / den).astype(o_ref.dtype)
        lse = jnp.where(lane_q, mxs[0], mxs[1]) + jnp.log(den)
        return pltpu.roll(lse, (LSE_LANES * HEADS_PER_SLAB * m - 48) % LANES, axis=1)

    lane = lax.broadcasted_iota(jnp.int32, (WIN_STEPS, LANES), 1)
    lse_c = None
    pending = {0: scores(0), 1: scores(1)}
    for m in range(N_SLABS):
        if m + 2 < N_SLABS:
            pending[m + 2] = scores(m + 2)
        part = finish(m, pending.pop(m))
        lse_c = part if lse_c is None else jnp.where(lane >= LSE_LANES * HEADS_PER_SLAB * m, part, lse_c)
    l_ref[at + (slice(None),)] = lse_c


def _attn_kernel(q_ref, kvc_ref, kvp_ref, bias_ref, o_ref, l_ref, *, ns):
    d = q_ref.shape[1]
    first_tile = pl.program_id(1) == 0

    def body(idx, carry):
        rr = idx // ns
        j = idx % ns
        rc = pl.ds(pl.multiple_of(j * WIN_STEPS, WIN_STEPS), WIN_STEPS)
        q = q_ref[0, rr, rc, :] * ATT_SCALE
        kv_prev = kvp_ref[0, rr]
        if ns > 1:
            rp = pl.ds(pl.multiple_of(jnp.maximum(j - 1, 0) * WIN_STEPS, WIN_STEPS), WIN_STEPS)
            kv_prev = jnp.where(j == 0, kv_prev, kvc_ref[0, rr, rp, :])
        kv = jnp.concatenate([kv_prev, kvc_ref[0, rr, rc, :]], axis=0)
        table = jnp.where(first_tile & (j == 0), 1, 0)
        _attn_unit(q, kv, bias_ref, table, o_ref, l_ref, (0, rr, rc))
        return carry

    lax.fori_loop(0, d * ns, body, 0, unroll=4)


def _attn_call(q, kv, bias, name):
    B, d, U, _ = q.shape
    ns = ATT_TILE // (WIN_STEPS * d)
    rows = ns * WIN_STEPS
    blk = lambda width: pl.BlockSpec((1, d, rows, width), lambda b, t: (b, 0, t, 0))
    return pl.pallas_call(
        functools.partial(_attn_kernel, ns=ns),
        grid=(B, U // rows),
        in_specs=[blk(ATT_WIDTH), blk(2 * ATT_WIDTH),
                  pl.BlockSpec((1, d, WIN_STEPS, 2 * ATT_WIDTH),
                               lambda b, t: (b, 0, jnp.maximum(t * ns - 1, 0), 0)),
                  pl.BlockSpec(bias.shape, lambda b, t: (0, 0, 0, 0))],
        out_specs=[blk(ATT_WIDTH), blk(LANES)],
        out_shape=[jax.ShapeDtypeStruct((B, d, U, ATT_WIDTH), BF16),
                   jax.ShapeDtypeStruct((B, d, U, LANES), F32)],
        compiler_params=_params(("arbitrary", "arbitrary")),
        name=name,
    )(q, kv, kv, bias)


def _stride_bias(rel_table, g, d):
    bucket = _t5_bucket(np.arange(WIN_STEPS + 1) * d)
    onehot = jnp.asarray(np.eye(N_BUCKETS, dtype=np.float32)[bucket])
    tbl = rel_table[:, g * ATT_HEADS:(g + 1) * ATT_HEADS].astype(F32)
    return jnp.dot(onehot, tbl, precision=lax.Precision.HIGHEST).T


def _prompt_bias(rel_table, g, d):
    vals = _stride_bias(rel_table, g, d)
    n = WIN_STEPS
    period = 3 * n
    wp = jnp.concatenate([jnp.full((ATT_HEADS, n - 1), NEG, F32), vals[:, ::-1],
                          jnp.full((ATT_HEADS, n), NEG, F32)], axis=1)
    flat = jnp.tile(wp, (1, n))[:, :n * (period - 1)]
    bias = flat.reshape(ATT_HEADS, n, period - 1)[:, :, n - 1:n - 1 + 2 * n]
    first = jnp.concatenate([jnp.full((ATT_HEADS, n, n), NEG, F32), bias[:, :, n:]], axis=2)
    return jnp.stack([bias, first])


SHIFT_VREGS = 128


def _sattn_kernel(*refs, T):
    q_refs = refs[0:3]
    kvnew_refs = refs[3:6]
    cache_refs = refs[6:9]
    bc_refs = refs[9:12]
    bn_refs = refs[12:15]
    o_ref = refs[15]
    co_refs = refs[16:19]
    kvn_refs = refs[19:22]
    HT = ATT_HEADS * T
    dn = (((1,), (1,)), ((), ()))
    row_head = lax.broadcasted_iota(jnp.int32, (HT, ATT_WIDTH), 0) // T
    col_head = lax.broadcasted_iota(jnp.int32, (HT, ATT_WIDTH), 1) // ATT_HEAD_DIM
    head_mask = row_head == col_head

    for g in range(N_GROUPS):
        rows = jnp.concatenate([jnp.zeros((LANES - T, 2 * ATT_WIDTH), F32), kvnew_refs[g][0]], axis=0)
        kvn_refs[g][0] = rows.T

    stats = []
    for g in range(N_GROUPS):
        q = q_refs[g][0] * ATT_SCALE
        qexp = jnp.where(head_mask, jnp.concatenate([q] * ATT_HEADS, axis=0), 0.0).astype(BF16)
        kn = kvn_refs[g][0, :ATT_WIDTH, :].astype(BF16)
        kc = cache_refs[g][0, :ATT_WIDTH, :].astype(BF16)
        lc = jnp.dot(qexp, kc, preferred_element_type=F32) + bc_refs[g][...]
        ln = jnp.dot(qexp, kn, preferred_element_type=F32) + bn_refs[g][...]
        mx = jnp.maximum(jnp.max(lc, -1, keepdims=True), jnp.max(ln, -1, keepdims=True))
        pc = jnp.exp(lc - mx)
        pn = jnp.exp(ln - mx)
        ssum = jnp.sum(pc, -1, keepdims=True) + jnp.sum(pn, -1, keepdims=True)
        stats.append((pc, pn, ssum, mx + jnp.log(ssum)))

    lse_max = jnp.maximum(jnp.maximum(stats[0][3], stats[1][3]), stats[2][3])
    es = [jnp.exp(st[3] - lse_max) for st in stats]
    esum = es[0] + es[1] + es[2]
    acc = jnp.zeros((HT, ATT_WIDTH), F32)
    for g in range(N_GROUPS):
        pc, pn, ssum, _ = stats[g]
        w = es[g] / (esum * ssum)
        vc = cache_refs[g][0, ATT_WIDTH:, :].astype(BF16)
        vn = kvn_refs[g][0, ATT_WIDTH:, :].astype(BF16)
        acc = acc + lax.dot_general((pc * w).astype(BF16), vc, dn, preferred_element_type=F32)
        acc = acc + lax.dot_general((pn * w).astype(BF16), vn, dn, preferred_element_type=F32)
    lane_head = lax.broadcasted_iota(jnp.int32, (T, ATT_WIDTH), 1) // ATT_HEAD_DIM
    o = jnp.zeros((T, ATT_WIDTH), F32)
    for h in range(ATT_HEADS):
        o = o + jnp.where(lane_head == h, acc[h * T:(h + 1) * T, :], 0.0)
    o_ref[0] = o

    for g in range(N_GROUPS):
        L = cache_refs[g].shape[2]
        nrows = cache_refs[g].shape[1]
        chunk = min(nrows, SHIFT_VREGS * SUBLANES * LANES // L)
        is_new = lax.broadcasted_iota(jnp.int32, (chunk, LANES), 1) >= LANES - T

        def shift_rows(i, carry, g=g, L=L, chunk=chunk, is_new=is_new):
            rs = pl.ds(pl.multiple_of(i * chunk, chunk), chunk)
            rolled = pltpu.roll(cache_refs[g][0, rs, :], L - T, axis=1)
            if L > LANES:
                co_refs[g][0, rs, 0:L - LANES] = rolled[:, 0:L - LANES]
            co_refs[g][0, rs, L - LANES:L] = jnp.where(is_new, kvn_refs[g][0, rs, :], rolled[:, L - LANES:])
            return carry

        lax.fori_loop(0, nrows // chunk, shift_rows, 0)


def _sample_bias(rel_table, g, W, d, Lb, T):
    vals = _stride_bias(rel_table, g, d)
    n = Lb + T
    by_dist = jnp.concatenate([vals[:, :, None], jnp.full((ATT_HEADS, WIN_STEPS + 1, d - 1), NEG, F32)],
                              axis=2).reshape(ATT_HEADS, (WIN_STEPS + 1) * d)
    if by_dist.shape[1] < n:
        by_dist = jnp.concatenate([by_dist, jnp.full((ATT_HEADS, n - by_dist.shape[1]), NEG, F32)], axis=1)
    rev = jnp.concatenate([by_dist[:, :n][:, ::-1], jnp.full((ATT_HEADS, T), NEG, F32)], axis=1)
    rows = jnp.stack([rev[:, T - 1 - t:T - 1 - t + n] for t in range(T)], axis=1)
    rows = rows.reshape(ATT_HEADS * T, n)
    bc = rows[:, :Lb]
    bn = jnp.concatenate([jnp.full((ATT_HEADS * T, LANES - T), NEG, F32), rows[:, Lb:]], axis=1)
    return bc, bn


def _sattn_call(qs, kvns, caches, bcs, bns, T):
    DB = caches[0].shape[0]
    rows = caches[0].shape[1]
    per_seq = lambda b: (b, 0, 0)
    const2 = lambda b: (0, 0)
    in_specs = ([pl.BlockSpec((1, T, ATT_WIDTH), per_seq)] * 3
                + [pl.BlockSpec((1,) + x.shape[1:], per_seq) for x in kvns]
                + [pl.BlockSpec((1,) + c.shape[1:], per_seq) for c in caches]
                + [pl.BlockSpec(x.shape, const2) for x in bcs]
                + [pl.BlockSpec(x.shape, const2) for x in bns])
    out_specs = ([pl.BlockSpec((1, T, ATT_WIDTH), per_seq)]
                 + [pl.BlockSpec((1,) + c.shape[1:], per_seq) for c in caches])
    out_shape = ([jax.ShapeDtypeStruct((DB, T, ATT_WIDTH), F32)]
                 + [jax.ShapeDtypeStruct(c.shape, F32) for c in caches])
    return pl.pallas_call(
        functools.partial(_sattn_kernel, T=T),
        grid=(DB,),
        in_specs=in_specs, out_specs=out_specs, out_shape=out_shape,
        scratch_shapes=[pltpu.VMEM((1, rows, LANES), F32)] * N_GROUPS,
        compiler_params=_params(("arbitrary",)),
        name="sample_attn",
    )(*qs, *kvns, *caches, *bcs, *bns)


def _scan_rows(x, op, fill):
    n = x.shape[0]
    rowid = lax.broadcasted_iota(jnp.int32, x.shape, 0)
    s = 1
    while s < n:
        shifted = pltpu.roll(x, s, axis=0)
        x = op(x, jnp.where(rowid >= s, shifted, fill))
        s *= 2
    return x


def _pad_rows(x, n, fill=0.0):
    if x.shape[0] == n:
        return x
    return jnp.concatenate([x, jnp.full((n - x.shape[0],) + x.shape[1:], fill, x.dtype)], axis=0)


def _mlstm_kernel(xm_ref, zm_ref, om_ref, g_ref, cprev_ref, C0_ref, n0_ref, m0_ref,
                  convw_ref, convb_ref, wq_ref, wk_ref, mnorm_ref, mskip_ref, bif_ref,
                  mo_ref, C_ref, n_ref, m_ref, xc_ref, *, L):
    LS = max(L, LANES)
    G = xm_ref.shape[0]
    DV = C_ref.shape[2]
    DK = C_ref.shape[3]
    c = pl.program_id(1)

    @pl.when(c == 0)
    def _():
        xc_ref[:, 0:SUBLANES, :] = cprev_ref[...]
        C_ref[...] = C0_ref[...]
        n_ref[...] = n0_ref[...]
        m_ref[...] = m0_ref[...]

    t_id = lax.broadcasted_iota(jnp.int32, (L, LS), 0)
    s_id = lax.broadcasted_iota(jnp.int32, (L, LS), 1)
    causal = s_id <= t_id
    eye = s_id == t_id
    dn_t = (((1,), (1,)), ((), ()))

    seqs = []
    for g in range(G):
        xm_b = xm_ref[g].astype(BF16)
        xc_ref[g, SUBLANES:SUBLANES + L, :] = xm_ref[g].astype(F32)
        conv = convb_ref[...]
        for j in range(CONV_WIDTH):
            off = SUBLANES - (CONV_WIDTH - 1) + j
            conv = conv + convw_ref[j:j + 1, :] * xc_ref[g, off:off + L, :]
        xc_ref[g, 0:SUBLANES, :] = xc_ref[g, L:L + SUBLANES, :]
        c_act = _silu(conv)

        i_pre = g_ref[g] + bif_ref[...]
        logf = jax.nn.log_sigmoid(pltpu.roll(i_pre, LANES - M_HEADS, axis=1))
        b = _scan_rows(logf, jnp.add, 0.0)
        a = i_pre - b
        ca = _scan_rows(a, jnp.maximum, NEG)
        m_prev = m_ref[g]
        mm = jnp.maximum(ca, m_prev)
        u = -mm
        bL = b[L - 1:L, :]
        m_new = bL + jnp.maximum(m_prev, ca[L - 1:L, :])
        m_ref[g] = m_new
        seqs.append(dict(xm_b=xm_b, c_act=c_act, cb=c_act.astype(BF16), a=a, u=u,
                         w_inter=jnp.exp(u + m_prev), emt=jnp.exp(-(b + mm)),
                         wk=jnp.exp(bL + a - m_new), wC=jnp.exp(bL + m_prev - m_new)))

    units = [(g, h) for g in range(G) for h in range(M_HEADS)]
    vsl = [slice(h * DV, (h + 1) * DV) for h in range(M_HEADS)]
    qs, ks, kws, vss, C_old, n_old = {}, {}, {}, {}, {}, {}
    for g, h in units:
        ch = seqs[g]["cb"][:, vsl[h]]
        qs[g, h] = (jnp.dot(ch, wq_ref[h], preferred_element_type=F32) * (DK ** -0.5)).astype(BF16)
        k32 = jnp.dot(ch, wk_ref[h], preferred_element_type=F32)
        ks[g, h] = _pad_rows(k32.astype(BF16), LS)
        kws[g, h] = _pad_rows(k32 * seqs[g]["wk"][:, h:h + 1], LS)
        vss[g, h] = _pad_rows(seqs[g]["xm_b"][:, vsl[h]], LS)
        C_old[g, h] = C_ref[g, h]
        n_old[g, h] = n_ref[g, h:h + 1, :]
    scs, inters = {}, {}
    for g, h in units:
        a_row = jnp.sum(jnp.where(eye, seqs[g]["a"][:, h:h + 1], 0.0), axis=0, keepdims=True)
        w_intra = jnp.exp(jnp.where(causal, seqs[g]["u"][:, h:h + 1] + a_row, NEG))
        scs[g, h] = lax.dot_general(qs[g, h], ks[g, h], dn_t, preferred_element_type=F32) * w_intra
        inters[g, h] = lax.dot_general(qs[g, h], C_old[g, h].astype(BF16), dn_t, preferred_element_type=F32)
    for g, h in units:
        wi = seqs[g]["w_inter"][:, h:h + 1]
        num = jnp.dot(scs[g, h].astype(BF16), vss[g, h], preferred_element_type=F32) + wi * inters[g, h]
        den = (jnp.sum(scs[g, h], -1, keepdims=True)
               + wi * jnp.sum(qs[g, h].astype(F32) * n_old[g, h], -1, keepdims=True))
        hcell = num / jnp.maximum(jnp.abs(den), seqs[g]["emt"][:, h:h + 1])
        mu = jnp.mean(hcell, -1, keepdims=True)
        hc = hcell - mu
        hn = hc * lax.rsqrt(jnp.mean(hc * hc, -1, keepdims=True) + EPS)
        out = ((om_ref[g, :, vsl[h]].astype(F32) * (hn * mnorm_ref[:, vsl[h]])
                + mskip_ref[:, vsl[h]] * seqs[g]["c_act"][:, vsl[h]]) * zm_ref[g, :, vsl[h]].astype(F32))
        mo_ref[g, :, vsl[h]] = out.astype(mo_ref.dtype)
    for g, h in units:
        wc = seqs[g]["wC"][:, h:h + 1]
        C_ref[g, h] = wc * C_old[g, h] + lax.dot_general(vss[g, h], kws[g, h].astype(BF16),
                                                         (((0,), (0,)), ((), ())),
                                                         preferred_element_type=F32)
        n_ref[g, h:h + 1, :] = wc * n_old[g, h] + jnp.sum(kws[g, h], axis=0, keepdims=True)


def _mlstm_call(xm, zm, om, gates, conv_prev, C0, n0, m0, convw, convb, wq, wk, mnorm, mskip, bif,
                *, L, G, out_dtype, name):
    N, S, M = xm.shape
    H, DV, DK = C0.shape[1:]
    seq = lambda b, c: (b, c, 0)
    per_b3 = lambda b, c: (b, 0, 0)
    per_b4 = lambda b, c: (b, 0, 0, 0)
    const2 = lambda b, c: (0, 0)
    const3 = lambda b, c: (0, 0, 0)
    in_specs = [pl.BlockSpec((G, L, M), seq), pl.BlockSpec((G, L, M), seq), pl.BlockSpec((G, L, M), seq),
                pl.BlockSpec((G, L, LANES), seq),
                pl.BlockSpec((G, SUBLANES, M), per_b3),
                pl.BlockSpec((G, H, DV, DK), per_b4),
                pl.BlockSpec((G, H, DK), per_b3),
                pl.BlockSpec((G, 1, LANES), per_b3),
                pl.BlockSpec(convw.shape, const2), pl.BlockSpec(convb.shape, const2),
                pl.BlockSpec(wq.shape, const3), pl.BlockSpec(wk.shape, const3),
                pl.BlockSpec(mnorm.shape, const2), pl.BlockSpec(mskip.shape, const2),
                pl.BlockSpec(bif.shape, const2)]
    out_specs = [pl.BlockSpec((G, L, M), seq),
                 pl.BlockSpec((G, H, DV, DK), per_b4),
                 pl.BlockSpec((G, H, DK), per_b3),
                 pl.BlockSpec((G, 1, LANES), per_b3)]
    out_shape = [jax.ShapeDtypeStruct((N, S, M), out_dtype),
                 jax.ShapeDtypeStruct((N, H, DV, DK), F32),
                 jax.ShapeDtypeStruct((N, H, DK), F32),
                 jax.ShapeDtypeStruct((N, 1, LANES), F32)]
    return pl.pallas_call(
        functools.partial(_mlstm_kernel, L=L),
        grid=(N // G, S // L),
        in_specs=in_specs, out_specs=out_specs, out_shape=out_shape,
        scratch_shapes=[pltpu.VMEM((G, L + 2 * SUBLANES, M), F32)],
        compiler_params=_params(("arbitrary", "arbitrary")),
        name=name,
    )(xm, zm, om, gates, conv_prev, C0, n0, m0, convw, convb, wq, wk, mnorm, mskip, bif)


def _token_order(src_ref, scr_ref):
    _, d, n, width = src_ref.shape
    if d == 1:
        return src_ref[0, 0].astype(F32)
    n_tiles = width // LANES
    for r in range(d):
        for s in range(n_tiles):
            scr_ref[s, pl.ds(r, n, stride=d), :] = src_ref[0, r, :, s * LANES:(s + 1) * LANES].astype(F32)
    return jnp.concatenate([scr_ref[s] for s in range(n_tiles)], axis=1)


def _post_kernel(*refs, merge):
    if merge:
        (x_ref, gate_ref, o0, o1, o2, l0, l1, l2, expand_ref, za_ref, mo_ref, ga_ref, gm_ref,
         wpa_ref, wpm_ref, wout_ref, fg_ref, y_ref, scr_ref) = refs
        ls = [_token_order(l, scr_ref) for l in (l0, l1, l2)]
        lmax = jnp.maximum(jnp.maximum(ls[0], ls[1]), ls[2])
        es = [jnp.exp(l - lmax) for l in ls]
        inv = 1.0 / (es[0] + es[1] + es[2])
        o_att = None
        for e, o in zip(es, (o0, o1, o2)):
            a = e * inv
            hi = a.astype(BF16)
            lo = (a - hi.astype(F32)).astype(BF16)
            a_wide = jnp.dot(jnp.concatenate([hi, lo], axis=1), expand_ref[...],
                             preferred_element_type=F32)
            term = a_wide * _token_order(o, scr_ref)
            o_att = term if o_att is None else o_att + term
    else:
        (x_ref, gate_ref, oa_ref, za_ref, mo_ref, ga_ref, gm_ref,
         wpa_ref, wpm_ref, wout_ref, fg_ref, y_ref) = refs
        o_att = oa_ref[0]
    a_in = (o_att * za_ref[0].astype(F32)).astype(BF16)
    a_br = jnp.dot(a_in, wpa_ref[...], preferred_element_type=F32)
    m_br = jnp.dot(mo_ref[0].astype(BF16), wpm_ref[...], preferred_element_type=F32)
    merged =ga_ref[0].astype(F32) * a_br + gm_ref[0].astype(F32) * m_br
    y = x_ref[0] + gate_ref[0] * jnp.dot(merged.astype(BF16), wout_ref[...], preferred_element_type=F32)
    ms = jnp.mean(y * y, axis=-1, keepdims=True)
    y_ref[0] = y * lax.rsqrt(ms + EPS) * fg_ref[...]


def _post_call(x3, gate3, att_inputs, za, mo, ga, gm, wpa, wpm, wout, fgain, *, tm, merge, name):
    B, S, D = x3.shape
    row = lambda b, i: (b, i, 0)
    const2 = lambda b, i: (0, 0)
    if gate3.shape[1] == 1:
        gate_spec = pl.BlockSpec((1, 1, D), lambda b, i: (b, 0, 0))
    else:
        gate_spec = pl.BlockSpec((1, tm, D), row)
    def blk(a):
        if a.ndim == 2:
            return pl.BlockSpec(a.shape, const2)
        if a.ndim == 4:
            d = a.shape[1]
            return pl.BlockSpec((1, d, tm // d, a.shape[3]), lambda b, i: (b, 0, i, 0))
        return pl.BlockSpec((1, tm, a.shape[2]), row)

    in_specs = ([pl.BlockSpec((1, tm, D), row), gate_spec]
                + [blk(a) for a in att_inputs]
                + [blk(za), blk(mo), blk(ga), blk(gm),
                   pl.BlockSpec(wpa.shape, const2), pl.BlockSpec(wpm.shape, const2),
                   pl.BlockSpec(wout.shape, const2), pl.BlockSpec(fgain.shape, const2)])
    scratch = [pltpu.VMEM((ATT_WIDTH // LANES, tm, LANES), F32)] if merge else []
    return pl.pallas_call(
        functools.partial(_post_kernel, merge=merge),
        grid=(B, S // tm),
        in_specs=in_specs,
        out_specs=pl.BlockSpec((1, tm, D), row),
        out_shape=jax.ShapeDtypeStruct((B, S, D), F32),
        scratch_shapes=scratch,
        compiler_params=_params(("arbitrary", "arbitrary")),
        name=name,
    )(x3, gate3, *att_inputs, za, mo, ga, gm, wpa, wpm, wout, fgain)


def _cast_kernel(x_ref, o_ref):
    o_ref[...] = x_ref[...].T.astype(o_ref.dtype)


def _cast_bf16_call(wt, ncols):
    k = wt.shape[1]
    tn = 1024
    return pl.pallas_call(
        _cast_kernel,
        grid=(ncols // tn,),
        in_specs=[pl.BlockSpec((tn, k), lambda j: (j, 0))],
        out_specs=pl.BlockSpec((k, tn), lambda j: (0, j)),
        out_shape=jax.ShapeDtypeStruct((k, ncols), BF16),
        compiler_params=_params(("arbitrary",)),
        name="cast_weights",
    )(wt)


def _tail_cast_kernel(a_ref, b_ref, c_ref, o_ref, *, n_gate, n_merge):
    rows = jnp.concatenate([a_ref[...], b_ref[...], c_ref[0:SUBLANES, :]], axis=0)
    gates = jnp.concatenate([rows[0:n_gate], jnp.zeros((LANES - n_gate, rows.shape[1]), F32)], axis=0)
    out = jnp.concatenate([rows[n_gate:n_gate + n_merge], gates], axis=0)
    o_ref[...] = out.T.astype(o_ref.dtype)


def _tail_cast_call(wt, first, n_gate, n_merge):
    k = wt.shape[1]
    tn = n_merge // 2
    j0 = first // tn
    assert first % tn == 0 and n_gate == SUBLANES
    return pl.pallas_call(
        functools.partial(_tail_cast_kernel, n_gate=n_gate, n_merge=n_merge),
        grid=(1,),
        in_specs=[pl.BlockSpec((tn, k), lambda j: (j0, 0), pipeline_mode=pl.Buffered(1)),
                  pl.BlockSpec((tn, k), lambda j: (j0 + 1, 0), pipeline_mode=pl.Buffered(1)),
                  pl.BlockSpec((tn, k), lambda j: (j0 + 2, 0), pipeline_mode=pl.Buffered(1))],
        out_specs=pl.BlockSpec((k, n_merge + LANES), lambda j: (0, 0)),
        out_shape=jax.ShapeDtypeStruct((k, n_merge + LANES), BF16),
        compiler_params=_params(("arbitrary",)),
        name="cast_tail_weights",
    )(wt, wt, wt)


def _projection_weights(w_in):
    D = w_in.shape[0]
    AW = ATT_WIDTH
    M = D
    off_k, off_v, off_za = 3 * AW, 6 * AW, 9 * AW
    off_xm = off_za + AW
    off_i = off_xm + 3 * M
    wt = jnp.transpose(w_in)
    w_main = _cast_bf16_call(wt, off_i)
    w_tail = _tail_cast_call(wt, off_i, 2 * M_HEADS, 2 * D)
    segs = {"za": ((0, off_za, AW),), "xm": ((0, off_xm, M),), "zm": ((0, off_xm + M, M),),
            "om": ((0, off_xm + 2 * M, M),), "ga": ((1, 0, D),), "gm": ((1, D, D),),
            "gates": ((1, 2 * D, LANES),)}
    for g in range(N_GROUPS):
        segs[f"q{g}"] = ((0, g * AW, AW),)
        segs[f"kv{g}"] = ((0, off_k + g * AW, AW), (0, off_v + g * AW, AW))
    return (w_main, w_tail), segs


def kernel(x_prompt, x_sample, cache_kv_w128, cache_kv_w512, cache_kv_w2048, state_conv, state_C, state_n, state_m, c_prompt, c_sample, rel_table, norm_gain, w_ada, b_ada, w_in, b_if, conv_w, conv_b, w_mq, w_mk, m_norm, m_skip, w_pa, w_pm, w_out, final_gain):
    B, S, D = x_prompt.shape
    DB, T, _ = x_sample.shape
    assert norm_gain.shape[0] == 1, "single-layer trunk"
    assert S % ATT_TILE == 0 and S % MLSTM_CHUNK == 0 and T == SUBLANES
    caches = (cache_kv_w128[0], cache_kv_w512[0], cache_kv_w2048[0])
    H = M_HEADS
    M = conv_w.shape[2]

    wp, segs = _projection_weights(w_in[0])
    names = ("q0", "q1", "q2", "za", "kv0", "kv1", "kv2", "xm", "zm", "om", "ga", "gm", "gates")
    seg_list = [segs[n] for n in names]
    gain = norm_gain[0].reshape(1, D)
    fgain = final_gain.reshape(1, D)
    wpa, wpm, wout = w_pa[0].astype(BF16), w_pm[0].astype(BF16), w_out[0].astype(BF16)
    wq, wk = w_mq[0].astype(BF16), w_mk[0].astype(BF16)
    convw, convb = conv_w[0], conv_b[0].reshape(1, M)
    mnorm, mskip = m_norm[0].reshape(1, M), m_skip[0].reshape(1, M)
    bif = jnp.concatenate([b_if[0], jnp.zeros((LANES - 2 * H,), F32)]).reshape(1, LANES)

    ada = _ada_call(jnp.concatenate([c_prompt, c_sample], axis=0), w_ada[0], b_ada[0])
    shift, scale, gate = ada[:, :D], ada[:, D:2 * D], ada[:, 2 * D:]

    gate_act = {"za": "silu", "zm": "silu", "om": "sigmoid", "ga": "sigmoid", "gm": "sigmoid"}
    acts = [gate_act.get(n) for n in names]

    R = DB * T
    rep = lambda t: jnp.repeat(t[B:], T, axis=0).reshape(1, R, D)
    s_shift, s_scale, s_gate = rep(shift), rep(scale), rep(gate)
    xs = x_sample.reshape(1, R, D)
    sr = dict(zip(names, _proj_call(xs, gain, s_scale, s_shift, wp, seg_list, [F32] * 13,
                                    tm=R, row0=0, rows=R, name="proj_sample", acts=acts)))
    bcs, bns = [], []
    for g, (win, dil) in enumerate(ATT_GROUPS):
        bc, bn = _sample_bias(rel_table, g, win, dil, caches[g].shape[1], T)
        bcs.append(bc)
        bns.append(bn)
    cache_t = [jnp.transpose(c, (0, 2, 3, 4, 1)).reshape(DB, 2 * ATT_WIDTH, c.shape[1]) for c in caches]
    kvn_t = [sr[f"kv{g}"].reshape(DB, T, 2 * ATT_WIDTH) for g in range(3)]
    sattn_args = ([sr[f"q{g}"].reshape(DB, T, ATT_WIDTH) for g in range(3)], kvn_t, cache_t, bcs, bns)

    p_shift, p_scale, p_gate = (t[:B].reshape(B, 1, D) for t in (shift, scale, gate))
    dts = [F32 if n == "gates" else BF16 for n in names]
    group_dil = {f"{p}{g}": dil for g, (_, dil) in enumerate(ATT_GROUPS) for p in ("q", "kv")}
    pr = dict(zip(names, _proj_call(x_prompt, gain, p_scale, p_shift, wp, seg_list, dts,
                                    tm=PROJ_TM, row0=0, rows=S, name="proj_prompt",
                                    dils=[group_dil.get(n, 1) for n in names], acts=acts)))
    att = []
    for g, (win, dil) in enumerate(ATT_GROUPS):
        planes = lambda a: a if a.ndim == 4 else a[:, None]
        att.append(_attn_call(planes(pr[f"q{g}"]), planes(pr[f"kv{g}"]), _prompt_bias(rel_table, g, dil),
                              f"attn_prompt_g{g}"))
    expand = np.zeros((2 * LANES, ATT_WIDTH), np.float32)
    for h in range(ATT_HEADS):
        expand[[LSE_LANES * h, LANES + LSE_LANES * h], h * ATT_HEAD_DIM:(h + 1) * ATT_HEAD_DIM] = 1.0
    expand = jnp.asarray(expand, BF16)
    mlstm_p_args = (pr["xm"], pr["zm"], pr["om"], pr["gates"],
                    jnp.zeros((B, SUBLANES, M), F32), jnp.zeros((B,) + state_C.shape[2:], F32),
                    jnp.zeros((B,) + state_n.shape[2:], F32), jnp.zeros((B, 1, LANES), F32),
                    convw, convb, wq, wk, mnorm, mskip, bif)
    mo_p, C_p, n_p, m_p = _mlstm_call(*mlstm_p_args, L=MLSTM_CHUNK, G=_largest_divisor(B, MLSTM_SEQS_PROMPT),
                                      out_dtype=BF16, name="mlstm_prompt")
    sa = _sattn_call(*sattn_args, T)
    y_prompt = _post_call(x_prompt, p_gate, [a[0] for a in att] + [a[1] for a in att] + [expand],
                          pr["za"], mo_p, pr["ga"], pr["gm"], wpa, wpm, wout, fgain,
                          tm=POST_TM, merge=True, name="post_prompt")
    def tail_weights(keys):
        cols, tsegs, c = [], [], 0
        for key in keys:
            pieces = []
            for wi, c0, wd in segs[key]:
                cols.append(wp[wi][:, c0:c0 + wd])
                pieces.append((0, c, wd))
                c += wd
            tsegs.append(tuple(pieces))
        return (jnp.concatenate(cols, axis=1),), tsegs

    w_max = min(ATT_GROUPS[-1][0], S)
    w2, segs2 = tail_weights(["kv2"])
    (kv2_t,) = _proj_call(x_prompt, gain, p_scale, p_shift, w2, segs2, [F32],
                          tm=PROJ_TM, row0=S - w_max, rows=w_max, name="tail_kv2", dils=[0])
    w_mid = min(ATT_GROUPS[1][0], S)
    w01, segs01 = tail_weights(["kv0", "kv1", "xm"])
    kv0_t, kv1_t, xm_t = _proj_call(x_prompt, gain, p_scale, p_shift, w01, segs01, [F32] * 3,
                                    tm=w_mid, row0=S - w_mid, rows=w_mid, name="tail_kv01", dils=[0, 0, 1])
    w0 = min(ATT_GROUPS[0][0], S)
    as_buffer = lambda t: jnp.transpose(t.reshape(B, 2, ATT_HEADS, ATT_HEAD_DIM, t.shape[2]), (0, 4, 1, 2, 3))[None]
    kv_p = [as_buffer(kv0_t[:, :, w_mid - w0:]), as_buffer(kv1_t), as_buffer(kv2_t)]
    conv_p = xm_t[:, w_mid - (CONV_WIDTH - 1):][None]

    o_att_s = sa[0].reshape(1, R, ATT_WIDTH)
    kv_s = [jnp.transpose(c.reshape(DB, 2, ATT_HEADS, ATT_HEAD_DIM, c.shape[2]), (0, 4, 1, 2, 3))[None]
            for c in sa[1:]]
    conv_prev_s = jnp.concatenate([jnp.zeros((DB, SUBLANES - (CONV_WIDTH - 1), M), F32), state_conv[0]], axis=1)
    m0_s = jnp.concatenate([state_m[0], jnp.zeros((DB, LANES - H), F32)], axis=1).reshape(DB, 1, LANES)
    seqv = lambda t: t.reshape(DB, T, t.shape[-1])
    mo_s, C_s, n_s, m_s = _mlstm_call(
        seqv(sr["xm"]), seqv(sr["zm"]), seqv(sr["om"]), seqv(sr["gates"]),
        conv_prev_s, state_C[0], state_n[0], m0_s,
        convw, convb, wq, wk, mnorm, mskip, bif, L=T, G=_largest_divisor(DB, MLSTM_SEQS_SAMPLE),
        out_dtype=F32, name="mlstm_sample")
    y_sample = _post_call(xs, s_gate, [o_att_s], sr["za"], mo_s.reshape(1, R, M), sr["ga"], sr["gm"],
                          wpa, wpm, wout, fgain, tm=R, merge=False, name="post_sample")
    conv_s = seqv(sr["xm"])[:, T - (CONV_WIDTH - 1):][None]

    return (y_prompt, y_sample.reshape(DB, T, D),
            kv_p[0], kv_s[0], kv_p[1], kv_s[1], kv_p[2], kv_s[2],
            conv_p, conv_s, C_p[None], C_s[None], n_p[None], n_s[None],
            m_p[:, 0, :H][None], m_s[:, 0, :H][None])
```

```python
import functools

import numpy as np
import jax
import jax.numpy as jnp
from jax import lax
from jax.experimental import pallas as pl
from jax.experimental.pallas import tpu as pltpu

F32 = jnp.float32
BF16 = jnp.bfloat16

ATT_GROUPS = ((128, 1), (512, 4), (2048, 16))
N_GROUPS = len(ATT_GROUPS)
ATT_HEADS = 8
ATT_HEAD_DIM = 64
ATT_WIDTH = ATT_HEADS * ATT_HEAD_DIM
WIN_STEPS = 128
ATT_SCALE = ATT_HEAD_DIM ** -0.5
N_BUCKETS = 32
MAX_DISTANCE = 2048
M_HEADS = 4
CONV_WIDTH = 4
EPS = 1e-6
NEG = -1e30

LANES = 128
SUBLANES = 8
VMEM_LIMIT = 56 * 1024 * 1024

ATT_TILE = 2048
PROJ_TM = 512
POST_TM = 1024
MLSTM_CHUNK = 256
MLSTM_SEQS_PROMPT = 2
MLSTM_SEQS_SAMPLE = 4


def _largest_divisor(n, cap):
    return max(g for g in range(1, cap + 1) if n % g == 0)


def _params(sem, vmem=VMEM_LIMIT):
    return pltpu.CompilerParams(dimension_semantics=sem, vmem_limit_bytes=vmem)


def _t5_bucket(dist):
    n = np.asarray(dist).astype(np.int64)
    max_exact = N_BUCKETS // 2
    nf = np.maximum(n, 1).astype(np.float32)
    large = max_exact + (np.log(nf / max_exact) / np.log(np.float32(MAX_DISTANCE / max_exact))
                         * (N_BUCKETS - max_exact)).astype(np.int64)
    large = np.minimum(large, N_BUCKETS - 1)
    return np.where(n < max_exact, n, large).astype(np.int32)


def _sigmoid_tanh(v):
    return 0.5 * jnp.tanh(0.5 * v) + 0.5


def _silu(x):
    return x * _sigmoid_tanh(x)


def _ada_kernel(c_ref, w_ref, b_ref, o_ref):
    s = _silu(c_ref[...])
    o_ref[...] = jnp.dot(s, w_ref[...], preferred_element_type=F32,
                         precision=lax.Precision.HIGHEST) + b_ref[...]


def _ada_call(c, w, b):
    n, d = c.shape
    width = w.shape[1]
    tn = width // 2 if width % (2 * LANES) == 0 else width
    return pl.pallas_call(
        _ada_kernel,
        grid=(width // tn,),
        in_specs=[pl.BlockSpec((n, d), lambda j: (0, 0)),
                  pl.BlockSpec((d, tn), lambda j: (0, j)),
                  pl.BlockSpec((1, tn), lambda j: (0, j))],
        out_specs=pl.BlockSpec((n, tn), lambda j: (0, j)),
        out_shape=jax.ShapeDtypeStruct((n, width), F32),
        compiler_params=_params(("arbitrary",)),
        name="ada",
    )(c, w, b.reshape(1, width))


_ACTIVATIONS = {None: lambda v: v, "silu": _silu, "sigmoid": _sigmoid_tanh}


def _proj_kernel(x_ref, gain_ref, scale_ref, shift_ref, *rest, n_w, segs, dils, acts):
    w_refs = rest[:n_w]
    rest = rest[n_w:]
    out_refs = rest[:len(segs)]
    x = x_ref[0]
    tm, D = x.shape
    ms = jnp.mean(x * x, axis=-1, keepdims=True)
    h = x * lax.rsqrt(ms + EPS) * gain_ref[...] * (1.0 + scale_ref[0]) + shift_ref[0]
    lhs = {1: h.astype(BF16)}
    lhs[0] = lhs[1]
    strides = sorted(set(dils) - {0, 1})
    if strides:
        hs_ref = rest[len(segs)]
        n_tiles = D // LANES
        for s in range(n_tiles):
            hs_ref[s] = h[:, s * LANES:(s + 1) * LANES]
        for d in strides:
            n = tm // d
            lhs[d] = jnp.concatenate(
                [jnp.concatenate([hs_ref[s, pl.ds(r, n, stride=d), :] for r in range(d)], axis=0)
                 for s in range(n_tiles)], axis=1).astype(BF16)
    for o_ref, pieces, d, act in zip(out_refs, segs, dils, acts):
        parts = [_ACTIVATIONS[act](jnp.dot(lhs[d], w_refs[wi][:, c0:c0 + width],
                                           preferred_element_type=F32)).astype(o_ref.dtype)
                 for wi, c0, width in pieces]
        res = parts[0] if len(parts) == 1 else jnp.concatenate(parts, axis=1)
        if d == 0:
            o_ref[0] = res.T
        elif d == 1:
            o_ref[0] = res
        else:
            n = tm // d
            for r in range(d):
                o_ref[0, r] = res[r * n:(r + 1) * n, :]


def _proj_call(x3, gain, scale3, shift3, ws, segs, dtypes, *, tm, row0, rows, name, dils=None, acts=None):
    B, S, D = x3.shape
    nrb = rows // tm
    rb0 = row0 // tm
    dils = tuple(dils) if dils is not None else (1,) * len(segs)
    acts = tuple(acts) if acts is not None else (None,) * len(segs)
    per_row = scale3.shape[1] != 1
    if per_row:
        mod_spec = pl.BlockSpec((1, tm, D), lambda b, i: (b, rb0 + i, 0))
    else:
        mod_spec = pl.BlockSpec((1, 1, D), lambda b, i: (b, 0, 0))
    out_shape, out_specs = [], []
    for pieces, dt, d in zip(segs, dtypes, dils):
        wd = sum(p[2] for p in pieces)
        if d == 0:
            out_shape.append(jax.ShapeDtypeStruct((B, wd, rows), dt))
            out_specs.append(pl.BlockSpec((1, wd, tm), lambda b, i: (b, 0, i)))
        elif d == 1:
            out_shape.append(jax.ShapeDtypeStruct((B, rows, wd), dt))
            out_specs.append(pl.BlockSpec((1, tm, wd), lambda b, i: (b, i, 0)))
        else:
            out_shape.append(jax.ShapeDtypeStruct((B, d, rows // d, wd), dt))
            out_specs.append(pl.BlockSpec((1, d, tm // d, wd), lambda b, i: (b, 0, i, 0)))
    scratch = [pltpu.VMEM((D // LANES, tm, LANES), F32)] if any(d > 1 for d in dils) else []
    return pl.pallas_call(
        functools.partial(_proj_kernel, n_w=len(ws), segs=tuple(segs), dils=dils, acts=acts),
        grid=(B, nrb),
        in_specs=[pl.BlockSpec((1, tm, D), lambda b, i: (b, rb0 + i, 0)),
                  pl.BlockSpec((1, D), lambda b, i: (0, 0)),
                  mod_spec, mod_spec]
                 + [pl.BlockSpec(w.shape, lambda b, i: (0, 0), pipeline_mode=pl.Buffered(1)) for w in ws],
        out_specs=out_specs,
        out_shape=out_shape,
        scratch_shapes=scratch,
        compiler_params=_params(("arbitrary", "arbitrary")),
        name=name,
    )(x3, gain, scale3, shift3, *ws)


HEADS_PER_SLAB = LANES // ATT_HEAD_DIM
N_SLABS = ATT_HEADS // HEADS_PER_SLAB
LSE_LANES = LANES // ATT_HEADS


def _attn_unit(q, kv, bias_ref, table, o_ref, l_ref, at):
    dn = (((1,), (1,)), ((), ()))
    nk = 2 * WIN_STEPS
    lane_q = lax.broadcasted_iota(jnp.int32, (WIN_STEPS, LANES), 1) < ATT_HEAD_DIM
    lane_k = lax.broadcasted_iota(jnp.int32, (nk, LANES), 1) < ATT_HEAD_DIM
    ones_lo = jnp.where(lane_k, 1.0, 0.0).astype(BF16)
    ones_hi = jnp.where(lane_k, 0.0, 1.0).astype(BF16)
    zero_q = jnp.zeros((WIN_STEPS, LANES), BF16)
    zero_k = jnp.zeros((nk, LANES), BF16)

    def scores(m):
        cs = slice(m * LANES, (m + 1) * LANES)
        qs = q[:, cs]
        ks = kv[:, cs]
        out = []
        for hh in range(HEADS_PER_SLAB):
            qm = jnp.where(lane_q, qs, zero_q) if hh == 0 else jnp.where(lane_q, zero_q, qs)
            out.append(lax.dot_general(qm, ks, dn, preferred_element_type=F32)
                       + bias_ref[table, m * HEADS_PER_SLAB + hh])
        return out

    def finish(m, ss):
        cs = slice(m * LANES, (m + 1) * LANES)
        vs = kv[:, ATT_WIDTH + m * LANES:ATT_WIDTH + (m + 1) * LANES]
        ps, mxs = [], []
        for s in ss:
            mx = jnp.max(jnp.maximum(s[:, :WIN_STEPS], s[:, WIN_STEPS:]), -1, keepdims=True)
            ps.append(jnp.exp(s - mx).astype(BF16))
            mxs.append(mx)
        pcat = jnp.concatenate(ps, axis=1)
        vpair = jnp.concatenate(
            [jnp.concatenate([jnp.where(lane_k, vs, zero_k), ones_lo], axis=1),
             jnp.concatenate([jnp.where(lane_k, zero_k, vs), ones_hi], axis=1)], axis=0)
        acc = jnp.dot(pcat, vpair, preferred_element_type=F32)
        den = acc[:, LANES:]
        o_ref[at + (cs,)] = (acc[:, :LANES] / den).astype(o_ref.dtype)
        lse = jnp.where(lane_q, mxs[0], mxs[1]) + jnp.log(den)
        return pltpu.roll(lse, (LSE_LANES * HEADS_PER_SLAB * m - 48) % LANES, axis=1)

    lane = lax.broadcasted_iota(jnp.int32, (WIN_STEPS, LANES), 1)
    lse_c = None
    pending = {0: scores(0), 1: scores(1)}
    for m in range(N_SLABS):
        if m + 2 < N_SLABS:
            pending[m + 2] = scores(m + 2)
        part = finish(m, pending.pop(m))
        lse_c = part if lse_c is None else jnp.where(lane >= LSE_LANES * HEADS_PER_SLAB * m, part, lse_c)
    l_ref[at + (slice(None),)] = lse_c


def _attn_kernel(q_ref, kvc_ref, kvp_ref, bias_ref, o_ref, l_ref, *, ns):
    d = q_ref.shape[1]
    first_tile = pl.program_id(1) == 0

    def body(idx, carry):
        rr = idx // ns
        j = idx % ns
        rc = pl.ds(pl.multiple_of(j * WIN_STEPS, WIN_STEPS), WIN_STEPS)
        q = q_ref[0, rr, rc, :] * ATT_SCALE
        kv_prev = kvp_ref[0, rr]
        if ns > 1:
            rp = pl.ds(pl.multiple_of(jnp.maximum(j - 1, 0) * WIN_STEPS, WIN_STEPS), WIN_STEPS)
            kv_prev = jnp.where(j == 0, kv_prev, kvc_ref[0, rr, rp, :])
        kv = jnp.concatenate([kv_prev, kvc_ref[0, rr, rc, :]], axis=0)
        table = jnp.where(first_tile & (j == 0), 1, 0)
        _attn_unit(q, kv, bias_ref, table, o_ref, l_ref, (0, rr, rc))
        return carry

    lax.fori_loop(0, d * ns, body, 0, unroll=4)


def _attn_call(q, kv, bias, name):
    B, d, U, _ = q.shape
    ns = ATT_TILE // (WIN_STEPS * d)
    rows = ns * WIN_STEPS
    blk = lambda width: pl.BlockSpec((1, d, rows, width), lambda b, t: (b, 0, t, 0))
    return pl.pallas_call(
        functools.partial(_attn_kernel, ns=ns),
        grid=(B, U // rows),
        in_specs=[blk(ATT_WIDTH), blk(2 * ATT_WIDTH),
                  pl.BlockSpec((1, d, WIN_STEPS, 2 * ATT_WIDTH),
                               lambda b, t: (b, 0, jnp.maximum(t * ns - 1, 0), 0)),
                  pl.BlockSpec(bias.shape, lambda b, t: (0, 0, 0, 0))],
        out_specs=[blk(ATT_WIDTH), blk(LANES)],
        out_shape=[jax.ShapeDtypeStruct((B, d, U, ATT_WIDTH), BF16),
                   jax.ShapeDtypeStruct((B, d, U, LANES), F32)],
        compiler_params=_params(("arbitrary", "arbitrary")),
        name=name,
    )(q, kv, kv, bias)


def _stride_bias(rel_table, g, d):
    bucket = _t5_bucket(np.arange(WIN_STEPS + 1) * d)
    onehot = jnp.asarray(np.eye(N_BUCKETS, dtype=np.float32)[bucket])
    tbl = rel_table[:, g * ATT_HEADS:(g + 1) * ATT_HEADS].astype(F32)
    return jnp.dot(onehot, tbl, precision=lax.Precision.HIGHEST).T


def _prompt_bias(rel_table, g, d):
    vals = _stride_bias(rel_table, g, d)
    n = WIN_STEPS
    period = 3 * n
    wp = jnp.concatenate([jnp.full((ATT_HEADS, n - 1), NEG, F32), vals[:, ::-1],
                          jnp.full((ATT_HEADS, n), NEG, F32)], axis=1)
    flat = jnp.tile(wp, (1, n))[:, :n * (period - 1)]
    bias = flat.reshape(ATT_HEADS, n, period - 1)[:, :, n - 1:n - 1 + 2 * n]
    first = jnp.concatenate([jnp.full((ATT_HEADS, n, n), NEG, F32), bias[:, :, n:]], axis=2)
    return jnp.stack([bias, first])


SHIFT_VREGS = 128


def _sattn_kernel(*refs, T):
    q_refs = refs[0:3]
    kvnew_refs = refs[3:6]
    cache_refs = refs[6:9]
    bc_refs = refs[9:12]
    bn_refs = refs[12:15]
    o_ref = refs[15]
    co_refs = refs[16:19]
    kvn_refs = refs[19:22]
    HT = ATT_HEADS * T
    dn = (((1,), (1,)), ((), ()))
    row_head = lax.broadcasted_iota(jnp.int32, (HT, ATT_WIDTH), 0) // T
    col_head = lax.broadcasted_iota(jnp.int32, (HT, ATT_WIDTH), 1) // ATT_HEAD_DIM
    head_mask = row_head == col_head

    for g in range(N_GROUPS):
        rows = jnp.concatenate([jnp.zeros((LANES - T, 2 * ATT_WIDTH), F32), kvnew_refs[g][0]], axis=0)
        kvn_refs[g][0] = rows.T

    stats = []
    for g in range(N_GROUPS):
        q = q_refs[g][0] * ATT_SCALE
        qexp = jnp.where(head_mask, jnp.concatenate([q] * ATT_HEADS, axis=0), 0.0).astype(BF16)
        kn = kvn_refs[g][0, :ATT_WIDTH, :].astype(BF16)
        kc = cache_refs[g][0, :ATT_WIDTH, :].astype(BF16)
        lc = jnp.dot(qexp, kc, preferred_element_type=F32) + bc_refs[g][...]
        ln = jnp.dot(qexp, kn, preferred_element_type=F32) + bn_refs[g][...]
        mx = jnp.maximum(jnp.max(lc, -1, keepdims=True), jnp.max(ln, -1, keepdims=True))
        pc = jnp.exp(lc - mx)
        pn = jnp.exp(ln - mx)
        ssum = jnp.sum(pc, -1, keepdims=True) + jnp.sum(pn, -1, keepdims=True)
        stats.append((pc, pn, ssum, mx + jnp.log(ssum)))

    lse_max = jnp.maximum(jnp.maximum(stats[0][3], stats[1][3]), stats[2][3])
    es = [jnp.exp(st[3] - lse_max) for st in stats]
    esum = es[0] + es[1] + es[2]
    acc = jnp.zeros((HT, ATT_WIDTH), F32)
    for g in range(N_GROUPS):
        pc, pn, ssum, _ = stats[g]
        w = es[g] / (esum * ssum)
        vc = cache_refs[g][0, ATT_WIDTH:, :].astype(BF16)
        vn = kvn_refs[g][0, ATT_WIDTH:, :].astype(BF16)
        acc = acc + lax.dot_general((pc * w).astype(BF16), vc, dn, preferred_element_type=F32)
        acc = acc + lax.dot_general((pn * w).astype(BF16), vn, dn, preferred_element_type=F32)
    lane_head = lax.broadcasted_iota(jnp.int32, (T, ATT_WIDTH), 1) // ATT_HEAD_DIM
    o = jnp.zeros((T, ATT_WIDTH), F32)
    for h in range(ATT_HEADS):
        o = o + jnp.where(lane_head == h, acc[h * T:(h + 1) * T, :], 0.0)
    o_ref[0] = o

    for g in range(N_GROUPS):
        L = cache_refs[g].shape[2]
        nrows = cache_refs[g].shape[1]
        chunk = min(nrows, SHIFT_VREGS * SUBLANES * LANES // L)
        is_new = lax.broadcasted_iota(jnp.int32, (chunk, LANES), 1) >= LANES - T

        def shift_rows(i, carry, g=g, L=L, chunk=chunk, is_new=is_new):
            rs = pl.ds(pl.multiple_of(i * chunk, chunk), chunk)
            rolled = pltpu.roll(cache_refs[g][0, rs, :], L - T, axis=1)
            if L > LANES:
                co_refs[g][0, rs, 0:L - LANES] = rolled[:, 0:L - LANES]
            co_refs[g][0, rs, L - LANES:L] = jnp.where(is_new, kvn_refs[g][0, rs, :], rolled[:, L - LANES:])
            return carry

        lax.fori_loop(0, nrows // chunk, shift_rows, 0)


def _sample_bias(rel_table, g, W, d, Lb, T):
    vals = _stride_bias(rel_table, g, d)
    n = Lb + T
    by_dist = jnp.concatenate([vals[:, :, None], jnp.full((ATT_HEADS, WIN_STEPS + 1, d - 1), NEG, F32)],
                              axis=2).reshape(ATT_HEADS, (WIN_STEPS + 1) * d)
    if by_dist.shape[1] < n:
        by_dist = jnp.concatenate([by_dist, jnp.full((ATT_HEADS, n - by_dist.shape[1]), NEG, F32)], axis=1)
    rev = jnp.concatenate([by_dist[:, :n][:, ::-1], jnp.full((ATT_HEADS, T), NEG, F32)], axis=1)
    rows = jnp.stack([rev[:, T - 1 - t:T - 1 - t + n] for t in range(T)], axis=1)
    rows = rows.reshape(ATT_HEADS * T, n)
    bc = rows[:, :Lb]
    bn = jnp.concatenate([jnp.full((ATT_HEADS * T, LANES - T), NEG, F32), rows[:, Lb:]], axis=1)
    return bc, bn


def _sattn_call(qs, kvns, caches, bcs, bns, T):
    DB = caches[0].shape[0]
    rows = caches[0].shape[1]
    per_seq = lambda b: (b, 0, 0)
    const2 = lambda b: (0, 0)
    in_specs = ([pl.BlockSpec((1, T, ATT_WIDTH), per_seq)] * 3
                + [pl.BlockSpec((1,) + x.shape[1:], per_seq) for x in kvns]
                + [pl.BlockSpec((1,) + c.shape[1:], per_seq) for c in caches]
                + [pl.BlockSpec(x.shape, const2) for x in bcs]
                + [pl.BlockSpec(x.shape, const2) for x in bns])
    out_specs = ([pl.BlockSpec((1, T, ATT_WIDTH), per_seq)]
                 + [pl.BlockSpec((1,) + c.shape[1:], per_seq) for c in caches])
    out_shape = ([jax.ShapeDtypeStruct((DB, T, ATT_WIDTH), F32)]
                 + [jax.ShapeDtypeStruct(c.shape, F32) for c in caches])
    return pl.pallas_call(
        functools.partial(_sattn_kernel, T=T),
        grid=(DB,),
        in_specs=in_specs, out_specs=out_specs, out_shape=out_shape,
        scratch_shapes=[pltpu.VMEM((1, rows, LANES), F32)] * N_GROUPS,
        compiler_params=_params(("arbitrary",)),
        name="sample_attn",
    )(*qs, *kvns, *caches, *bcs, *bns)


def _scan_rows(x, op, fill):
    n = x.shape[0]
    rowid = lax.broadcasted_iota(jnp.int32, x.shape, 0)
    s = 1
    while s < n:
        shifted = pltpu.roll(x, s, axis=0)
        x = op(x, jnp.where(rowid >= s, shifted, fill))
        s *= 2
    return x


def _pad_rows(x, n, fill=0.0):
    if x.shape[0] == n:
        return x
    return jnp.concatenate([x, jnp.full((n - x.shape[0],) + x.shape[1:], fill, x.dtype)], axis=0)


def _mlstm_kernel(xm_ref, zm_ref, om_ref, g_ref, cprev_ref, C0_ref, n0_ref, m0_ref,
                  convw_ref, convb_ref, wq_ref, wk_ref, mnorm_ref, mskip_ref, bif_ref,
                  mo_ref, C_ref, n_ref, m_ref, xc_ref, *, L):
    LS = max(L, LANES)
    G = xm_ref.shape[0]
    DV = C_ref.shape[2]
    DK = C_ref.shape[3]
    c = pl.program_id(1)

    @pl.when(c == 0)
    def _():
        xc_ref[...] = cprev_ref[...]
        C_ref[...] = C0_ref[...]
        n_ref[...] = n0_ref[...]
        m_ref[...] = m0_ref[...]

    t_id = lax.broadcasted_iota(jnp.int32, (L, LS), 0)
    s_id = lax.broadcasted_iota(jnp.int32, (L, LS), 1)
    causal = s_id <= t_id
    eye = s_id == t_id
    dn_t = (((1,), (1,)), ((), ()))

    seqs = []
    for g in range(G):
        xm_b = xm_ref[g].astype(BF16)
        xm_f = xm_ref[g].astype(F32)
        hist = jnp.concatenate([xc_ref[g], xm_f], axis=0)
        conv = convb_ref[...] + convw_ref[CONV_WIDTH - 1:CONV_WIDTH, :] * xm_f
        for k in range(1, CONV_WIDTH):
            conv = conv + (convw_ref[CONV_WIDTH - 1 - k:CONV_WIDTH - k, :]
                           * pltpu.roll(hist, k, axis=0)[SUBLANES:, :])
        xc_ref[g] = xm_f[L - SUBLANES:, :]
        c_act = _silu(conv)

        i_pre = g_ref[g] + bif_ref[...]
        logf = jax.nn.log_sigmoid(pltpu.roll(i_pre, LANES - M_HEADS, axis=1))
        b = _scan_rows(logf, jnp.add, 0.0)
        a = i_pre - b
        ca = _scan_rows(a, jnp.maximum, NEG)
        m_prev = m_ref[g]
        mm = jnp.maximum(ca, m_prev)
        u = -mm
        bL = b[L - 1:L, :]
        m_new = bL + jnp.maximum(m_prev, ca[L - 1:L, :])
        m_ref[g] = m_new
        seqs.append(dict(xm_b=xm_b, c_act=c_act, cb=c_act.astype(BF16), a=a, u=u,
                         w_inter=jnp.exp(u + m_prev), emt=jnp.exp(-(b + mm)),
                         wk=jnp.exp(bL + a - m_new), wC=jnp.exp(bL + m_prev - m_new)))

    units = [(g, h) for g in range(G) for h in range(M_HEADS)]
    vsl = [slice(h * DV, (h + 1) * DV) for h in range(M_HEADS)]
    qs, ks, kws, vss, C_old, n_old = {}, {}, {}, {}, {}, {}
    for g, h in units:
        ch = seqs[g]["cb"][:, vsl[h]]
        qs[g, h] = (jnp.dot(ch, wq_ref[h], preferred_element_type=F32) * (DK ** -0.5)).astype(BF16)
        k32 = jnp.dot(ch, wk_ref[h], preferred_element_type=F32)
        ks[g, h] = _pad_rows(k32.astype(BF16), LS)
        kws[g, h] = _pad_rows(k32 * seqs[g]["wk"][:, h:h + 1], LS)
        vss[g, h] = _pad_rows(seqs[g]["xm_b"][:, vsl[h]], LS)
        C_old[g, h] = C_ref[g, h]
        n_old[g, h] = n_ref[g, h:h + 1, :]
    scs, inters = {}, {}
    for g, h in units:
        a_row = jnp.sum(jnp.where(eye, seqs[g]["a"][:, h:h + 1], 0.0), axis=0, keepdims=True)
        w_intra = jnp.exp(jnp.where(causal, seqs[g]["u"][:, h:h + 1] + a_row, NEG))
        scs[g, h] = lax.dot_general(qs[g, h], ks[g, h], dn_t, preferred_element_type=F32) * w_intra
        inters[g, h] = lax.dot_general(qs[g, h], C_old[g, h].astype(BF16), dn_t, preferred_element_type=F32)
    for g, h in units:
        wi = seqs[g]["w_inter"][:, h:h + 1]
        num = jnp.dot(scs[g, h].astype(BF16), vss[g, h], preferred_element_type=F32) + wi * inters[g, h]
        den = (jnp.sum(scs[g, h], -1, keepdims=True)
               + wi * jnp.sum(qs[g, h].astype(F32) * n_old[g, h], -1, keepdims=True))
        hcell = num / jnp.maximum(jnp.abs(den), seqs[g]["emt"][:, h:h + 1])
        mu = jnp.mean(hcell, -1, keepdims=True)
        hc = hcell - mu
        hn = hc * lax.rsqrt(jnp.mean(hc * hc, -1, keepdims=True) + EPS)
        out = ((om_ref[g, :, vsl[h]].astype(F32) * (hn * mnorm_ref[:, vsl[h]])
                + mskip_ref[:, vsl[h]] * seqs[g]["c_act"][:, vsl[h]]) * zm_ref[g, :, vsl[h]].astype(F32))
        mo_ref[g, :, vsl[h]] = out.astype(mo_ref.dtype)
    for g, h in units:
        wc = seqs[g]["wC"][:, h:h + 1]
        C_ref[g, h] = wc * C_old[g, h] + lax.dot_general(vss[g, h], kws[g, h].astype(BF16),
                                                         (((0,), (0,)), ((), ())),
                                                         preferred_element_type=F32)
        n_ref[g, h:h + 1, :] = wc * n_old[g, h] + jnp.sum(kws[g, h], axis=0, keepdims=True)


def _mlstm_call(xm, zm, om, gates, conv_prev, C0, n0, m0, convw, convb, wq, wk, mnorm, mskip, bif,
                *, L, G, out_dtype, name):
    N, S, M = xm.shape
    H, DV, DK = C0.shape[1:]
    seq = lambda b, c: (b, c, 0)
    per_b3 = lambda b, c: (b, 0, 0)
    per_b4 = lambda b, c: (b, 0, 0, 0)
    const2 = lambda b, c: (0, 0)
    const3 = lambda b, c: (0, 0, 0)
    in_specs = [pl.BlockSpec((G, L, M), seq), pl.BlockSpec((G, L, M), seq), pl.BlockSpec((G, L, M), seq),
                pl.BlockSpec((G, L, LANES), seq),
                pl.BlockSpec((G, SUBLANES, M), per_b3),
                pl.BlockSpec((G, H, DV, DK), per_b4),
                pl.BlockSpec((G, H, DK), per_b3),
                pl.BlockSpec((G, 1, LANES), per_b3),
                pl.BlockSpec(convw.shape, const2), pl.BlockSpec(convb.shape, const2),
                pl.BlockSpec(wq.shape, const3), pl.BlockSpec(wk.shape, const3),
                pl.BlockSpec(mnorm.shape, const2), pl.BlockSpec(mskip.shape, const2),
                pl.BlockSpec(bif.shape, const2)]
    out_specs = [pl.BlockSpec((G, L, M), seq),
                 pl.BlockSpec((G, H, DV, DK), per_b4),
                 pl.BlockSpec((G, H, DK), per_b3),
                 pl.BlockSpec((G, 1, LANES), per_b3)]
    out_shape = [jax.ShapeDtypeStruct((N, S, M), out_dtype),
                 jax.ShapeDtypeStruct((N, H, DV, DK), F32),
                 jax.ShapeDtypeStruct((N, H, DK), F32),
                 jax.ShapeDtypeStruct((N, 1, LANES), F32)]
    return pl.pallas_call(
        functools.partial(_mlstm_kernel, L=L),
        grid=(N // G, S // L),
        in_specs=in_specs, out_specs=out_specs, out_shape=out_shape,
        scratch_shapes=[pltpu.VMEM((G, SUBLANES, M), F32)],
        compiler_params=_params(("arbitrary", "arbitrary")),
        name=name,
    )(xm, zm, om, gates, conv_prev, C0, n0, m0, convw, convb, wq, wk, mnorm, mskip, bif)


def _token_order(src_ref, scr_ref):
    _, d, n, width = src_ref.shape
    if d == 1:
        return src_ref[0, 0].astype(F32)
    n_tiles = width // LANES
    for r in range(d):
        for s in range(n_tiles):
            scr_ref[s, pl.ds(r, n, stride=d), :] = src_ref[0, r, :, s * LANES:(s + 1) * LANES].astype(F32)
    return jnp.concatenate([scr_ref[s] for s in range(n_tiles)], axis=1)


def _post_kernel(*refs, merge):
    if merge:
        (x_ref, gate_ref, o0, o1, o2, l0, l1, l2, expand_ref, za_ref, mo_ref, ga_ref, gm_ref,
         wpa_ref, wpm_ref, wout_ref, fg_ref, y_ref, scr_ref) = refs
        ls = [_token_order(l, scr_ref) for l in (l0, l1, l2)]
        lmax = jnp.maximum(jnp.maximum(ls[0], ls[1]), ls[2])
        es = [jnp.exp(l - lmax) for l in ls]
        inv = 1.0 / (es[0] + es[1] + es[2])
        o_att = None
        for e, o in zip(es, (o0, o1, o2)):
            a = e * inv
            hi = a.astype(BF16)
            lo = (a - hi.astype(F32)).astype(BF16)
            a_wide = jnp.dot(jnp.concatenate([hi, lo], axis=1), expand_ref[...],
                             preferred_element_type=F32)
            term = a_wide * _token_order(o, scr_ref)
            o_att = term if o_att is None else o_att + term
    else:
        (x_ref, gate_ref, oa_ref, za_ref, mo_ref, ga_ref, gm_ref,
         wpa_ref, wpm_ref, wout_ref, fg_ref, y_ref) = refs
        o_att = oa_ref[0]
    a_in = (o_att * za_ref[0].astype(F32)).astype(BF16)
    a_br = jnp.dot(a_in, wpa_ref[...], preferred_element_type=F32)
    m_br = jnp.dot(mo_ref[0].astype(BF16), wpm_ref[...], preferred_element_type=F32)
    merged = ga_ref[0].astype(F32) * a_br + gm_ref[0].astype(F32) * m_br
    y = x_ref[0] + gate_ref[0] * jnp.dot(merged.astype(BF16), wout_ref[...], preferred_element_type=F32)
    ms = jnp.mean(y * y, axis=-1, keepdims=True)
    y_ref[0] = y * lax.rsqrt(ms + EPS) * fg_ref[...]


def _post_call(x3, gate3, att_inputs, za, mo, ga, gm, wpa, wpm, wout, fgain, *, tm, merge, name):
    B, S, D = x3.shape
    row = lambda b, i: (b, i, 0)
    const2 = lambda b, i: (0, 0)
    if gate3.shape[1] == 1:
        gate_spec = pl.BlockSpec((1, 1, D), lambda b, i: (b, 0, 0))
    else:
        gate_spec = pl.BlockSpec((1, tm, D), row)
    def blk(a):
        if a.ndim == 2:
            return pl.BlockSpec(a.shape, const2)
        if a.ndim == 4:
            d = a.shape[1]
            return pl.BlockSpec((1, d, tm // d, a.shape[3]), lambda b, i: (b, 0, i, 0))
        return pl.BlockSpec((1, tm, a.shape[2]), row)

    in_specs = ([pl.BlockSpec((1, tm, D), row), gate_spec]
                + [blk(a) for a in att_inputs]
                + [blk(za), blk(mo), blk(ga), blk(gm),
                   pl.BlockSpec(wpa.shape, const2), pl.BlockSpec(wpm.shape, const2),
                   pl.BlockSpec(wout.shape, const2), pl.BlockSpec(fgain.shape, const2)])
    scratch = [pltpu.VMEM((ATT_WIDTH // LANES, tm, LANES), F32)] if merge else []
    return pl.pallas_call(
        functools.partial(_post_kernel, merge=merge),
        grid=(B, S // tm),
        in_specs=in_specs,
        out_specs=pl.BlockSpec((1, tm, D), row),
        out_shape=jax.ShapeDtypeStruct((B, S, D), F32),
        scratch_shapes=scratch,
        compiler_params=_params(("arbitrary", "arbitrary")),
        name=name,
    )(x3, gate3, *att_inputs, za, mo, ga, gm, wpa, wpm, wout, fgain)


def _cast_kernel(x_ref, o_ref):
    o_ref[...] = x_ref[...].T.astype(o_ref.dtype)


def _cast_bf16_call(wt, ncols):
    k = wt.shape[1]
    tn = 1024
    return pl.pallas_call(
        _cast_kernel,
        grid=(ncols // tn,),
        in_specs=[pl.BlockSpec((tn, k), lambda j: (j, 0))],
        out_specs=pl.BlockSpec((k, tn), lambda j: (0, j)),
        out_shape=jax.ShapeDtypeStruct((k, ncols), BF16),
        compiler_params=_params(("arbitrary",)),
        name="cast_weights",
    )(wt)


def _tail_cast_kernel(a_ref, b_ref, c_ref, o_ref, *, n_gate, n_merge):
    rows = jnp.concatenate([a_ref[...], b_ref[...], c_ref[0:SUBLANES, :]], axis=0)
    gates = jnp.concatenate([rows[0:n_gate], jnp.zeros((LANES - n_gate, rows.shape[1]), F32)], axis=0)
    out = jnp.concatenate([rows[n_gate:n_gate + n_merge], gates], axis=0)
    o_ref[...] = out.T.astype(o_ref.dtype)


def _tail_cast_call(wt, first, n_gate, n_merge):
    k = wt.shape[1]
    tn = n_merge // 2
    j0 = first // tn
    assert first % tn == 0 and n_gate == SUBLANES
    return pl.pallas_call(
        functools.partial(_tail_cast_kernel, n_gate=n_gate, n_merge=n_merge),
        grid=(1,),
        in_specs=[pl.BlockSpec((tn, k), lambda j: (j0, 0), pipeline_mode=pl.Buffered(1)),
                  pl.BlockSpec((tn, k), lambda j: (j0 + 1, 0), pipeline_mode=pl.Buffered(1)),
                  pl.BlockSpec((tn, k), lambda j: (j0 + 2, 0), pipeline_mode=pl.Buffered(1))],
        out_specs=pl.BlockSpec((k, n_merge + LANES), lambda j: (0, 0)),
        out_shape=jax.ShapeDtypeStruct((k, n_merge + LANES), BF16),
        compiler_params=_params(("arbitrary",)),
        name="cast_tail_weights",
    )(wt, wt, wt)


def _projection_weights(w_in):
    D = w_in.shape[0]
    AW = ATT_WIDTH
    M = D
    off_k, off_v, off_za = 3 * AW, 6 * AW, 9 * AW
    off_xm = off_za + AW
    off_i = off_xm + 3 * M
    wt = jnp.transpose(w_in)
    w_main = _cast_bf16_call(wt, off_i)
    w_tail = _tail_cast_call(wt, off_i, 2 * M_HEADS, 2 * D)
    segs = {"za": ((0, off_za, AW),), "xm": ((0, off_xm, M),), "zm": ((0, off_xm + M, M),),
            "om": ((0, off_xm + 2 * M, M),), "ga": ((1, 0, D),), "gm": ((1, D, D),),
            "gates": ((1, 2 * D, LANES),)}
    for g in range(N_GROUPS):
        segs[f"q{g}"] = ((0, g * AW, AW),)
        segs[f"kv{g}"] = ((0, off_k + g * AW, AW), (0, off_v + g * AW, AW))
    return (w_main, w_tail), segs


def kernel(x_prompt, x_sample, cache_kv_w128, cache_kv_w512, cache_kv_w2048, state_conv, state_C, state_n, state_m, c_prompt, c_sample, rel_table, norm_gain, w_ada, b_ada, w_in, b_if, conv_w, conv_b, w_mq, w_mk, m_norm, m_skip, w_pa, w_pm, w_out, final_gain):
    B, S, D = x_prompt.shape
    DB, T, _ = x_sample.shape
    assert norm_gain.shape[0] == 1, "single-layer trunk"
    assert S % ATT_TILE == 0 and S % MLSTM_CHUNK == 0 and T == SUBLANES
    caches = (cache_kv_w128[0], cache_kv_w512[0], cache_kv_w2048[0])
    H = M_HEADS
    M = conv_w.shape[2]

    wp, segs = _projection_weights(w_in[0])
    names = ("q0", "q1", "q2", "za", "kv0", "kv1", "kv2", "xm", "zm", "om", "ga", "gm", "gates")
    seg_list = [segs[n] for n in names]
    gain = norm_gain[0].reshape(1, D)
    fgain = final_gain.reshape(1, D)
    wpa, wpm, wout = w_pa[0].astype(BF16), w_pm[0].astype(BF16), w_out[0].astype(BF16)
    wq, wk = w_mq[0].astype(BF16), w_mk[0].astype(BF16)
    convw, convb = conv_w[0], conv_b[0].reshape(1, M)
    mnorm, mskip = m_norm[0].reshape(1, M), m_skip[0].reshape(1, M)
    bif = jnp.concatenate([b_if[0], jnp.zeros((LANES - 2 * H,), F32)]).reshape(1, LANES)

    ada = _ada_call(jnp.concatenate([c_prompt, c_sample], axis=0), w_ada[0], b_ada[0])
    shift, scale, gate = ada[:, :D], ada[:, D:2 * D], ada[:, 2 * D:]

    gate_act = {"za": "silu", "zm": "silu", "om": "sigmoid", "ga": "sigmoid", "gm": "sigmoid"}
    acts = [gate_act.get(n) for n in names]

    R = DB * T
    rep = lambda t: jnp.repeat(t[B:], T, axis=0).reshape(1, R, D)
    s_shift, s_scale, s_gate = rep(shift), rep(scale), rep(gate)
    xs = x_sample.reshape(1, R, D)
    sr = dict(zip(names, _proj_call(xs, gain, s_scale, s_shift, wp, seg_list, [F32] * 13,
                                    tm=R, row0=0, rows=R, name="proj_sample", acts=acts)))
    bcs, bns = [], []
    for g, (win, dil) in enumerate(ATT_GROUPS):
        bc, bn = _sample_bias(rel_table, g, win, dil, caches[g].shape[1], T)
        bcs.append(bc)
        bns.append(bn)
    cache_t = [jnp.transpose(c, (0, 2, 3, 4, 1)).reshape(DB, 2 * ATT_WIDTH, c.shape[1]) for c in caches]
    kvn_t = [sr[f"kv{g}"].reshape(DB, T, 2 * ATT_WIDTH) for g in range(3)]
    sattn_args = ([sr[f"q{g}"].reshape(DB, T, ATT_WIDTH) for g in range(3)], kvn_t, cache_t, bcs, bns)

    p_shift, p_scale, p_gate = (t[:B].reshape(B, 1, D) for t in (shift, scale, gate))
    dts = [F32 if n == "gates" else BF16 for n in names]
    group_dil = {f"{p}{g}": dil for g, (_, dil) in enumerate(ATT_GROUPS) for p in ("q", "kv")}
    pr = dict(zip(names, _proj_call(x_prompt, gain, p_scale, p_shift, wp, seg_list, dts,
                                    tm=PROJ_TM, row0=0, rows=S, name="proj_prompt",
                                    dils=[group_dil.get(n, 1) for n in names], acts=acts)))
    att = []
    for g, (win, dil) in enumerate(ATT_GROUPS):
        planes = lambda a: a if a.ndim == 4 else a[:, None]
        att.append(_attn_call(planes(pr[f"q{g}"]), planes(pr[f"kv{g}"]), _prompt_bias(rel_table, g, dil),
                              f"attn_prompt_g{g}"))
    expand = np.zeros((2 * LANES, ATT_WIDTH), np.float32)
    for h in range(ATT_HEADS):
        expand[[LSE_LANES * h, LANES + LSE_LANES * h], h * ATT_HEAD_DIM:(h + 1) * ATT_HEAD_DIM] = 1.0
    expand = jnp.asarray(expand, BF16)
    mlstm_p_args = (pr["xm"], pr["zm"], pr["om"], pr["gates"],
                    jnp.zeros((B, SUBLANES, M), F32), jnp.zeros((B,) + state_C.shape[2:], F32),
                    jnp.zeros((B,) + state_n.shape[2:], F32), jnp.zeros((B, 1, LANES), F32),
                    convw, convb, wq, wk, mnorm, mskip, bif)
    mo_p, C_p, n_p, m_p = _mlstm_call(*mlstm_p_args, L=MLSTM_CHUNK, G=_largest_divisor(B, MLSTM_SEQS_PROMPT),
                                      out_dtype=BF16, name="mlstm_prompt")
    sa = _sattn_call(*sattn_args, T)
    y_prompt = _post_call(x_prompt, p_gate, [a[0] for a in att] + [a[1] for a in att] + [expand],
                          pr["za"], mo_p, pr["ga"], pr["gm"], wpa, wpm, wout, fgain,
                          tm=POST_TM, merge=True, name="post_prompt")
    def tail_weights(keys):
        cols, tsegs, c = [], [], 0
        for key in keys:
            pieces = []
            for wi, c0, wd in segs[key]:
                cols.append(wp[wi][:, c0:c0 + wd])
                pieces.append((0, c, wd))
                c += wd
            tsegs.append(tuple(pieces))
        return (jnp.concatenate(cols, axis=1),), tsegs

    w_max = min(ATT_GROUPS[-1][0], S)
    w2, segs2 = tail_weights(["kv2"])
    (kv2_t,) = _proj_call(x_prompt, gain, p_scale, p_shift, w2, segs2, [F32],
                          tm=min(w_max, 2 * PROJ_TM), row0=S - w_max, rows=w_max, name="tail_kv2", dils=[0])
    w_mid = min(ATT_GROUPS[1][0], S)
    w01, segs01 = tail_weights(["kv0", "kv1", "xm"])
    kv0_t, kv1_t, xm_t = _proj_call(x_prompt, gain, p_scale, p_shift, w01, segs01, [F32] * 3,
                                    tm=w_mid, row0=S - w_mid, rows=w_mid, name="tail_kv01", dils=[0, 0, 1])
    w0 = min(ATT_GROUPS[0][0], S)
    as_buffer = lambda t: jnp.transpose(t.reshape(B, 2, ATT_HEADS, ATT_HEAD_DIM, t.shape[2]), (0, 4, 1, 2, 3))[None]
    kv_p = [as_buffer(kv0_t[:, :, w_mid - w0:]), as_buffer(kv1_t), as_buffer(kv2_t)]
    conv_p = xm_t[:, w_mid - (CONV_WIDTH - 1):][None]

    o_att_s = sa[0].reshape(1, R, ATT_WIDTH)
    kv_s = [jnp.transpose(c.reshape(DB, 2, ATT_HEADS, ATT_HEAD_DIM, c.shape[2]), (0, 4, 1, 2, 3))[None]
            for c in sa[1:]]
    conv_prev_s = jnp.concatenate([jnp.zeros((DB, SUBLANES - (CONV_WIDTH - 1), M), F32), state_conv[0]], axis=1)
    m0_s = jnp.concatenate([state_m[0], jnp.zeros((DB, LANES - H), F32)], axis=1).reshape(DB, 1, LANES)
    seqv = lambda t: t.reshape(DB, T, t.shape[-1])
    mo_s, C_s, n_s, m_s = _mlstm_call(
        seqv(sr["xm"]), seqv(sr["zm"]), seqv(sr["om"]), seqv(sr["gates"]),
        conv_prev_s, state_C[0], state_n[0], m0_s,
        convw, convb, wq, wk, mnorm, mskip, bif, L=T, G=_largest_divisor(DB, MLSTM_SEQS_SAMPLE),
        out_dtype=F32, name="mlstm_sample")
    y_sample = _post_call(xs, s_gate, [o_att_s], sr["za"], mo_s.reshape(1, R, M), sr["ga"], sr["gm"],
                          wpa, wpm, wout, fgain, tm=R, merge=False, name="post_sample")
    conv_s = seqv(sr["xm"])[:, T - (CONV_WIDTH - 1):][None]

    return (y_prompt, y_sample.reshape(DB, T, D),
            kv_p[0], kv_s[0], kv_p[1], kv_s[1], kv_p[2], kv_s[2],
            conv_p, conv_s, C_p[None], C_s[None], n_p[None], n_s[None],
            m_p[:, 0, :H][None], m_s[:, 0, :H][None])
```

```python
import functools

import numpy as np
import jax
import jax.numpy as jnp
from jax import lax
from jax.experimental import pallas as pl
from jax.experimental.pallas import tpu as pltpu

F32 = jnp.float32
BF16 = jnp.bfloat16

ATT_GROUPS = ((128, 1), (512, 4), (2048, 16))
N_GROUPS = len(ATT_GROUPS)
ATT_HEADS = 8
ATT_HEAD_DIM = 64
ATT_WIDTH = ATT_HEADS * ATT_HEAD_DIM
WIN_STEPS = 128
ATT_SCALE = ATT_HEAD_DIM ** -0.5
N_BUCKETS = 32
MAX_DISTANCE = 2048
M_HEADS = 4
CONV_WIDTH = 4
EPS = 1e-6
NEG = -1e30

LANES = 128
SUBLANES = 8
VMEM_LIMIT = 56 * 1024 * 1024

ATT_TILE = 2048
PROJ_TM = 512
POST_TM = 1024
MLSTM_CHUNK = 256
MLSTM_SEQS_PROMPT = 2
MLSTM_SEQS_SAMPLE = 4


def _largest_divisor(n, cap):
    return max(g for g in range(1, cap + 1) if n % g == 0)


def _params(sem, vmem=VMEM_LIMIT):
    return pltpu.CompilerParams(dimension_semantics=sem, vmem_limit_bytes=vmem)


def _t5_bucket(dist):
    n = np.asarray(dist).astype(np.int64)
    max_exact = N_BUCKETS // 2
    nf = np.maximum(n, 1).astype(np.float32)
    large = max_exact + (np.log(nf / max_exact) / np.log(np.float32(MAX_DISTANCE / max_exact))
                         * (N_BUCKETS - max_exact)).astype(np.int64)
    large = np.minimum(large, N_BUCKETS - 1)
    return np.where(n < max_exact, n, large).astype(np.int32)


def _sigmoid_tanh(v):
    return 0.5 * jnp.tanh(0.5 * v) + 0.5


def _silu(x):
    return x * _sigmoid_tanh(x)


def _ada_kernel(c_ref, w_ref, b_ref, o_ref):
    s = _silu(c_ref[...])
    o_ref[...] = jnp.dot(s, w_ref[...], preferred_element_type=F32,
                         precision=lax.Precision.HIGHEST) + b_ref[...]


def _ada_call(c, w, b):
    n, d = c.shape
    width = w.shape[1]
    tn = width // 2 if width % (2 * LANES) == 0 else width
    return pl.pallas_call(
        _ada_kernel,
        grid=(width // tn,),
        in_specs=[pl.BlockSpec((n, d), lambda j: (0, 0)),
                  pl.BlockSpec((d, tn), lambda j: (0, j)),
                  pl.BlockSpec((1, tn), lambda j: (0, j))],
        out_specs=pl.BlockSpec((n, tn), lambda j: (0, j)),
        out_shape=jax.ShapeDtypeStruct((n, width), F32),
        compiler_params=_params(("arbitrary",)),
        name="ada",
    )(c, w, b.reshape(1, width))


_ACTIVATIONS = {None: lambda v: v, "silu": _silu, "sigmoid": _sigmoid_tanh}


def _proj_kernel(x_ref, gain_ref, scale_ref, shift_ref, *rest, n_w, segs, dils, acts):
    w_refs = rest[:n_w]
    rest = rest[n_w:]
    out_refs = rest[:len(segs)]
    x = x_ref[0]
    tm, D = x.shape
    ms = jnp.mean(x * x, axis=-1, keepdims=True)
    h = x * lax.rsqrt(ms + EPS) * gain_ref[...] * (1.0 + scale_ref[0]) + shift_ref[0]
    lhs = {1: h.astype(BF16)}
    lhs[0] = lhs[1]
    strides = sorted(set(dils) - {0, 1})
    if strides:
        hs_ref = rest[len(segs)]
        n_tiles = D // LANES
        for s in range(n_tiles):
            hs_ref[s] = h[:, s * LANES:(s + 1) * LANES]
        for d in strides:
            n = tm // d
            lhs[d] = jnp.concatenate(
                [jnp.concatenate([hs_ref[s, pl.ds(r, n, stride=d), :] for r in range(d)], axis=0)
                 for s in range(n_tiles)], axis=1).astype(BF16)
    for o_ref, pieces, d, act in zip(out_refs, segs, dils, acts):
        parts = [_ACTIVATIONS[act](jnp.dot(lhs[d], w_refs[wi][:, c0:c0 + width],
                                           preferred_element_type=F32)).astype(o_ref.dtype)
                 for wi, c0, width in pieces]
        res = parts[0] if len(parts) == 1 else jnp.concatenate(parts, axis=1)
        if d == 0:
            o_ref[0] = res.T
        elif d == 1:
            o_ref[0] = res
        else:
            n = tm // d
            for r in range(d):
                o_ref[0, r] = res[r * n:(r + 1) * n, :]


def _proj_call(x3, gain, scale3, shift3, ws, segs, dtypes, *, tm, row0, rows, name, dils=None, acts=None):
    B, S, D = x3.shape
    nrb = rows // tm
    rb0 = row0 // tm
    dils = tuple(dils) if dils is not None else (1,) * len(segs)
    acts = tuple(acts) if acts is not None else (None,) * len(segs)
    per_row = scale3.shape[1] != 1
    if per_row:
        mod_spec = pl.BlockSpec((1, tm, D), lambda b, i: (b, rb0 + i, 0))
    else:
        mod_spec = pl.BlockSpec((1, 1, D), lambda b, i: (b, 0, 0))
    out_shape, out_specs = [], []
    for pieces, dt, d in zip(segs, dtypes, dils):
        wd = sum(p[2] for p in pieces)
        if d == 0:
            out_shape.append(jax.ShapeDtypeStruct((B, wd, rows), dt))
            out_specs.append(pl.BlockSpec((1, wd, tm), lambda b, i: (b, 0, i)))
        elif d == 1:
            out_shape.append(jax.ShapeDtypeStruct((B, rows, wd), dt))
            out_specs.append(pl.BlockSpec((1, tm, wd), lambda b, i: (b, i, 0)))
        else:
            out_shape.append(jax.ShapeDtypeStruct((B, d, rows // d, wd), dt))
            out_specs.append(pl.BlockSpec((1, d, tm // d, wd), lambda b, i: (b, 0, i, 0)))
    scratch = [pltpu.VMEM((D // LANES, tm, LANES), F32)] if any(d > 1 for d in dils) else []
    return pl.pallas_call(
        functools.partial(_proj_kernel, n_w=len(ws), segs=tuple(segs), dils=dils, acts=acts),
        grid=(B, nrb),
        in_specs=[pl.BlockSpec((1, tm, D), lambda b, i: (b, rb0 + i, 0)),
                  pl.BlockSpec((1, D), lambda b, i: (0, 0)),
                  mod_spec, mod_spec]
                 + [pl.BlockSpec(w.shape, lambda b, i: (0, 0), pipeline_mode=pl.Buffered(1)) for w in ws],
        out_specs=out_specs,
        out_shape=out_shape,
        scratch_shapes=scratch,
        compiler_params=_params(("arbitrary", "arbitrary")),
        name=name,
    )(x3, gain, scale3, shift3, *ws)


HEADS_PER_SLAB = LANES // ATT_HEAD_DIM
N_SLABS = ATT_HEADS // HEADS_PER_SLAB
LSE_LANES = LANES // ATT_HEADS


def _attn_unit(q, kv, bias_ref, table, o_ref, l_ref, at):
    dn = (((1,), (1,)), ((), ()))
    nk = 2 * WIN_STEPS
    lane_q = lax.broadcasted_iota(jnp.int32, (WIN_STEPS, LANES), 1) < ATT_HEAD_DIM
    lane_k = lax.broadcasted_iota(jnp.int32, (nk, LANES), 1) < ATT_HEAD_DIM
    ones_lo = jnp.where(lane_k, 1.0, 0.0).astype(BF16)
    ones_hi = jnp.where(lane_k, 0.0, 1.0).astype(BF16)
    zero_q = jnp.zeros((WIN_STEPS, LANES), BF16)
    zero_k = jnp.zeros((nk, LANES), BF16)

    def scores(m):
        cs = slice(m * LANES, (m + 1) * LANES)
        qs = q[:, cs]
        ks = kv[:, cs]
        out = []
        for hh in range(HEADS_PER_SLAB):
            qm = jnp.where(lane_q, qs, zero_q) if hh == 0 else jnp.where(lane_q, zero_q, qs)
            out.append(lax.dot_general(qm, ks, dn, preferred_element_type=F32)
                       + bias_ref[table, m * HEADS_PER_SLAB + hh])
        return out

    def finish(m, ss):
        cs = slice(m * LANES, (m + 1) * LANES)
        vs = kv[:, ATT_WIDTH + m * LANES:ATT_WIDTH + (m + 1) * LANES]
        ps, mxs = [], []
        for s in ss:
            mx = jnp.max(jnp.maximum(s[:, :WIN_STEPS], s[:, WIN_STEPS:]), -1, keepdims=True)
            ps.append(jnp.exp(s - mx).astype(BF16))
            mxs.append(mx)
        pcat = jnp.concatenate(ps, axis=1)
        vpair = jnp.concatenate(
            [jnp.concatenate([jnp.where(lane_k, vs, zero_k), ones_lo], axis=1),
             jnp.concatenate([jnp.where(lane_k, zero_k, vs), ones_hi], axis=1)], axis=0)
        acc = jnp.dot(pcat, vpair, preferred_element_type=F32)
        den = acc[:, LANES:]
        o_ref[at + (cs,)] = (acc[:, :LANES] / den).astype(o_ref.dtype)
        lse = jnp.where(lane_q, mxs[0], mxs[1]) + jnp.log(den)
        return pltpu.roll(lse, (LSE_LANES * HEADS_PER_SLAB * m - 48) % LANES, axis=1)

    lane = lax.broadcasted_iota(jnp.int32, (WIN_STEPS, LANES), 1)
    lse_c = None
    pending = {0: scores(0), 1: scores(1)}
    for m in range(N_SLABS):
        if m + 2 < N_SLABS:
            pending[m + 2] = scores(m + 2)
        part = finish(m, pending.pop(m))
        lse_c = part if lse_c is None else jnp.where(lane >= LSE_LANES * HEADS_PER_SLAB * m, part, lse_c)
    l_ref[at + (slice(None),)] = lse_c


def _attn_kernel(q_ref, kvc_ref, kvp_ref, bias_ref, o_ref, l_ref, *, ns):
    d = q_ref.shape[1]
    first_tile = pl.program_id(1) == 0

    def body(idx, carry):
        rr = idx // ns
        j = idx % ns
        rc = pl.ds(pl.multiple_of(j * WIN_STEPS, WIN_STEPS), WIN_STEPS)
        q = q_ref[0, rr, rc, :] * ATT_SCALE
        kv_prev = kvp_ref[0, rr]
        if ns > 1:
            rp = pl.ds(pl.multiple_of(jnp.maximum(j - 1, 0) * WIN_STEPS, WIN_STEPS), WIN_STEPS)
            kv_prev = jnp.where(j == 0, kv_prev, kvc_ref[0, rr, rp, :])
        kv = jnp.concatenate([kv_prev, kvc_ref[0, rr, rc, :]], axis=0)
        table = jnp.where(first_tile & (j == 0), 1, 0)
        _attn_unit(q, kv, bias_ref, table, o_ref, l_ref, (0, rr, rc))
        return carry

    lax.fori_loop(0, d * ns, body, 0, unroll=4)


def _attn_call(q, kv, bias, g, name):
    B, d, U, _ = q.shape
    ns = ATT_TILE // (WIN_STEPS * d)
    rows = ns * WIN_STEPS
    blk = lambda width: pl.BlockSpec((1, d, rows, width), lambda b, t: (b, 0, t, 0))
    return pl.pallas_call(
        functools.partial(_attn_kernel, ns=ns),
        grid=(B, U // rows),
        in_specs=[blk(ATT_WIDTH), blk(2 * ATT_WIDTH),
                  pl.BlockSpec((1, d, WIN_STEPS, 2 * ATT_WIDTH),
                               lambda b, t: (b, 0, jnp.maximum(t * ns - 1, 0), 0)),
                  pl.BlockSpec((bias.shape[0], ATT_HEADS) + bias.shape[2:], lambda b, t: (0, g, 0, 0))],
        out_specs=[blk(ATT_WIDTH), blk(LANES)],
        out_shape=[jax.ShapeDtypeStruct((B, d, U, ATT_WIDTH), BF16),
                   jax.ShapeDtypeStruct((B, d, U, LANES), F32)],
        compiler_params=_params(("arbitrary", "arbitrary")),
        name=name,
    )(q, kv, kv, bias)


def _stride_bias(rel_table):
    eye = np.eye(N_BUCKETS, dtype=np.float32)
    onehot = np.stack([eye[_t5_bucket(np.arange(WIN_STEPS + 1) * d)] for _, d in ATT_GROUPS])
    tbl = rel_table.astype(F32).reshape(N_BUCKETS, N_GROUPS, ATT_HEADS)
    return jnp.einsum("gmb,bgh->ghm", jnp.asarray(onehot), tbl, precision=lax.Precision.HIGHEST)


def _toeplitz(rows, n_shifts, first, width):
    pitch = rows.shape[1]
    assert first - (n_shifts - 1) >= 0 and first + width <= pitch
    rows = jnp.concatenate([rows, rows[:, :1]], axis=1)
    flat = jnp.tile(rows, (1, n_shifts))[:, :n_shifts * pitch]
    return flat.reshape(rows.shape[0], n_shifts, pitch)[:, :, first:first + width]


def _prompt_bias(vals):
    n = WIN_STEPS
    v = vals.reshape(-1, n + 1)
    r = v.shape[0]
    by_offset = jnp.concatenate([jnp.full((r, n - 1), NEG, F32), v[:, ::-1], jnp.full((r, n), NEG, F32)], axis=1)
    bias = _toeplitz(by_offset, n, n - 1, 2 * n)
    first = jnp.concatenate([jnp.full((r, n, n), NEG, F32), bias[:, :, n:]], axis=2)
    return jnp.stack([bias, first])


SHIFT_VREGS = 128


def _sattn_kernel(*refs, T):
    q_refs = refs[0:3]
    kvnew_refs = refs[3:6]
    cache_refs = refs[6:9]
    bc_refs = refs[9:12]
    bn_refs = refs[12:15]
    o_ref = refs[15]
    co_refs = refs[16:19]
    kvn_refs = refs[19:22]
    HT = ATT_HEADS * T
    dn = (((1,), (1,)), ((), ()))
    row_head = lax.broadcasted_iota(jnp.int32, (HT, ATT_WIDTH), 0) // T
    col_head = lax.broadcasted_iota(jnp.int32, (HT, ATT_WIDTH), 1) // ATT_HEAD_DIM
    head_mask = row_head == col_head

    for g in range(N_GROUPS):
        rows = jnp.concatenate([jnp.zeros((LANES - T, 2 * ATT_WIDTH), F32), kvnew_refs[g][0]], axis=0)
        kvn_refs[g][0] = rows.T

    stats = []
    for g in range(N_GROUPS):
        q = q_refs[g][0] * ATT_SCALE
        qexp = jnp.where(head_mask, jnp.concatenate([q] * ATT_HEADS, axis=0), 0.0).astype(BF16)
        kn = kvn_refs[g][0, :ATT_WIDTH, :].astype(BF16)
        kc = cache_refs[g][0, :ATT_WIDTH, :].astype(BF16)
        lc = jnp.dot(qexp, kc, preferred_element_type=F32) + bc_refs[g][...]
        ln = jnp.dot(qexp, kn, preferred_element_type=F32) + bn_refs[g][...]
        mx = jnp.maximum(jnp.max(lc, -1, keepdims=True), jnp.max(ln, -1, keepdims=True))
        pc = jnp.exp(lc - mx)
        pn = jnp.exp(ln - mx)
        ssum = jnp.sum(pc, -1, keepdims=True) + jnp.sum(pn, -1, keepdims=True)
        stats.append((pc, pn, ssum, mx + jnp.log(ssum)))

    lse_max = jnp.maximum(jnp.maximum(stats[0][3], stats[1][3]), stats[2][3])
    es = [jnp.exp(st[3] - lse_max) for st in stats]
    esum = es[0] + es[1] + es[2]
    acc = jnp.zeros((HT, ATT_WIDTH), F32)
    for g in range(N_GROUPS):
        pc, pn, ssum, _ = stats[g]
        w = es[g] / (esum * ssum)
        vc = cache_refs[g][0, ATT_WIDTH:, :].astype(BF16)
        vn = kvn_refs[g][0, ATT_WIDTH:, :].astype(BF16)
        acc = acc + lax.dot_general((pc * w).astype(BF16), vc, dn, preferred_element_type=F32)
        acc = acc + lax.dot_general((pn * w).astype(BF16), vn, dn, preferred_element_type=F32)
    lane_head = lax.broadcasted_iota(jnp.int32, (T, ATT_WIDTH), 1) // ATT_HEAD_DIM
    o = jnp.zeros((T, ATT_WIDTH), F32)
    for h in range(ATT_HEADS):
        o = o + jnp.where(lane_head == h, acc[h * T:(h + 1) * T, :], 0.0)
    o_ref[0] = o

    for g in range(N_GROUPS):
        L = cache_refs[g].shape[2]
        nrows = cache_refs[g].shape[1]
        chunk = min(nrows, SHIFT_VREGS * SUBLANES * LANES // L)
        is_new = lax.broadcasted_iota(jnp.int32, (chunk, LANES), 1) >= LANES - T

        def shift_rows(i, carry, g=g, L=L, chunk=chunk, is_new=is_new):
            rs = pl.ds(pl.multiple_of(i * chunk, chunk), chunk)
            rolled = pltpu.roll(cache_refs[g][0, rs, :], L - T, axis=1)
            if L > LANES:
                co_refs[g][0, rs, 0:L - LANES] = rolled[:, 0:L - LANES]
            co_refs[g][0, rs, L - LANES:L] = jnp.where(is_new, kvn_refs[g][0, rs, :], rolled[:, L - LANES:])
            return carry

        lax.fori_loop(0, nrows // chunk, shift_rows, 0)


def _sample_bias(vals, d, Lb, T):
    n = Lb + T
    by_dist = jnp.concatenate([vals[:, :, None], jnp.full((ATT_HEADS, WIN_STEPS + 1, d - 1), NEG, F32)],
                              axis=2).reshape(ATT_HEADS, (WIN_STEPS + 1) * d)
    if by_dist.shape[1] < n:
        by_dist = jnp.concatenate([by_dist, jnp.full((ATT_HEADS, n - by_dist.shape[1]), NEG, F32)], axis=1)
    rev = jnp.concatenate([by_dist[:, :n][:, ::-1], jnp.full((ATT_HEADS, T), NEG, F32)], axis=1)
    rows = _toeplitz(rev, T, T - 1, n).reshape(ATT_HEADS * T, n)
    bc = rows[:, :Lb]
    bn = jnp.concatenate([jnp.full((ATT_HEADS * T, LANES - T), NEG, F32), rows[:, Lb:]], axis=1)
    return bc, bn


def _sattn_call(qs, kvns, caches, bcs, bns, T):
    DB = caches[0].shape[0]
    rows = caches[0].shape[1]
    per_seq = lambda b: (b, 0, 0)
    const2 = lambda b: (0, 0)
    in_specs = ([pl.BlockSpec((1, T, ATT_WIDTH), per_seq)] * 3
                + [pl.BlockSpec((1,) + x.shape[1:], per_seq) for x in kvns]
                + [pl.BlockSpec((1,) + c.shape[1:], per_seq) for c in caches]
                + [pl.BlockSpec(x.shape, const2) for x in bcs]
                + [pl.BlockSpec(x.shape, const2) for x in bns])
    out_specs = ([pl.BlockSpec((1, T, ATT_WIDTH), per_seq)]
                 + [pl.BlockSpec((1,) + c.shape[1:], per_seq) for c in caches])
    out_shape = ([jax.ShapeDtypeStruct((DB, T, ATT_WIDTH), F32)]
                 + [jax.ShapeDtypeStruct(c.shape, F32) for c in caches])
    return pl.pallas_call(
        functools.partial(_sattn_kernel, T=T),
        grid=(DB,),
        in_specs=in_specs, out_specs=out_specs, out_shape=out_shape,
        scratch_shapes=[pltpu.VMEM((1, rows, LANES), F32)] * N_GROUPS,
        compiler_params=_params(("arbitrary",)),
        name="sample_attn",
    )(*qs, *kvns, *caches, *bcs, *bns)


def _scan_rows(x, op, fill):
    n = x.shape[0]
    rowid = lax.broadcasted_iota(jnp.int32, x.shape, 0)
    s = 1
    while s < n:
        shifted = pltpu.roll(x, s, axis=0)
        x = op(x, jnp.where(rowid >= s, shifted, fill))
        s *= 2
    return x


def _pad_rows(x, n, fill=0.0):
    if x.shape[0] == n:
        return x
    return jnp.concatenate([x, jnp.full((n - x.shape[0],) + x.shape[1:], fill, x.dtype)], axis=0)


def _mlstm_kernel(xm_ref, zm_ref, om_ref, g_ref, cprev_ref, C0_ref, n0_ref, m0_ref,
                  convw_ref, convb_ref, wq_ref, wk_ref, mnorm_ref, mskip_ref, bif_ref,
                  mo_ref, C_ref, n_ref, m_ref, xc_ref, *, L):
    LS = max(L, LANES)
    G = xm_ref.shape[0]
    DV = C_ref.shape[2]
    DK = C_ref.shape[3]
    c = pl.program_id(1)

    @pl.when(c == 0)
    def _():
        xc_ref[...] = cprev_ref[...]
        C_ref[...] = C0_ref[...]
        n_ref[...] = n0_ref[...]
        m_ref[...] = m0_ref[...]

    t_id = lax.broadcasted_iota(jnp.int32, (L, LS), 0)
    s_id = lax.broadcasted_iota(jnp.int32, (L, LS), 1)
    causal = s_id <= t_id
    eye = s_id == t_id
    dn_t = (((1,), (1,)), ((), ()))

    seqs = []
    for g in range(G):
        xm_b = xm_ref[g].astype(BF16)
        xm_f = xm_ref[g].astype(F32)
        hist = jnp.concatenate([xc_ref[g], xm_f], axis=0)
        conv = convb_ref[...] + convw_ref[CONV_WIDTH - 1:CONV_WIDTH, :] * xm_f
        for k in range(1, CONV_WIDTH):
            conv = conv + (convw_ref[CONV_WIDTH - 1 - k:CONV_WIDTH - k, :]
                           * pltpu.roll(hist, k, axis=0)[SUBLANES:, :])
        xc_ref[g] = xm_f[L - SUBLANES:, :]
        c_act = _silu(conv)

        i_pre = g_ref[g] + bif_ref[...]
        logf = jax.nn.log_sigmoid(pltpu.roll(i_pre, LANES - M_HEADS, axis=1))
        b = _scan_rows(logf, jnp.add, 0.0)
        a = i_pre - b
        ca = _scan_rows(a, jnp.maximum, NEG)
        m_prev = m_ref[g]
        mm = jnp.maximum(ca, m_prev)
        u = -mm
        bL = b[L - 1:L, :]
        m_new = bL + jnp.maximum(m_prev, ca[L - 1:L, :])
        m_ref[g] = m_new
        seqs.append(dict(xm_b=xm_b, c_act=c_act, cb=c_act.astype(BF16), a=a, u=u,
                         w_inter=jnp.exp(u + m_prev), emt=jnp.exp(-(b + mm)),
                         wk=jnp.exp(bL + a - m_new), wC=jnp.exp(bL + m_prev - m_new)))

    units = [(g, h) for g in range(G) for h in range(M_HEADS)]
    vsl = [slice(h * DV, (h + 1) * DV) for h in range(M_HEADS)]
    qs, ks, kws, vss, C_old, n_old = {}, {}, {}, {}, {}, {}
    for g, h in units:
        ch = seqs[g]["cb"][:, vsl[h]]
        qs[g, h] = (jnp.dot(ch, wq_ref[h], preferred_element_type=F32) * (DK ** -0.5)).astype(BF16)
        k32 = jnp.dot(ch, wk_ref[h], preferred_element_type=F32)
        ks[g, h] = _pad_rows(k32.astype(BF16), LS)
        kws[g, h] = _pad_rows(k32 * seqs[g]["wk"][:, h:h + 1], LS)
        vss[g, h] = _pad_rows(seqs[g]["xm_b"][:, vsl[h]], LS)
        C_old[g, h] = C_ref[g, h]
        n_old[g, h] = n_ref[g, h:h + 1, :]
    scs, inters = {}, {}
    for g, h in units:
        a_row = jnp.sum(jnp.where(eye, seqs[g]["a"][:, h:h + 1], 0.0), axis=0, keepdims=True)
        w_intra = jnp.exp(jnp.where(causal, seqs[g]["u"][:, h:h + 1] + a_row, NEG))
        scs[g, h] = lax.dot_general(qs[g, h], ks[g, h], dn_t, preferred_element_type=F32) * w_intra
        inters[g, h] = lax.dot_general(qs[g, h], C_old[g, h].astype(BF16), dn_t, preferred_element_type=F32)
    for g, h in units:
        wi = seqs[g]["w_inter"][:, h:h + 1]
        num = jnp.dot(scs[g, h].astype(BF16), vss[g, h], preferred_element_type=F32) + wi * inters[g, h]
        den = (jnp.sum(scs[g, h], -1, keepdims=True)
               + wi * jnp.sum(qs[g, h].astype(F32) * n_old[g, h], -1, keepdims=True))
        hcell = num / jnp.maximum(jnp.abs(den), seqs[g]["emt"][:, h:h + 1])
        mu = jnp.mean(hcell, -1, keepdims=True)
        hc = hcell - mu
        hn = hc * lax.rsqrt(jnp.mean(hc * hc, -1, keepdims=True) + EPS)
        out = ((om_ref[g, :, vsl[h]].astype(F32) * (hn * mnorm_ref[:, vsl[h]])
                + mskip_ref[:, vsl[h]] * seqs[g]["c_act"][:, vsl[h]]) * zm_ref[g, :, vsl[h]].astype(F32))
        mo_ref[g, :, vsl[h]] = out.astype(mo_ref.dtype)
    for g, h in units:
        wc = seqs[g]["wC"][:, h:h + 1]
        C_ref[g, h] = wc * C_old[g, h] + lax.dot_general(vss[g, h], kws[g, h].astype(BF16),
                                                         (((0,), (0,)), ((), ())),
                                                         preferred_element_type=F32)
        n_ref[g, h:h + 1, :] = wc * n_old[g, h] + jnp.sum(kws[g, h], axis=0, keepdims=True)


def _mlstm_call(xm, zm, om, gates, conv_prev, C0, n0, m0, convw, convb, wq, wk, mnorm, mskip, bif,
                *, L, G, out_dtype, name):
    N, S, M = xm.shape
    H, DV, DK = C0.shape[1:]
    seq = lambda b, c: (b, c, 0)
    per_b3 = lambda b, c: (b, 0, 0)
    per_b4 = lambda b, c: (b, 0, 0, 0)
    const2 = lambda b, c: (0, 0)
    const3 = lambda b, c: (0, 0, 0)
    in_specs = [pl.BlockSpec((G, L, M), seq), pl.BlockSpec((G, L, M), seq), pl.BlockSpec((G, L, M), seq),
                pl.BlockSpec((G, L, LANES), seq),
                pl.BlockSpec((G, SUBLANES, M), per_b3),
                pl.BlockSpec((G, H, DV, DK), per_b4),
                pl.BlockSpec((G, H, DK), per_b3),
                pl.BlockSpec((G, 1, LANES), per_b3),
                pl.BlockSpec(convw.shape, const2), pl.BlockSpec(convb.shape, const2),
                pl.BlockSpec(wq.shape, const3), pl.BlockSpec(wk.shape, const3),
                pl.BlockSpec(mnorm.shape, const2), pl.BlockSpec(mskip.shape, const2),
                pl.BlockSpec(bif.shape, const2)]
    out_specs = [pl.BlockSpec((G, L, M), seq),
                 pl.BlockSpec((G, H, DV, DK), per_b4),
                 pl.BlockSpec((G, H, DK), per_b3),
                 pl.BlockSpec((G, 1, LANES), per_b3)]
    out_shape = [jax.ShapeDtypeStruct((N, S, M), out_dtype),
                 jax.ShapeDtypeStruct((N, H, DV, DK), F32),
                 jax.ShapeDtypeStruct((N, H, DK), F32),
                 jax.ShapeDtypeStruct((N, 1, LANES), F32)]
    return pl.pallas_call(
        functools.partial(_mlstm_kernel, L=L),
        grid=(N // G, S // L),
        in_specs=in_specs, out_specs=out_specs, out_shape=out_shape,
        scratch_shapes=[pltpu.VMEM((G, SUBLANES, M), F32)],
        compiler_params=_params(("arbitrary", "arbitrary")),
        name=name,
    )(xm, zm, om, gates, conv_prev, C0, n0, m0, convw, convb, wq, wk, mnorm, mskip, bif)


def _token_order(src_ref, scr_ref):
    _, d, n, width = src_ref.shape
    if d == 1:
        return src_ref[0, 0].astype(F32)
    n_tiles = width // LANES
    for r in range(d):
        for s in range(n_tiles):
            scr_ref[s, pl.ds(r, n, stride=d), :] = src_ref[0, r, :, s * LANES:(s + 1) * LANES].astype(F32)
    return jnp.concatenate([scr_ref[s] for s in range(n_tiles)], axis=1)


def _post_kernel(*refs, merge):
    if merge:
        (x_ref, gate_ref, o0, o1, o2, l0, l1, l2, expand_ref, za_ref, mo_ref, ga_ref, gm_ref,
         wpa_ref, wpm_ref, wout_ref, fg_ref, y_ref, scr_ref) = refs
        ls = [_token_order(l, scr_ref) for l in (l0, l1, l2)]
        lmax = jnp.maximum(jnp.maximum(ls[0], ls[1]), ls[2])
        es = [jnp.exp(l - lmax) for l in ls]
        inv = 1.0 / (es[0] + es[1] + es[2])
        o_att = None
        for e, o in zip(es, (o0, o1, o2)):
            a = e * inv
            hi = a.astype(BF16)
            lo = (a - hi.astype(F32)).astype(BF16)
            a_wide = jnp.dot(jnp.concatenate([hi, lo], axis=1), expand_ref[...],
                             preferred_element_type=F32)
            term = a_wide * _token_order(o, scr_ref)
            o_att = term if o_att is None else o_att + term
    else:
        (x_ref, gate_ref, oa_ref, za_ref, mo_ref, ga_ref, gm_ref,
         wpa_ref, wpm_ref, wout_ref, fg_ref, y_ref) = refs
        o_att = oa_ref[0]
    a_in = (o_att * za_ref[0].astype(F32)).astype(BF16)
    a_br = jnp.dot(a_in, wpa_ref[...], preferred_element_type=F32)
    m_br = jnp.dot(mo_ref[0].astype(BF16), wpm_ref[...], preferred_element_type=F32)
    merged = ga_ref[0].astype(F32) * a_br + gm_ref[0].astype(F32) * m_br
    y = x_ref[0] + gate_ref[0] * jnp.dot(merged.astype(BF16), wout_ref[...], preferred_element_type=F32)
    ms = jnp.mean(y * y, axis=-1, keepdims=True)
    y_ref[0] = y * lax.rsqrt(ms + EPS) * fg_ref[...]


def _post_call(x3, gate3, att_inputs, za, mo, ga, gm, wpa, wpm, wout, fgain, *, tm, merge, name):
    B, S, D = x3.shape
    row = lambda b, i: (b, i, 0)
    const2 = lambda b, i: (0, 0)
    if gate3.shape[1] == 1:
        gate_spec = pl.BlockSpec((1, 1, D), lambda b, i: (b, 0, 0))
    else:
        gate_spec = pl.BlockSpec((1, tm, D), row)
    def blk(a):
        if a.ndim == 2:
            return pl.BlockSpec(a.shape, const2)
        if a.ndim == 4:
            d = a.shape[1]
            return pl.BlockSpec((1, d, tm // d, a.shape[3]), lambda b, i: (b, 0, i, 0))
        return pl.BlockSpec((1, tm, a.shape[2]), row)

    in_specs = ([pl.BlockSpec((1, tm, D), row), gate_spec]
                + [blk(a) for a in att_inputs]
                + [blk(za), blk(mo), blk(ga), blk(gm),
                   pl.BlockSpec(wpa.shape, const2), pl.BlockSpec(wpm.shape, const2),
                   pl.BlockSpec(wout.shape, const2), pl.BlockSpec(fgain.shape, const2)])
    scratch = [pltpu.VMEM((ATT_WIDTH // LANES, tm, LANES), F32)] if merge else []
    return pl.pallas_call(
        functools.partial(_post_kernel, merge=merge),
        grid=(B, S // tm),
        in_specs=in_specs,
        out_specs=pl.BlockSpec((1, tm, D), row),
        out_shape=jax.ShapeDtypeStruct((B, S, D), F32),
        scratch_shapes=scratch,
        compiler_params=_params(("arbitrary", "arbitrary")),
        name=name,
    )(x3, gate3, *att_inputs, za, mo, ga, gm, wpa, wpm, wout, fgain)


def _cast_kernel(x_ref, o_ref):
    o_ref[...] = x_ref[...].T.astype(o_ref.dtype)


def _cast_bf16_call(wt, ncols):
    k = wt.shape[1]
    tn = 1024
    return pl.pallas_call(
        _cast_kernel,
        grid=(ncols // tn,),
        in_specs=[pl.BlockSpec((tn, k), lambda j: (j, 0))],
        out_specs=pl.BlockSpec((k, tn), lambda j: (0, j)),
        out_shape=jax.ShapeDtypeStruct((k, ncols), BF16),
        compiler_params=_params(("arbitrary",)),
        name="cast_weights",
    )(wt)


def _tail_cast_kernel(a_ref, b_ref, c_ref, o_ref, *, n_gate, n_merge):
    rows = jnp.concatenate([a_ref[...], b_ref[...], c_ref[0:SUBLANES, :]], axis=0)
    gates = jnp.concatenate([rows[0:n_gate], jnp.zeros((LANES - n_gate, rows.shape[1]), F32)], axis=0)
    out = jnp.concatenate([rows[n_gate:n_gate + n_merge], gates], axis=0)
    o_ref[...] = out.T.astype(o_ref.dtype)


def _tail_cast_call(wt, first, n_gate, n_merge):
    k = wt.shape[1]
    tn = n_merge // 2
    j0 = first // tn
    assert first % tn == 0 and n_gate == SUBLANES
    return pl.pallas_call(
        functools.partial(_tail_cast_kernel, n_gate=n_gate, n_merge=n_merge),
        grid=(1,),
        in_specs=[pl.BlockSpec((tn, k), lambda j: (j0, 0), pipeline_mode=pl.Buffered(1)),
                  pl.BlockSpec((tn, k), lambda j: (j0 + 1, 0), pipeline_mode=pl.Buffered(1)),
                  pl.BlockSpec((tn, k), lambda j: (j0 + 2, 0), pipeline_mode=pl.Buffered(1))],
        out_specs=pl.BlockSpec((k, n_merge + LANES), lambda j: (0, 0)),
        out_shape=jax.ShapeDtypeStruct((k, n_merge + LANES), BF16),
        compiler_params=_params(("arbitrary",)),
        name="cast_tail_weights",
    )(wt, wt, wt)


def _projection_weights(w_in):
    D = w_in.shape[0]
    AW = ATT_WIDTH
    M = D
    off_k, off_v, off_za = 3 * AW, 6 * AW, 9 * AW
    off_xm = off_za + AW
    off_i = off_xm + 3 * M
    wt = jnp.transpose(w_in)
    w_main = _cast_bf16_call(wt, off_i)
    w_tail = _tail_cast_call(wt, off_i, 2 * M_HEADS, 2 * D)
    segs = {"za": ((0, off_za, AW),), "xm": ((0, off_xm, M),), "zm": ((0, off_xm + M, M),),
            "om": ((0, off_xm + 2 * M, M),), "ga": ((1, 0, D),), "gm": ((1, D, D),),
            "gates": ((1, 2 * D, LANES),)}
    for g in range(N_GROUPS):
        segs[f"q{g}"] = ((0, g * AW, AW),)
        segs[f"kv{g}"] = ((0, off_k + g * AW, AW), (0, off_v + g * AW, AW))
    return (w_main, w_tail), segs


def kernel(x_prompt, x_sample, cache_kv_w128, cache_kv_w512, cache_kv_w2048, state_conv, state_C, state_n, state_m, c_prompt, c_sample, rel_table, norm_gain, w_ada, b_ada, w_in, b_if, conv_w, conv_b, w_mq, w_mk, m_norm, m_skip, w_pa, w_pm, w_out, final_gain):
    B, S, D = x_prompt.shape
    DB, T, _ = x_sample.shape
    assert norm_gain.shape[0] == 1, "single-layer trunk"
    assert S % ATT_TILE == 0 and S % MLSTM_CHUNK == 0 and T == SUBLANES
    caches = (cache_kv_w128[0], cache_kv_w512[0], cache_kv_w2048[0])
    H = M_HEADS
    M = conv_w.shape[2]

    wp, segs = _projection_weights(w_in[0])
    names = ("q0", "q1", "q2", "za", "kv0", "kv1", "kv2", "xm", "zm", "om", "ga", "gm", "gates")
    seg_list = [segs[n] for n in names]
    gain = norm_gain[0].reshape(1, D)
    fgain = final_gain.reshape(1, D)
    wpa, wpm, wout = w_pa[0].astype(BF16), w_pm[0].astype(BF16), w_out[0].astype(BF16)
    wq, wk = w_mq[0].astype(BF16), w_mk[0].astype(BF16)
    convw, convb = conv_w[0], conv_b[0].reshape(1, M)
    mnorm, mskip = m_norm[0].reshape(1, M), m_skip[0].reshape(1, M)
    bif = jnp.concatenate([b_if[0], jnp.zeros((LANES - 2 * H,), F32)]).reshape(1, LANES)

    ada = _ada_call(jnp.concatenate([c_prompt, c_sample], axis=0), w_ada[0], b_ada[0])
    shift, scale, gate = ada[:, :D], ada[:, D:2 * D], ada[:, 2 * D:]

    gate_act = {"za": "silu", "zm": "silu", "om": "sigmoid", "ga": "sigmoid", "gm": "sigmoid"}
    acts = [gate_act.get(n) for n in names]

    R = DB * T
    rep = lambda t: jnp.repeat(t[B:], T, axis=0).reshape(1, R, D)
    s_shift, s_scale, s_gate = rep(shift), rep(scale), rep(gate)
    xs = x_sample.reshape(1, R, D)
    sr = dict(zip(names, _proj_call(xs, gain, s_scale, s_shift, wp, seg_list, [F32] * 13,
                                    tm=R, row0=0, rows=R, name="proj_sample", acts=acts)))
    stride_bias = _stride_bias(rel_table)
    prompt_bias = _prompt_bias(stride_bias)
    bcs, bns = [], []
    for g, (win, dil) in enumerate(ATT_GROUPS):
        bc, bn = _sample_bias(stride_bias[g], dil, caches[g].shape[1], T)
        bcs.append(bc)
        bns.append(bn)
    cache_t = [jnp.transpose(c, (0, 2, 3, 4, 1)).reshape(DB, 2 * ATT_WIDTH, c.shape[1]) for c in caches]
    kvn_t = [sr[f"kv{g}"].reshape(DB, T, 2 * ATT_WIDTH) for g in range(3)]
    sattn_args = ([sr[f"q{g}"].reshape(DB, T, ATT_WIDTH) for g in range(3)], kvn_t, cache_t, bcs, bns)

    p_shift, p_scale, p_gate = (t[:B].reshape(B, 1, D) for t in (shift, scale, gate))
    dts = [F32 if n == "gates" else BF16 for n in names]
    group_dil = {f"{p}{g}": dil for g, (_, dil) in enumerate(ATT_GROUPS) for p in ("q", "kv")}
    pr = dict(zip(names, _proj_call(x_prompt, gain, p_scale, p_shift, wp, seg_list, dts,
                                    tm=PROJ_TM, row0=0, rows=S, name="proj_prompt",
                                    dils=[group_dil.get(n, 1) for n in names], acts=acts)))
    att = []
    for g, (win, dil) in enumerate(ATT_GROUPS):
        planes = lambda a: a if a.ndim == 4 else a[:, None]
        att.append(_attn_call(planes(pr[f"q{g}"]), planes(pr[f"kv{g}"]), prompt_bias, g,
                              f"attn_prompt_g{g}"))
    expand = np.zeros((2 * LANES, ATT_WIDTH), np.float32)
    for h in range(ATT_HEADS):
        expand[[LSE_LANES * h, LANES + LSE_LANES * h], h * ATT_HEAD_DIM:(h + 1) * ATT_HEAD_DIM] = 1.0
    expand = jnp.asarray(expand, BF16)
    mlstm_p_args = (pr["xm"], pr["zm"], pr["om"], pr["gates"],
                    jnp.zeros((B, SUBLANES, M), F32), jnp.zeros((B,) + state_C.shape[2:], F32),
                    jnp.zeros((B,) + state_n.shape[2:], F32), jnp.zeros((B, 1, LANES), F32),
                    convw, convb, wq, wk, mnorm, mskip, bif)
    mo_p, C_p, n_p, m_p = _mlstm_call(*mlstm_p_args, L=MLSTM_CHUNK, G=_largest_divisor(B, MLSTM_SEQS_PROMPT),
                                      out_dtype=BF16, name="mlstm_prompt")
    sa = _sattn_call(*sattn_args, T)
    y_prompt = _post_call(x_prompt, p_gate, [a[0] for a in att] + [a[1] for a in att] + [expand],
                          pr["za"], mo_p, pr["ga"], pr["gm"], wpa, wpm, wout, fgain,
                          tm=POST_TM, merge=True, name="post_prompt")
    def tail_weights(keys):
        cols, tsegs, c = [], [], 0
        for key in keys:
            pieces = []
            for wi, c0, wd in segs[key]:
                cols.append(wp[wi][:, c0:c0 + wd])
                pieces.append((0, c, wd))
                c += wd
            tsegs.append(tuple(pieces))
        return (jnp.concatenate(cols, axis=1),), tsegs

    w_max = min(ATT_GROUPS[-1][0], S)
    w2, segs2 = tail_weights(["kv2"])
    (kv2_t,) = _proj_call(x_prompt, gain, p_scale, p_shift, w2, segs2, [F32],
                          tm=min(w_max, 2 * PROJ_TM), row0=S - w_max, rows=w_max, name="tail_kv2", dils=[0])
    w_mid = min(ATT_GROUPS[1][0], S)
    w01, segs01 = tail_weights(["kv0", "kv1", "xm"])
    kv0_t, kv1_t, xm_t = _proj_call(x_prompt, gain, p_scale, p_shift, w01, segs01, [F32] * 3,
                                    tm=w_mid, row0=S - w_mid, rows=w_mid, name="tail_kv01", dils=[0, 0, 1])
    w0 = min(ATT_GROUPS[0][0], S)
    as_buffer = lambda t: jnp.transpose(t.reshape(B, 2, ATT_HEADS, ATT_HEAD_DIM, t.shape[2]), (0, 4, 1, 2, 3))[None]
    kv_p = [as_buffer(kv0_t[:, :, w_mid - w0:]), as_buffer(kv1_t), as_buffer(kv2_t)]
    conv_p = xm_t[:, w_mid - (CONV_WIDTH - 1):][None]

    o_att_s = sa[0].reshape(1, R, ATT_WIDTH)
    kv_s = [jnp.transpose(c.reshape(DB, 2, ATT_HEADS, ATT_HEAD_DIM, c.shape[2]), (0, 4, 1, 2, 3))[None]
            for c in sa[1:]]
    conv_prev_s = jnp.concatenate([jnp.zeros((DB, SUBLANES - (CONV_WIDTH - 1), M), F32), state_conv[0]], axis=1)
    m0_s = jnp.concatenate([state_m[0], jnp.zeros((DB, LANES - H), F32)], axis=1).reshape(DB, 1, LANES)
    seqv = lambda t: t.reshape(DB, T, t.shape[-1])
    mo_s, C_s, n_s, m_s = _mlstm_call(
        seqv(sr["xm"]), seqv(sr["zm"]), seqv(sr["om"]), seqv(sr["gates"]),
        conv_prev_s, state_C[0], state_n[0], m0_s,
        convw, convb, wq, wk, mnorm, mskip, bif, L=T, G=_largest_divisor(DB, MLSTM_SEQS_SAMPLE),
        out_dtype=F32, name="mlstm_sample")
    y_sample = _post_call(xs, s_gate, [o_att_s], sr["za"], mo_s.reshape(1, R, M), sr["ga"], sr["gm"],
                          wpa, wpm, wout, fgain, tm=R, merge=False, name="post_sample")
    conv_s = seqv(sr["xm"])[:, T - (CONV_WIDTH - 1):][None]

    return (y_prompt, y_sample.reshape(DB, T, D),
            kv_p[0], kv_s[0], kv_p[1], kv_s[1], kv_p[2], kv_s[2],
            conv_p, conv_s, C_p[None], C_s[None], n_p[None], n_s[None],
            m_p[:, 0, :H][None], m_s[:, 0, :H][None])
```

```python
import functools

import numpy as np
import jax
import jax.numpy as jnp
from jax import lax
from jax.experimental import pallas as pl
from jax.experimental.pallas import tpu as pltpu

F32 = jnp.float32
BF16 = jnp.bfloat16

ATT_GROUPS = ((128, 1), (512, 4), (2048, 16))
N_GROUPS = len(ATT_GROUPS)
ATT_HEADS = 8
ATT_HEAD_DIM = 64
ATT_WIDTH = ATT_HEADS * ATT_HEAD_DIM
WIN_STEPS = 128
ATT_SCALE = ATT_HEAD_DIM ** -0.5
N_BUCKETS = 32
MAX_DISTANCE = 2048
M_HEADS = 4
CONV_WIDTH = 4
EPS = 1e-6
NEG = -1e30

LANES = 128
SUBLANES = 8
VMEM_LIMIT = 56 * 1024 * 1024

ATT_TILE = 2048
PROJ_TM = 512
POST_TM = 1024
MLSTM_CHUNK = 256
MLSTM_SEQS_PROMPT = 2
MLSTM_SEQS_SAMPLE = 8


def _largest_divisor(n, cap):
    return max(g for g in range(1, cap + 1) if n % g == 0)


def _params(sem, vmem=VMEM_LIMIT):
    return pltpu.CompilerParams(dimension_semantics=sem, vmem_limit_bytes=vmem)


def _t5_bucket(dist):
    n = np.asarray(dist).astype(np.int64)
    max_exact = N_BUCKETS // 2
    nf = np.maximum(n, 1).astype(np.float32)
    large = max_exact + (np.log(nf / max_exact) / np.log(np.float32(MAX_DISTANCE / max_exact))
                         * (N_BUCKETS - max_exact)).astype(np.int64)
    large = np.minimum(large, N_BUCKETS - 1)
    return np.where(n < max_exact, n, large).astype(np.int32)


def _sigmoid_tanh(v):
    return 0.5 * jnp.tanh(0.5 * v) + 0.5


def _silu(x):
    return x * _sigmoid_tanh(x)


def _ada_kernel(c_ref, w_ref, b_ref, o_ref):
    s = _silu(c_ref[...])
    o_ref[...] = jnp.dot(s, w_ref[...], preferred_element_type=F32,
                         precision=lax.Precision.HIGHEST) + b_ref[...]


def _ada_call(c, w, b):
    n, d = c.shape
    width = w.shape[1]
    tn = width // 3 if width % (3 * LANES) == 0 else width
    return pl.pallas_call(
        _ada_kernel,
        grid=(width // tn,),
        in_specs=[pl.BlockSpec((n, d), lambda j: (0, 0)),
                  pl.BlockSpec((d, tn), lambda j: (0, j)),
                  pl.BlockSpec((1, tn), lambda j: (0, j))],
        out_specs=pl.BlockSpec((n, tn), lambda j: (0, j)),
        out_shape=jax.ShapeDtypeStruct((n, width), F32),
        compiler_params=_params(("arbitrary",)),
        name="ada",
    )(c, w, b.reshape(1, width))


_ACTIVATIONS = {None: lambda v: v, "silu": _silu, "sigmoid": _sigmoid_tanh}


def _proj_kernel(x_ref, gain_ref, scale_ref, shift_ref, *rest, n_w, segs, dils, acts):
    w_refs = rest[:n_w]
    rest = rest[n_w:]
    out_refs = rest[:len(segs)]
    x = x_ref[0]
    tm, D = x.shape
    ms = jnp.mean(x * x, axis=-1, keepdims=True)
    h = x * lax.rsqrt(ms + EPS) * gain_ref[...] * (1.0 + scale_ref[0]) + shift_ref[0]
    lhs = {1: h.astype(BF16)}
    lhs[0] = lhs[1]
    strides = sorted(set(dils) - {0, 1})
    if strides:
        hs_ref = rest[len(segs)]
        n_tiles = D // LANES
        for s in range(n_tiles):
            hs_ref[s] = h[:, s * LANES:(s + 1) * LANES]
        for d in strides:
            n = tm // d
            lhs[d] = jnp.concatenate(
                [jnp.concatenate([hs_ref[s, pl.ds(r, n, stride=d), :] for r in range(d)], axis=0)
                 for s in range(n_tiles)], axis=1).astype(BF16)
    for o_ref, pieces, d, act in zip(out_refs, segs, dils, acts):
        parts = [_ACTIVATIONS[act](jnp.dot(lhs[d], w_refs[wi][:, c0:c0 + width],
                                           preferred_element_type=F32)).astype(o_ref.dtype)
                 for wi, c0, width in pieces]
        res = parts[0] if len(parts) == 1 else jnp.concatenate(parts, axis=1)
        if d == 0:
            o_ref[0] = res.T
        elif d == 1:
            o_ref[0] = res
        else:
            n = tm // d
            for r in range(d):
                o_ref[0, r] = res[r * n:(r + 1) * n, :]


def _proj_call(x3, gain, scale3, shift3, ws, segs, dtypes, *, tm, row0, rows, name, dils=None, acts=None):
    B, S, D = x3.shape
    nrb = rows // tm
    rb0 = row0 // tm
    dils = tuple(dils) if dils is not None else (1,) * len(segs)
    acts = tuple(acts) if acts is not None else (None,) * len(segs)
    per_row = scale3.shape[1] != 1
    if per_row:
        mod_spec = pl.BlockSpec((1, tm, D), lambda b, i: (b, rb0 + i, 0))
    else:
        mod_spec = pl.BlockSpec((1, 1, D), lambda b, i: (b, 0, 0))
    out_shape, out_specs = [], []
    for pieces, dt, d in zip(segs, dtypes, dils):
        wd = sum(p[2] for p in pieces)
        if d == 0:
            out_shape.append(jax.ShapeDtypeStruct((B, wd, rows), dt))
            out_specs.append(pl.BlockSpec((1, wd, tm), lambda b, i: (b, 0, i)))
        elif d == 1:
            out_shape.append(jax.ShapeDtypeStruct((B, rows, wd), dt))
            out_specs.append(pl.BlockSpec((1, tm, wd), lambda b, i: (b, i, 0)))
        else:
            out_shape.append(jax.ShapeDtypeStruct((B, d, rows // d, wd), dt))
            out_specs.append(pl.BlockSpec((1, d, tm // d, wd), lambda b, i: (b, 0, i, 0)))
    scratch = [pltpu.VMEM((D // LANES, tm, LANES), F32)] if any(d > 1 for d in dils) else []
    return pl.pallas_call(
        functools.partial(_proj_kernel, n_w=len(ws), segs=tuple(segs), dils=dils, acts=acts),
        grid=(B, nrb),
        in_specs=[pl.BlockSpec((1, tm, D), lambda b, i: (b, rb0 + i, 0)),
                  pl.BlockSpec((1, D), lambda b, i: (0, 0)),
                  mod_spec, mod_spec]
                 + [pl.BlockSpec(w.shape, lambda b, i: (0, 0), pipeline_mode=pl.Buffered(1)) for w in ws],
        out_specs=out_specs,
        out_shape=out_shape,
        scratch_shapes=scratch,
        compiler_params=_params(("arbitrary", "arbitrary")),
        name=name,
    )(x3, gain, scale3, shift3, *ws)


HEADS_PER_SLAB = LANES // ATT_HEAD_DIM
N_SLABS = ATT_HEADS // HEADS_PER_SLAB
LSE_LANES = LANES // ATT_HEADS


def _attn_unit(q, kv, bias_ref, table, o_ref, l_ref, at):
    dn = (((1,), (1,)), ((), ()))
    nk = 2 * WIN_STEPS
    lane_q = lax.broadcasted_iota(jnp.int32, (WIN_STEPS, LANES), 1) < ATT_HEAD_DIM
    lane_k = lax.broadcasted_iota(jnp.int32, (nk, LANES), 1) < ATT_HEAD_DIM
    ones_lo = jnp.where(lane_k, 1.0, 0.0).astype(BF16)
    ones_hi = jnp.where(lane_k, 0.0, 1.0).astype(BF16)
    zero_q = jnp.zeros((WIN_STEPS, LANES), BF16)
    zero_k = jnp.zeros((nk, LANES), BF16)

    def scores(m):
        cs = slice(m * LANES, (m + 1) * LANES)
        qs = q[:, cs]
        ks = kv[:, cs]
        out = []
        for hh in range(HEADS_PER_SLAB):
            qm = jnp.where(lane_q, qs, zero_q) if hh == 0 else jnp.where(lane_q, zero_q, qs)
            out.append(lax.dot_general(qm, ks, dn, preferred_element_type=F32)
                       + bias_ref[table, m * HEADS_PER_SLAB + hh])
        return out

    def finish(m, ss):
        cs = slice(m * LANES, (m + 1) * LANES)
        vs = kv[:, ATT_WIDTH + m * LANES:ATT_WIDTH + (m + 1) * LANES]
        ps, mxs = [], []
        for s in ss:
            mx = jnp.max(jnp.maximum(s[:, :WIN_STEPS], s[:, WIN_STEPS:]), -1, keepdims=True)
            ps.append(jnp.exp(s - mx).astype(BF16))
            mxs.append(mx)
        pcat = jnp.concatenate(ps, axis=1)
        vpair = jnp.concatenate(
            [jnp.concatenate([jnp.where(lane_k, vs, zero_k), ones_lo], axis=1),
             jnp.concatenate([jnp.where(lane_k, zero_k, vs), ones_hi], axis=1)], axis=0)
        acc = jnp.dot(pcat, vpair, preferred_element_type=F32)
        den = acc[:, LANES:]
        o_ref[at + (cs,)] = (acc[:, :LANES] / den).astype(o_ref.dtype)
        lse = jnp.where(lane_q, mxs[0], mxs[1]) + jnp.log(den)
        return pltpu.roll(lse, (LSE_LANES * HEADS_PER_SLAB * m - 48) % LANES, axis=1)

    lane = lax.broadcasted_iota(jnp.int32, (WIN_STEPS, LANES), 1)
    lse_c = None
    pending = {0: scores(0), 1: scores(1)}
    for m in range(N_SLABS):
        if m + 2 < N_SLABS:
            pending[m + 2] = scores(m + 2)
        part = finish(m, pending.pop(m))
        lse_c = part if lse_c is None else jnp.where(lane >= LSE_LANES * HEADS_PER_SLAB * m, part, lse_c)
    l_ref[at + (slice(None),)] = lse_c


def _attn_kernel(q_ref, kvc_ref, kvp_ref, bias_ref, o_ref, l_ref, *, ns):
    d = q_ref.shape[1]
    first_tile = pl.program_id(1) == 0

    def body(idx, carry):
        rr = idx // ns
        j = idx % ns
        rc = pl.ds(pl.multiple_of(j * WIN_STEPS, WIN_STEPS), WIN_STEPS)
        q = q_ref[0, rr, rc, :] * ATT_SCALE
        kv_prev = kvp_ref[0, rr]
        if ns > 1:
            rp = pl.ds(pl.multiple_of(jnp.maximum(j - 1, 0) * WIN_STEPS, WIN_STEPS), WIN_STEPS)
            kv_prev = jnp.where(j == 0, kv_prev, kvc_ref[0, rr, rp, :])
        kv = jnp.concatenate([kv_prev, kvc_ref[0, rr, rc, :]], axis=0)
        table = jnp.where(first_tile & (j == 0), 1, 0)
        _attn_unit(q, kv, bias_ref, table, o_ref, l_ref, (0, rr, rc))
        return carry

    lax.fori_loop(0, d * ns, body, 0, unroll=4)


def _attn_call(q, kv, bias, name):
    B, d, U, _ = q.shape
    ns = ATT_TILE // (WIN_STEPS * d)
    rows = ns * WIN_STEPS
    blk = lambda width: pl.BlockSpec((1, d, rows, width), lambda b, t: (b, 0, t, 0))
    return pl.pallas_call(
        functools.partial(_attn_kernel, ns=ns),
        grid=(B, U // rows),
        in_specs=[blk(ATT_WIDTH), blk(2 * ATT_WIDTH),
                  pl.BlockSpec((1, d, WIN_STEPS, 2 * ATT_WIDTH),
                               lambda b, t: (b, 0, jnp.maximum(t * ns - 1, 0), 0)),
                  pl.BlockSpec(bias.shape, lambda b, t: (0, 0, 0, 0))],
        out_specs=[blk(ATT_WIDTH), blk(LANES)],
        out_shape=[jax.ShapeDtypeStruct((B, d, U, ATT_WIDTH), BF16),
                   jax.ShapeDtypeStruct((B, d, U, LANES), F32)],
        compiler_params=_params(("arbitrary", "arbitrary")),
        name=name,
    )(q, kv, kv, bias)


def _stride_bias(rel_table, g, d):
    bucket = _t5_bucket(np.arange(WIN_STEPS + 1) * d)
    onehot = jnp.asarray(np.eye(N_BUCKETS, dtype=np.float32)[bucket])
    tbl = rel_table[:, g * ATT_HEADS:(g + 1) * ATT_HEADS].astype(F32)
    return jnp.dot(onehot, tbl, precision=lax.Precision.HIGHEST).T


def _prompt_bias(rel_table, g, d):
    vals = _stride_bias(rel_table, g, d)
    n = WIN_STEPS
    period = 3 * n
    wp = jnp.concatenate([jnp.full((ATT_HEADS, n - 1), NEG, F32), vals[:, ::-1],
                          jnp.full((ATT_HEADS, n), NEG, F32)], axis=1)
    flat = jnp.tile(wp, (1, n))[:, :n * (period - 1)]
    bias = flat.reshape(ATT_HEADS, n, period - 1)[:, :, n - 1:n - 1 + 2 * n]
    first = jnp.concatenate([jnp.full((ATT_HEADS, n, n), NEG, F32), bias[:, :, n:]], axis=2)
    return jnp.stack([bias, first])


SHIFT_VREGS = 256


def _sattn_kernel(*refs, T):
    q_refs = refs[0:3]
    kvnew_refs = refs[3:6]
    cache_refs = refs[6:9]
    bc_refs = refs[9:12]
    bn_refs = refs[12:15]
    o_ref = refs[15]
    co_refs = refs[16:19]
    kvn_refs = refs[19:22]
    HT = ATT_HEADS * T
    dn = (((1,), (1,)), ((), ()))
    row_head = lax.broadcasted_iota(jnp.int32, (HT, ATT_WIDTH), 0) // T
    col_head = lax.broadcasted_iota(jnp.int32, (HT, ATT_WIDTH), 1) // ATT_HEAD_DIM
    head_mask = row_head == col_head

    for g in range(N_GROUPS):
        rows = jnp.concatenate([jnp.zeros((LANES - T, 2 * ATT_WIDTH), F32), kvnew_refs[g][0]], axis=0)
        kvn_refs[g][0] = rows.T

    stats = []
    for g in range(N_GROUPS):
        q = q_refs[g][0] * ATT_SCALE
        qexp = jnp.where(head_mask, jnp.concatenate([q] * ATT_HEADS, axis=0), 0.0).astype(BF16)
        kn = kvn_refs[g][0, :ATT_WIDTH, :].astype(BF16)
        kc = cache_refs[g][0, :ATT_WIDTH, :].astype(BF16)
        lc = jnp.dot(qexp, kc, preferred_element_type=F32) + bc_refs[g][...]
        ln = jnp.dot(qexp, kn, preferred_element_type=F32) + bn_refs[g][...]
        mx = jnp.maximum(jnp.max(lc, -1, keepdims=True), jnp.max(ln, -1, keepdims=True))
        pc = jnp.exp(lc - mx)
        pn = jnp.exp(ln - mx)
        ssum = jnp.sum(pc, -1, keepdims=True) + jnp.sum(pn, -1, keepdims=True)
        stats.append((pc, pn, ssum, mx + jnp.log(ssum)))

    lse_max = jnp.maximum(jnp.maximum(stats[0][3], stats[1][3]), stats[2][3])
    es = [jnp.exp(st[3] - lse_max) for st in stats]
    esum = es[0] + es[1] + es[2]
    acc = jnp.zeros((HT, ATT_WIDTH), F32)
    for g in range(N_GROUPS):
        pc, pn, ssum, _ = stats[g]
        w = es[g] / (esum * ssum)
        vc = cache_refs[g][0, ATT_WIDTH:, :].astype(BF16)
        vn = kvn_refs[g][0, ATT_WIDTH:, :].astype(BF16)
        acc = acc + lax.dot_general((pc * w).astype(BF16), vc, dn, preferred_element_type=F32)
        acc = acc + lax.dot_general((pn * w).astype(BF16), vn, dn, preferred_element_type=F32)
    lane_head = lax.broadcasted_iota(jnp.int32, (T, ATT_WIDTH), 1) // ATT_HEAD_DIM
    o = jnp.zeros((T, ATT_WIDTH), F32)
    for h in range(ATT_HEADS):
        o = o + jnp.where(lane_head == h, acc[h * T:(h + 1) * T, :], 0.0)
    o_ref[0] = o

    for g in range(N_GROUPS):
        L = cache_refs[g].shape[2]
        nrows = cache_refs[g].shape[1]
        chunk = min(nrows, SHIFT_VREGS * SUBLANES * LANES // L)
        is_new = lax.broadcasted_iota(jnp.int32, (chunk, LANES), 1) >= LANES - T

        def shift_rows(i, carry, g=g, L=L, chunk=chunk, is_new=is_new):
            rs = pl.ds(pl.multiple_of(i * chunk, chunk), chunk)
            rolled = pltpu.roll(cache_refs[g][0, rs, :], L - T, axis=1)
            if L > LANES:
                co_refs[g][0, rs, 0:L - LANES] = rolled[:, 0:L - LANES]
            co_refs[g][0, rs, L - LANES:L] = jnp.where(is_new, kvn_refs[g][0, rs, :], rolled[:, L - LANES:])
            return carry

        lax.fori_loop(0, nrows // chunk, shift_rows, 0)


def _sample_bias(rel_table, g, W, d, Lb, T):
    vals = _stride_bias(rel_table, g, d)
    n = Lb + T
    by_dist = jnp.concatenate([vals[:, :, None], jnp.full((ATT_HEADS, WIN_STEPS + 1, d - 1), NEG, F32)],
                              axis=2).reshape(ATT_HEADS, (WIN_STEPS + 1) * d)
    if by_dist.shape[1] < n:
        by_dist = jnp.concatenate([by_dist, jnp.full((ATT_HEADS, n - by_dist.shape[1]), NEG, F32)], axis=1)
    rev = jnp.concatenate([by_dist[:, :n][:, ::-1], jnp.full((ATT_HEADS, T), NEG, F32)], axis=1)
    rows = jnp.stack([rev[:, T - 1 - t:T - 1 - t + n] for t in range(T)], axis=1)
    rows = rows.reshape(ATT_HEADS * T, n)
    bc = rows[:, :Lb]
    bn = jnp.concatenate([jnp.full((ATT_HEADS * T, LANES - T), NEG, F32), rows[:, Lb:]], axis=1)
    return bc, bn


def _sattn_call(qs, kvns, caches, bcs, bns, T):
    DB = caches[0].shape[0]
    rows = caches[0].shape[1]
    per_seq = lambda b: (b, 0, 0)
    const2 = lambda b: (0, 0)
    in_specs = ([pl.BlockSpec((1, T, ATT_WIDTH), per_seq)] * 3
                + [pl.BlockSpec((1,) + x.shape[1:], per_seq) for x in kvns]
                + [pl.BlockSpec((1,) + c.shape[1:], per_seq) for c in caches]
                + [pl.BlockSpec(x.shape, const2) for x in bcs]
                + [pl.BlockSpec(x.shape, const2) for x in bns])
    out_specs = ([pl.BlockSpec((1, T, ATT_WIDTH), per_seq)]
                 + [pl.BlockSpec((1,) + c.shape[1:], per_seq) for c in caches])
    out_shape = ([jax.ShapeDtypeStruct((DB, T, ATT_WIDTH), F32)]
                 + [jax.ShapeDtypeStruct(c.shape, F32) for c in caches])
    return pl.pallas_call(
        functools.partial(_sattn_kernel, T=T),
        grid=(DB,),
        in_specs=in_specs, out_specs=out_specs, out_shape=out_shape,
        scratch_shapes=[pltpu.VMEM((1, rows, LANES), F32)] * N_GROUPS,
        compiler_params=_params(("arbitrary",)),
        name="sample_attn",
    )(*qs, *kvns, *caches, *bcs, *bns)


def _scan_rows(x, op, fill):
    n = x.shape[0]
    rowid = lax.broadcasted_iota(jnp.int32, x.shape, 0)
    s = 1
    while s < n:
        shifted = pltpu.roll(x, s, axis=0)
        x = op(x, jnp.where(rowid >= s, shifted, fill))
        s *= 2
    return x


def _pad_rows(x, n, fill=0.0):
    if x.shape[0] == n:
        return x
    return jnp.concatenate([x, jnp.full((n - x.shape[0],) + x.shape[1:], fill, x.dtype)], axis=0)


def _mlstm_kernel(xm_ref, zm_ref, om_ref, g_ref, cprev_ref, C0_ref, n0_ref, m0_ref,
                  convw_ref, convb_ref, wq_ref, wk_ref, mnorm_ref, mskip_ref, bif_ref,
                  mo_ref, C_ref, n_ref, m_ref, xc_ref, *, L):
    LS = max(L, LANES)
    G = xm_ref.shape[0]
    DV = C_ref.shape[2]
    DK = C_ref.shape[3]
    c = pl.program_id(1)

    @pl.when(c == 0)
    def _():
        xc_ref[...] = cprev_ref[...]
        C_ref[...] = C0_ref[...]
        n_ref[...] = n0_ref[...]
        m_ref[...] = m0_ref[...]

    t_id = lax.broadcasted_iota(jnp.int32, (L, LS), 0)
    s_id = lax.broadcasted_iota(jnp.int32, (L, LS), 1)
    causal = s_id <= t_id
    eye = s_id == t_id
    dn_t = (((1,), (1,)), ((), ()))

    seqs = []
    for g in range(G):
        xm_b = xm_ref[g].astype(BF16)
        xm_f = xm_ref[g].astype(F32)
        hist = jnp.concatenate([xc_ref[g], xm_f], axis=0)
        conv = convb_ref[...] + convw_ref[CONV_WIDTH - 1:CONV_WIDTH, :] * xm_f
        for k in range(1, CONV_WIDTH):
            conv = conv + (convw_ref[CONV_WIDTH - 1 - k:CONV_WIDTH - k, :]
                           * pltpu.roll(hist, k, axis=0)[SUBLANES:, :])
        xc_ref[g] = xm_f[L - SUBLANES:, :]
        c_act = _silu(conv)

        i_pre = g_ref[g] + bif_ref[...]
        logf = jax.nn.log_sigmoid(pltpu.roll(i_pre, LANES - M_HEADS, axis=1))
        b = _scan_rows(logf, jnp.add, 0.0)
        a = i_pre - b
        ca = _scan_rows(a, jnp.maximum, NEG)
        m_prev = m_ref[g]
        mm = jnp.maximum(ca, m_prev)
        u = -mm
        bL = b[L - 1:L, :]
        m_new = bL + jnp.maximum(m_prev, ca[L - 1:L, :])
        m_ref[g] = m_new
        seqs.append(dict(xm_b=xm_b, c_act=c_act, cb=c_act.astype(BF16), a=a, u=u,
                         w_inter=jnp.exp(u + m_prev), emt=jnp.exp(-(b + mm)),
                         wk=jnp.exp(bL + a - m_new), wC=jnp.exp(bL + m_prev - m_new)))

    units = [(g, h) for g in range(G) for h in range(M_HEADS)]
    vsl = [slice(h * DV, (h + 1) * DV) for h in range(M_HEADS)]
    qs, ks, kws, vss, C_old, n_old = {}, {}, {}, {}, {}, {}
    for g, h in units:
        ch = seqs[g]["cb"][:, vsl[h]]
        qs[g, h] = (jnp.dot(ch, wq_ref[h], preferred_element_type=F32) * (DK ** -0.5)).astype(BF16)
        k32 = jnp.dot(ch, wk_ref[h], preferred_element_type=F32)
        ks[g, h] = _pad_rows(k32.astype(BF16), LS)
        kws[g, h] = _pad_rows(k32 * seqs[g]["wk"][:, h:h + 1], LS)
        vss[g, h] = _pad_rows(seqs[g]["xm_b"][:, vsl[h]], LS)
        C_old[g, h] = C_ref[g, h]
        n_old[g, h] = n_ref[g, h:h + 1, :]
    scs, inters = {}, {}
    for g, h in units:
        a_row = jnp.sum(jnp.where(eye, seqs[g]["a"][:, h:h + 1], 0.0), axis=0, keepdims=True)
        w_intra = jnp.exp(jnp.where(causal, seqs[g]["u"][:, h:h + 1] + a_row, NEG))
        scs[g, h] = lax.dot_general(qs[g, h], ks[g, h], dn_t, preferred_element_type=F32) * w_intra
        inters[g, h] = lax.dot_general(qs[g, h], C_old[g, h].astype(BF16), dn_t, preferred_element_type=F32)
    for g, h in units:
        wi = seqs[g]["w_inter"][:, h:h + 1]
        num = jnp.dot(scs[g, h].astype(BF16), vss[g, h], preferred_element_type=F32) + wi * inters[g, h]
        den = (jnp.sum(scs[g, h], -1, keepdims=True)
               + wi * jnp.sum(qs[g, h].astype(F32) * n_old[g, h], -1, keepdims=True))
        hcell = num / jnp.maximum(jnp.abs(den), seqs[g]["emt"][:, h:h + 1])
        mu = jnp.mean(hcell, -1, keepdims=True)
        hc = hcell - mu
        hn = hc * lax.rsqrt(jnp.mean(hc * hc, -1, keepdims=True) + EPS)
        out = ((om_ref[g, :, vsl[h]].astype(F32) * (hn * mnorm_ref[:, vsl[h]])
                + mskip_ref[:, vsl[h]] * seqs[g]["c_act"][:, vsl[h]]) * zm_ref[g, :, vsl[h]].astype(F32))
        mo_ref[g, :, vsl[h]] = out.astype(mo_ref.dtype)
    for g, h in units:
        wc = seqs[g]["wC"][:, h:h + 1]
        C_ref[g, h] = wc * C_old[g, h] + lax.dot_general(vss[g, h], kws[g, h].astype(BF16),
                                                         (((0,), (0,)), ((), ())),
                                                         preferred_element_type=F32)
        n_ref[g, h:h + 1, :] = wc * n_old[g, h] + jnp.sum(kws[g, h], axis=0, keepdims=True)


def _mlstm_call(xm, zm, om, gates, conv_prev, C0, n0, m0, convw, convb, wq, wk, mnorm, mskip, bif,
                *, L, G, out_dtype, name):
    N, S, M = xm.shape
    H, DV, DK = C0.shape[1:]
    seq = lambda b, c: (b, c, 0)
    per_b3 = lambda b, c: (b, 0, 0)
    per_b4 = lambda b, c: (b, 0, 0, 0)
    const2 = lambda b, c: (0, 0)
    const3 = lambda b, c: (0, 0, 0)
    in_specs = [pl.BlockSpec((G, L, M), seq), pl.BlockSpec((G, L, M), seq), pl.BlockSpec((G, L, M), seq),
                pl.BlockSpec((G, L, LANES), seq),
                pl.BlockSpec((G, SUBLANES, M), per_b3),
                pl.BlockSpec((G, H, DV, DK), per_b4),
                pl.BlockSpec((G, H, DK), per_b3),
                pl.BlockSpec((G, 1, LANES), per_b3),
                pl.BlockSpec(convw.shape, const2), pl.BlockSpec(convb.shape, const2),
                pl.BlockSpec(wq.shape, const3), pl.BlockSpec(wk.shape, const3),
                pl.BlockSpec(mnorm.shape, const2), pl.BlockSpec(mskip.shape, const2),
                pl.BlockSpec(bif.shape, const2)]
    out_specs = [pl.BlockSpec((G, L, M), seq),
                 pl.BlockSpec((G, H, DV, DK), per_b4),
                 pl.BlockSpec((G, H, DK), per_b3),
                 pl.BlockSpec((G, 1, LANES), per_b3)]
    out_shape = [jax.ShapeDtypeStruct((N, S, M), out_dtype),
                 jax.ShapeDtypeStruct((N, H, DV, DK), F32),
                 jax.ShapeDtypeStruct((N, H, DK), F32),
                 jax.ShapeDtypeStruct((N, 1, LANES), F32)]
    return pl.pallas_call(
        functools.partial(_mlstm_kernel, L=L),
        grid=(N // G, S // L),
        in_specs=in_specs, out_specs=out_specs, out_shape=out_shape,
        scratch_shapes=[pltpu.VMEM((G, SUBLANES, M), F32)],
        compiler_params=_params(("arbitrary", "arbitrary")),
        name=name,
    )(xm, zm, om, gates, conv_prev, C0, n0, m0, convw, convb, wq, wk, mnorm, mskip, bif)


def _token_order(src_ref, scr_ref):
    _, d, n, width = src_ref.shape
    if d == 1:
        return src_ref[0, 0].astype(F32)
    n_tiles = width // LANES
    for r in range(d):
        for s in range(n_tiles):
            scr_ref[s, pl.ds(r, n, stride=d), :] = src_ref[0, r, :, s * LANES:(s + 1) * LANES].astype(F32)
    return jnp.concatenate([scr_ref[s] for s in range(n_tiles)], axis=1)


def _post_kernel(*refs, merge):
    if merge:
        (x_ref, gate_ref, o0, o1, o2, l0, l1, l2, expand_ref, za_ref, mo_ref, ga_ref, gm_ref,
         wpa_ref, wpm_ref, wout_ref, fg_ref, y_ref, scr_ref) = refs
        ls = [_token_order(l, scr_ref) for l in (l0, l1, l2)]
        lmax = jnp.maximum(jnp.maximum(ls[0], ls[1]), ls[2])
        es = [jnp.exp(l - lmax) for l in ls]
        inv = 1.0 / (es[0] + es[1] + es[2])
        o_att = None
        for e, o in zip(es, (o0, o1, o2)):
            a = e * inv
            hi = a.astype(BF16)
            lo = (a - hi.astype(F32)).astype(BF16)
            a_wide = jnp.dot(jnp.concatenate([hi, lo], axis=1), expand_ref[...],
                             preferred_element_type=F32)
            term = a_wide * _token_order(o, scr_ref)
            o_att = term if o_att is None else o_att + term
    else:
        (x_ref, gate_ref, oa_ref, za_ref, mo_ref, ga_ref, gm_ref,
         wpa_ref, wpm_ref, wout_ref, fg_ref, y_ref) = refs
        o_att = oa_ref[0]
    a_in = (o_att * za_ref[0].astype(F32)).astype(BF16)
    a_br = jnp.dot(a_in, wpa_ref[...], preferred_element_type=F32)
    m_br = jnp.dot(mo_ref[0].astype(BF16), wpm_ref[...], preferred_element_type=F32)
    merged = ga_ref[0].astype(F32) * a_br + gm_ref[0].astype(F32) * m_br
    y = x_ref[0] + gate_ref[0] * jnp.dot(merged.astype(BF16), wout_ref[...], preferred_element_type=F32)
    ms = jnp.mean(y * y, axis=-1, keepdims=True)
    y_ref[0] = y * lax.rsqrt(ms + EPS) * fg_ref[...]


def _post_call(x3, gate3, att_inputs, za, mo, ga, gm, wpa, wpm, wout, fgain, *, tm, merge, name):
    B, S, D = x3.shape
    row = lambda b, i: (b, i, 0)
    const2 = lambda b, i: (0, 0)
    if gate3.shape[1] == 1:
        gate_spec = pl.BlockSpec((1, 1, D), lambda b, i: (b, 0, 0))
    else:
        gate_spec = pl.BlockSpec((1, tm, D), row)
    def blk(a):
        if a.ndim == 2:
            return pl.BlockSpec(a.shape, const2)
        if a.ndim == 4:
            d = a.shape[1]
            return pl.BlockSpec((1, d, tm // d, a.shape[3]), lambda b, i: (b, 0, i, 0))
        return pl.BlockSpec((1, tm, a.shape[2]), row)

    in_specs = ([pl.BlockSpec((1, tm, D), row), gate_spec]
                + [blk(a) for a in att_inputs]
                + [blk(za), blk(mo), blk(ga), blk(gm),
                   pl.BlockSpec(wpa.shape, const2), pl.BlockSpec(wpm.shape, const2),
                   pl.BlockSpec(wout.shape, const2), pl.BlockSpec(fgain.shape, const2)])
    scratch = [pltpu.VMEM((ATT_WIDTH // LANES, tm, LANES), F32)] if merge else []
    return pl.pallas_call(
        functools.partial(_post_kernel, merge=merge),
        grid=(B, S // tm),
        in_specs=in_specs,
        out_specs=pl.BlockSpec((1, tm, D), row),
        out_shape=jax.ShapeDtypeStruct((B, S, D), F32),
        scratch_shapes=scratch,
        compiler_params=_params(("arbitrary", "arbitrary")),
        name=name,
    )(x3, gate3, *att_inputs, za, mo, ga, gm, wpa, wpm, wout, fgain)


def _cast_kernel(x_ref, o_ref):
    o_ref[...] = x_ref[...].T.astype(o_ref.dtype)


def _cast_bf16_call(wt, ncols):
    k = wt.shape[1]
    tn = 1024
    return pl.pallas_call(
        _cast_kernel,
        grid=(ncols // tn,),
        in_specs=[pl.BlockSpec((tn, k), lambda j: (j, 0))],
        out_specs=pl.BlockSpec((k, tn), lambda j: (0, j)),
        out_shape=jax.ShapeDtypeStruct((k, ncols), BF16),
        compiler_params=_params(("arbitrary",)),
        name="cast_weights",
    )(wt)


def _tail_cast_kernel(a_ref, b_ref, c_ref, o_ref, *, n_gate, n_merge):
    rows = jnp.concatenate([a_ref[...], b_ref[...], c_ref[0:SUBLANES, :]], axis=0)
    gates = jnp.concatenate([rows[0:n_gate], jnp.zeros((LANES - n_gate, rows.shape[1]), F32)], axis=0)
    out = jnp.concatenate([rows[n_gate:n_gate + n_merge], gates], axis=0)
    o_ref[...] = out.T.astype(o_ref.dtype)


def _tail_cast_call(wt, first, n_gate, n_merge):
    k = wt.shape[1]
    tn = n_merge // 2
    j0 = first // tn
    assert first % tn == 0 and n_gate == SUBLANES
    return pl.pallas_call(
        functools.partial(_tail_cast_kernel, n_gate=n_gate, n_merge=n_merge),
        grid=(1,),
        in_specs=[pl.BlockSpec((tn, k), lambda j: (j0, 0), pipeline_mode=pl.Buffered(1)),
                  pl.BlockSpec((tn, k), lambda j: (j0 + 1, 0), pipeline_mode=pl.Buffered(1)),
                  pl.BlockSpec((tn, k), lambda j: (j0 + 2, 0), pipeline_mode=pl.Buffered(1))],
        out_specs=pl.BlockSpec((k, n_merge + LANES), lambda j: (0, 0)),
        out_shape=jax.ShapeDtypeStruct((k, n_merge + LANES), BF16),
        compiler_params=_params(("arbitrary",)),
        name="cast_tail_weights",
    )(wt, wt, wt)


def _projection_weights(w_in):
    D = w_in.shape[0]
    AW = ATT_WIDTH
    M = D
    off_k, off_v, off_za = 3 * AW, 6 * AW, 9 * AW
    off_xm = off_za + AW
    off_i = off_xm + 3 * M
    wt = jnp.transpose(w_in)
    w_main = _cast_bf16_call(wt, off_i)
    w_tail = _tail_cast_call(wt, off_i, 2 * M_HEADS, 2 * D)
    segs = {"za": ((0, off_za, AW),), "xm": ((0, off_xm, M),), "zm": ((0, off_xm + M, M),),
            "om": ((0, off_xm + 2 * M, M),), "ga": ((1, 0, D),), "gm": ((1, D, D),),
            "gates": ((1, 2 * D, LANES),)}
    for g in range(N_GROUPS):
        segs[f"q{g}"] = ((0, g * AW, AW),)
        segs[f"kv{g}"] = ((0, off_k + g * AW, AW), (0, off_v + g * AW, AW))
    return (w_main, w_tail), segs


def kernel(x_prompt, x_sample, cache_kv_w128, cache_kv_w512, cache_kv_w2048, state_conv, state_C, state_n, state_m, c_prompt, c_sample, rel_table, norm_gain, w_ada, b_ada, w_in, b_if, conv_w, conv_b, w_mq, w_mk, m_norm, m_skip, w_pa, w_pm, w_out, final_gain):
    B, S, D = x_prompt.shape
    DB, T, _ = x_sample.shape
    assert norm_gain.shape[0] == 1, "single-layer trunk"
    assert S % ATT_TILE == 0 and S % MLSTM_CHUNK == 0 and T == SUBLANES
    caches = (cache_kv_w128[0], cache_kv_w512[0], cache_kv_w2048[0])
    H = M_HEADS
    M = conv_w.shape[2]

    wp, segs = _projection_weights(w_in[0])
    names = ("q0", "q1", "q2", "za", "kv0", "kv1", "kv2", "xm", "zm", "om", "ga", "gm", "gates")
    seg_list = [segs[n] for n in names]
    gain = norm_gain[0].reshape(1, D)
    fgain = final_gain.reshape(1, D)
    wpa, wpm, wout = w_pa[0].astype(BF16), w_pm[0].astype(BF16), w_out[0].astype(BF16)
    wq, wk = w_mq[0].astype(BF16), w_mk[0].astype(BF16)
    convw, convb = conv_w[0], conv_b[0].reshape(1, M)
    mnorm, mskip = m_norm[0].reshape(1, M), m_skip[0].reshape(1, M)
    bif = jnp.concatenate([b_if[0], jnp.zeros((LANES - 2 * H,), F32)]).reshape(1, LANES)

    ada = _ada_call(jnp.concatenate([c_prompt, c_sample], axis=0), w_ada[0], b_ada[0])
    shift, scale, gate = ada[:, :D], ada[:, D:2 * D], ada[:, 2 * D:]

    gate_act = {"za": "silu", "zm": "silu", "om": "sigmoid", "ga": "sigmoid", "gm": "sigmoid"}
    acts = [gate_act.get(n) for n in names]

    R = DB * T
    rep = lambda t: jnp.repeat(t[B:], T, axis=0).reshape(1, R, D)
    s_shift, s_scale, s_gate = rep(shift), rep(scale), rep(gate)
    xs = x_sample.reshape(1, R, D)
    sr = dict(zip(names, _proj_call(xs, gain, s_scale, s_shift, wp, seg_list, [F32] * 13,
                                    tm=R, row0=0, rows=R, name="proj_sample", acts=acts)))
    bcs, bns = [], []
    for g, (win, dil) in enumerate(ATT_GROUPS):
        bc, bn = _sample_bias(rel_table, g, win, dil, caches[g].shape[1], T)
        bcs.append(bc)
        bns.append(bn)
    cache_t = [jnp.transpose(c, (0, 2, 3, 4, 1)).reshape(DB, 2 * ATT_WIDTH, c.shape[1]) for c in caches]
    kvn_t = [sr[f"kv{g}"].reshape(DB, T, 2 * ATT_WIDTH) for g in range(3)]
    sattn_args = ([sr[f"q{g}"].reshape(DB, T, ATT_WIDTH) for g in range(3)], kvn_t, cache_t, bcs, bns)

    p_shift, p_scale, p_gate = (t[:B].reshape(B, 1, D) for t in (shift, scale, gate))
    dts = [F32 if n == "gates" else BF16 for n in names]
    group_dil = {f"{p}{g}": dil for g, (_, dil) in enumerate(ATT_GROUPS) for p in ("q", "kv")}
    pr = dict(zip(names, _proj_call(x_prompt, gain, p_scale, p_shift, wp, seg_list, dts,
                                    tm=PROJ_TM, row0=0, rows=S, name="proj_prompt",
                                    dils=[group_dil.get(n, 1) for n in names], acts=acts)))
    att = []
    for g, (win, dil) in enumerate(ATT_GROUPS):
        planes = lambda a: a if a.ndim == 4 else a[:, None]
        att.append(_attn_call(planes(pr[f"q{g}"]), planes(pr[f"kv{g}"]), _prompt_bias(rel_table, g, dil),
                              f"attn_prompt_g{g}"))
    expand = np.zeros((2 * LANES, ATT_WIDTH), np.float32)
    for h in range(ATT_HEADS):
        expand[[LSE_LANES * h, LANES + LSE_LANES * h], h * ATT_HEAD_DIM:(h + 1) * ATT_HEAD_DIM] = 1.0
    expand = jnp.asarray(expand, BF16)
    mlstm_p_args = (pr["xm"], pr["zm"], pr["om"], pr["gates"],
                    jnp.zeros((B, SUBLANES, M), F32), jnp.zeros((B,) + state_C.shape[2:], F32),
                    jnp.zeros((B,) + state_n.shape[2:], F32), jnp.zeros((B, 1, LANES), F32),
                    convw, convb, wq, wk, mnorm, mskip, bif)
    mo_p, C_p, n_p, m_p = _mlstm_call(*mlstm_p_args, L=MLSTM_CHUNK, G=_largest_divisor(B, MLSTM_SEQS_PROMPT),
                                      out_dtype=BF16, name="mlstm_prompt")
    sa = _sattn_call(*sattn_args, T)
    y_prompt = _post_call(x_prompt, p_gate, [a[0] for a in att] + [a[1] for a in att] + [expand],
                          pr["za"], mo_p, pr["ga"], pr["gm"], wpa, wpm, wout, fgain,
                          tm=POST_TM, merge=True, name="post_prompt")
    def tail_weights(keys):
        cols, tsegs, c = [], [], 0
        for key in keys:
            pieces = []
            for wi, c0, wd in segs[key]:
                cols.append(wp[wi][:, c0:c0 + wd])
                pieces.append((0, c, wd))
                c += wd
            tsegs.append(tuple(pieces))
        return (jnp.concatenate(cols, axis=1),), tsegs

    w_max = min(ATT_GROUPS[-1][0], S)
    w2, segs2 = tail_weights(["kv2"])
    (kv2_t,) = _proj_call(x_prompt, gain, p_scale, p_shift, w2, segs2, [F32],
                          tm=min(w_max, 2 * PROJ_TM), row0=S - w_max, rows=w_max, name="tail_kv2", dils=[0])
    w_mid = min(ATT_GROUPS[1][0], S)
    w01, segs01 = tail_weights(["kv0", "kv1", "xm"])
    kv0_t, kv1_t, xm_t = _proj_call(x_prompt, gain, p_scale, p_shift, w01, segs01, [F32] * 3,
                                    tm=w_mid, row0=S - w_mid, rows=w_mid, name="tail_kv01", dils=[0, 0, 1])
    w0 = min(ATT_GROUPS[0][0], S)
    as_buffer = lambda t: jnp.transpose(t.reshape(B, 2, ATT_HEADS, ATT_HEAD_DIM, t.shape[2]), (0, 4, 1, 2, 3))[None]
    kv_p = [as_buffer(kv0_t[:, :, w_mid - w0:]), as_buffer(kv1_t), as_buffer(kv2_t)]
    conv_p = xm_t[:, w_mid - (CONV_WIDTH - 1):][None]

    o_att_s = sa[0].reshape(1, R, ATT_WIDTH)
    kv_s = [jnp.transpose(c.reshape(DB, 2, ATT_HEADS, ATT_HEAD_DIM, c.shape[2]), (0, 4, 1, 2, 3))[None]
            for c in sa[1:]]
    conv_prev_s = jnp.concatenate([jnp.zeros((DB, SUBLANES - (CONV_WIDTH - 1), M), F32), state_conv[0]], axis=1)
    m0_s = jnp.concatenate([state_m[0], jnp.zeros((DB, LANES - H), F32)], axis=1).reshape(DB, 1, LANES)
    seqv = lambda t: t.reshape(DB, T, t.shape[-1])
    mo_s, C_s, n_s, m_s = _mlstm_call(
        seqv(sr["xm"]), seqv(sr["zm"]), seqv(sr["om"]), seqv(sr["gates"]),
        conv_prev_s, state_C[0], state_n[0], m0_s,
        convw, convb, wq, wk, mnorm, mskip, bif, L=T, G=_largest_divisor(DB, MLSTM_SEQS_SAMPLE),
        out_dtype=F32, name="mlstm_sample")
    y_sample = _post_call(xs, s_gate, [o_att_s], sr["za"], mo_s.reshape(1, R, M), sr["ga"], sr["gm"],
                          wpa, wpm, wout, fgain, tm=R, merge=False, name="post_sample")
    conv_s = seqv(sr["xm"])[:, T - (CONV_WIDTH - 1):][None]

    return (y_prompt, y_sample.reshape(DB, T, D),
            kv_p[0], kv_s[0], kv_p[1], kv_s[1], kv_p[2], kv_s[2],
            conv_p, conv_s, C_p[None], C_s[None], n_p[None], n_s[None],
            m_p[:, 0, :H][None], m_s[:, 0, :H][None])
```

```python
import functools

import numpy as np
import jax
import jax.numpy as jnp
from jax import lax
from jax.experimental import pallas as pl
from jax.experimental.pallas import tpu as pltpu

F32 = jnp.float32
BF16 = jnp.bfloat16

ATT_GROUPS = ((128, 1), (512, 4), (2048, 16))
N_GROUPS = len(ATT_GROUPS)
ATT_HEADS = 8
ATT_HEAD_DIM = 64
ATT_WIDTH = ATT_HEADS * ATT_HEAD_DIM
WIN_STEPS = 128
ATT_SCALE = ATT_HEAD_DIM ** -0.5
N_BUCKETS = 32
MAX_DISTANCE = 2048
M_HEADS = 4
CONV_WIDTH = 4
EPS = 1e-6
NEG = -1e30

LANES = 128
SUBLANES = 8
VMEM_LIMIT = 56 * 1024 * 1024

ATT_TILE = 2048
PROJ_TM = 512
POST_TM = 1024
MLSTM_CHUNK = 256
MLSTM_SEQS_PROMPT = 2
MLSTM_SEQS_SAMPLE = 8


def _largest_divisor(n, cap):
    return max(g for g in range(1, cap + 1) if n % g == 0)


def _params(sem, vmem=VMEM_LIMIT):
    return pltpu.CompilerParams(dimension_semantics=sem, vmem_limit_bytes=vmem)


def _t5_bucket(dist):
    n = np.asarray(dist).astype(np.int64)
    max_exact = N_BUCKETS // 2
    nf = np.maximum(n, 1).astype(np.float32)
    large = max_exact + (np.log(nf / max_exact) / np.log(np.float32(MAX_DISTANCE / max_exact))
                         * (N_BUCKETS - max_exact)).astype(np.int64)
    large = np.minimum(large, N_BUCKETS - 1)
    return np.where(n < max_exact, n, large).astype(np.int32)


def _sigmoid_tanh(v):
    return 0.5 * jnp.tanh(0.5 * v) + 0.5


def _silu(x):
    return x * _sigmoid_tanh(x)


def _ada_kernel(c_ref, w_ref, b_ref, o_ref):
    s = _silu(c_ref[...])
    o_ref[...] = jnp.dot(s, w_ref[...], preferred_element_type=F32,
                         precision=lax.Precision.HIGHEST) + b_ref[...]


def _ada_call(c, w, b):
    n, d = c.shape
    width = w.shape[1]
    tn = width // 3 if width % (3 * LANES) == 0 else width
    return pl.pallas_call(
        _ada_kernel,
        grid=(width // tn,),
        in_specs=[pl.BlockSpec((n, d), lambda j: (0, 0)),
                  pl.BlockSpec((d, tn), lambda j: (0, j)),
                  pl.BlockSpec((1, tn), lambda j: (0, j))],
        out_specs=pl.BlockSpec((n, tn), lambda j: (0, j)),
        out_shape=jax.ShapeDtypeStruct((n, width), F32),
        compiler_params=_params(("arbitrary",)),
        name="ada",
    )(c, w, b.reshape(1, width))


_ACTIVATIONS = {None: lambda v: v, "silu": _silu, "sigmoid": _sigmoid_tanh}


def _proj_kernel(x_ref, gain_ref, scale_ref, shift_ref, *rest, n_w, segs, dils, acts):
    w_refs = rest[:n_w]
    rest = rest[n_w:]
    out_refs = rest[:len(segs)]
    x = x_ref[0]
    tm, D = x.shape
    ms = jnp.mean(x * x, axis=-1, keepdims=True)
    h = x * lax.rsqrt(ms + EPS) * gain_ref[...] * (1.0 + scale_ref[0]) + shift_ref[0]
    lhs = {1: h.astype(BF16)}
    lhs[0] = lhs[1]
    strides = sorted(set(dils) - {0, 1})
    if strides:
        hs_ref = rest[len(segs)]
        n_tiles = D // LANES
        for s in range(n_tiles):
            hs_ref[s] = h[:, s * LANES:(s + 1) * LANES]
        for d in strides:
            n = tm // d
            lhs[d] = jnp.concatenate(
                [jnp.concatenate([hs_ref[s, pl.ds(r, n, stride=d), :] for r in range(d)], axis=0)
                 for s in range(n_tiles)], axis=1).astype(BF16)
    for o_ref, pieces, d, act in zip(out_refs, segs, dils, acts):
        parts = [_ACTIVATIONS[act](jnp.dot(lhs[d], w_refs[wi][:, c0:c0 + width],
                                           preferred_element_type=F32)).astype(o_ref.dtype)
                 for wi, c0, width in pieces]
        res = parts[0] if len(parts) == 1 else jnp.concatenate(parts, axis=1)
        if d == 0:
            o_ref[0] = res.T
        elif d == 1:
            o_ref[0] = res
        else:
            n = tm // d
            for r in range(d):
                o_ref[0, r] = res[r * n:(r + 1) * n, :]


def _proj_call(x3, gain, scale3, shift3, ws, segs, dtypes, *, tm, row0, rows, name, dils=None, acts=None):
    B, S, D = x3.shape
    nrb = rows // tm
    rb0 = row0 // tm
    dils = tuple(dils) if dils is not None else (1,) * len(segs)
    acts = tuple(acts) if acts is not None else (None,) * len(segs)
    per_row = scale3.shape[1] != 1
    if per_row:
        mod_spec = pl.BlockSpec((1, tm, D), lambda b, i: (b, rb0 + i, 0))
    else:
        mod_spec = pl.BlockSpec((1, 1, D), lambda b, i: (b, 0, 0))
    out_shape, out_specs = [], []
    for pieces, dt, d in zip(segs, dtypes, dils):
        wd = sum(p[2] for p in pieces)
        if d == 0:
            out_shape.append(jax.ShapeDtypeStruct((B, wd, rows), dt))
            out_specs.append(pl.BlockSpec((1, wd, tm), lambda b, i: (b, 0, i)))
        elif d == 1:
            out_shape.append(jax.ShapeDtypeStruct((B, rows, wd), dt))
            out_specs.append(pl.BlockSpec((1, tm, wd), lambda b, i: (b, i, 0)))
        else:
            out_shape.append(jax.ShapeDtypeStruct((B, d, rows // d, wd), dt))
            out_specs.append(pl.BlockSpec((1, d, tm // d, wd), lambda b, i: (b, 0, i, 0)))
    scratch = [pltpu.VMEM((D // LANES, tm, LANES), F32)] if any(d > 1 for d in dils) else []
    return pl.pallas_call(
        functools.partial(_proj_kernel, n_w=len(ws), segs=tuple(segs), dils=dils, acts=acts),
        grid=(B, nrb),
        in_specs=[pl.BlockSpec((1, tm, D), lambda b, i: (b, rb0 + i, 0)),
                  pl.BlockSpec((1, D), lambda b, i: (0, 0)),
                  mod_spec, mod_spec]
                 + [pl.BlockSpec(w.shape, lambda b, i: (0, 0), pipeline_mode=pl.Buffered(1)) for w in ws],
        out_specs=out_specs,
        out_shape=out_shape,
        scratch_shapes=scratch,
        compiler_params=_params(("arbitrary", "arbitrary")),
        name=name,
    )(x3, gain, scale3, shift3, *ws)


HEADS_PER_SLAB = LANES // ATT_HEAD_DIM
N_SLABS = ATT_HEADS // HEADS_PER_SLAB
LSE_LANES = LANES // ATT_HEADS


def _attn_unit(q, kv, bias_ref, table, o_ref, l_ref, at):
    dn = (((1,), (1,)), ((), ()))
    nk = 2 * WIN_STEPS
    lane_q = lax.broadcasted_iota(jnp.int32, (WIN_STEPS, LANES), 1) < ATT_HEAD_DIM
    lane_k = lax.broadcasted_iota(jnp.int32, (nk, LANES), 1) < ATT_HEAD_DIM
    ones_lo = jnp.where(lane_k, 1.0, 0.0).astype(BF16)
    ones_hi = jnp.where(lane_k, 0.0, 1.0).astype(BF16)
    zero_q = jnp.zeros((WIN_STEPS, LANES), BF16)
    zero_k = jnp.zeros((nk, LANES), BF16)

    def scores(m):
        cs = slice(m * LANES, (m + 1) * LANES)
        qs = q[:, cs]
        ks = kv[:, cs]
        out = []
        for hh in range(HEADS_PER_SLAB):
            qm = jnp.where(lane_q, qs, zero_q) if hh == 0 else jnp.where(lane_q, zero_q, qs)
            out.append(lax.dot_general(qm, ks, dn, preferred_element_type=F32)
                       + bias_ref[table, m * HEADS_PER_SLAB + hh])
        return out

    def finish(m, ss):
        cs = slice(m * LANES, (m + 1) * LANES)
        vs = kv[:, ATT_WIDTH + m * LANES:ATT_WIDTH + (m + 1) * LANES]
        ps, mxs = [], []
        for s in ss:
            mx = jnp.max(jnp.maximum(s[:, :WIN_STEPS], s[:, WIN_STEPS:]), -1, keepdims=True)
            ps.append(jnp.exp(s - mx).astype(BF16))
            mxs.append(mx)
        pcat = jnp.concatenate(ps, axis=1)
        vpair = jnp.concatenate(
            [jnp.concatenate([jnp.where(lane_k, vs, zero_k), ones_lo], axis=1),
             jnp.concatenate([jnp.where(lane_k, zero_k, vs), ones_hi], axis=1)], axis=0)
        acc = jnp.dot(pcat, vpair, preferred_element_type=F32)
        den = acc[:, LANES:]
        o_ref[at + (cs,)] = (acc[:, :LANES] / den).astype(o_ref.dtype)
        lse = jnp.where(lane_q, mxs[0], mxs[1]) + jnp.log(den)
        return pltpu.roll(lse, (LSE_LANES * HEADS_PER_SLAB * m - 48) % LANES, axis=1)

    lane = lax.broadcasted_iota(jnp.int32, (WIN_STEPS, LANES), 1)
    lse_c = None
    pending = {0: scores(0), 1: scores(1)}
    for m in range(N_SLABS):
        if m + 2 < N_SLABS:
            pending[m + 2] = scores(m + 2)
        part = finish(m, pending.pop(m))
        lse_c = part if lse_c is None else jnp.where(lane >= LSE_LANES * HEADS_PER_SLAB * m, part, lse_c)
    l_ref[at + (slice(None),)] = lse_c


def _attn_kernel(q_ref, kvc_ref, kvp_ref, bias_ref, o_ref, l_ref, *, ns):
    d = q_ref.shape[1]
    first_tile = pl.program_id(1) == 0

    def body(idx, carry):
        rr = idx // ns
        j = idx % ns
        rc = pl.ds(pl.multiple_of(j * WIN_STEPS, WIN_STEPS), WIN_STEPS)
        q = q_ref[0, rr, rc, :] * ATT_SCALE
        kv_prev = kvp_ref[0, rr]
        if ns > 1:
            rp = pl.ds(pl.multiple_of(jnp.maximum(j - 1, 0) * WIN_STEPS, WIN_STEPS), WIN_STEPS)
            kv_prev = jnp.where(j == 0, kv_prev, kvc_ref[0, rr, rp, :])
        kv = jnp.concatenate([kv_prev, kvc_ref[0, rr, rc, :]], axis=0)
        table = jnp.where(first_tile & (j == 0), 1, 0)
        _attn_unit(q, kv, bias_ref, table, o_ref, l_ref, (0, rr, rc))
        return carry

    lax.fori_loop(0, d * ns, body, 0, unroll=4)


def _attn_call(q, kv, bias, name):
    B, d, U, _ = q.shape
    ns = ATT_TILE // (WIN_STEPS * d)
    rows = ns * WIN_STEPS
    blk = lambda width: pl.BlockSpec((1, d, rows, width), lambda b, t: (b, 0, t, 0))
    return pl.pallas_call(
        functools.partial(_attn_kernel, ns=ns),
        grid=(B, U // rows),
        in_specs=[blk(ATT_WIDTH), blk(2 * ATT_WIDTH),
                  pl.BlockSpec((1, d, WIN_STEPS, 2 * ATT_WIDTH),
                               lambda b, t: (b, 0, jnp.maximum(t * ns - 1, 0), 0)),
                  pl.BlockSpec(bias.shape, lambda b, t: (0, 0, 0, 0))],
        out_specs=[blk(ATT_WIDTH), blk(LANES)],
        out_shape=[jax.ShapeDtypeStruct((B, d, U, ATT_WIDTH), BF16),
                   jax.ShapeDtypeStruct((B, d, U, LANES), F32)],
        compiler_params=_params(("arbitrary", "arbitrary")),
        name=name,
    )(q, kv, kv, bias)


def _stride_bias(rel_table, g, d):
    bucket = _t5_bucket(np.arange(WIN_STEPS + 1) * d)
    onehot = jnp.asarray(np.eye(N_BUCKETS, dtype=np.float32)[bucket])
    tbl = rel_table[:, g * ATT_HEADS:(g + 1) * ATT_HEADS].astype(F32)
    return jnp.dot(onehot, tbl, precision=lax.Precision.HIGHEST).T


def _prompt_bias(rel_table, g, d):
    vals = _stride_bias(rel_table, g, d)
    n = WIN_STEPS
    period = 3 * n
    wp = jnp.concatenate([jnp.full((ATT_HEADS, n - 1), NEG, F32), vals[:, ::-1],
                          jnp.full((ATT_HEADS, n), NEG, F32)], axis=1)
    flat = jnp.tile(wp, (1, n))[:, :n * (period - 1)]
    bias = flat.reshape(ATT_HEADS, n, period - 1)[:, :, n - 1:n - 1 + 2 * n]
    first = jnp.concatenate([jnp.full((ATT_HEADS, n, n), NEG, F32), bias[:, :, n:]], axis=2)
    return jnp.stack([bias, first])


SHIFT_VREGS = 512


def _sattn_kernel(*refs, T):
    q_refs = refs[0:3]
    kvnew_refs = refs[3:6]
    cache_refs = refs[6:9]
    bc_refs = refs[9:12]
    bn_refs = refs[12:15]
    o_ref = refs[15]
    co_refs = refs[16:19]
    kvn_refs = refs[19:22]
    HT = ATT_HEADS * T
    dn = (((1,), (1,)), ((), ()))
    row_head = lax.broadcasted_iota(jnp.int32, (HT, ATT_WIDTH), 0) // T
    col_head = lax.broadcasted_iota(jnp.int32, (HT, ATT_WIDTH), 1) // ATT_HEAD_DIM
    head_mask = row_head == col_head

    for g in range(N_GROUPS):
        rows = jnp.concatenate([jnp.zeros((LANES - T, 2 * ATT_WIDTH), F32), kvnew_refs[g][0]], axis=0)
        kvn_refs[g][0] = rows.T

    stats = []
    for g in range(N_GROUPS):
        q = q_refs[g][0] * ATT_SCALE
        qexp = jnp.where(head_mask, jnp.concatenate([q] * ATT_HEADS, axis=0), 0.0).astype(BF16)
        kn = kvn_refs[g][0, :ATT_WIDTH, :].astype(BF16)
        kc = cache_refs[g][0, :ATT_WIDTH, :].astype(BF16)
        lc = jnp.dot(qexp, kc, preferred_element_type=F32) + bc_refs[g][...]
        ln = jnp.dot(qexp, kn, preferred_element_type=F32) + bn_refs[g][...]
        mx = jnp.maximum(jnp.max(lc, -1, keepdims=True), jnp.max(ln, -1, keepdims=True))
        pc = jnp.exp(lc - mx)
        pn = jnp.exp(ln - mx)
        ssum = jnp.sum(pc, -1, keepdims=True) + jnp.sum(pn, -1, keepdims=True)
        stats.append((pc, pn, ssum, mx + jnp.log(ssum)))

    lse_max = jnp.maximum(jnp.maximum(stats[0][3], stats[1][3]), stats[2][3])
    es = [jnp.exp(st[3] - lse_max) for st in stats]
    esum = es[0] + es[1] + es[2]
    acc = jnp.zeros((HT, ATT_WIDTH), F32)
    for g in range(N_GROUPS):
        pc, pn, ssum, _ = stats[g]
        w = es[g] / (esum * ssum)
        vc = cache_refs[g][0, ATT_WIDTH:, :].astype(BF16)
        vn = kvn_refs[g][0, ATT_WIDTH:, :].astype(BF16)
        acc = acc + lax.dot_general((pc * w).astype(BF16), vc, dn, preferred_element_type=F32)
        acc = acc + lax.dot_general((pn * w).astype(BF16), vn, dn, preferred_element_type=F32)
    lane_head = lax.broadcasted_iota(jnp.int32, (T, ATT_WIDTH), 1) // ATT_HEAD_DIM
    o = jnp.zeros((T, ATT_WIDTH), F32)
    for h in range(ATT_HEADS):
        o = o + jnp.where(lane_head == h, acc[h * T:(h + 1) * T, :], 0.0)
    o_ref[0] = o

    for g in range(N_GROUPS):
        L = cache_refs[g].shape[2]
        nrows = cache_refs[g].shape[1]
        chunk = min(nrows, SHIFT_VREGS * SUBLANES * LANES // L)
        is_new = lax.broadcasted_iota(jnp.int32, (chunk, LANES), 1) >= LANES - T

        def shift_rows(i, carry, g=g, L=L, chunk=chunk, is_new=is_new):
            rs = pl.ds(pl.multiple_of(i * chunk, chunk), chunk)
            rolled = pltpu.roll(cache_refs[g][0, rs, :], L - T, axis=1)
            if L > LANES:
                co_refs[g][0, rs, 0:L - LANES] = rolled[:, 0:L - LANES]
            co_refs[g][0, rs, L - LANES:L] = jnp.where(is_new, kvn_refs[g][0, rs, :], rolled[:, L - LANES:])
            return carry

        lax.fori_loop(0, nrows // chunk, shift_rows, 0)


def _sample_bias(rel_table, g, W, d, Lb, T):
    vals = _stride_bias(rel_table, g, d)
    n = Lb + T
    by_dist = jnp.concatenate([vals[:, :, None], jnp.full((ATT_HEADS, WIN_STEPS + 1, d - 1), NEG, F32)],
                              axis=2).reshape(ATT_HEADS, (WIN_STEPS + 1) * d)
    if by_dist.shape[1] < n:
        by_dist = jnp.concatenate([by_dist, jnp.full((ATT_HEADS, n - by_dist.shape[1]), NEG, F32)], axis=1)
    rev = jnp.concatenate([by_dist[:, :n][:, ::-1], jnp.full((ATT_HEADS, T), NEG, F32)], axis=1)
    rows = jnp.stack([rev[:, T - 1 - t:T - 1 - t + n] for t in range(T)], axis=1)
    rows = rows.reshape(ATT_HEADS * T, n)
    bc = rows[:, :Lb]
    bn = jnp.concatenate([jnp.full((ATT_HEADS * T, LANES - T), NEG, F32), rows[:, Lb:]], axis=1)
    return bc, bn


def _sattn_call(qs, kvns, caches, bcs, bns, T):
    DB = caches[0].shape[0]
    rows = caches[0].shape[1]
    per_seq = lambda b: (b, 0, 0)
    const2 = lambda b: (0, 0)
    in_specs = ([pl.BlockSpec((1, T, ATT_WIDTH), per_seq)] * 3
                + [pl.BlockSpec((1,) + x.shape[1:], per_seq) for x in kvns]
                + [pl.BlockSpec((1,) + c.shape[1:], per_seq) for c in caches]
                + [pl.BlockSpec(x.shape, const2) for x in bcs]
                + [pl.BlockSpec(x.shape, const2) for x in bns])
    out_specs = ([pl.BlockSpec((1, T, ATT_WIDTH), per_seq)]
                 + [pl.BlockSpec((1,) + c.shape[1:], per_seq) for c in caches])
    out_shape = ([jax.ShapeDtypeStruct((DB, T, ATT_WIDTH), F32)]
                 + [jax.ShapeDtypeStruct(c.shape, F32) for c in caches])
    return pl.pallas_call(
        functools.partial(_sattn_kernel, T=T),
        grid=(DB,),
        in_specs=in_specs, out_specs=out_specs, out_shape=out_shape,
        scratch_shapes=[pltpu.VMEM((1, rows, LANES), F32)] * N_GROUPS,
        compiler_params=_params(("arbitrary",)),
        name="sample_attn",
    )(*qs, *kvns, *caches, *bcs, *bns)


def _scan_rows(x, op, fill):
    n = x.shape[0]
    rowid = lax.broadcasted_iota(jnp.int32, x.shape, 0)
    s = 1
    while s < n:
        shifted = pltpu.roll(x, s, axis=0)
        x = op(x, jnp.where(rowid >= s, shifted, fill))
        s *= 2
    return x


def _pad_rows(x, n, fill=0.0):
    if x.shape[0] == n:
        return x
    return jnp.concatenate([x, jnp.full((n - x.shape[0],) + x.shape[1:], fill, x.dtype)], axis=0)


def _mlstm_kernel(xm_ref, zm_ref, om_ref, g_ref, cprev_ref, C0_ref, n0_ref, m0_ref,
                  convw_ref, convb_ref, wq_ref, wk_ref, mnorm_ref, mskip_ref, bif_ref,
                  mo_ref, C_ref, n_ref, m_ref, xc_ref, *, L):
    LS = max(L, LANES)
    G = xm_ref.shape[0]
    DV = C_ref.shape[2]
    DK = C_ref.shape[3]
    c = pl.program_id(1)

    @pl.when(c == 0)
    def _():
        xc_ref[...] = cprev_ref[...]
        C_ref[...] = C0_ref[...]
        n_ref[...] = n0_ref[...]
        m_ref[...] = m0_ref[...]

    t_id = lax.broadcasted_iota(jnp.int32, (L, LS), 0)
    s_id = lax.broadcasted_iota(jnp.int32, (L, LS), 1)
    causal = s_id <= t_id
    eye = s_id == t_id
    dn_t = (((1,), (1,)), ((), ()))

    seqs = []
    for g in range(G):
        xm_b = xm_ref[g].astype(BF16)
        xm_f = xm_ref[g].astype(F32)
        hist = jnp.concatenate([xc_ref[g], xm_f], axis=0)
        conv = convb_ref[...] + convw_ref[CONV_WIDTH - 1:CONV_WIDTH, :] * xm_f
        for k in range(1, CONV_WIDTH):
            conv = conv + (convw_ref[CONV_WIDTH - 1 - k:CONV_WIDTH - k, :]
                           * pltpu.roll(hist, k, axis=0)[SUBLANES:, :])
        xc_ref[g] = xm_f[L - SUBLANES:, :]
        c_act = _silu(conv)

        i_pre = g_ref[g] + bif_ref[...]
        logf = jax.nn.log_sigmoid(pltpu.roll(i_pre, LANES - M_HEADS, axis=1))
        b = _scan_rows(logf, jnp.add, 0.0)
        a = i_pre - b
        ca = _scan_rows(a, jnp.maximum, NEG)
        m_prev = m_ref[g]
        mm = jnp.maximum(ca, m_prev)
        u = -mm
        bL = b[L - 1:L, :]
        m_new = bL + jnp.maximum(m_prev, ca[L - 1:L, :])
        m_ref[g] = m_new
        seqs.append(dict(xm_b=xm_b, c_act=c_act, cb=c_act.astype(BF16), a=a, u=u,
                         w_inter=jnp.exp(u + m_prev), emt=jnp.exp(-(b + mm)),
                         wk=jnp.exp(bL + a - m_new), wC=jnp.exp(bL + m_prev - m_new)))

    units = [(g, h) for g in range(G) for h in range(M_HEADS)]
    vsl = [slice(h * DV, (h + 1) * DV) for h in range(M_HEADS)]
    qs, ks, kws, vss, C_old, n_old = {}, {}, {}, {}, {}, {}
    for g, h in units:
        ch = seqs[g]["cb"][:, vsl[h]]
        qs[g, h] = (jnp.dot(ch, wq_ref[h], preferred_element_type=F32) * (DK ** -0.5)).astype(BF16)
        k32 = jnp.dot(ch, wk_ref[h], preferred_element_type=F32)
        ks[g, h] = _pad_rows(k32.astype(BF16), LS)
        kws[g, h] = _pad_rows(k32 * seqs[g]["wk"][:, h:h + 1], LS)
        vss[g, h] = _pad_rows(seqs[g]["xm_b"][:, vsl[h]], LS)
        C_old[g, h] = C_ref[g, h]
        n_old[g, h] = n_ref[g, h:h + 1, :]
    scs, inters = {}, {}
    for g, h in units:
        a_row = jnp.sum(jnp.where(eye, seqs[g]["a"][:, h:h + 1], 0.0), axis=0, keepdims=True)
        w_intra = jnp.exp(jnp.where(causal, seqs[g]["u"][:, h:h + 1] + a_row, NEG))
        scs[g, h] = lax.dot_general(qs[g, h], ks[g, h], dn_t, preferred_element_type=F32) * w_intra
        inters[g, h] = lax.dot_general(qs[g, h], C_old[g, h].astype(BF16), dn_t, preferred_element_type=F32)
    for g, h in units:
        wi = seqs[g]["w_inter"][:, h:h + 1]
        num = jnp.dot(scs[g, h].astype(BF16), vss[g, h], preferred_element_type=F32) + wi * inters[g, h]
        den = (jnp.sum(scs[g, h], -1, keepdims=True)
               + wi * jnp.sum(qs[g, h].astype(F32) * n_old[g, h], -1, keepdims=True))
        hcell = num / jnp.maximum(jnp.abs(den), seqs[g]["emt"][:, h:h + 1])
        mu = jnp.mean(hcell, -1, keepdims=True)
        hc = hcell - mu
        hn = hc * lax.rsqrt(jnp.mean(hc * hc, -1, keepdims=True) + EPS)
        out = ((om_ref[g, :, vsl[h]].astype(F32) * (hn * mnorm_ref[:, vsl[h]])
                + mskip_ref[:, vsl[h]] * seqs[g]["c_act"][:, vsl[h]]) * zm_ref[g, :, vsl[h]].astype(F32))
        mo_ref[g, :, vsl[h]] = out.astype(mo_ref.dtype)
    for g, h in units:
        wc = seqs[g]["wC"][:, h:h + 1]
        C_ref[g, h] = wc * C_old[g, h] + lax.dot_general(vss[g, h], kws[g, h].astype(BF16),
                                                         (((0,), (0,)), ((), ())),
                                                         preferred_element_type=F32)
        n_ref[g, h:h + 1, :] = wc * n_old[g, h] + jnp.sum(kws[g, h], axis=0, keepdims=True)


def _mlstm_call(xm, zm, om, gates, conv_prev, C0, n0, m0, convw, convb, wq, wk, mnorm, mskip, bif,
                *, L, G, out_dtype, name):
    N, S, M = xm.shape
    H, DV, DK = C0.shape[1:]
    seq = lambda b, c: (b, c, 0)
    per_b3 = lambda b, c: (b, 0, 0)
    per_b4 = lambda b, c: (b, 0, 0, 0)
    const2 = lambda b, c: (0, 0)
    const3 = lambda b, c: (0, 0, 0)
    in_specs = [pl.BlockSpec((G, L, M), seq), pl.BlockSpec((G, L, M), seq), pl.BlockSpec((G, L, M), seq),
                pl.BlockSpec((G, L, LANES), seq),
                pl.BlockSpec((G, SUBLANES, M), per_b3),
                pl.BlockSpec((G, H, DV, DK), per_b4),
                pl.BlockSpec((G, H, DK), per_b3),
                pl.BlockSpec((G, 1, LANES), per_b3),
                pl.BlockSpec(convw.shape, const2), pl.BlockSpec(convb.shape, const2),
                pl.BlockSpec(wq.shape, const3), pl.BlockSpec(wk.shape, const3),
                pl.BlockSpec(mnorm.shape, const2), pl.BlockSpec(mskip.shape, const2),
                pl.BlockSpec(bif.shape, const2)]
    out_specs = [pl.BlockSpec((G, L, M), seq),
                 pl.BlockSpec((G, H, DV, DK), per_b4),
                 pl.BlockSpec((G, H, DK), per_b3),
                 pl.BlockSpec((G, 1, LANES), per_b3)]
    out_shape = [jax.ShapeDtypeStruct((N, S, M), out_dtype),
                 jax.ShapeDtypeStruct((N, H, DV, DK), F32),
                 jax.ShapeDtypeStruct((N, H, DK), F32),
                 jax.ShapeDtypeStruct((N, 1, LANES), F32)]
    return pl.pallas_call(
        functools.partial(_mlstm_kernel, L=L),
        grid=(N // G, S // L),
        in_specs=in_specs, out_specs=out_specs, out_shape=out_shape,
        scratch_shapes=[pltpu.VMEM((G, SUBLANES, M), F32)],
        compiler_params=_params(("arbitrary", "arbitrary")),
        name=name,
    )(xm, zm, om, gates, conv_prev, C0, n0, m0, convw, convb, wq, wk, mnorm, mskip, bif)


def _token_order(src_ref, scr_ref):
    _, d, n, width = src_ref.shape
    if d == 1:
        return src_ref[0, 0].astype(F32)
    n_tiles = width // LANES
    for r in range(d):
        for s in range(n_tiles):
            scr_ref[s, pl.ds(r, n, stride=d), :] = src_ref[0, r, :, s * LANES:(s + 1) * LANES].astype(F32)
    return jnp.concatenate([scr_ref[s] for s in range(n_tiles)], axis=1)


def _post_kernel(*refs, merge):
    if merge:
        (x_ref, gate_ref, o0, o1, o2, l0, l1, l2, expand_ref, za_ref, mo_ref, ga_ref, gm_ref,
         wpa_ref, wpm_ref, wout_ref, fg_ref, y_ref, scr_ref) = refs
        ls = [_token_order(l, scr_ref) for l in (l0, l1, l2)]
        lmax = jnp.maximum(jnp.maximum(ls[0], ls[1]), ls[2])
        es = [jnp.exp(l - lmax) for l in ls]
        inv = 1.0 / (es[0] + es[1] + es[2])
        o_att = None
        for e, o in zip(es, (o0, o1, o2)):
            a = e * inv
            hi = a.astype(BF16)
            lo = (a - hi.astype(F32)).astype(BF16)
            a_wide = jnp.dot(jnp.concatenate([hi, lo], axis=1), expand_ref[...],
                             preferred_element_type=F32)
            term = a_wide * _token_order(o, scr_ref)
            o_att = term if o_att is None else o_att + term
    else:
        (x_ref, gate_ref, oa_ref, za_ref, mo_ref, ga_ref, gm_ref,
         wpa_ref, wpm_ref, wout_ref, fg_ref, y_ref) = refs
        o_att = oa_ref[0]
    a_in = (o_att * za_ref[0].astype(F32)).astype(BF16)
    a_br = jnp.dot(a_in, wpa_ref[...], preferred_element_type=F32)
    m_br = jnp.dot(mo_ref[0].astype(BF16), wpm_ref[...], preferred_element_type=F32)
    merged = ga_ref[0].astype(F32) * a_br + gm_ref[0].astype(F32) * m_br
    y = x_ref[0] + gate_ref[0] * jnp.dot(merged.astype(BF16), wout_ref[...], preferred_element_type=F32)
    ms = jnp.mean(y * y, axis=-1, keepdims=True)
    y_ref[0] = y * lax.rsqrt(ms + EPS) * fg_ref[...]


def _post_call(x3, gate3, att_inputs, za, mo, ga, gm, wpa, wpm, wout, fgain, *, tm, merge, name):
    B, S, D = x3.shape
    row = lambda b, i: (b, i, 0)
    const2 = lambda b, i: (0, 0)
    if gate3.shape[1] == 1:
        gate_spec = pl.BlockSpec((1, 1, D), lambda b, i: (b, 0, 0))
    else:
        gate_spec = pl.BlockSpec((1, tm, D), row)
    def blk(a):
        if a.ndim == 2:
            return pl.BlockSpec(a.shape, const2)
        if a.ndim == 4:
            d = a.shape[1]
            return pl.BlockSpec((1, d, tm // d, a.shape[3]), lambda b, i: (b, 0, i, 0))
        return pl.BlockSpec((1, tm, a.shape[2]), row)

    in_specs = ([pl.BlockSpec((1, tm, D), row), gate_spec]
                + [blk(a) for a in att_inputs]
                + [blk(za), blk(mo), blk(ga), blk(gm),
                   pl.BlockSpec(wpa.shape, const2), pl.BlockSpec(wpm.shape, const2),
                   pl.BlockSpec(wout.shape, const2), pl.BlockSpec(fgain.shape, const2)])
    scratch = [pltpu.VMEM((ATT_WIDTH // LANES, tm, LANES), F32)] if merge else []
    return pl.pallas_call(
        functools.partial(_post_kernel, merge=merge),
        grid=(B, S // tm),
        in_specs=in_specs,
        out_specs=pl.BlockSpec((1, tm, D), row),
        out_shape=jax.ShapeDtypeStruct((B, S, D), F32),
        scratch_shapes=scratch,
        compiler_params=_params(("arbitrary", "arbitrary")),
        name=name,
    )(x3, gate3, *att_inputs, za, mo, ga, gm, wpa, wpm, wout, fgain)


def _cast_kernel(x_ref, o_ref):
    o_ref[...] = x_ref[...].T.astype(o_ref.dtype)


def _cast_bf16_call(wt, ncols):
    k = wt.shape[1]
    tn = 1024
    return pl.pallas_call(
        _cast_kernel,
        grid=(ncols // tn,),
        in_specs=[pl.BlockSpec((tn, k), lambda j: (j, 0))],
        out_specs=pl.BlockSpec((k, tn), lambda j: (0, j)),
        out_shape=jax.ShapeDtypeStruct((k, ncols), BF16),
        compiler_params=_params(("arbitrary",)),
        name="cast_weights",
    )(wt)


def _tail_cast_kernel(a_ref, b_ref, c_ref, o_ref, *, n_gate, n_merge):
    rows = jnp.concatenate([a_ref[...], b_ref[...], c_ref[0:SUBLANES, :]], axis=0)
    gates = jnp.concatenate([rows[0:n_gate], jnp.zeros((LANES - n_gate, rows.shape[1]), F32)], axis=0)
    out = jnp.concatenate([rows[n_gate:n_gate + n_merge], gates], axis=0)
    o_ref[...] = out.T.astype(o_ref.dtype)


def _tail_cast_call(wt, first, n_gate, n_merge):
    k = wt.shape[1]
    tn = n_merge // 2
    j0 = first // tn
    assert first % tn == 0 and n_gate == SUBLANES
    return pl.pallas_call(
        functools.partial(_tail_cast_kernel, n_gate=n_gate, n_merge=n_merge),
        grid=(1,),
        in_specs=[pl.BlockSpec((tn, k), lambda j: (j0, 0), pipeline_mode=pl.Buffered(1)),
                  pl.BlockSpec((tn, k), lambda j: (j0 + 1, 0), pipeline_mode=pl.Buffered(1)),
                  pl.BlockSpec((tn, k), lambda j: (j0 + 2, 0), pipeline_mode=pl.Buffered(1))],
        out_specs=pl.BlockSpec((k, n_merge + LANES), lambda j: (0, 0)),
        out_shape=jax.ShapeDtypeStruct((k, n_merge + LANES), BF16),
        compiler_params=_params(("arbitrary",)),
        name="cast_tail_weights",
    )(wt, wt, wt)


def _projection_weights(w_in):
    D = w_in.shape[0]
    AW = ATT_WIDTH
    M = D
    off_k, off_v, off_za = 3 * AW, 6 * AW, 9 * AW
    off_xm = off_za + AW
    off_i = off_xm + 3 * M
    wt = jnp.transpose(w_in)
    w_main = _cast_bf16_call(wt, off_i)
    w_tail = _tail_cast_call(wt, off_i, 2 * M_HEADS, 2 * D)
    segs = {"za": ((0, off_za, AW),), "xm": ((0, off_xm, M),), "zm": ((0, off_xm + M, M),),
            "om": ((0, off_xm + 2 * M, M),), "ga": ((1, 0, D),), "gm": ((1, D, D),),
            "gates": ((1, 2 * D, LANES),)}
    for g in range(N_GROUPS):
        segs[f"q{g}"] = ((0, g * AW, AW),)
        segs[f"kv{g}"] = ((0, off_k + g * AW, AW), (0, off_v + g * AW, AW))
    return (w_main, w_tail), segs


def kernel(x_prompt, x_sample, cache_kv_w128, cache_kv_w512, cache_kv_w2048, state_conv, state_C, state_n, state_m, c_prompt, c_sample, rel_table, norm_gain, w_ada, b_ada, w_in, b_if, conv_w, conv_b, w_mq, w_mk, m_norm, m_skip, w_pa, w_pm, w_out, final_gain):
    B, S, D = x_prompt.shape
    DB, T, _ = x_sample.shape
    assert norm_gain.shape[0] == 1, "single-layer trunk"
    assert S % ATT_TILE == 0 and S % MLSTM_CHUNK == 0 and T == SUBLANES
    caches = (cache_kv_w128[0], cache_kv_w512[0], cache_kv_w2048[0])
    H = M_HEADS
    M = conv_w.shape[2]

    wp, segs = _projection_weights(w_in[0])
    names = ("q0", "q1", "q2", "za", "kv0", "kv1", "kv2", "xm", "zm", "om", "ga", "gm", "gates")
    seg_list = [segs[n] for n in names]
    gain = norm_gain[0].reshape(1, D)
    fgain = final_gain.reshape(1, D)
    wpa, wpm, wout = w_pa[0].astype(BF16), w_pm[0].astype(BF16), w_out[0].astype(BF16)
    wq, wk = w_mq[0].astype(BF16), w_mk[0].astype(BF16)
    convw, convb = conv_w[0], conv_b[0].reshape(1, M)
    mnorm, mskip = m_norm[0].reshape(1, M), m_skip[0].reshape(1, M)
    bif = jnp.concatenate([b_if[0], jnp.zeros((LANES - 2 * H,), F32)]).reshape(1, LANES)

    ada = _ada_call(jnp.concatenate([c_prompt, c_sample], axis=0), w_ada[0], b_ada[0])
    shift, scale, gate = ada[:, :D], ada[:, D:2 * D], ada[:, 2 * D:]

    gate_act = {"za": "silu", "zm": "silu", "om": "sigmoid", "ga": "sigmoid", "gm": "sigmoid"}
    acts = [gate_act.get(n) for n in names]

    R = DB * T
    rep = lambda t: jnp.repeat(t[B:], T, axis=0).reshape(1, R, D)
    s_shift, s_scale, s_gate = rep(shift), rep(scale), rep(gate)
    xs = x_sample.reshape(1, R, D)
    sr = dict(zip(names, _proj_call(xs, gain, s_scale, s_shift, wp, seg_list, [F32] * 13,
                                    tm=R, row0=0, rows=R, name="proj_sample", acts=acts)))
    bcs, bns = [], []
    for g, (win, dil) in enumerate(ATT_GROUPS):
        bc, bn = _sample_bias(rel_table, g, win, dil, caches[g].shape[1], T)
        bcs.append(bc)
        bns.append(bn)
    cache_t = [jnp.transpose(c, (0, 2, 3, 4, 1)).reshape(DB, 2 * ATT_WIDTH, c.shape[1]) for c in caches]
    kvn_t = [sr[f"kv{g}"].reshape(DB, T, 2 * ATT_WIDTH) for g in range(3)]
    sattn_args = ([sr[f"q{g}"].reshape(DB, T, ATT_WIDTH) for g in range(3)], kvn_t, cache_t, bcs, bns)

    p_shift, p_scale, p_gate = (t[:B].reshape(B, 1, D) for t in (shift, scale, gate))
    dts = [F32 if n == "gates" else BF16 for n in names]
    group_dil = {f"{p}{g}": dil for g, (_, dil) in enumerate(ATT_GROUPS) for p in ("q", "kv")}
    pr = dict(zip(names, _proj_call(x_prompt, gain, p_scale, p_shift, wp, seg_list, dts,
                                    tm=PROJ_TM, row0=0, rows=S, name="proj_prompt",
                                    dils=[group_dil.get(n, 1) for n in names], acts=acts)))
    att = []
    for g, (win, dil) in enumerate(ATT_GROUPS):
        planes = lambda a: a if a.ndim == 4 else a[:, None]
        att.append(_attn_call(planes(pr[f"q{g}"]), planes(pr[f"kv{g}"]), _prompt_bias(rel_table, g, dil),
                              f"attn_prompt_g{g}"))
    expand = np.zeros((2 * LANES, ATT_WIDTH), np.float32)
    for h in range(ATT_HEADS):
        expand[[LSE_LANES * h, LANES + LSE_LANES * h], h * ATT_HEAD_DIM:(h + 1) * ATT_HEAD_DIM] = 1.0
    expand = jnp.asarray(expand, BF16)
    mlstm_p_args = (pr["xm"], pr["zm"], pr["om"], pr["gates"],
                    jnp.zeros((B, SUBLANES, M), F32), jnp.zeros((B,) + state_C.shape[2:], F32),
                    jnp.zeros((B,) + state_n.shape[2:], F32), jnp.zeros((B, 1, LANES), F32),
                    convw, convb, wq, wk, mnorm, mskip, bif)
    mo_p, C_p, n_p, m_p = _mlstm_call(*mlstm_p_args, L=MLSTM_CHUNK, G=_largest_divisor(B, MLSTM_SEQS_PROMPT),
                                      out_dtype=BF16, name="mlstm_prompt")
    sa = _sattn_call(*sattn_args, T)
    y_prompt = _post_call(x_prompt, p_gate, [a[0] for a in att] + [a[1] for a in att] + [expand],
                          pr["za"], mo_p, pr["ga"], pr["gm"], wpa, wpm, wout, fgain,
                          tm=POST_TM, merge=True, name="post_prompt")
    def tail_weights(keys):
        cols, tsegs, c = [], [], 0
        for key in keys:
            pieces = []
            for wi, c0, wd in segs[key]:
                cols.append(wp[wi][:, c0:c0 + wd])
                pieces.append((0, c, wd))
                c += wd
            tsegs.append(tuple(pieces))
        return (jnp.concatenate(cols, axis=1),), tsegs

    w_max = min(ATT_GROUPS[-1][0], S)
    w2, segs2 = tail_weights(["kv2"])
    (kv2_t,) = _proj_call(x_prompt, gain, p_scale, p_shift, w2, segs2, [F32],
                          tm=min(w_max, 2 * PROJ_TM), row0=S - w_max, rows=w_max, name="tail_kv2", dils=[0])
    w_mid = min(ATT_GROUPS[1][0], S)
    w01, segs01 = tail_weights(["kv0", "kv1", "xm"])
    kv0_t, kv1_t, xm_t = _proj_call(x_prompt, gain, p_scale, p_shift, w01, segs01, [F32] * 3,
                                    tm=w_mid, row0=S - w_mid, rows=w_mid, name="tail_kv01", dils=[0, 0, 1])
    w0 = min(ATT_GROUPS[0][0], S)
    as_buffer = lambda t: jnp.transpose(t.reshape(B, 2, ATT_HEADS, ATT_HEAD_DIM, t.shape[2]), (0, 4, 1, 2, 3))[None]
    kv_p = [as_buffer(kv0_t[:, :, w_mid - w0:]), as_buffer(kv1_t), as_buffer(kv2_t)]
    conv_p = xm_t[:, w_mid - (CONV_WIDTH - 1):][None]

    o_att_s = sa[0].reshape(1, R, ATT_WIDTH)
    kv_s = [jnp.transpose(c.reshape(DB, 2, ATT_HEADS, ATT_HEAD_DIM, c.shape[2]), (0, 4, 1, 2, 3))[None]
            for c in sa[1:]]
    conv_prev_s = jnp.concatenate([jnp.zeros((DB, SUBLANES - (CONV_WIDTH - 1), M), F32), state_conv[0]], axis=1)
    m0_s = jnp.concatenate([state_m[0], jnp.zeros((DB, LANES - H), F32)], axis=1).reshape(DB, 1, LANES)
    seqv = lambda t: t.reshape(DB, T, t.shape[-1])
    mo_s, C_s, n_s, m_s = _mlstm_call(
        seqv(sr["xm"]), seqv(sr["zm"]), seqv(sr["om"]), seqv(sr["gates"]),
        conv_prev_s, state_C[0], state_n[0], m0_s,
        convw, convb, wq, wk, mnorm, mskip, bif, L=T, G=_largest_divisor(DB, MLSTM_SEQS_SAMPLE),
        out_dtype=F32, name="mlstm_sample")
    y_sample = _post_call(xs, s_gate, [o_att_s], sr["za"], mo_s.reshape(1, R, M), sr["ga"], sr["gm"],
                          wpa, wpm, wout, fgain, tm=R, merge=False, name="post_sample")
    conv_s = seqv(sr["xm"])[:, T - (CONV_WIDTH - 1):][None]

    return (y_prompt, y_sample.reshape(DB, T, D),
            kv_p[0], kv_s[0], kv_p[1], kv_s[1], kv_p[2], kv_s[2],
            conv_p, conv_s, C_p[None], C_s[None], n_p[None], n_s[None],
            m_p[:, 0, :H][None], m_s[:, 0, :H][None])
```
